```python
import math
import jax, jax.numpy as jnp
from jax import lax
import numpy as np

D_MODEL = 1024
BATCH = 8
SEQ = 4096
DEPTH = 1
DEC_BATCH = 32
DEC_SEQ = 4
PAST_LEN = 16384
PAGE_SIZE = 128

NSA_HEADS = 8
NSA_KV_HEADS = 2
NSA_DH = 64
NSA_GROUP = NSA_HEADS // NSA_KV_HEADS
NSA_W = NSA_HEADS * NSA_DH
KV_W = NSA_KV_HEADS * NSA_DH
CMP_STRIDE = 16
CMP_LEN = 2 * CMP_STRIDE
CMP_HIDDEN = 256
SEL_BLOCK = 64
SEL_TOPK = 16
WINDOW = 512
Q_BLOCK = 64
MLSTM_HEADS = 4
MLSTM_DH = 128
MLSTM_W = MLSTM_HEADS * MLSTM_DH
CONV_W = 4
MLSTM_CHUNK = 64
MEM_TOKENS = 256
XA_HEADS = 4
XA_DH = D_MODEL // XA_HEADS
D_FF = -(-(8 * D_MODEL) // (3 * 256)) * 256
IN_SPLITS = (NSA_W, KV_W, KV_W, KV_W, KV_W, KV_W, KV_W, 3 * NSA_HEADS, MLSTM_W, MLSTM_W, MLSTM_W, MLSTM_HEADS, MLSTM_HEADS)
IN_COLS = NSA_W + 6 * KV_W + 3 * NSA_HEADS + 3 * MLSTM_W + 2 * MLSTM_HEADS
EPS = 1e-6
NEG_INF = -1e30
FORCE_SCORE = 1e9

kernel_name = 'hymba_nsa_mlstm_decode_step'


def rmsnorm(x, g):
    xf = x.astype(jnp.float32)
    y = xf * lax.rsqrt(jnp.mean(xf * xf, axis=-1, keepdims=True) + EPS)
    return (y * g).astype(x.dtype)


def alibi_slopes(n):
    return 2.0 ** (-8.0 * jnp.arange(1, n + 1, dtype=jnp.float32) / n)


def masked_softmax(s, valid):
    s = jnp.where(valid, s.astype(jnp.float32), NEG_INF)
    mx = jnp.max(s, axis=-1, keepdims=True)
    p = jnp.where(valid, jnp.exp(s - mx), 0.0)
    return p / jnp.maximum(jnp.sum(p, axis=-1, keepdims=True), 1e-30)


def split_cols(a):
    cuts = np.cumsum(IN_SPLITS)[:-1].tolist()
    return jnp.split(a, cuts, axis=-1)


def pad_rows(a, mult):
    extra = (-a.shape[1]) % mult
    return jnp.pad(a, ((0, 0), (0, extra), (0, 0), (0, 0)))


def gather_pages(pool, page_table):
    g = pool[page_table]
    return g.reshape((page_table.shape[0], -1) + pool.shape[2:])


def compress(kv, pe, w1, b1, w2):
    b, tk, hk, dh = kv.shape
    nc = tk // CMP_STRIDE
    half = CMP_STRIDE * dh
    ch = kv.reshape(b, nc, CMP_STRIDE, hk, dh).transpose(0, 1, 3, 2, 4).reshape(b, nc, hk, half)
    first = ch @ w1[:half]
    second = ch @ w1[half:]
    second = jnp.concatenate([second[:, 1:], jnp.zeros_like(second[:, :1])], axis=1)
    hid = first + second + (pe.reshape(-1) @ w1 + b1)
    return jax.nn.gelu(hid) @ w2


def cmp_to_sel(p):
    r = SEL_BLOCK // CMP_STRIDE
    pp = jnp.pad(p, [(0, 0)] * (p.ndim - 1) + [(1, r - 1)])
    pp = pp.reshape(p.shape[:-1] + (p.shape[-1] // r + 1, r))
    return pp[..., :-1, 1:].sum(-1) + 0.5 * (pp[..., :-1, 0] + pp[..., 1:, 0])


def nsa_attend(q, k_cmp, v_cmp, k_slc, v_slc, k_win, v_win, gates, offset, w_pre):
    b, tq, h, dh = q.shape
    qb = math.gcd(tq, Q_BLOCK)
    nb = tq // qb
    nc = k_cmp.shape[1]
    ns = k_slc.shape[1] // SEL_BLOCK
    n_top = min(SEL_TOPK, ns)
    slopes = alibi_slopes(h).reshape(1, NSA_KV_HEADS, NSA_GROUP, 1, 1)
    qg = (q * dh ** -0.5).reshape(b, nb, qb, NSA_KV_HEADS, NSA_GROUP, dh).transpose(1, 0, 3, 4, 2, 5)
    kc = k_cmp.transpose(0, 2, 1, 3)
    vc = v_cmp.transpose(0, 2, 1, 3)
    ks = k_slc.reshape(b, ns, SEL_BLOCK, NSA_KV_HEADS, dh).transpose(0, 3, 1, 2, 4)
    vs = v_slc.reshape(b, ns, SEL_BLOCK, NSA_KV_HEADS, dh).transpose(0, 3, 1, 2, 4)
    kw = k_win.transpose(0, 2, 1, 3)
    vw = v_win.transpose(0, 2, 1, 3)
    cmp_end = jnp.arange(nc) * CMP_STRIDE + (CMP_LEN - 1)
    blk_id = jnp.arange(ns)
    gather = jax.vmap(jax.vmap(lambda blocks, idx: blocks[idx]))

    def one_block(args):
        i, qi = args
        t = offset + i * qb + jnp.arange(qb)
        d_c = t[:, None] - cmp_end[None, :]
        s_c = jnp.einsum('bkgqd,bkcd->bkgqc', qi, kc) - slopes * d_c
        p_c = masked_softmax(s_c, d_c >= 0)
        o_c = jnp.einsum('bkgqc,bkcd->bkgqd', p_c, vc)
        imp = cmp_to_sel(p_c.sum(2))
        cur = t[:, None] // SEL_BLOCK
        forced = (blk_id == 0) | (blk_id == cur) | (blk_id == cur - 1)
        imp = jnp.where(forced, FORCE_SCORE, imp)
        imp = jnp.where(blk_id * SEL_BLOCK <= t[:, None], imp, NEG_INF)
        _, idx = lax.top_k(imp, n_top)
        k_sel = gather(ks, idx).reshape(b, NSA_KV_HEADS, qb, n_top * SEL_BLOCK, dh)
        v_sel = gather(vs, idx).reshape(b, NSA_KV_HEADS, qb, n_top * SEL_BLOCK, dh)
        pos_s = (idx[..., None] * SEL_BLOCK + jnp.arange(SEL_BLOCK)).reshape(b, NSA_KV_HEADS, 1, qb, n_top * SEL_BLOCK)
        d_s = t[:, None] - pos_s
        s_s = jnp.einsum('bkgqd,bkqsd->bkgqs', qi, k_sel) - slopes * d_s
        p_s = masked_softmax(s_s, d_s >= 0)
        o_s = jnp.einsum('bkgqs,bkqsd->bkgqd', p_s, v_sel)
        kwi = lax.dynamic_slice_in_dim(kw, i * qb, w_pre + qb, axis=2)
        vwi = lax.dynamic_slice_in_dim(vw, i * qb, w_pre + qb, axis=2)
        pos_w = offset - w_pre + i * qb + jnp.arange(w_pre + qb)
        d_w = t[:, None] - pos_w[None, :]
        valid_w = (d_w >= 0) & (d_w < WINDOW) & (pos_w[None, :] >= 0)
        s_w = jnp.einsum('bkgqd,bkwd->bkgqw', qi, kwi) - slopes * d_w
        p_w = masked_softmax(s_w, valid_w)
        o_w = jnp.einsum('bkgqw,bkwd->bkgqd', p_w, vwi)
        return jnp.stack([o_c, o_s, o_w], axis=-1)

    o = lax.map(one_block, (jnp.arange(nb), qg))
    o = o.transpose(1, 0, 4, 2, 3, 5, 6).reshape(b, tq, h, dh, 3)
    return jnp.sum(o * gates[:, :, :, None, :], axis=-1).astype(q.dtype)


def mlstm_chunked(q, k, v, i_pre, f_pre, C0, n0, m0):
    b, t, nh, dh = q.shape
    L = math.gcd(t, MLSTM_CHUNK)
    nck = t // L

    def to_chunks(a):
        return a.astype(jnp.float32).reshape((b, nck, L) + a.shape[2:]).swapaxes(0, 1)

    qc, kc, vc, ic = to_chunks(q), to_chunks(k), to_chunks(v), to_chunks(i_pre)
    lf = to_chunks(jax.nn.log_sigmoid(f_pre.astype(jnp.float32)))
    causal = jnp.tril(jnp.ones((L, L), dtype=bool))[None, :, :, None]

    def step(carry, xs):
        C, n, m = carry
        qi, ki, vi, ii, lfi = xs
        bcum = jnp.cumsum(lfi, axis=1)
        dmat = jnp.where(causal, bcum[:, :, None, :] - bcum[:, None, :, :] + ii[:, None, :, :], NEG_INF)
        inter = bcum + m[:, None, :]
        m_q = jnp.maximum(inter, jnp.max(dmat, axis=2))
        a = jnp.exp(dmat - m_q[:, :, None, :]) * jnp.einsum('blhd,bshd->blsh', qi, ki)
        w_inter = jnp.exp(inter - m_q)
        num = jnp.einsum('blsh,bshd->blhd', a, vi) + w_inter[..., None] * jnp.einsum('bhed,blhd->blhe', C, qi)
        den = jnp.sum(a, axis=2) + w_inter * jnp.einsum('bhd,blhd->blh', n, qi)
        h_out = num / jnp.maximum(jnp.abs(den), jnp.exp(-m_q))[..., None]
        btot = bcum[:, -1]
        dec = btot[:, None, :] - bcum + ii
        m_new = jnp.maximum(btot + m, jnp.max(dec, axis=1))
        w_s = jnp.exp(dec - m_new[:, None, :])
        w_c = jnp.exp(btot + m - m_new)
        C_new = w_c[..., None, None] * C + jnp.einsum('bsh,bshe,bshd->bhed', w_s, vi, ki)
        n_new = w_c[..., None] * n + jnp.einsum('bsh,bshd->bhd', w_s, ki)
        return (C_new, n_new, m_new), h_out

    init = (C0.astype(jnp.float32), n0.astype(jnp.float32), m0.astype(jnp.float32))
    (C, n, m), h = lax.scan(step, init, (qc, kc, vc, ic, lf))
    return h.swapaxes(0, 1).reshape(b, t, nh, dh), C, n, m


def decoder_layer(x, past_kc, past_vc, past_ks, past_vs, win_k, win_v, conv_buf, C0, n0, m0, mem_k, mem_v, w):
    b, t, _ = x.shape
    offset = past_kc.shape[1]
    w_pre = win_k.shape[1]
    h = rmsnorm(x, w['g_mix'])
    (q, kc_new, vc_new, ks_new, vs_new, kw_new, vw_new, g_nsa, xm, vm, om, i_pre, f_pre) = split_cols(h @ w['w_in'])
    kvs = (b, t, NSA_KV_HEADS, NSA_DH)
    q = q.reshape(b, t, NSA_HEADS, NSA_DH)
    kc_new, vc_new, ks_new, vs_new, kw_new, vw_new = [a.reshape(kvs) for a in (kc_new, vc_new, ks_new, vs_new, kw_new, vw_new)]
    kc_all = pad_rows(jnp.concatenate([past_kc.astype(x.dtype), kc_new], axis=1), SEL_BLOCK)
    vc_all = pad_rows(jnp.concatenate([past_vc.astype(x.dtype), vc_new], axis=1), SEL_BLOCK)
    ks_all = pad_rows(jnp.concatenate([past_ks.astype(x.dtype), ks_new], axis=1), SEL_BLOCK)
    vs_all = pad_rows(jnp.concatenate([past_vs.astype(x.dtype), vs_new], axis=1), SEL_BLOCK)
    k_cmp = compress(kc_all, *w['cmp_k'])
    v_cmp = compress(vc_all, *w['cmp_v'])
    kw_all = jnp.concatenate([win_k.astype(x.dtype), kw_new], axis=1)
    vw_all = jnp.concatenate([win_v.astype(x.dtype), vw_new], axis=1)
    gates = jax.nn.sigmoid(g_nsa + w['b_gate']).reshape(b, t, NSA_HEADS, 3)
    o_nsa = nsa_attend(q, k_cmp, v_cmp, ks_all, vs_all, kw_all, vw_all, gates, offset, w_pre)
    o_nsa = rmsnorm(o_nsa, w['g_head_nsa']).reshape(b, t, NSA_W)
    xm_all = jnp.concatenate([conv_buf.astype(x.dtype), xm], axis=1)
    xconv = sum(w['conv_w'][j] * xm_all[:, j:j + t] for j in range(CONV_W)) + w['conv_b']
    xconv = jax.nn.silu(xconv).reshape(b, t, MLSTM_HEADS, MLSTM_DH)
    qm = jnp.einsum('bthd,hde->bthe', xconv, w['w_qm'])
    km = jnp.einsum('bthd,hde->bthe', xconv, w['w_km']) * MLSTM_DH ** -0.5
    vm = vm.reshape(b, t, MLSTM_HEADS, MLSTM_DH)
    hm, C, n, m = mlstm_chunked(qm, km, vm, i_pre + w['b_i'], f_pre + w['b_f'], C0, n0, m0)
    hm = rmsnorm(hm, w['g_head_m']) * jax.nn.sigmoid(om.astype(jnp.float32)).reshape(b, t, MLSTM_HEADS, MLSTM_DH)
    mix = jnp.concatenate([o_nsa, hm.reshape(b, t, MLSTM_W).astype(x.dtype)], axis=-1) @ w['w_out']
    x = x + mix
    h = rmsnorm(x, w['g_xa'])
    qx = (h @ w['w_xq']).reshape(b, t, XA_HEADS, XA_DH) * XA_DH ** -0.5
    p = jax.nn.softmax(jnp.einsum('bthd,bmhd->bhtm', qx, mem_k).astype(jnp.float32), axis=-1)
    ox = jnp.einsum('bhtm,bmhd->bthd', p, mem_v).reshape(b, t, D_MODEL).astype(x.dtype)
    x = x + ox @ w['w_xo']
    h = rmsnorm(x, w['g_ffn'])
    x = x + (jax.nn.silu(h @ w['w_gate']) * (h @ w['w_up'])) @ w['w_down']
    keep = min(WINDOW, offset + t)
    return x, (kc_new, vc_new, ks_new, vs_new, kw_all[:, kw_all.shape[1] - keep:], vw_all[:, vw_all.shape[1] - keep:], C, n, m, xm_all[:, -(CONV_W - 1):])


def setup_inputs(seed: int = 0) -> dict:
    key = jax.random.key(seed)
    keys = iter(jax.random.split(key, 96))

    def nrm(shape, scale):
        return jax.random.normal(next(keys), shape, jnp.float32) * scale

    def gain(shape):
        return 1.0 + nrm(shape, 0.05)

    L = DEPTH
    n_pages = PAST_LEN // PAGE_SIZE
    n_pool = (DEC_BATCH * n_pages * 5) // 4
    win_buf = min(WINDOW, PAST_LEN)
    pool_shape = (L, n_pool, PAGE_SIZE, NSA_KV_HEADS, NSA_DH)
    page_table = jax.random.permutation(next(keys), n_pool)[: DEC_BATCH * n_pages].reshape(DEC_BATCH, n_pages).astype(jnp.int32)
    return {
        'x_prompt': nrm((BATCH, SEQ, D_MODEL), 1.0),
        'x_sample': nrm((DEC_BATCH, DEC_SEQ, D_MODEL), 1.0),
        'cache_k_cmp': nrm(pool_shape, 1.0),
        'cache_v_cmp': nrm(pool_shape, 1.0),
        'cache_k_slc': nrm(pool_shape, 1.0),
        'cache_v_slc': nrm(pool_shape, 1.0),
        'state_k_win': nrm((L, DEC_BATCH, win_buf, NSA_KV_HEADS, NSA_DH), 1.0),
        'state_v_win': nrm((L, DEC_BATCH, win_buf, NSA_KV_HEADS, NSA_DH), 1.0),
        'state_conv': nrm((L, DEC_BATCH, CONV_W - 1, MLSTM_W), 1.0),
        'state_C': nrm((L, DEC_BATCH, MLSTM_HEADS, MLSTM_DH, MLSTM_DH), 0.3),
        'state_n': jnp.abs(nrm((L, DEC_BATCH, MLSTM_HEADS, MLSTM_DH), 1.0)),
        'state_m': nrm((L, DEC_BATCH, MLSTM_HEADS), 0.5),
        'cache_mem_k': nrm((L, DEC_BATCH, MEM_TOKENS, XA_HEADS, XA_DH), 1.0),
        'cache_mem_v': nrm((L, DEC_BATCH, MEM_TOKENS, XA_HEADS, XA_DH), 1.0),
        'page_table': page_table,
        'mem_prompt': nrm((BATCH, MEM_TOKENS, D_MODEL), 1.0),
        'g_mix': gain((L, D_MODEL)),
        'w_in': nrm((L, D_MODEL, IN_COLS), D_MODEL ** -0.5),
        'b_gate': nrm((L, 3 * NSA_HEADS), 0.1),
        'cmp_pe_k': nrm((L, CMP_LEN, NSA_DH), 0.1),
        'cmp_w1_k': nrm((L, CMP_LEN * NSA_DH, CMP_HIDDEN), (CMP_LEN * NSA_DH) ** -0.5),
        'cmp_b1_k': nrm((L, CMP_HIDDEN), 0.02),
        'cmp_w2_k': nrm((L, CMP_HIDDEN, NSA_DH), CMP_HIDDEN ** -0.5),
        'cmp_pe_v': nrm((L, CMP_LEN, NSA_DH), 0.1),
        'cmp_w1_v': nrm((L, CMP_LEN * NSA_DH, CMP_HIDDEN), (CMP_LEN * NSA_DH) ** -0.5),
        'cmp_b1_v': nrm((L, CMP_HIDDEN), 0.02),
        'cmp_w2_v': nrm((L, CMP_HIDDEN, NSA_DH), CMP_HIDDEN ** -0.5),
        'g_head_nsa': gain((L, NSA_HEADS, NSA_DH)),
        'conv_w': nrm((L, CONV_W, MLSTM_W), CONV_W ** -0.5),
        'conv_b': nrm((L, MLSTM_W), 0.02),
        'w_qm': nrm((L, MLSTM_HEADS, MLSTM_DH, MLSTM_DH), MLSTM_DH ** -0.5),
        'w_km': nrm((L, MLSTM_HEADS, MLSTM_DH, MLSTM_DH), MLSTM_DH ** -0.5),
        'b_i': nrm((L, MLSTM_HEADS), 0.1),
        'b_f': jnp.linspace(3.0, 6.0, MLSTM_HEADS)[None, :] + nrm((L, MLSTM_HEADS), 0.1),
        'g_head_m': gain((L, MLSTM_HEADS, MLSTM_DH)),
        'w_out': nrm((L, D_MODEL, D_MODEL), D_MODEL ** -0.5),
        'g_xa': gain((L, D_MODEL)),
        'g_mem': gain((L, D_MODEL)),
        'w_xq': nrm((L, D_MODEL, D_MODEL), D_MODEL ** -0.5),
        'w_xk': nrm((L, D_MODEL, D_MODEL), D_MODEL ** -0.5),
        'w_xv': nrm((L, D_MODEL, D_MODEL), D_MODEL ** -0.5),
        'w_xo': nrm((L, D_MODEL, D_MODEL), D_MODEL ** -0.5),
        'g_ffn': gain((L, D_MODEL)),
        'w_gate': nrm((L, D_MODEL, D_FF), D_MODEL ** -0.5),
        'w_up': nrm((L, D_MODEL, D_FF), D_MODEL ** -0.5),
        'w_down': nrm((L, D_FF, D_MODEL), D_FF ** -0.5),
        'g_final': gain((D_MODEL,)),
    }


def reference(x_prompt, x_sample, cache_k_cmp, cache_v_cmp, cache_k_slc, cache_v_slc, state_k_win, state_v_win,
              state_conv, state_C, state_n, state_m, cache_mem_k, cache_mem_v, page_table, mem_prompt,
              g_mix, w_in, b_gate, cmp_pe_k, cmp_w1_k, cmp_b1_k, cmp_w2_k, cmp_pe_v, cmp_w1_v, cmp_b1_v, cmp_w2_v,
              g_head_nsa, conv_w, conv_b, w_qm, w_km, b_i, b_f, g_head_m, w_out, g_xa, g_mem, w_xq, w_xk, w_xv, w_xo,
              g_ffn, w_gate, w_up, w_down, g_final):
    xp, xs = x_prompt, x_sample
    b = xp.shape[0]
    outs_p, outs_s = [], []
    for l in range(DEPTH):
        w = dict(g_mix=g_mix[l], w_in=w_in[l], b_gate=b_gate[l],
                 cmp_k=(cmp_pe_k[l], cmp_w1_k[l], cmp_b1_k[l], cmp_w2_k[l]),
                 cmp_v=(cmp_pe_v[l], cmp_w1_v[l], cmp_b1_v[l], cmp_w2_v[l]),
                 g_head_nsa=g_head_nsa[l], conv_w=conv_w[l], conv_b=conv_b[l], w_qm=w_qm[l], w_km=w_km[l],
                 b_i=b_i[l], b_f=b_f[l], g_head_m=g_head_m[l], w_out=w_out[l], g_xa=g_xa[l], w_xq=w_xq[l],
                 w_xo=w_xo[l], g_ffn=g_ffn[l], w_gate=w_gate[l], w_up=w_up[l], w_down=w_down[l])
        z_past = jnp.zeros((b, 0, NSA_KV_HEADS, NSA_DH), xp.dtype)
        z_win = jnp.zeros((b, WINDOW, NSA_KV_HEADS, NSA_DH), xp.dtype)
        z_conv = jnp.zeros((b, CONV_W - 1, MLSTM_W), xp.dtype)
        z_C = jnp.zeros((b, MLSTM_HEADS, MLSTM_DH, MLSTM_DH), jnp.float32)
        z_n = jnp.zeros((b, MLSTM_HEADS, MLSTM_DH), jnp.float32)
        z_m = jnp.zeros((b, MLSTM_HEADS), jnp.float32)
        mem_n = rmsnorm(mem_prompt, g_mem[l])
        mk = (mem_n @ w_xk[l]).reshape(b, MEM_TOKENS, XA_HEADS, XA_DH)
        mv = (mem_n @ w_xv[l]).reshape(b, MEM_TOKENS, XA_HEADS, XA_DH)
        xp, st_p = decoder_layer(xp, z_past, z_past, z_past, z_past, z_win, z_win, z_conv, z_C, z_n, z_m, mk, mv, w)
        outs_p.append(st_p + (mk, mv))
        xs, st_s = decoder_layer(xs, gather_pages(cache_k_cmp[l], page_table), gather_pages(cache_v_cmp[l], page_table),
                                 gather_pages(cache_k_slc[l], page_table), gather_pages(cache_v_slc[l], page_table),
                                 state_k_win[l], state_v_win[l], state_conv[l], state_C[l], state_n[l], state_m[l],
                                 cache_mem_k[l], cache_mem_v[l], w)
        outs_s.append(st_s)
    y_prompt = rmsnorm(xp, g_final)
    y_sample = rmsnorm(xs, g_final)
    (p_k_cmp, p_v_cmp, p_k_slc, p_v_slc, p_k_win, p_v_win, p_C, p_n, p_m, p_conv, p_mem_k, p_mem_v) = [jnp.stack(a) for a in zip(*outs_p)]
    (s_k_cmp, s_v_cmp, s_k_slc, s_v_slc, s_k_win, s_v_win, s_C, s_n, s_m, s_conv) = [jnp.stack(a) for a in zip(*outs_s)]
    return (y_prompt, y_sample, p_k_cmp, p_v_cmp, p_k_slc, p_v_slc, p_k_win, p_v_win, p_C, p_n, p_m, p_conv, p_mem_k, p_mem_v,
            s_k_cmp, s_v_cmp, s_k_slc, s_v_slc, s_k_win, s_v_win, s_C, s_n, s_m, s_conv)
```

```python
import functools

import numpy as np
import jax
import jax.numpy as jnp
from jax import lax
from jax.experimental import pallas as pl
from jax.experimental.pallas import tpu as pltpu

F32 = jnp.float32
BF16 = jnp.bfloat16
HIGHEST = lax.Precision.HIGHEST

EPS = 1e-6
NEG_INF = -1e30
FORCE_SCORE = 1e9
PAD_SCORE = -2e38
TAKEN_SCORE = -3e38

LANES = 128
NSA_HEADS = 8
NSA_KV_HEADS = 2
NSA_GROUP = 4
NSA_DH = 64
CMP_STRIDE = 16
SEL_BLOCK = 64
SEL_TOPK = 16
WINDOW = 512
Q_BLOCK = 64
PAGE = 128
CMP_PER_PAGE = PAGE // CMP_STRIDE
MLSTM_HEADS = 4
MLSTM_DH = 128
XA_HEADS = 4
XA_DH = 256
ALIBI = tuple(2.0 ** (-(h + 1)) for h in range(NSA_HEADS))

VMEM_LIMIT = 56 * 1024 * 1024


def _params(sem):
    return pltpu.CompilerParams(dimension_semantics=sem, vmem_limit_bytes=VMEM_LIMIT)


def _mm(a, b):
    return jnp.dot(a.astype(BF16), b.astype(BF16), preferred_element_type=F32)


def _mm_nt(a, b):
    return lax.dot_general(a.astype(BF16), b.astype(BF16), (((1,), (1,)), ((), ())),
                           preferred_element_type=F32)


def _mm_tn(a, b):
    return lax.dot_general(a.astype(BF16), b.astype(BF16), (((0,), (0,)), ((), ())),
                           preferred_element_type=F32)


def _mm_f32(a, b):
    return jnp.dot(a, b, precision=HIGHEST, preferred_element_type=F32)


def _rms(x, g):
    return x * lax.rsqrt(jnp.mean(x * x, axis=-1, keepdims=True) + EPS) * g


def _sigmoid(x):
    return 1.0 / (1.0 + jnp.exp(-x))


def _log_sigmoid(x):
    return jnp.minimum(x, 0.0) - jnp.log(1.0 + jnp.exp(-jnp.abs(x)))


def _gelu_tanh(x):
    return 0.5 * x * (1.0 + jnp.tanh(0.7978845608028654 * (x + 0.044715 * x * x * x)))


def _topk_mask(imp, k):
    col = lax.broadcasted_iota(jnp.int32, imp.shape, 1)
    sel = jnp.zeros(imp.shape, F32)
    work = imp
    for _ in range(k):
        m = jnp.max(work, axis=-1, keepdims=True)
        idx = jnp.min(jnp.where(work == m, col, jnp.int32(2 ** 30)), axis=-1, keepdims=True)
        hit = col == idx
        sel = jnp.where(hit, 1.0, sel)
        work = jnp.where(hit, TAKEN_SCORE, work)
    return sel


def _softmax_parts(s, valid):
    s = jnp.where(valid, s, NEG_INF)
    mx = jnp.max(s, axis=-1, keepdims=True)
    p = jnp.where(valid, jnp.exp(s - mx), 0.0)
    return p, mx, jnp.sum(p, axis=-1, keepdims=True)


def _q_rows(q_ref, hk, nq):
    lane = lax.broadcasted_iota(jnp.int32, (1, LANES), 1)
    own = (lane >= hk * NSA_DH) & (lane < (hk + 1) * NSA_DH)
    parts = []
    for g in range(NSA_GROUP):
        h = hk * NSA_GROUP + g
        blk = q_ref[0, :, (h // 2) * LANES:(h // 2 + 1) * LANES]
        if (h % 2) != hk:
            blk = pltpu.roll(blk, NSA_DH, 1)
        parts.append(blk)
    qh = jnp.concatenate(parts, axis=0)
    return jnp.where(own, qh * (NSA_DH ** -0.5), 0.0).astype(BF16), own


def _row_consts(hk, nq, t0):
    rows = NSA_GROUP * nq
    row = lax.broadcasted_iota(jnp.int32, (rows, 1), 0)
    slope = jnp.full((rows, 1), ALIBI[hk * NSA_GROUP + NSA_GROUP - 1], F32)
    for g in range(NSA_GROUP - 2, -1, -1):
        slope = jnp.where(row < (g + 1) * nq, ALIBI[hk * NSA_GROUP + g], slope)
    tq = t0 + (row & (nq - 1))
    return slope, tq


def _cmp_branch(qh, slope, tq, kc_ref, vc_ref, a_ref, nq):
    ncp = kc_ref.shape[1]
    s = _mm_nt(qh, kc_ref[0])
    cend = lax.broadcasted_iota(jnp.int32, (1, ncp), 1) * CMP_STRIDE + (2 * CMP_STRIDE - 1)
    d = tq - cend
    valid = d >= 0
    p, _, l = _softmax_parts(s - slope * d.astype(F32), valid)
    p = p / jnp.maximum(l, 1e-30)
    o_c = _mm(p, vc_ref[0])
    psum = p[0:nq]
    for g in range(1, NSA_GROUP):
        psum = psum + p[g * nq:(g + 1) * nq]
    return o_c, _mm_f32(psum, a_ref[...])


def _select(imp, tq_q, ns, n_top):
    nsp = imp.shape[1]
    blk = lax.broadcasted_iota(jnp.int32, (1, nsp), 1)
    cur = tq_q >> 6
    forced = (blk == 0) | (blk == cur) | (blk == cur - 1)
    imp = jnp.where(forced, FORCE_SCORE, imp)
    imp = jnp.where(blk * SEL_BLOCK <= tq_q, imp, NEG_INF)
    imp = jnp.where(blk < ns, imp, PAD_SCORE)
    return _topk_mask(imp, n_top)


def _flash_step(qh, kch, vch, pos, tq, slope, sel_rows, carry):
    m, l, acc = carry
    nsp = sel_rows.shape[1]
    s = _mm_nt(qh, kch)
    d = tq - pos
    blk = lax.broadcasted_iota(jnp.int32, (nsp, 1), 0)
    expand = (blk == (pos >> 6)).astype(BF16)
    chosen = jnp.dot(sel_rows, expand, preferred_element_type=F32) > 0.5
    valid = (d >= 0) & chosen
    s = jnp.where(valid, s - slope * d.astype(F32), NEG_INF)
    m_new = jnp.maximum(m, jnp.max(s, axis=-1, keepdims=True))
    alpha = jnp.exp(m - m_new)
    p = jnp.where(valid, jnp.exp(s - m_new), 0.0)
    l = alpha * l + jnp.sum(p, axis=-1, keepdims=True)
    acc = alpha * acc + _mm(p, vch)
    return m_new, l, acc


def _combine_heads(o_c, o_s, o_w, gates, gh_ref, own, hk, nq, out_pairs):
    for g in range(NSA_GROUP):
        h = hk * NSA_GROUP + g
        r = slice(g * nq, (g + 1) * nq)
        og = (gates[:, 3 * h:3 * h + 1] * o_c[r] + gates[:, 3 * h + 1:3 * h + 2] * o_s[r]
              + gates[:, 3 * h + 2:3 * h + 3] * o_w[r])
        og = jnp.where(own, og, 0.0)
        ms = jnp.sum(og * og, axis=-1, keepdims=True) * (1.0 / NSA_DH)
        y = og * lax.rsqrt(ms + EPS) * gh_ref[h:h + 1, :]
        if (h % 2) != hk:
            y = pltpu.roll(y, NSA_DH, 1)
        out_pairs[h // 2] = y if out_pairs[h // 2] is None else out_pairs[h // 2] + y


def _norm_proj_kernel(x_ref, g_ref, *refs, n):
    h = _rms(x_ref[...], g_ref[...]).astype(BF16)
    for w_ref, o_ref in zip(refs[:n], refs[n:]):
        o_ref[...] = jnp.dot(h, w_ref[...], preferred_element_type=F32)


def _norm_proj(x, g, ws, tm):
    m, d = x.shape
    n = len(ws)
    return pl.pallas_call(
        functools.partial(_norm_proj_kernel, n=n),
        grid=(m // tm,),
        in_specs=[pl.BlockSpec((tm, d), lambda i: (i, 0)), pl.BlockSpec((1, d), lambda i: (0, 0))]
        + [pl.BlockSpec(w.shape, lambda i: (0, 0)) for w in ws],
        out_specs=[pl.BlockSpec((tm, w.shape[1]), lambda i: (i, 0)) for w in ws],
        out_shape=[jax.ShapeDtypeStruct((m, w.shape[1]), F32) for w in ws],
        compiler_params=_params(("parallel",)),
        name="norm_proj",
    )(x, g.reshape(1, d), *ws)


def _compress_kernel(tbl_ref, pool_ref, last_ref, wr_ref, pe1_ref, pe2_ref, b1_ref, w2_ref, o_ref,
                     buf, acc, cst, sem, *, ch, nchunks):
    b = pl.program_id(0)
    j = pl.program_id(1)
    nblk = ch * CMP_PER_PAGE
    half = wr_ref.shape[2] // 2

    def page_copy(k, slot):
        return pltpu.make_async_copy(pool_ref.at[tbl_ref[b, k]], buf.at[pl.ds(slot * PAGE, PAGE)],
                                     sem.at[slot])

    for s in range(ch):
        page_copy(j * ch + s, s).start()

    @pl.when(j < nchunks - 1)
    def _():
        page_copy(jnp.minimum((j + 1) * ch, nchunks * ch - 1), ch).start()

    @pl.when(j == nchunks - 1)
    def _():
        buf[pl.ds(ch * PAGE, PAGE), :] = last_ref[0]

    @pl.when((b == 0) & (j == 0))
    def _():
        c = jnp.zeros((8, half), F32)
        for r in range(CMP_STRIDE):
            c = c + _mm(jnp.broadcast_to(pe1_ref[r:r + 1, :], (8, LANES)), wr_ref[r, :, 0:half])
            c = c + _mm(jnp.broadcast_to(pe2_ref[r:r + 1, :], (8, LANES)), wr_ref[r, :, half:2 * half])
        cst[...] = c + b1_ref[...]

    for s in range(ch):
        page_copy(j * ch + s, s).wait()

    @pl.when(j < nchunks - 1)
    def _():
        page_copy(jnp.minimum((j + 1) * ch, nchunks * ch - 1), ch).wait()

    for r in range(CMP_STRIDE):
        xr = buf[pl.ds(r, nblk + 8, stride=CMP_STRIDE), :]
        part = _mm(xr, wr_ref[r])
        if r == 0:
            acc[...] = part
        else:
            acc[...] += part
    hid = acc[0:nblk, 0:half] + acc[pl.ds(1, nblk), half:2 * half] + cst[0:1, :]
    o_ref[0] = _mm(_gelu_tanh(hid), w2_ref[...])


def _compress(table, pool, last_next, pe, w1, b1, w2, ch):
    nb, npg = table.shape
    nchunks = npg // ch
    hid = w1.shape[1]
    dh = NSA_DH
    wa = w1[:CMP_STRIDE * dh].reshape(CMP_STRIDE, dh, hid)
    wb = w1[CMP_STRIDE * dh:].reshape(CMP_STRIDE, dh, hid)
    z = jnp.zeros_like(wa)
    wr = jnp.concatenate([jnp.concatenate([wa, z, wb, z], axis=2),
                          jnp.concatenate([z, wa, z, wb], axis=2)], axis=1).astype(BF16)
    pe1 = jnp.concatenate([pe[:CMP_STRIDE], pe[:CMP_STRIDE]], axis=1)
    pe2 = jnp.concatenate([pe[CMP_STRIDE:], pe[CMP_STRIDE:]], axis=1)
    b1t = jnp.concatenate([b1, b1]).reshape(1, 2 * hid)
    zz = jnp.zeros_like(w2)
    w2bd = jnp.concatenate([jnp.concatenate([w2, zz], axis=1),
                            jnp.concatenate([zz, w2], axis=1)], axis=0).astype(BF16)
    nblk = ch * CMP_PER_PAGE
    grid_spec = pltpu.PrefetchScalarGridSpec(
        num_scalar_prefetch=1,
        grid=(nb, nchunks),
        in_specs=[pl.BlockSpec(memory_space=pl.ANY),
                  pl.BlockSpec((1, PAGE, LANES), lambda b, j, t: (b, 0, 0)),
                  pl.BlockSpec(wr.shape, lambda b, j, t: (0, 0, 0)),
                  pl.BlockSpec(pe1.shape, lambda b, j, t: (0, 0)),
                  pl.BlockSpec(pe2.shape, lambda b, j, t: (0, 0)),
                  pl.BlockSpec(b1t.shape, lambda b, j, t: (0, 0)),
                  pl.BlockSpec(w2bd.shape, lambda b, j, t: (0, 0))],
        out_specs=pl.BlockSpec((1, nblk, LANES), lambda b, j, t: (b, j, 0)),
        scratch_shapes=[pltpu.VMEM(((ch + 1) * PAGE, LANES), F32),
                        pltpu.VMEM((nblk + 8, 4 * hid), F32),
                        pltpu.VMEM((8, 2 * hid), F32),
                        pltpu.SemaphoreType.DMA((ch + 1,))],
    )
    return pl.pallas_call(
        functools.partial(_compress_kernel, ch=ch, nchunks=nchunks),
        grid_spec=grid_spec,
        out_shape=jax.ShapeDtypeStruct((nb, npg * CMP_PER_PAGE, LANES), F32),
        compiler_params=_params(("arbitrary", "arbitrary")),
        name="compress",
    )(table, pool, last_next, wr, pe1, pe2, b1t, w2bd)


def _sel_matrix(nc_rows, nc_valid, ns_cols):
    a = np.zeros((nc_rows, ns_cols), np.float32)
    for j in range(nc_valid // 4):
        for c, wgt in ((4 * j - 1, 0.5), (4 * j, 1.0), (4 * j + 1, 1.0), (4 * j + 2, 1.0), (4 * j + 3, 0.5)):
            if 0 <= c < nc_valid:
                a[c, j] += wgt
    return jnp.asarray(a)


def _nsa_prompt_kernel(qi_ref, ci_ref, q_ref, sm_ref, kc_ref, vc_ref, ks_ref, vs_ref, kw_ref, vw_ref, a_ref, bg_ref,
                       gh_ref, o_ref, q_s, sel_s, oc_s, ow_s, m_s, l_s, acc_s, *, nq, ns, n_top, kchunk, wkeys):
    step = pl.program_id(1)
    i = qi_ref[step]
    c = ci_ref[step]
    t0 = i * nq
    rows = NSA_GROUP * nq
    last = (t0 + nq + kchunk - 1) // kchunk - 1

    @pl.when(c == 0)
    def _():
        for hk in range(NSA_KV_HEADS):
            qh, _ = _q_rows(q_ref, hk, nq)
            slope, tq = _row_consts(hk, nq, t0)
            o_c, imp = _cmp_branch(qh, slope, tq, kc_ref, vc_ref, a_ref, nq)
            sel = _select(imp, tq[0:nq], ns, n_top)
            q_s[hk] = qh
            sel_s[hk] = jnp.concatenate([sel] * NSA_GROUP, axis=0).astype(BF16)
            oc_s[hk] = o_c
            w0 = pl.multiple_of(jnp.maximum(t0 + nq - wkeys, 0), LANES)
            pos = w0 + lax.broadcasted_iota(jnp.int32, (1, wkeys), 1)
            d = tq - pos
            s = _mm_nt(qh, kw_ref[0, pl.ds(w0, wkeys), :])
            p, _, l_w = _softmax_parts(s - slope * d.astype(F32), (d >= 0) & (d < WINDOW))
            ow_s[hk] = _mm(p, vw_ref[0, pl.ds(w0, wkeys), :]) / jnp.maximum(l_w, 1e-30)
            m_s[hk] = jnp.full((rows, LANES), NEG_INF, F32)
            l_s[hk] = jnp.zeros((rows, LANES), F32)
            acc_s[hk] = jnp.zeros((rows, LANES), F32)

    k0 = pl.multiple_of(c * kchunk, kchunk)
    pos = k0 + lax.broadcasted_iota(jnp.int32, (1, kchunk), 1)
    for hk in range(NSA_KV_HEADS):
        slope, tq = _row_consts(hk, nq, t0)
        m_n, l_n, acc_n = _flash_step(q_s[hk], ks_ref[0, pl.ds(k0, kchunk), :], vs_ref[0, pl.ds(k0, kchunk), :],
                                      pos, tq, slope, sel_s[hk], (m_s[hk][:, 0:1], l_s[hk][:, 0:1], acc_s[hk]))
        m_s[hk] = jnp.broadcast_to(m_n, (rows, LANES))
        l_s[hk] = jnp.broadcast_to(l_n, (rows, LANES))
        acc_s[hk] = acc_n

    @pl.when(c == last)
    def _():
        gates = _sigmoid(sm_ref[0] + bg_ref[...])
        lane = lax.broadcasted_iota(jnp.int32, (1, LANES), 1)
        out_pairs = [None] * (NSA_HEADS // 2)
        for hk in range(NSA_KV_HEADS):
            own = (lane >= hk * NSA_DH) & (lane < (hk + 1) * NSA_DH)
            o_s = acc_s[hk] / jnp.maximum(l_s[hk][:, 0:1], 1e-30)
            _combine_heads(oc_s[hk], o_s, ow_s[hk], gates, gh_ref, own, hk, nq, out_pairs)
        for k, y in enumerate(out_pairs):
            o_ref[0, :, k * LANES:(k + 1) * LANES] = y


def _nsa_prompt(q, sm, k_cmp, v_cmp, ks, vs, kw, vw, bg, gh, nq):
    b, t, _ = q.shape
    nc = k_cmp.shape[1]
    ns = t // SEL_BLOCK
    nsp = -(-ns // LANES) * LANES
    a = _sel_matrix(nc, nc, nsp)
    kchunk = min(512, t)
    wkeys = min(-(-(WINDOW + nq) // LANES) * LANES, t)
    pairs = [(i, c) for i in range(t // nq) for c in range(-(-((i + 1) * nq) // kchunk))]
    qi = jnp.asarray(np.array([p[0] for p in pairs], np.int32))
    ci = jnp.asarray(np.array([p[1] for p in pairs], np.int32))
    rows = NSA_GROUP * nq
    full = lambda n: pl.BlockSpec((1, n, LANES), lambda bi, s, qi, ci: (bi, 0, 0))
    const = lambda shape: pl.BlockSpec(shape, lambda bi, s, qi, ci: (0,) * len(shape))
    tok = lambda w: pl.BlockSpec((1, nq, w), lambda bi, s, qi, ci: (bi, qi[s], 0))
    grid_spec = pltpu.PrefetchScalarGridSpec(
        num_scalar_prefetch=2,
        grid=(b, len(pairs)),
        in_specs=[tok(4 * LANES), tok(LANES), full(nc), full(nc), full(t), full(t), full(t), full(t),
                  const(a.shape), const((1, LANES)), const((NSA_HEADS, LANES))],
        out_specs=tok(4 * LANES),
        scratch_shapes=[pltpu.VMEM((NSA_KV_HEADS, rows, LANES), BF16),
                        pltpu.VMEM((NSA_KV_HEADS, rows, nsp), BF16)]
        + [pltpu.VMEM((NSA_KV_HEADS, rows, LANES), F32)] * 5,
    )
    return pl.pallas_call(
        functools.partial(_nsa_prompt_kernel, nq=nq, ns=ns, n_top=min(SEL_TOPK, ns), kchunk=kchunk, wkeys=wkeys),
        grid_spec=grid_spec,
        out_shape=jax.ShapeDtypeStruct((b, t, 4 * LANES), F32),
        compiler_params=_params(("arbitrary", "arbitrary")),
        name="nsa_prompt",
    )(qi, ci, q, sm, k_cmp, v_cmp, ks, vs, kw, vw, a, bg, gh)


def _nsa_sample_kernel(pt_ref, q_ref, sm_ref, kc_ref, vc_ref, ks_pool, vs_pool, ksn_ref, vsn_ref,
                       wk_ref, wv_ref, kwn_ref, vwn_ref, a_ref, bg_ref, gh_ref, o_ref,
                       kbuf, vbuf, sem, q_s, sel_s, oc_s, ow_s, m_s, l_s, acc_s,
                       *, ch, nchunks, offset, ns, n_top, kchunk):
    b = pl.program_id(0)
    j = pl.program_id(1)
    nq = q_ref.shape[1]
    rows = NSA_GROUP * nq
    w_pre = wk_ref.shape[1]

    def page_copies(s):
        page = pt_ref[b, j * ch + s]
        dst = pl.ds(s * PAGE, PAGE)
        return (pltpu.make_async_copy(ks_pool.at[page], kbuf.at[dst], sem.at[0]),
                pltpu.make_async_copy(vs_pool.at[page], vbuf.at[dst], sem.at[1]))

    for s in range(ch):
        ck, cv = page_copies(s)
        ck.start()
        cv.start()

    @pl.when(j == 0)
    def _():
        for hk in range(NSA_KV_HEADS):
            qh, _ = _q_rows(q_ref, hk, nq)
            slope, tq = _row_consts(hk, nq, offset)
            o_c, imp = _cmp_branch(qh, slope, tq, kc_ref, vc_ref, a_ref, nq)
            sel = _select(imp, tq[0:nq], ns, n_top)
            q_s[hk] = qh
            sel_s[hk] = jnp.concatenate([sel] * NSA_GROUP, axis=0).astype(BF16)
            oc_s[hk] = o_c
            pos1 = offset - w_pre + lax.broadcasted_iota(jnp.int32, (1, w_pre), 1)
            d1 = tq - pos1
            pos2 = offset + lax.broadcasted_iota(jnp.int32, (1, kwn_ref.shape[1]), 1)
            d2 = tq - pos2
            v1 = (d1 >= 0) & (d1 < WINDOW)
            v2 = (d2 >= 0) & (d2 < WINDOW)
            s1 = jnp.where(v1, _mm_nt(qh, wk_ref[0]) - slope * d1.astype(F32), NEG_INF)
            s2 = jnp.where(v2, _mm_nt(qh, kwn_ref[0]) - slope * d2.astype(F32), NEG_INF)
            mx = jnp.maximum(jnp.max(s1, axis=-1, keepdims=True), jnp.max(s2, axis=-1, keepdims=True))
            p1 = jnp.where(v1, jnp.exp(s1 - mx), 0.0)
            p2 = jnp.where(v2, jnp.exp(s2 - mx), 0.0)
            l_w = jnp.sum(p1, axis=-1, keepdims=True) + jnp.sum(p2, axis=-1, keepdims=True)
            ow_s[hk] = (_mm(p1, wv_ref[0]) + _mm(p2, vwn_ref[0])) / jnp.maximum(l_w, 1e-30)
            m_s[hk] = jnp.full((rows, LANES), NEG_INF, F32)
            l_s[hk] = jnp.zeros((rows, LANES), F32)
            acc_s[hk] = jnp.zeros((rows, LANES), F32)

    for s in range(ch):
        ck, cv = page_copies(s)
        ck.wait()
        cv.wait()

    base = j * (ch * PAGE)
    for hk in range(NSA_KV_HEADS):
        qh = q_s[hk]
        slope, tq = _row_consts(hk, nq, offset)
        carry = (m_s[hk][:, 0:1], l_s[hk][:, 0:1], acc_s[hk])
        for c in range(ch * PAGE // kchunk):
            pos = base + c * kchunk + lax.broadcasted_iota(jnp.int32, (1, kchunk), 1)
            carry = _flash_step(qh, kbuf[pl.ds(c * kchunk, kchunk), :], vbuf[pl.ds(c * kchunk, kchunk), :],
                                pos, tq, slope, sel_s[hk], carry)
        m_s[hk] = jnp.broadcast_to(carry[0], (rows, LANES))
        l_s[hk] = jnp.broadcast_to(carry[1], (rows, LANES))
        acc_s[hk] = carry[2]

    @pl.when(j == nchunks - 1)
    def _():
        gates = _sigmoid(sm_ref[0] + bg_ref[...])
        lane = lax.broadcasted_iota(jnp.int32, (1, LANES), 1)
        out_pairs = [None] * (NSA_HEADS // 2)
        for hk in range(NSA_KV_HEADS):
            own = (lane >= hk * NSA_DH) & (lane < (hk + 1) * NSA_DH)
            qh = q_s[hk]
            slope, tq = _row_consts(hk, nq, offset)
            pos = offset + lax.broadcasted_iota(jnp.int32, (1, ksn_ref.shape[1]), 1)
            carry = (m_s[hk][:, 0:1], l_s[hk][:, 0:1], acc_s[hk])
            _, l_f, acc_f = _flash_step(qh, ksn_ref[0], vsn_ref[0], pos, tq, slope, sel_s[hk], carry)
            o_s = acc_f / jnp.maximum(l_f, 1e-30)
            _combine_heads(oc_s[hk], o_s, ow_s[hk], gates, gh_ref, own, hk, nq, out_pairs)
        for k, y in enumerate(out_pairs):
            o_ref[0, :, k * LANES:(k + 1) * LANES] = y


def _nsa_sample(page_table, q, sm, k_cmp, v_cmp, ks_pool, vs_pool, ks_new, vs_new, win_k, win_v, kw_new, vw_new,
                bg, gh, nc_valid, ch):
    b, nq, _ = q.shape
    npg = page_table.shape[1]
    nchunks = npg // ch
    offset = npg * PAGE
    ncp = k_cmp.shape[1]
    ns = nc_valid // 4
    nsp = -(-ns // LANES) * LANES
    a = _sel_matrix(ncp, nc_valid, nsp)
    rows = NSA_GROUP * nq
    kchunk = min(512, ch * PAGE)
    per_b = lambda n, w=LANES: pl.BlockSpec((1, n, w), lambda bi, j, t: (bi, 0, 0))
    const = lambda shape: pl.BlockSpec(shape, lambda bi, j, t: (0,) * len(shape))
    grid_spec = pltpu.PrefetchScalarGridSpec(
        num_scalar_prefetch=1,
        grid=(b, nchunks),
        in_specs=[per_b(nq, 4 * LANES), per_b(nq), per_b(ncp), per_b(ncp),
                  pl.BlockSpec(memory_space=pl.ANY), pl.BlockSpec(memory_space=pl.ANY),
                  per_b(PAGE), per_b(PAGE), per_b(win_k.shape[1]), per_b(win_v.shape[1]), per_b(PAGE), per_b(PAGE),
                  const(a.shape), const((1, LANES)), const((NSA_HEADS, LANES))],
        out_specs=per_b(nq, 4 * LANES),
        scratch_shapes=[pltpu.VMEM((ch * PAGE, LANES), F32), pltpu.VMEM((ch * PAGE, LANES), F32),
                        pltpu.SemaphoreType.DMA((2,)),
                        pltpu.VMEM((NSA_KV_HEADS, rows, LANES), BF16),
                        pltpu.VMEM((NSA_KV_HEADS, rows, nsp), BF16),
                        pltpu.VMEM((NSA_KV_HEADS, rows, LANES), F32),
                        pltpu.VMEM((NSA_KV_HEADS, rows, LANES), F32),
                        pltpu.VMEM((NSA_KV_HEADS, rows, LANES), F32),
                        pltpu.VMEM((NSA_KV_HEADS, rows, LANES), F32),
                        pltpu.VMEM((NSA_KV_HEADS, rows, LANES), F32)],
    )
    return pl.pallas_call(
        functools.partial(_nsa_sample_kernel, ch=ch, nchunks=nchunks, offset=offset, ns=ns,
                          n_top=min(SEL_TOPK, ns), kchunk=kchunk),
        grid_spec=grid_spec,
        out_shape=jax.ShapeDtypeStruct((b, nq, 4 * LANES), F32),
        compiler_params=_params(("arbitrary", "arbitrary")),
        name="nsa_sample",
    )(page_table, q, sm, k_cmp, v_cmp, ks_pool, vs_pool, ks_new, vs_new, win_k, win_v, kw_new, vw_new, a, bg, gh)


def _mlstm_kernel(xm_ref, vm_ref, om_ref, sm_ref, smt_ref, conv0_ref, cw_ref, cb_ref, wq_ref, wk_ref,
                  bcol_ref, brow_ref, gh_ref, c0_ref, n0_ref, m0_ref,
                  hm_ref, c_ref, n_ref, m_ref, xs_ref, *, L, t_valid):
    c = pl.program_id(1)
    dh = MLSTM_DH

    @pl.when(c == 0)
    def _():
        xs_ref[0:8, :] = jnp.zeros((8, xs_ref.shape[1]), F32)
        xs_ref[5:8, :] = conv0_ref[0]
        c_ref[...] = c0_ref[...]
        n_ref[...] = n0_ref[...]
        m_ref[...] = m0_ref[...]

    xs_ref[8:8 + L, :] = xm_ref[0]
    xc = cb_ref[...]
    for jj in range(4):
        xc = xc + cw_ref[jj:jj + 1, :] * xs_ref[pl.ds(5 + jj, L), :]
    xc = xc * _sigmoid(xc)
    xs_ref[0:8, :] = xs_ref[L:L + 8, :]

    pre_col = sm_ref[0] + brow_ref[...]
    pre_row = smt_ref[0, 0] + bcol_ref[...]
    lf_col = _log_sigmoid(pre_col)
    lf_row = _log_sigmoid(pre_row)
    if t_valid < L:
        rid = lax.broadcasted_iota(jnp.int32, (L, 1), 0) < t_valid
        cid = lax.broadcasted_iota(jnp.int32, (1, L), 1) < t_valid
        lf_col = jnp.where(rid, lf_col, 0.0)
        lf_row = jnp.where(cid, lf_row, 0.0)
        pre_col = jnp.where(rid, pre_col, NEG_INF)
        pre_row = jnp.where(cid, pre_row, NEG_INF)
    ri = lax.broadcasted_iota(jnp.int32, (L, L), 0)
    ci = lax.broadcasted_iota(jnp.int32, (L, L), 1)
    causal = ci <= ri
    bcum_col = _mm_f32(causal.astype(F32), lf_col)
    bcum_row = _mm_f32(lf_row, (ri <= ci).astype(F32))

    for h in range(MLSTM_HEADS):
        hs = slice(h * dh, (h + 1) * dh)
        xh = xc[:, hs]
        q = _mm(xh, wq_ref[h])
        k = _mm(xh, wk_ref[h]) * (dh ** -0.5)
        v = vm_ref[0, :, hs]
        bc = bcum_col[:, 28 + h:29 + h]
        ic = pre_col[:, 24 + h:25 + h]
        br = bcum_row[4 + h:5 + h, :]
        ir = pre_row[h:h + 1, :]
        mh = m_ref[0, :, h:h + 1]
        ch_ = c_ref[0, h]
        nh = n_ref[0, h:h + 1, :]
        dmat = jnp.where(causal, bc - br + ir, NEG_INF)
        inter = bc + mh
        mq = jnp.maximum(inter, jnp.max(dmat, axis=1, keepdims=True))
        a = jnp.exp(dmat - mq) * _mm_nt(q, k)
        wi = jnp.exp(inter - mq)
        num = _mm(a, v) + wi * _mm_nt(q, ch_)
        den = jnp.sum(a, axis=1, keepdims=True) + wi * jnp.sum(q * nh, axis=1, keepdims=True)
        hout = num / jnp.maximum(jnp.abs(den), jnp.exp(-mq))
        btot = bc[L - 1:L, :]
        dec_r = btot - br + ir
        dec_c = btot - bc + ic
        m_new = jnp.maximum(btot + mh, jnp.max(dec_r, axis=1, keepdims=True))
        ws_c = jnp.exp(dec_c - m_new)
        w_c = jnp.exp(btot + mh - m_new)
        c_ref[0, h] = w_c * ch_ + _mm_tn(v * ws_c, k)
        n_ref[0, h:h + 1, :] = w_c * nh + jnp.sum(k * ws_c, axis=0, keepdims=True)
        m_ref[0, :, h:h + 1] = m_new
        y = _rms(hout, gh_ref[h:h + 1, :]) * _sigmoid(om_ref[0, :, hs])
        hm_ref[0, :, hs] = y


def _mlstm(xm, vm, om, sm, conv0, conv_w, conv_b, w_qm, w_km, b_i, b_f, g_head, c0, n0, m0, L, t_valid):
    b, t, w = xm.shape
    nck = t // L
    smt = sm[:, :, 24:32].reshape(b, nck, L, 8).transpose(0, 1, 3, 2)
    brow = jnp.zeros((1, LANES), F32).at[0, 24:28].set(b_i).at[0, 28:32].set(b_f)
    bcol = jnp.concatenate([b_i, b_f]).reshape(8, 1)
    m0p = jnp.zeros((b, 1, LANES), F32).at[:, 0, :MLSTM_HEADS].set(m0)
    tok = lambda: pl.BlockSpec((1, L, w), lambda bi, c: (bi, c, 0))
    const = lambda shape: pl.BlockSpec(shape, lambda bi, c: (0,) * len(shape))
    state = lambda shape: pl.BlockSpec((1,) + shape, lambda bi, c: (bi,) + (0,) * len(shape))
    hm, c_out, n_out, m_out = pl.pallas_call(
        functools.partial(_mlstm_kernel, L=L, t_valid=t_valid),
        grid=(b, nck),
        in_specs=[tok(), tok(), tok(),
                  pl.BlockSpec((1, L, LANES), lambda bi, c: (bi, c, 0)),
                  pl.BlockSpec((1, 1, 8, L), lambda bi, c: (bi, c, 0, 0)),
                  state(conv0.shape[1:]),
                  const(conv_w.shape), const((1, w)), const(w_qm.shape), const(w_km.shape),
                  const((8, 1)), const((1, LANES)), const(g_head.shape),
                  state(c0.shape[1:]), state(n0.shape[1:]), state((1, LANES))],
        out_specs=[tok(), state(c0.shape[1:]), state(n0.shape[1:]), state((1, LANES))],
        out_shape=[jax.ShapeDtypeStruct((b, t, w), F32), jax.ShapeDtypeStruct(c0.shape, F32),
                   jax.ShapeDtypeStruct(n0.shape, F32), jax.ShapeDtypeStruct((b, 1, LANES), F32)],
        scratch_shapes=[pltpu.VMEM((L + 8, w), F32)],
        compiler_params=_params(("arbitrary", "arbitrary")),
        name="mlstm",
    )(xm, vm, om, sm, smt, conv0, conv_w, conv_b.reshape(1, w), w_qm.astype(BF16), w_km.astype(BF16),
      bcol, brow, g_head, c0, n0, m0p)
    return hm, c_out, n_out, m_out[:, 0, :MLSTM_HEADS]


def _mix_kernel(x_ref, on_ref, hm_ref, wo1_ref, wo2_ref, gxa_ref, wxq_ref, x1_ref, qx_ref):
    x1 = x_ref[...] + _mm(on_ref[...], wo1_ref[...]) + _mm(hm_ref[...], wo2_ref[...])
    x1_ref[...] = x1
    qx_ref[...] = _mm(_rms(x1, gxa_ref[...]), wxq_ref[...]) * (XA_DH ** -0.5)


def _mix(x, o_nsa, hm, w_out, g_xa, w_xq, tm):
    m, d = x.shape
    half = o_nsa.shape[1]
    row = lambda w: pl.BlockSpec((tm, w), lambda i: (i, 0))
    const = lambda shape: pl.BlockSpec(shape, lambda i: (0, 0))
    return pl.pallas_call(
        _mix_kernel,
        grid=(m // tm,),
        in_specs=[row(d), row(half), row(half), const((half, d)), const((half, d)), const((1, d)), const((d, d))],
        out_specs=[row(d), row(d)],
        out_shape=[jax.ShapeDtypeStruct((m, d), F32)] * 2,
        compiler_params=_params(("parallel",)),
        name="mix",
    )(x, o_nsa, hm, w_out[:half].astype(BF16), w_out[half:].astype(BF16), g_xa.reshape(1, d), w_xq.astype(BF16))


def _xattn_kernel(qx_ref, mk_ref, mv_ref, o_ref):
    for h in range(XA_HEADS):
        hs = slice(h * XA_DH, (h + 1) * XA_DH)
        s = _mm_nt(qx_ref[0, :, hs], mk_ref[0, :, hs])
        p = jnp.exp(s - jnp.max(s, axis=-1, keepdims=True))
        o_ref[0, :, hs] = _mm(p, mv_ref[0, :, hs]) / jnp.sum(p, axis=-1, keepdims=True)


def _xattn(qx, mem_k, mem_v, tq):
    b, t, d = qx.shape
    nm = mem_k.shape[1]
    return pl.pallas_call(
        _xattn_kernel,
        grid=(b, t // tq),
        in_specs=[pl.BlockSpec((1, tq, d), lambda bi, i: (bi, i, 0)),
                  pl.BlockSpec((1, nm, d), lambda bi, i: (bi, 0, 0)),
                  pl.BlockSpec((1, nm, d), lambda bi, i: (bi, 0, 0))],
        out_specs=pl.BlockSpec((1, tq, d), lambda bi, i: (bi, i, 0)),
        out_shape=jax.ShapeDtypeStruct((b, t, d), F32),
        compiler_params=_params(("parallel", "parallel")),
        name="xattn",
    )(qx, mem_k, mem_v)


def _ffn_kernel(x1_ref, ox_ref, wxo_ref, gf_ref, wg_ref, wu_ref, wd_ref, gfin_ref, y_ref, x2_s, h_s, acc_s):
    j = pl.program_id(1)

    @pl.when(j == 0)
    def _():
        x2 = x1_ref[...] + _mm(ox_ref[...], wxo_ref[...])
        x2_s[...] = x2
        h_s[...] = _rms(x2, gf_ref[...]).astype(BF16)
        acc_s[...] = jnp.zeros(acc_s.shape, F32)

    h = h_s[...]
    g = jnp.dot(h, wg_ref[...], preferred_element_type=F32)
    u = jnp.dot(h, wu_ref[...], preferred_element_type=F32)
    acc_s[...] += _mm(g * _sigmoid(g) * u, wd_ref[...])

    @pl.when(j == pl.num_programs(1) - 1)
    def _():
        y_ref[...] = _rms(x2_s[...] + acc_s[...], gfin_ref[...])


def _ffn(x1, ox, w_xo, g_ffn, w_gate, w_up, w_down, g_final, tm, tf):
    m, d = x1.shape
    dff = w_gate.shape[1]
    row = pl.BlockSpec((tm, d), lambda i, j: (i, 0))
    vec = pl.BlockSpec((1, d), lambda i, j: (0, 0))
    return pl.pallas_call(
        _ffn_kernel,
        grid=(m // tm, dff // tf),
        in_specs=[row, row, pl.BlockSpec((d, d), lambda i, j: (0, 0)), vec,
                  pl.BlockSpec((d, tf), lambda i, j: (0, j)), pl.BlockSpec((d, tf), lambda i, j: (0, j)),
                  pl.BlockSpec((tf, d), lambda i, j: (j, 0)), vec],
        out_specs=row,
        out_shape=jax.ShapeDtypeStruct((m, d), F32),
        scratch_shapes=[pltpu.VMEM((tm, d), F32), pltpu.VMEM((tm, d), BF16), pltpu.VMEM((tm, d), F32)],
        compiler_params=_params(("parallel", "arbitrary")),
        name="ffn",
    )(x1, ox, w_xo.astype(BF16), g_ffn.reshape(1, d), w_gate.astype(BF16), w_up.astype(BF16),
      w_down.astype(BF16), g_final.reshape(1, d))


def _split_w_in(w_in, nsa_w, kv_w, mlstm_w):
    cuts = np.cumsum([nsa_w] + [kv_w] * 6 + [3 * NSA_HEADS] + [mlstm_w] * 3 + [MLSTM_HEADS] * 2)
    parts = jnp.split(w_in, cuts[:-1].tolist(), axis=1)
    small = jnp.concatenate([parts[7], parts[11], parts[12]], axis=1)
    small = jnp.pad(small, ((0, 0), (0, LANES - small.shape[1])))
    ws = [parts[0]] + list(parts[1:7]) + [small] + list(parts[8:11])
    return [w.astype(BF16) for w in ws]


def _tail(x1, ox, w, tm, b, t):
    d = x1.shape[1]
    dff = w["w_gate"].shape[1]
    tf = dff // 2 if (dff // 2) % LANES == 0 else dff
    y = _ffn(x1, ox.reshape(-1, d), w["w_xo"], w["g_ffn"], w["w_gate"], w["w_up"], w["w_down"], w["g_final"], tm, tf)
    return y.reshape(b, t, d)


def _gate_consts(w):
    bg = jnp.pad(w["b_gate"], (0, LANES - w["b_gate"].shape[0])).reshape(1, LANES)
    gh = jnp.concatenate([w["g_head_nsa"], w["g_head_nsa"]], axis=1)
    return bg, gh


def _prompt_group(x, mem, w):
    b, t, d = x.shape
    m = b * t
    tm = 512
    q, kc, vc, ks, vs, kw, vw, sm, xm, vm, om = _norm_proj(x.reshape(m, d), w["g_mix"], w["w_in_parts"], tm)
    r3 = lambda a: a.reshape(b, t, a.shape[-1])
    npg = t // PAGE
    table = jnp.arange(b * npg, dtype=jnp.int32).reshape(b, npg)
    zeros_next = jnp.zeros((b, PAGE, LANES), F32)
    k_cmp = _compress(table, kc.reshape(b * npg, PAGE, LANES), zeros_next, *w["cmp_k"], ch=npg)
    v_cmp = _compress(table, vc.reshape(b * npg, PAGE, LANES), zeros_next, *w["cmp_v"], ch=npg)
    bg, gh = _gate_consts(w)
    o_nsa = _nsa_prompt(r3(q), r3(sm), k_cmp, v_cmp, r3(ks), r3(vs), r3(kw), r3(vw), bg, gh, 128)
    L = 64
    hm, c_out, n_out, m_out = _mlstm(
        r3(xm), r3(vm), r3(om), r3(sm), jnp.zeros((b, 3, xm.shape[1]), F32), w["conv_w"], w["conv_b"],
        w["w_qm"], w["w_km"], w["b_i"], w["b_f"], w["g_head_m"],
        jnp.zeros((b, MLSTM_HEADS, MLSTM_DH, MLSTM_DH), F32), jnp.zeros((b, MLSTM_HEADS, MLSTM_DH), F32),
        jnp.zeros((b, MLSTM_HEADS), F32), L, L)
    nm = mem.shape[1]
    mk, mv = _norm_proj(mem.reshape(b * nm, d), w["g_mem"], [w["w_xk"].astype(BF16), w["w_xv"].astype(BF16)],
                        min(512, b * nm))
    x1, qx = _mix(x.reshape(m, d), o_nsa.reshape(m, -1), hm.reshape(m, -1), w["w_out"], w["g_xa"], w["w_xq"], tm)
    ox = _xattn(qx.reshape(b, t, d), mk.reshape(b, nm, d), mv.reshape(b, nm, d), 512)
    y = _tail(x1, ox, w, tm, b, t)
    kv5 = lambda a: a.reshape(1, b, t, NSA_KV_HEADS, NSA_DH)
    keep = min(WINDOW, t)
    xm3 = r3(xm)
    states = (kv5(kc), kv5(vc), kv5(ks), kv5(vs), kv5(kw)[:, :, t - keep:], kv5(vw)[:, :, t - keep:],
              c_out[None], n_out[None], m_out[None], xm3[None, :, t - 3:],
              mk.reshape(1, b, nm, XA_HEADS, XA_DH), mv.reshape(1, b, nm, XA_HEADS, XA_DH))
    return y, states


def _sample_group(x, pools, page_table, win_k, win_v, conv0, c0, n0, m0, mem_k, mem_v, w):
    b, t, d = x.shape
    m = b * t
    tp = 8
    tm = min(m, 512)
    q, kc, vc, ks, vs, kw, vw, sm, xm, vm, om = _norm_proj(x.reshape(m, d), w["g_mix"], w["w_in_parts"], tm)
    r3 = lambda a: a.reshape(b, t, a.shape[-1])
    pad_t = lambda a: jnp.pad(r3(a), ((0, 0), (0, tp - t), (0, 0)))
    npg = page_table.shape[1]
    past = npg * PAGE
    ch = min(32, npg)
    pool_kc, pool_vc, pool_ks, pool_vs = [p.reshape(p.shape[0], PAGE, LANES) for p in pools]
    nc_valid = (past + (-(-t // SEL_BLOCK)) * SEL_BLOCK) // CMP_STRIDE
    ncp = -(-nc_valid // LANES) * LANES

    pad_page = lambda a: jnp.pad(r3(a), ((0, 0), (0, PAGE - t), (0, 0)))

    def compressed(pool, new_rows, cw):
        nxt = pad_page(new_rows)
        main = _compress(page_table, pool, nxt, *cw, ch=ch)
        tail = _compress(jnp.arange(b, dtype=jnp.int32).reshape(1, b), nxt, jnp.zeros((1, PAGE, LANES), F32),
                         *cw, ch=b)
        n_tail = nc_valid - npg * CMP_PER_PAGE
        full = jnp.concatenate([main, tail.reshape(b, CMP_PER_PAGE, LANES)[:, :n_tail]], axis=1)
        return jnp.pad(full, ((0, 0), (0, ncp - nc_valid), (0, 0)))

    k_cmp = compressed(pool_kc, kc, w["cmp_k"])
    v_cmp = compressed(pool_vc, vc, w["cmp_v"])
    bg, gh = _gate_consts(w)
    o_nsa = _nsa_sample(page_table, pad_t(q), pad_t(sm), k_cmp, v_cmp, pool_ks, pool_vs, pad_page(ks), pad_page(vs),
                        win_k.reshape(b, -1, LANES), win_v.reshape(b, -1, LANES), pad_page(kw), pad_page(vw),
                        bg, gh, nc_valid, ch)[:, :t]
    hm, c_out, n_out, m_out = _mlstm(pad_t(xm), pad_t(vm), pad_t(om), pad_t(sm), conv0, w["conv_w"], w["conv_b"],
                                     w["w_qm"], w["w_km"], w["b_i"], w["b_f"], w["g_head_m"], c0, n0, m0, tp, t)
    hm = hm[:, :t]
    x1, qx = _mix(x.reshape(m, d), o_nsa.reshape(m, -1), hm.reshape(m, -1), w["w_out"], w["g_xa"], w["w_xq"], tm)
    nm = mem_k.shape[1]
    ox = _xattn(pad_t(qx), mem_k.reshape(b, nm, d), mem_v.reshape(b, nm, d), tp)[:, :t]
    y = _tail(x1, ox, w, tm, b, t)
    kv5 = lambda a: a.reshape(1, b, t, NSA_KV_HEADS, NSA_DH)
    keep = min(WINDOW, past + t)
    win5 = lambda old, new: jnp.concatenate([old, kv5(new)[0]], axis=1)[None, :, -keep:]
    conv_all = jnp.concatenate([conv0, r3(xm)], axis=1)
    states = (kv5(kc), kv5(vc), kv5(ks), kv5(vs), win5(win_k, kw), win5(win_v, vw),
              c_out[None], n_out[None], m_out[None], conv_all[None, :, -3:])
    return y, states


def kernel(x_prompt, x_sample, cache_k_cmp, cache_v_cmp, cache_k_slc, cache_v_slc, state_k_win, state_v_win, state_conv, state_C, state_n, state_m, cache_mem_k, cache_mem_v, page_table, mem_prompt, g_mix, w_in, b_gate, cmp_pe_k, cmp_w1_k, cmp_b1_k, cmp_w2_k, cmp_pe_v, cmp_w1_v, cmp_b1_v, cmp_w2_v, g_head_nsa, conv_w, conv_b, w_qm, w_km, b_i, b_f, g_head_m, w_out, g_xa, g_mem, w_xq, w_xk, w_xv, w_xo, g_ffn, w_gate, w_up, w_down, g_final):
    assert w_in.shape[0] == 1, "single-layer decoder"
    l = 0
    w = dict(g_mix=g_mix[l], b_gate=b_gate[l],
             w_in_parts=_split_w_in(w_in[l], NSA_HEADS * NSA_DH, NSA_KV_HEADS * NSA_DH, MLSTM_HEADS * MLSTM_DH),
             cmp_k=(cmp_pe_k[l], cmp_w1_k[l], cmp_b1_k[l], cmp_w2_k[l]),
             cmp_v=(cmp_pe_v[l], cmp_w1_v[l], cmp_b1_v[l], cmp_w2_v[l]),
             g_head_nsa=g_head_nsa[l], conv_w=conv_w[l], conv_b=conv_b[l], w_qm=w_qm[l], w_km=w_km[l],
             b_i=b_i[l], b_f=b_f[l], g_head_m=g_head_m[l], w_out=w_out[l], g_xa=g_xa[l], g_mem=g_mem[l],
             w_xq=w_xq[l], w_xk=w_xk[l], w_xv=w_xv[l], w_xo=w_xo[l], g_ffn=g_ffn[l], w_gate=w_gate[l],
             w_up=w_up[l], w_down=w_down[l], g_final=g_final)
    y_p, st_p = _prompt_group(x_prompt, mem_prompt, w)
    pools = (cache_k_cmp[l], cache_v_cmp[l], cache_k_slc[l], cache_v_slc[l])
    y_s, st_s = _sample_group(x_sample, pools, page_table, state_k_win[l], state_v_win[l], state_conv[l],
                              state_C[l], state_n[l], state_m[l], cache_mem_k[l], cache_mem_v[l], w)
    return (y_p, y_s) + st_p + st_s
```

```python
import functools

import numpy as np
import jax
import jax.numpy as jnp
from jax import lax
from jax.experimental import pallas as pl
from jax.experimental.pallas import tpu as pltpu

F32 = jnp.float32
BF16 = jnp.bfloat16
HIGHEST = lax.Precision.HIGHEST

EPS = 1e-6
NEG_INF = -1e30
FORCE_SCORE = 1e9
PAD_SCORE = -2e38
TAKEN_SCORE = -3e38

LANES = 128
NSA_HEADS = 8
NSA_KV_HEADS = 2
NSA_GROUP = 4
NSA_DH = 64
CMP_STRIDE = 16
SEL_BLOCK = 64
SEL_TOPK = 16
WINDOW = 512
Q_BLOCK = 64
PAGE = 128
CMP_PER_PAGE = PAGE // CMP_STRIDE
MLSTM_HEADS = 4
MLSTM_DH = 128
XA_HEADS = 4
XA_DH = 256
ALIBI = tuple(2.0 ** (-(h + 1)) for h in range(NSA_HEADS))

VMEM_LIMIT = 56 * 1024 * 1024


def _params(sem):
    return pltpu.CompilerParams(dimension_semantics=sem, vmem_limit_bytes=VMEM_LIMIT)


def _mm(a, b):
    return jnp.dot(a.astype(BF16), b.astype(BF16), preferred_element_type=F32)


def _mm_nt(a, b):
    return lax.dot_general(a.astype(BF16), b.astype(BF16), (((1,), (1,)), ((), ())),
                           preferred_element_type=F32)


def _mm_tn(a, b):
    return lax.dot_general(a.astype(BF16), b.astype(BF16), (((0,), (0,)), ((), ())),
                           preferred_element_type=F32)


def _mm_f32(a, b):
    return jnp.dot(a, b, precision=HIGHEST, preferred_element_type=F32)


def _rms(x, g):
    return x * lax.rsqrt(jnp.mean(x * x, axis=-1, keepdims=True) + EPS) * g


def _sigmoid(x):
    return 1.0 / (1.0 + jnp.exp(-x))


def _log_sigmoid(x):
    return jnp.minimum(x, 0.0) - jnp.log(1.0 + jnp.exp(-jnp.abs(x)))


def _gelu_tanh(x):
    return 0.5 * x * (1.0 + jnp.tanh(0.7978845608028654 * (x + 0.044715 * x * x * x)))


def _topk_mask(imp, k):
    col = lax.broadcasted_iota(jnp.int32, imp.shape, 1)
    sel = jnp.zeros(imp.shape, F32)
    work = imp
    for _ in range(k):
        m = jnp.max(work, axis=-1, keepdims=True)
        idx = jnp.min(jnp.where(work == m, col, jnp.int32(2 ** 30)), axis=-1, keepdims=True)
        hit = col == idx
        sel = jnp.where(hit, 1.0, sel)
        work = jnp.where(hit, TAKEN_SCORE, work)
    return sel


def _softmax_parts(s, valid):
    s = jnp.where(valid, s, NEG_INF)
    mx = jnp.max(s, axis=-1, keepdims=True)
    p = jnp.where(valid, jnp.exp(s - mx), 0.0)
    return p, mx, jnp.sum(p, axis=-1, keepdims=True)


def _q_rows_f32(q_ref, hk, nq):
    lane = lax.broadcasted_iota(jnp.int32, (1, LANES), 1)
    own = (lane >= hk * NSA_DH) & (lane < (hk + 1) * NSA_DH)
    parts = []
    for g in range(NSA_GROUP):
        h = hk * NSA_GROUP + g
        blk = q_ref[0, :, (h // 2) * LANES:(h // 2 + 1) * LANES]
        if (h % 2) != hk:
            blk = pltpu.roll(blk, NSA_DH, 1)
        parts.append(blk)
    qh = jnp.concatenate(parts, axis=0)
    return jnp.where(own, qh * (NSA_DH ** -0.5), 0.0), own


def _q_rows(q_ref, hk, nq):
    qh, own = _q_rows_f32(q_ref, hk, nq)
    return qh.astype(BF16), own


def _feature_lanes(hk):
    base = (1 - hk) * NSA_DH
    return base, base + 1, base + 2


def _select_t(imp_t, tq_lane, ns, n_top):
    blk = lax.broadcasted_iota(jnp.int32, (imp_t.shape[0], 1), 0)
    cur = tq_lane >> 6
    forced = (blk == 0) | (blk == cur) | (blk == cur - 1)
    work = jnp.where(forced, FORCE_SCORE, imp_t)
    work = jnp.where(blk * SEL_BLOCK <= tq_lane, work, NEG_INF)
    work = jnp.where(blk < ns, work, PAD_SCORE)
    sel = jnp.zeros(imp_t.shape, F32)
    for _ in range(n_top):
        m = jnp.max(work, axis=0, keepdims=True)
        idx = jnp.min(jnp.where(work == m, blk, jnp.int32(2 ** 30)), axis=0, keepdims=True)
        hit = blk == idx
        sel = jnp.where(hit, 1.0, sel)
        work = jnp.where(hit, TAKEN_SCORE, work)
    return sel


def _row_consts(hk, nq, t0):
    rows = NSA_GROUP * nq
    row = lax.broadcasted_iota(jnp.int32, (rows, 1), 0)
    slope = jnp.full((rows, 1), ALIBI[hk * NSA_GROUP + NSA_GROUP - 1], F32)
    for g in range(NSA_GROUP - 2, -1, -1):
        slope = jnp.where(row < (g + 1) * nq, ALIBI[hk * NSA_GROUP + g], slope)
    tq = t0 + (row & (nq - 1))
    return slope, tq


def _cmp_branch(qh, slope, tq, kc_ref, vc_ref, a_ref, nq):
    ncp = kc_ref.shape[1]
    s = _mm_nt(qh, kc_ref[0])
    cend = lax.broadcasted_iota(jnp.int32, (1, ncp), 1) * CMP_STRIDE + (2 * CMP_STRIDE - 1)
    d = tq - cend
    valid = d >= 0
    p, _, l = _softmax_parts(s - slope * d.astype(F32), valid)
    p = p / jnp.maximum(l, 1e-30)
    o_c = _mm(p, vc_ref[0])
    psum = p[0:nq]
    for g in range(1, NSA_GROUP):
        psum = psum + p[g * nq:(g + 1) * nq]
    return o_c, _mm_f32(psum, a_ref[...])


def _select(imp, tq_q, ns, n_top):
    nsp = imp.shape[1]
    blk = lax.broadcasted_iota(jnp.int32, (1, nsp), 1)
    cur = tq_q >> 6
    forced = (blk == 0) | (blk == cur) | (blk == cur - 1)
    imp = jnp.where(forced, FORCE_SCORE, imp)
    imp = jnp.where(blk * SEL_BLOCK <= tq_q, imp, NEG_INF)
    imp = jnp.where(blk < ns, imp, PAD_SCORE)
    return _topk_mask(imp, n_top)


def _flash_step(qh, kch, vch, pos, tq, slope, sel_rows, carry):
    m, l, acc = carry
    nsp = sel_rows.shape[1]
    s = _mm_nt(qh, kch)
    d = tq - pos
    blk = lax.broadcasted_iota(jnp.int32, (nsp, 1), 0)
    expand = (blk == (pos >> 6)).astype(BF16)
    chosen = jnp.dot(sel_rows, expand, preferred_element_type=F32) > 0.5
    valid = (d >= 0) & chosen
    s = jnp.where(valid, s - slope * d.astype(F32), NEG_INF)
    m_new = jnp.maximum(m, jnp.max(s, axis=-1, keepdims=True))
    alpha = jnp.exp(m - m_new)
    p = jnp.where(valid, jnp.exp(s - m_new), 0.0)
    l = alpha * l + jnp.sum(p, axis=-1, keepdims=True)
    acc = alpha * acc + _mm(p, vch)
    return m_new, l, acc


def _combine_heads(o_c, o_s, o_w, gates, gh_ref, own, hk, nq, out_pairs):
    for g in range(NSA_GROUP):
        h = hk * NSA_GROUP + g
        r = slice(g * nq, (g + 1) * nq)
        og = (gates[:, 3 * h:3 * h + 1] * o_c[r] + gates[:, 3 * h + 1:3 * h + 2] * o_s[r]
              + gates[:, 3 * h + 2:3 * h + 3] * o_w[r])
        og = jnp.where(own, og, 0.0)
        ms = jnp.sum(og * og, axis=-1, keepdims=True) * (1.0 / NSA_DH)
        y = og * lax.rsqrt(ms + EPS) * gh_ref[h:h + 1, :]
        if (h % 2) != hk:
            y = pltpu.roll(y, NSA_DH, 1)
        out_pairs[h // 2] = y if out_pairs[h // 2] is None else out_pairs[h // 2] + y


def _norm_proj_kernel(x_ref, g_ref, *refs, n):
    h = _rms(x_ref[...], g_ref[...]).astype(BF16)
    for w_ref, o_ref in zip(refs[:n], refs[n:]):
        o_ref[...] = jnp.dot(h, w_ref[...], preferred_element_type=F32)


def _norm_proj(x, g, ws, tm):
    m, d = x.shape
    n = len(ws)
    return pl.pallas_call(
        functools.partial(_norm_proj_kernel, n=n),
        grid=(m // tm,),
        in_specs=[pl.BlockSpec((tm, d), lambda i: (i, 0)), pl.BlockSpec((1, d), lambda i: (0, 0))]
        + [pl.BlockSpec(w.shape, lambda i: (0, 0)) for w in ws],
        out_specs=[pl.BlockSpec((tm, w.shape[1]), lambda i: (i, 0)) for w in ws],
        out_shape=[jax.ShapeDtypeStruct((m, w.shape[1]), F32) for w in ws],
        compiler_params=_params(("parallel",)),
        name="norm_proj",
    )(x, g.reshape(1, d), *ws)


def _compress_kernel(tbl_ref, pool_ref, last_ref, wr_ref, pe1_ref, pe2_ref, b1_ref, w2_ref, o_ref,
                     buf, acc, cst, sem, *, ch, nchunks):
    b = pl.program_id(0)
    j = pl.program_id(1)
    nblk = ch * CMP_PER_PAGE
    half = wr_ref.shape[2] // 2

    def page_copy(k, slot):
        return pltpu.make_async_copy(pool_ref.at[tbl_ref[b, k]], buf.at[pl.ds(slot * PAGE, PAGE)],
                                     sem.at[slot])

    for s in range(ch):
        page_copy(j * ch + s, s).start()

    @pl.when(j < nchunks - 1)
    def _():
        page_copy(jnp.minimum((j + 1) * ch, nchunks * ch - 1), ch).start()

    @pl.when(j == nchunks - 1)
    def _():
        buf[pl.ds(ch * PAGE, PAGE), :] = last_ref[0]

    @pl.when((b == 0) & (j == 0))
    def _():
        c = jnp.zeros((8, half), F32)
        for r in range(CMP_STRIDE):
            c = c + _mm(jnp.broadcast_to(pe1_ref[r:r + 1, :], (8, LANES)), wr_ref[r, :, 0:half])
            c = c + _mm(jnp.broadcast_to(pe2_ref[r:r + 1, :], (8, LANES)), wr_ref[r, :, half:2 * half])
        cst[...] = c + b1_ref[...]

    for s in range(ch):
        page_copy(j * ch + s, s).wait()

    @pl.when(j < nchunks - 1)
    def _():
        page_copy(jnp.minimum((j + 1) * ch, nchunks * ch - 1), ch).wait()

    for r in range(CMP_STRIDE):
        xr = buf[pl.ds(r, nblk + 8, stride=CMP_STRIDE), :]
        part = _mm(xr, wr_ref[r])
        if r == 0:
            acc[...] = part
        else:
            acc[...] += part
    hid = acc[0:nblk, 0:half] + acc[pl.ds(1, nblk), half:2 * half] + cst[0:1, :]
    o_ref[0] = _mm(_gelu_tanh(hid), w2_ref[...])


def _compress(table, pool, last_next, pe, w1, b1, w2, ch):
    nb, npg = table.shape
    nchunks = npg // ch
    hid = w1.shape[1]
    dh = NSA_DH
    wa = w1[:CMP_STRIDE * dh].reshape(CMP_STRIDE, dh, hid)
    wb = w1[CMP_STRIDE * dh:].reshape(CMP_STRIDE, dh, hid)
    z = jnp.zeros_like(wa)
    wr = jnp.concatenate([jnp.concatenate([wa, z, wb, z], axis=2),
                          jnp.concatenate([z, wa, z, wb], axis=2)], axis=1).astype(BF16)
    pe1 = jnp.concatenate([pe[:CMP_STRIDE], pe[:CMP_STRIDE]], axis=1)
    pe2 = jnp.concatenate([pe[CMP_STRIDE:], pe[CMP_STRIDE:]], axis=1)
    b1t = jnp.concatenate([b1, b1]).reshape(1, 2 * hid)
    zz = jnp.zeros_like(w2)
    w2bd = jnp.concatenate([jnp.concatenate([w2, zz], axis=1),
                            jnp.concatenate([zz, w2], axis=1)], axis=0).astype(BF16)
    nblk = ch * CMP_PER_PAGE
    grid_spec = pltpu.PrefetchScalarGridSpec(
        num_scalar_prefetch=1,
        grid=(nb, nchunks),
        in_specs=[pl.BlockSpec(memory_space=pl.ANY),
                  pl.BlockSpec((1, PAGE, LANES), lambda b, j, t: (b, 0, 0)),
                  pl.BlockSpec(wr.shape, lambda b, j, t: (0, 0, 0)),
                  pl.BlockSpec(pe1.shape, lambda b, j, t: (0, 0)),
                  pl.BlockSpec(pe2.shape, lambda b, j, t: (0, 0)),
                  pl.BlockSpec(b1t.shape, lambda b, j, t: (0, 0)),
                  pl.BlockSpec(w2bd.shape, lambda b, j, t: (0, 0))],
        out_specs=pl.BlockSpec((1, nblk, LANES), lambda b, j, t: (b, j, 0)),
        scratch_shapes=[pltpu.VMEM(((ch + 1) * PAGE, LANES), F32),
                        pltpu.VMEM((nblk + 8, 4 * hid), F32),
                        pltpu.VMEM((8, 2 * hid), F32),
                        pltpu.SemaphoreType.DMA((ch + 1,))],
    )
    return pl.pallas_call(
        functools.partial(_compress_kernel, ch=ch, nchunks=nchunks),
        grid_spec=grid_spec,
        out_shape=jax.ShapeDtypeStruct((nb, npg * CMP_PER_PAGE, LANES), F32),
        compiler_params=_params(("arbitrary", "arbitrary")),
        name="compress",
    )(table, pool, last_next, wr, pe1, pe2, b1t, w2bd)


def _sel_matrix(nc_rows, nc_valid, ns_cols):
    a = np.zeros((nc_rows, ns_cols), np.float32)
    for j in range(nc_valid // 4):
        for c, wgt in ((4 * j - 1, 0.5), (4 * j, 1.0), (4 * j + 1, 1.0), (4 * j + 2, 1.0), (4 * j + 3, 0.5)):
            if 0 <= c < nc_valid:
                a[c, j] += wgt
    return jnp.asarray(a)


def _nsa_prompt_kernel(qi_ref, ci_ref, q_ref, sm_ref, kc_ref, vc_ref, ks_ref, vs_ref, kw_ref, vw_ref, at_ref, bg_ref,
                       gh_ref, o_ref, ksa, vsa, kwa, vwa, kca, vca, qa_s, oc_s, ow_s, m_s, acc_s,
                       *, nq, ns, n_top, kchunk, wkeys, prep_rows):
    step = pl.program_id(1)
    i = qi_ref[step]
    c = ci_ref[step]
    t0 = i * nq
    rows = NSA_GROUP * nq
    t = ks_ref.shape[1]
    nc = kc_ref.shape[1]
    last = (t0 + nq + kchunk - 1) // kchunk - 1
    lane = lax.broadcasted_iota(jnp.int32, (1, LANES), 1)

    @pl.when(step == 0)
    def _():
        for hk in range(NSA_KV_HEADS):
            own = (lane >= hk * NSA_DH) & (lane < (hk + 1) * NSA_DH)
            f0, f1, f2 = _feature_lanes(hk)
            cidx = lax.broadcasted_iota(jnp.int32, (nc, 1), 0).astype(F32)
            kca[hk] = jnp.where(own, kc_ref[0], jnp.where(lane == f2, cidx, 0.0)).astype(BF16)
            vca[hk] = vc_ref[0].astype(BF16)

            def prep(r, _):
                r0 = pl.multiple_of(r * prep_rows, prep_rows)
                rs = pl.ds(r0, prep_rows)
                pos = r0 + lax.broadcasted_iota(jnp.int32, (prep_rows, 1), 0)
                blk = pos >> 6
                feat = jnp.where(lane == f0, blk.astype(F32), jnp.where(lane == f1, (pos & 63).astype(F32), 0.0))
                ksa[hk, rs, 0:LANES] = jnp.where(own, ks_ref[0, rs, :], feat).astype(BF16)
                ksa[hk, rs, LANES:2 * LANES] = (lane == blk).astype(BF16)
                kwa[hk, rs, :] = jnp.where(own, kw_ref[0, rs, :], feat).astype(BF16)
                vsa[hk, rs, :] = jnp.where(own, vs_ref[0, rs, :], 1.0).astype(BF16)
                vwa[hk, rs, :] = jnp.where(own, vw_ref[0, rs, :], 1.0).astype(BF16)
                return 0

            lax.fori_loop(0, t // prep_rows, prep, 0)

    @pl.when(c == 0)
    def _():
        tq_lane = t0 + lax.broadcasted_iota(jnp.int32, (1, nq), 1)
        for hk in range(NSA_KV_HEADS):
            f0, f1, f2 = _feature_lanes(hk)
            qh, _ = _q_rows_f32(q_ref, hk, nq)
            slope, tq = _row_consts(hk, nq, t0)
            qlo = jnp.where(lane == f0, slope * SEL_BLOCK,
                            jnp.where(lane == f1, slope, jnp.where(lane == f2, slope * CMP_STRIDE, qh))).astype(BF16)
            s = lax.dot_general(qlo, kca[hk], (((1,), (1,)), ((), ())), preferred_element_type=F32)
            cend = lax.broadcasted_iota(jnp.int32, (1, nc), 1) * CMP_STRIDE + (2 * CMP_STRIDE - 1)
            p, _, l_c = _softmax_parts(s, cend <= tq)
            p = p / jnp.maximum(l_c, 1e-30)
            oc_s[hk] = jnp.dot(p.astype(BF16), vca[hk], preferred_element_type=F32)
            psum = p[0:nq]
            for g in range(1, NSA_GROUP):
                psum = psum + p[g * nq:(g + 1) * nq]
            imp_t = lax.dot_general(at_ref[...], psum, (((1,), (1,)), ((), ())), precision=HIGHEST,
                                    preferred_element_type=F32)
            sel_t = _select_t(imp_t, tq_lane, ns, n_top)
            bias = (jnp.transpose(sel_t) - 1.0) * 1e30
            qa_s[hk, :, 0:LANES] = qlo
            qa_s[hk, :, LANES:2 * LANES] = jnp.concatenate([bias] * NSA_GROUP, axis=0).astype(BF16)
            w0 = pl.multiple_of(jnp.maximum(t0 + nq - wkeys, 0), LANES)
            d = tq - (w0 + lax.broadcasted_iota(jnp.int32, (1, wkeys), 1))
            s = lax.dot_general(qlo, kwa[hk, pl.ds(w0, wkeys), :], (((1,), (1,)), ((), ())),
                                preferred_element_type=F32)
            p, _, _ = _softmax_parts(s, (d >= 0) & (d < WINDOW))
            ow = jnp.dot(p.astype(BF16), vwa[hk, pl.ds(w0, wkeys), :], preferred_element_type=F32)
            ow_s[hk] = ow / jnp.maximum(pltpu.roll(ow, NSA_DH, 1), 1e-30)
            m_s[hk] = jnp.full((rows, LANES), NEG_INF, F32)
            acc_s[hk] = jnp.zeros((rows, LANES), F32)

    k0 = pl.multiple_of(c * kchunk, kchunk)

    def flash(causal):
        for hk in range(NSA_KV_HEADS):
            s = lax.dot_general(qa_s[hk], ksa[hk, pl.ds(k0, kchunk), :], (((1,), (1,)), ((), ())),
                                preferred_element_type=F32)
            if causal:
                _, tq = _row_consts(hk, nq, t0)
                s = jnp.where(k0 + lax.broadcasted_iota(jnp.int32, (1, kchunk), 1) <= tq, s, NEG_INF)
            m_old = m_s[hk][:, 0:1]
            m_new = jnp.maximum(m_old, jnp.max(s, axis=-1, keepdims=True))
            p = jnp.exp(s - m_new).astype(BF16)
            acc_s[hk] = jnp.exp(m_old - m_new) * acc_s[hk] + jnp.dot(p, vsa[hk, pl.ds(k0, kchunk), :],
                                                                   preferred_element_type=F32)
            m_s[hk] = jnp.broadcast_to(m_new, (rows, LANES))

    @pl.when(c < last)
    def _():
        flash(False)

    @pl.when(c == last)
    def _():
        flash(True)
        gates = _sigmoid(sm_ref[0] + bg_ref[...])
        out_pairs = [None] * (NSA_HEADS // 2)
        for hk in range(NSA_KV_HEADS):
            own = (lane >= hk * NSA_DH) & (lane < (hk + 1) * NSA_DH)
            acc = acc_s[hk]
            o_s = acc / jnp.maximum(pltpu.roll(acc, NSA_DH, 1), 1e-30)
            _combine_heads(oc_s[hk], o_s, ow_s[hk], gates, gh_ref, own, hk, nq, out_pairs)
        for k, y in enumerate(out_pairs):
            o_ref[0, :, k * LANES:(k + 1) * LANES] = y


def _nsa_prompt(q, sm, k_cmp, v_cmp, ks, vs, kw, vw, bg, gh, nq):
    b, t, _ = q.shape
    nc = k_cmp.shape[1]
    ns = t // SEL_BLOCK
    assert nq == LANES and ns <= LANES and t % 512 == 0, "prompt NSA kernel: 128-query blocks, at most 128 blocks"
    a = _sel_matrix(nc, nc, LANES).T
    kchunk = min(512, t)
    wkeys = min(-(-(WINDOW + nq) // LANES) * LANES, t)
    pairs = [(i, c) for i in range(t // nq) for c in range(-(-((i + 1) * nq) // kchunk))]
    qi = jnp.asarray(np.array([p[0] for p in pairs], np.int32))
    ci = jnp.asarray(np.array([p[1] for p in pairs], np.int32))
    rows = NSA_GROUP * nq
    full = lambda n: pl.BlockSpec((1, n, LANES), lambda bi, s, qi, ci: (bi, 0, 0))
    const = lambda shape: pl.BlockSpec(shape, lambda bi, s, qi, ci: (0,) * len(shape))
    tok = lambda w: pl.BlockSpec((1, nq, w), lambda bi, s, qi, ci: (bi, qi[s], 0))
    grid_spec = pltpu.PrefetchScalarGridSpec(
        num_scalar_prefetch=2,
        grid=(b, len(pairs)),
        in_specs=[tok(4 * LANES), tok(LANES), full(nc), full(nc), full(t), full(t), full(t), full(t),
                  const(a.shape), const((1, LANES)), const((NSA_HEADS, LANES))],
        out_specs=tok(4 * LANES),
        scratch_shapes=[pltpu.VMEM((NSA_KV_HEADS, t, 2 * LANES), BF16)]
        + [pltpu.VMEM((NSA_KV_HEADS, t, LANES), BF16)] * 3
        + [pltpu.VMEM((NSA_KV_HEADS, nc, LANES), BF16)] * 2
        + [pltpu.VMEM((NSA_KV_HEADS, rows, 2 * LANES), BF16)]
        + [pltpu.VMEM((NSA_KV_HEADS, rows, LANES), F32)] * 4,
    )
    return pl.pallas_call(
        functools.partial(_nsa_prompt_kernel, nq=nq, ns=ns, n_top=min(SEL_TOPK, ns), kchunk=kchunk, wkeys=wkeys,
                          prep_rows=512),
        grid_spec=grid_spec,
        out_shape=jax.ShapeDtypeStruct((b, t, 4 * LANES), F32),
        compiler_params=_params(("arbitrary", "arbitrary")),
        name="nsa_prompt",
    )(qi, ci, q, sm, k_cmp, v_cmp, ks, vs, kw, vw, a, bg, gh)


def _nsa_sample_kernel(pt_ref, q_ref, sm_ref, kc_ref, vc_ref, ks_pool, vs_pool, ksn_ref, vsn_ref,
                       wk_ref, wv_ref, kwn_ref, vwn_ref, a_ref, bg_ref, gh_ref, o_ref,
                       kbuf, vbuf, sem, q_s, sel_s, oc_s, ow_s, m_s, l_s, acc_s,
                       *, ch, nchunks, offset, ns, n_top, kchunk):
    b = pl.program_id(0)
    j = pl.program_id(1)
    nq = q_ref.shape[1]
    rows = NSA_GROUP * nq
    w_pre = wk_ref.shape[1]

    def page_copies(s):
        page = pt_ref[b, j * ch + s]
        dst = pl.ds(s * PAGE, PAGE)
        return (pltpu.make_async_copy(ks_pool.at[page], kbuf.at[dst], sem.at[0]),
                pltpu.make_async_copy(vs_pool.at[page], vbuf.at[dst], sem.at[1]))

    for s in range(ch):
        ck, cv = page_copies(s)
        ck.start()
        cv.start()

    @pl.when(j == 0)
    def _():
        for hk in range(NSA_KV_HEADS):
            qh, _ = _q_rows(q_ref, hk, nq)
            slope, tq = _row_consts(hk, nq, offset)
            o_c, imp = _cmp_branch(qh, slope, tq, kc_ref, vc_ref, a_ref, nq)
            sel = _select(imp, tq[0:nq], ns, n_top)
            q_s[hk] = qh
            sel_s[hk] = jnp.concatenate([sel] * NSA_GROUP, axis=0).astype(BF16)
            oc_s[hk] = o_c
            pos1 = offset - w_pre + lax.broadcasted_iota(jnp.int32, (1, w_pre), 1)
            d1 = tq - pos1
            pos2 = offset + lax.broadcasted_iota(jnp.int32, (1, kwn_ref.shape[1]), 1)
            d2 = tq - pos2
            v1 = (d1 >= 0) & (d1 < WINDOW)
            v2 = (d2 >= 0) & (d2 < WINDOW)
            s1 = jnp.where(v1, _mm_nt(qh, wk_ref[0]) - slope * d1.astype(F32), NEG_INF)
            s2 = jnp.where(v2, _mm_nt(qh, kwn_ref[0]) - slope * d2.astype(F32), NEG_INF)
            mx = jnp.maximum(jnp.max(s1, axis=-1, keepdims=True), jnp.max(s2, axis=-1, keepdims=True))
            p1 = jnp.where(v1, jnp.exp(s1 - mx), 0.0)
            p2 = jnp.where(v2, jnp.exp(s2 - mx), 0.0)
            l_w = jnp.sum(p1, axis=-1, keepdims=True) + jnp.sum(p2, axis=-1, keepdims=True)
            ow_s[hk] = (_mm(p1, wv_ref[0]) + _mm(p2, vwn_ref[0])) / jnp.maximum(l_w, 1e-30)
            m_s[hk] = jnp.full((rows, LANES), NEG_INF, F32)
            l_s[hk] = jnp.zeros((rows, LANES), F32)
            acc_s[hk] = jnp.zeros((rows, LANES), F32)

    for s in range(ch):
        ck, cv = page_copies(s)
        ck.wait()
        cv.wait()

    base = j * (ch * PAGE)
    for hk in range(NSA_KV_HEADS):
        qh = q_s[hk]
        slope, tq = _row_consts(hk, nq, offset)
        carry = (m_s[hk][:, 0:1], l_s[hk][:, 0:1], acc_s[hk])
        for c in range(ch * PAGE // kchunk):
            pos = base + c * kchunk + lax.broadcasted_iota(jnp.int32, (1, kchunk), 1)
            carry = _flash_step(qh, kbuf[pl.ds(c * kchunk, kchunk), :], vbuf[pl.ds(c * kchunk, kchunk), :],
                                pos, tq, slope, sel_s[hk], carry)
        m_s[hk] = jnp.broadcast_to(carry[0], (rows, LANES))
        l_s[hk] = jnp.broadcast_to(carry[1], (rows, LANES))
        acc_s[hk] = carry[2]

    @pl.when(j == nchunks - 1)
    def _():
        gates = _sigmoid(sm_ref[0] + bg_ref[...])
        lane = lax.broadcasted_iota(jnp.int32, (1, LANES), 1)
        out_pairs = [None] * (NSA_HEADS // 2)
        for hk in range(NSA_KV_HEADS):
            own = (lane >= hk * NSA_DH) & (lane < (hk + 1) * NSA_DH)
            qh = q_s[hk]
            slope, tq = _row_consts(hk, nq, offset)
            pos = offset + lax.broadcasted_iota(jnp.int32, (1, ksn_ref.shape[1]), 1)
            carry = (m_s[hk][:, 0:1], l_s[hk][:, 0:1], acc_s[hk])
            _, l_f, acc_f = _flash_step(qh, ksn_ref[0], vsn_ref[0], pos, tq, slope, sel_s[hk], carry)
            o_s = acc_f / jnp.maximum(l_f, 1e-30)
            _combine_heads(oc_s[hk], o_s, ow_s[hk], gates, gh_ref, own, hk, nq, out_pairs)
        for k, y in enumerate(out_pairs):
            o_ref[0, :, k * LANES:(k + 1) * LANES] = y


def _nsa_sample(page_table, q, sm, k_cmp, v_cmp, ks_pool, vs_pool, ks_new, vs_new, win_k, win_v, kw_new, vw_new,
                bg, gh, nc_valid, ch):
    b, nq, _ = q.shape
    npg = page_table.shape[1]
    nchunks = npg // ch
    offset = npg * PAGE
    ncp = k_cmp.shape[1]
    ns = nc_valid // 4
    nsp = -(-ns // LANES) * LANES
    a = _sel_matrix(ncp, nc_valid, nsp)
    rows = NSA_GROUP * nq
    kchunk = min(512, ch * PAGE)
    per_b = lambda n, w=LANES: pl.BlockSpec((1, n, w), lambda bi, j, t: (bi, 0, 0))
    const = lambda shape: pl.BlockSpec(shape, lambda bi, j, t: (0,) * len(shape))
    grid_spec = pltpu.PrefetchScalarGridSpec(
        num_scalar_prefetch=1,
        grid=(b, nchunks),
        in_specs=[per_b(nq, 4 * LANES), per_b(nq), per_b(ncp), per_b(ncp),
                  pl.BlockSpec(memory_space=pl.ANY), pl.BlockSpec(memory_space=pl.ANY),
                  per_b(PAGE), per_b(PAGE), per_b(win_k.shape[1]), per_b(win_v.shape[1]), per_b(PAGE), per_b(PAGE),
                  const(a.shape), const((1, LANES)), const((NSA_HEADS, LANES))],
        out_specs=per_b(nq, 4 * LANES),
        scratch_shapes=[pltpu.VMEM((ch * PAGE, LANES), F32), pltpu.VMEM((ch * PAGE, LANES), F32),
                        pltpu.SemaphoreType.DMA((2,)),
                        pltpu.VMEM((NSA_KV_HEADS, rows, LANES), BF16),
                        pltpu.VMEM((NSA_KV_HEADS, rows, nsp), BF16),
                        pltpu.VMEM((NSA_KV_HEADS, rows, LANES), F32),
                        pltpu.VMEM((NSA_KV_HEADS, rows, LANES), F32),
                        pltpu.VMEM((NSA_KV_HEADS, rows, LANES), F32),
                        pltpu.VMEM((NSA_KV_HEADS, rows, LANES), F32),
                        pltpu.VMEM((NSA_KV_HEADS, rows, LANES), F32)],
    )
    return pl.pallas_call(
        functools.partial(_nsa_sample_kernel, ch=ch, nchunks=nchunks, offset=offset, ns=ns,
                          n_top=min(SEL_TOPK, ns), kchunk=kchunk),
        grid_spec=grid_spec,
        out_shape=jax.ShapeDtypeStruct((b, nq, 4 * LANES), F32),
        compiler_params=_params(("arbitrary", "arbitrary")),
        name="nsa_sample",
    )(page_table, q, sm, k_cmp, v_cmp, ks_pool, vs_pool, ks_new, vs_new, win_k, win_v, kw_new, vw_new, a, bg, gh)


def _mlstm_kernel(xm_ref, vm_ref, om_ref, sm_ref, smt_ref, conv0_ref, cw_ref, cb_ref, wq_ref, wk_ref,
                  bcol_ref, brow_ref, gh_ref, c0_ref, n0_ref, m0_ref,
                  hm_ref, c_ref, n_ref, m_ref, xs_ref, *, L, t_valid):
    c = pl.program_id(1)
    dh = MLSTM_DH

    @pl.when(c == 0)
    def _():
        xs_ref[0:8, :] = jnp.zeros((8, xs_ref.shape[1]), F32)
        xs_ref[5:8, :] = conv0_ref[0]
        c_ref[...] = c0_ref[...]
        n_ref[...] = n0_ref[...]
        m_ref[...] = m0_ref[...]

    xs_ref[8:8 + L, :] = xm_ref[0]
    xc = cb_ref[...]
    for jj in range(4):
        xc = xc + cw_ref[jj:jj + 1, :] * xs_ref[pl.ds(5 + jj, L), :]
    xc = xc * _sigmoid(xc)
    xs_ref[0:8, :] = xs_ref[L:L + 8, :]

    pre_col = sm_ref[0] + brow_ref[...]
    pre_row = smt_ref[0, 0] + bcol_ref[...]
    lf_col = _log_sigmoid(pre_col)
    lf_row = _log_sigmoid(pre_row)
    if t_valid < L:
        rid = lax.broadcasted_iota(jnp.int32, (L, 1), 0) < t_valid
        cid = lax.broadcasted_iota(jnp.int32, (1, L), 1) < t_valid
        lf_col = jnp.where(rid, lf_col, 0.0)
        lf_row = jnp.where(cid, lf_row, 0.0)
        pre_col = jnp.where(rid, pre_col, NEG_INF)
        pre_row = jnp.where(cid, pre_row, NEG_INF)
    ri = lax.broadcasted_iota(jnp.int32, (L, L), 0)
    ci = lax.broadcasted_iota(jnp.int32, (L, L), 1)
    causal = ci <= ri
    bcum_col = _mm_f32(causal.astype(F32), lf_col)
    bcum_row = _mm_f32(lf_row, (ri <= ci).astype(F32))

    for h in range(MLSTM_HEADS):
        hs = slice(h * dh, (h + 1) * dh)
        xh = xc[:, hs]
        q = _mm(xh, wq_ref[h])
        k = _mm(xh, wk_ref[h]) * (dh ** -0.5)
        v = vm_ref[0, :, hs]
        bc = bcum_col[:, 28 + h:29 + h]
        ic = pre_col[:, 24 + h:25 + h]
        br = bcum_row[4 + h:5 + h, :]
        ir = pre_row[h:h + 1, :]
        mh = m_ref[0, :, h:h + 1]
        ch_ = c_ref[0, h]
        nh = n_ref[0, h:h + 1, :]
        dmat = jnp.where(causal, bc - br + ir, NEG_INF)
        inter = bc + mh
        mq = jnp.maximum(inter, jnp.max(dmat, axis=1, keepdims=True))
        a = jnp.exp(dmat - mq) * _mm_nt(q, k)
        wi = jnp.exp(inter - mq)
        num = _mm(a, v) + wi * _mm_nt(q, ch_)
        den = jnp.sum(a, axis=1, keepdims=True) + wi * jnp.sum(q * nh, axis=1, keepdims=True)
        hout = num / jnp.maximum(jnp.abs(den), jnp.exp(-mq))
        btot = bc[L - 1:L, :]
        dec_r = btot - br + ir
        dec_c = btot - bc + ic
        m_new = jnp.maximum(btot + mh, jnp.max(dec_r, axis=1, keepdims=True))
        ws_c = jnp.exp(dec_c - m_new)
        w_c = jnp.exp(btot + mh - m_new)
        c_ref[0, h] = w_c * ch_ + _mm_tn(v * ws_c, k)
        n_ref[0, h:h + 1, :] = w_c * nh + jnp.sum(k * ws_c, axis=0, keepdims=True)
        m_ref[0, :, h:h + 1] = m_new
        y = _rms(hout, gh_ref[h:h + 1, :]) * _sigmoid(om_ref[0, :, hs])
        hm_ref[0, :, hs] = y


def _mlstm(xm, vm, om, sm, conv0, conv_w, conv_b, w_qm, w_km, b_i, b_f, g_head, c0, n0, m0, L, t_valid):
    b, t, w = xm.shape
    nck = t // L
    smt = sm[:, :, 24:32].reshape(b, nck, L, 8).transpose(0, 1, 3, 2)
    brow = jnp.zeros((1, LANES), F32).at[0, 24:28].set(b_i).at[0, 28:32].set(b_f)
    bcol = jnp.concatenate([b_i, b_f]).reshape(8, 1)
    m0p = jnp.zeros((b, 1, LANES), F32).at[:, 0, :MLSTM_HEADS].set(m0)
    tok = lambda: pl.BlockSpec((1, L, w), lambda bi, c: (bi, c, 0))
    const = lambda shape: pl.BlockSpec(shape, lambda bi, c: (0,) * len(shape))
    state = lambda shape: pl.BlockSpec((1,) + shape, lambda bi, c: (bi,) + (0,) * len(shape))
    hm, c_out, n_out, m_out = pl.pallas_call(
        functools.partial(_mlstm_kernel, L=L, t_valid=t_valid),
        grid=(b, nck),
        in_specs=[tok(), tok(), tok(),
                  pl.BlockSpec((1, L, LANES), lambda bi, c: (bi, c, 0)),
                  pl.BlockSpec((1, 1, 8, L), lambda bi, c: (bi, c, 0, 0)),
                  state(conv0.shape[1:]),
                  const(conv_w.shape), const((1, w)), const(w_qm.shape), const(w_km.shape),
                  const((8, 1)), const((1, LANES)), const(g_head.shape),
                  state(c0.shape[1:]), state(n0.shape[1:]), state((1, LANES))],
        out_specs=[tok(), state(c0.shape[1:]), state(n0.shape[1:]), state((1, LANES))],
        out_shape=[jax.ShapeDtypeStruct((b, t, w), F32), jax.ShapeDtypeStruct(c0.shape, F32),
                   jax.ShapeDtypeStruct(n0.shape, F32), jax.ShapeDtypeStruct((b, 1, LANES), F32)],
        scratch_shapes=[pltpu.VMEM((L + 8, w), F32)],
        compiler_params=_params(("arbitrary", "arbitrary")),
        name="mlstm",
    )(xm, vm, om, sm, smt, conv0, conv_w, conv_b.reshape(1, w), w_qm.astype(BF16), w_km.astype(BF16),
      bcol, brow, g_head, c0, n0, m0p)
    return hm, c_out, n_out, m_out[:, 0, :MLSTM_HEADS]


def _mix_kernel(x_ref, on_ref, hm_ref, wo1_ref, wo2_ref, gxa_ref, wxq_ref, x1_ref, qx_ref):
    x1 = x_ref[...] + _mm(on_ref[...], wo1_ref[...]) + _mm(hm_ref[...], wo2_ref[...])
    x1_ref[...] = x1
    qx_ref[...] = _mm(_rms(x1, gxa_ref[...]), wxq_ref[...]) * (XA_DH ** -0.5)


def _mix(x, o_nsa, hm, w_out, g_xa, w_xq, tm):
    m, d = x.shape
    half = o_nsa.shape[1]
    row = lambda w: pl.BlockSpec((tm, w), lambda i: (i, 0))
    const = lambda shape: pl.BlockSpec(shape, lambda i: (0, 0))
    return pl.pallas_call(
        _mix_kernel,
        grid=(m // tm,),
        in_specs=[row(d), row(half), row(half), const((half, d)), const((half, d)), const((1, d)), const((d, d))],
        out_specs=[row(d), row(d)],
        out_shape=[jax.ShapeDtypeStruct((m, d), F32)] * 2,
        compiler_params=_params(("parallel",)),
        name="mix",
    )(x, o_nsa, hm, w_out[:half].astype(BF16), w_out[half:].astype(BF16), g_xa.reshape(1, d), w_xq.astype(BF16))


def _xattn_kernel(qx_ref, mk_ref, mv_ref, o_ref):
    for h in range(XA_HEADS):
        hs = slice(h * XA_DH, (h + 1) * XA_DH)
        s = _mm_nt(qx_ref[0, :, hs], mk_ref[0, :, hs])
        p = jnp.exp(s - jnp.max(s, axis=-1, keepdims=True))
        o_ref[0, :, hs] = _mm(p, mv_ref[0, :, hs]) / jnp.sum(p, axis=-1, keepdims=True)


def _xattn(qx, mem_k, mem_v, tq):
    b, t, d = qx.shape
    nm = mem_k.shape[1]
    return pl.pallas_call(
        _xattn_kernel,
        grid=(b, t // tq),
        in_specs=[pl.BlockSpec((1, tq, d), lambda bi, i: (bi, i, 0)),
                  pl.BlockSpec((1, nm, d), lambda bi, i: (bi, 0, 0)),
                  pl.BlockSpec((1, nm, d), lambda bi, i: (bi, 0, 0))],
        out_specs=pl.BlockSpec((1, tq, d), lambda bi, i: (bi, i, 0)),
        out_shape=jax.ShapeDtypeStruct((b, t, d), F32),
        compiler_params=_params(("parallel", "parallel")),
        name="xattn",
    )(qx, mem_k, mem_v)


def _ffn_kernel(x1_ref, ox_ref, wxo_ref, gf_ref, wg_ref, wu_ref, wd_ref, gfin_ref, y_ref, x2_s, h_s, acc_s):
    j = pl.program_id(1)

    @pl.when(j == 0)
    def _():
        x2 = x1_ref[...] + _mm(ox_ref[...], wxo_ref[...])
        x2_s[...] = x2
        h_s[...] = _rms(x2, gf_ref[...]).astype(BF16)
        acc_s[...] = jnp.zeros(acc_s.shape, F32)

    h = h_s[...]
    g = jnp.dot(h, wg_ref[...], preferred_element_type=F32)
    u = jnp.dot(h, wu_ref[...], preferred_element_type=F32)
    acc_s[...] += _mm(g * _sigmoid(g) * u, wd_ref[...])

    @pl.when(j == pl.num_programs(1) - 1)
    def _():
        y_ref[...] = _rms(x2_s[...] + acc_s[...], gfin_ref[...])


def _ffn(x1, ox, w_xo, g_ffn, w_gate, w_up, w_down, g_final, tm, tf):
    m, d = x1.shape
    dff = w_gate.shape[1]
    row = pl.BlockSpec((tm, d), lambda i, j: (i, 0))
    vec = pl.BlockSpec((1, d), lambda i, j: (0, 0))
    return pl.pallas_call(
        _ffn_kernel,
        grid=(m // tm, dff // tf),
        in_specs=[row, row, pl.BlockSpec((d, d), lambda i, j: (0, 0)), vec,
                  pl.BlockSpec((d, tf), lambda i, j: (0, j)), pl.BlockSpec((d, tf), lambda i, j: (0, j)),
                  pl.BlockSpec((tf, d), lambda i, j: (j, 0)), vec],
        out_specs=row,
        out_shape=jax.ShapeDtypeStruct((m, d), F32),
        scratch_shapes=[pltpu.VMEM((tm, d), F32), pltpu.VMEM((tm, d), BF16), pltpu.VMEM((tm, d), F32)],
        compiler_params=_params(("parallel", "arbitrary")),
        name="ffn",
    )(x1, ox, w_xo.astype(BF16), g_ffn.reshape(1, d), w_gate.astype(BF16), w_up.astype(BF16),
      w_down.astype(BF16), g_final.reshape(1, d))


def _split_w_in(w_in, nsa_w, kv_w, mlstm_w):
    cuts = np.cumsum([nsa_w] + [kv_w] * 6 + [3 * NSA_HEADS] + [mlstm_w] * 3 + [MLSTM_HEADS] * 2)
    parts = jnp.split(w_in, cuts[:-1].tolist(), axis=1)
    small = jnp.concatenate([parts[7], parts[11], parts[12]], axis=1)
    small = jnp.pad(small, ((0, 0), (0, LANES - small.shape[1])))
    ws = [parts[0]] + list(parts[1:7]) + [small] + list(parts[8:11])
    return [w.astype(BF16) for w in ws]


def _tail(x1, ox, w, tm, b, t):
    d = x1.shape[1]
    dff = w["w_gate"].shape[1]
    tf = dff // 2 if (dff // 2) % LANES == 0 else dff
    y = _ffn(x1, ox.reshape(-1, d), w["w_xo"], w["g_ffn"], w["w_gate"], w["w_up"], w["w_down"], w["g_final"], tm, tf)
    return y.reshape(b, t, d)


def _gate_consts(w):
    bg = jnp.pad(w["b_gate"], (0, LANES - w["b_gate"].shape[0])).reshape(1, LANES)
    gh = jnp.concatenate([w["g_head_nsa"], w["g_head_nsa"]], axis=1)
    return bg, gh


def _prompt_group(x, mem, w):
    b, t, d = x.shape
    m = b * t
    tm = 512
    q, kc, vc, ks, vs, kw, vw, sm, xm, vm, om = _norm_proj(x.reshape(m, d), w["g_mix"], w["w_in_parts"], tm)
    r3 = lambda a: a.reshape(b, t, a.shape[-1])
    npg = t // PAGE
    table = jnp.arange(b * npg, dtype=jnp.int32).reshape(b, npg)
    zeros_next = jnp.zeros((b, PAGE, LANES), F32)
    k_cmp = _compress(table, kc.reshape(b * npg, PAGE, LANES), zeros_next, *w["cmp_k"], ch=npg)
    v_cmp = _compress(table, vc.reshape(b * npg, PAGE, LANES), zeros_next, *w["cmp_v"], ch=npg)
    bg, gh = _gate_consts(w)
    o_nsa = _nsa_prompt(r3(q), r3(sm), k_cmp, v_cmp, r3(ks), r3(vs), r3(kw), r3(vw), bg, gh, 128)
    L = next(c for c in (256, 128, 64) if t % c == 0)
    hm, c_out, n_out, m_out = _mlstm(
        r3(xm), r3(vm), r3(om), r3(sm), jnp.zeros((b, 3, xm.shape[1]), F32), w["conv_w"], w["conv_b"],
        w["w_qm"], w["w_km"], w["b_i"], w["b_f"], w["g_head_m"],
        jnp.zeros((b, MLSTM_HEADS, MLSTM_DH, MLSTM_DH), F32), jnp.zeros((b, MLSTM_HEADS, MLSTM_DH), F32),
        jnp.zeros((b, MLSTM_HEADS), F32), L, L)
    nm = mem.shape[1]
    mk, mv = _norm_proj(mem.reshape(b * nm, d), w["g_mem"], [w["w_xk"].astype(BF16), w["w_xv"].astype(BF16)],
                        min(512, b * nm))
    x1, qx = _mix(x.reshape(m, d), o_nsa.reshape(m, -1), hm.reshape(m, -1), w["w_out"], w["g_xa"], w["w_xq"], tm)
    ox = _xattn(qx.reshape(b, t, d), mk.reshape(b, nm, d), mv.reshape(b, nm, d), 512)
    y = _tail(x1, ox, w, tm, b, t)
    kv5 = lambda a: a.reshape(1, b, t, NSA_KV_HEADS, NSA_DH)
    keep = min(WINDOW, t)
    xm3 = r3(xm)
    states = (kv5(kc), kv5(vc), kv5(ks), kv5(vs), kv5(kw)[:, :, t - keep:], kv5(vw)[:, :, t - keep:],
              c_out[None], n_out[None], m_out[None], xm3[None, :, t - 3:],
              mk.reshape(1, b, nm, XA_HEADS, XA_DH), mv.reshape(1, b, nm, XA_HEADS, XA_DH))
    return y, states


def _sample_group(x, pools, page_table, win_k, win_v, conv0, c0, n0, m0, mem_k, mem_v, w):
    b, t, d = x.shape
    m = b * t
    tp = 8
    tm = min(m, 512)
    q, kc, vc, ks, vs, kw, vw, sm, xm, vm, om = _norm_proj(x.reshape(m, d), w["g_mix"], w["w_in_parts"], tm)
    r3 = lambda a: a.reshape(b, t, a.shape[-1])
    pad_t = lambda a: jnp.pad(r3(a), ((0, 0), (0, tp - t), (0, 0)))
    npg = page_table.shape[1]
    past = npg * PAGE
    ch = min(32, npg)
    pool_kc, pool_vc, pool_ks, pool_vs = [p.reshape(p.shape[0], PAGE, LANES) for p in pools]
    nc_valid = (past + (-(-t // SEL_BLOCK)) * SEL_BLOCK) // CMP_STRIDE
    ncp = -(-nc_valid // LANES) * LANES

    pad_page = lambda a: jnp.pad(r3(a), ((0, 0), (0, PAGE - t), (0, 0)))

    def compressed(pool, new_rows, cw):
        nxt = pad_page(new_rows)
        main = _compress(page_table, pool, nxt, *cw, ch=ch)
        tail = _compress(jnp.arange(b, dtype=jnp.int32).reshape(1, b), nxt, jnp.zeros((1, PAGE, LANES), F32),
                         *cw, ch=b)
        n_tail = nc_valid - npg * CMP_PER_PAGE
        full = jnp.concatenate([main, tail.reshape(b, CMP_PER_PAGE, LANES)[:, :n_tail]], axis=1)
        return jnp.pad(full, ((0, 0), (0, ncp - nc_valid), (0, 0)))

    k_cmp = compressed(pool_kc, kc, w["cmp_k"])
    v_cmp = compressed(pool_vc, vc, w["cmp_v"])
    bg, gh = _gate_consts(w)
    o_nsa = _nsa_sample(page_table, pad_t(q), pad_t(sm), k_cmp, v_cmp, pool_ks, pool_vs, pad_page(ks), pad_page(vs),
                        win_k.reshape(b, -1, LANES), win_v.reshape(b, -1, LANES), pad_page(kw), pad_page(vw),
                        bg, gh, nc_valid, ch)[:, :t]
    hm, c_out, n_out, m_out = _mlstm(pad_t(xm), pad_t(vm), pad_t(om), pad_t(sm), conv0, w["conv_w"], w["conv_b"],
                                     w["w_qm"], w["w_km"], w["b_i"], w["b_f"], w["g_head_m"], c0, n0, m0, tp, t)
    hm = hm[:, :t]
    x1, qx = _mix(x.reshape(m, d), o_nsa.reshape(m, -1), hm.reshape(m, -1), w["w_out"], w["g_xa"], w["w_xq"], tm)
    nm = mem_k.shape[1]
    ox = _xattn(pad_t(qx), mem_k.reshape(b, nm, d), mem_v.reshape(b, nm, d), tp)[:, :t]
    y = _tail(x1, ox, w, tm, b, t)
    kv5 = lambda a: a.reshape(1, b, t, NSA_KV_HEADS, NSA_DH)
    keep = min(WINDOW, past + t)
    win5 = lambda old, new: jnp.concatenate([old, kv5(new)[0]], axis=1)[None, :, -keep:]
    conv_all = jnp.concatenate([conv0, r3(xm)], axis=1)
    states = (kv5(kc), kv5(vc), kv5(ks), kv5(vs), win5(win_k, kw), win5(win_v, vw),
              c_out[None], n_out[None], m_out[None], conv_all[None, :, -3:])
    return y, states


def kernel(x_prompt, x_sample, cache_k_cmp, cache_v_cmp, cache_k_slc, cache_v_slc, state_k_win, state_v_win, state_conv, state_C, state_n, state_m, cache_mem_k, cache_mem_v, page_table, mem_prompt, g_mix, w_in, b_gate, cmp_pe_k, cmp_w1_k, cmp_b1_k, cmp_w2_k, cmp_pe_v, cmp_w1_v, cmp_b1_v, cmp_w2_v, g_head_nsa, conv_w, conv_b, w_qm, w_km, b_i, b_f, g_head_m, w_out, g_xa, g_mem, w_xq, w_xk, w_xv, w_xo, g_ffn, w_gate, w_up, w_down, g_final):
    assert w_in.shape[0] == 1, "single-layer decoder"
    l = 0
    w = dict(g_mix=g_mix[l], b_gate=b_gate[l],
             w_in_parts=_split_w_in(w_in[l], NSA_HEADS * NSA_DH, NSA_KV_HEADS * NSA_DH, MLSTM_HEADS * MLSTM_DH),
             cmp_k=(cmp_pe_k[l], cmp_w1_k[l], cmp_b1_k[l], cmp_w2_k[l]),
             cmp_v=(cmp_pe_v[l], cmp_w1_v[l], cmp_b1_v[l], cmp_w2_v[l]),
             g_head_nsa=g_head_nsa[l], conv_w=conv_w[l], conv_b=conv_b[l], w_qm=w_qm[l], w_km=w_km[l],
             b_i=b_i[l], b_f=b_f[l], g_head_m=g_head_m[l], w_out=w_out[l], g_xa=g_xa[l], g_mem=g_mem[l],
             w_xq=w_xq[l], w_xk=w_xk[l], w_xv=w_xv[l], w_xo=w_xo[l], g_ffn=g_ffn[l], w_gate=w_gate[l],
             w_up=w_up[l], w_down=w_down[l], g_final=g_final)
    y_p, st_p = _prompt_group(x_prompt, mem_prompt, w)
    pools = (cache_k_cmp[l], cache_v_cmp[l], cache_k_slc[l], cache_v_slc[l])
    y_s, st_s = _sample_group(x_sample, pools, page_table, state_k_win[l], state_v_win[l], state_conv[l],
                              state_C[l], state_n[l], state_m[l], cache_mem_k[l], cache_mem_v[l], w)
    return (y_p, y_s) + st_p + st_s
```

```python
import functools

import numpy as np
import jax
import jax.numpy as jnp
from jax import lax
from jax.experimental import pallas as pl
from jax.experimental.pallas import tpu as pltpu

F32 = jnp.float32
BF16 = jnp.bfloat16
HIGHEST = lax.Precision.HIGHEST

EPS = 1e-6
NEG_INF = -1e30
FORCE_SCORE = 1e9
PAD_SCORE = -2e38
TAKEN_SCORE = -3e38

LANES = 128
NSA_HEADS = 8
NSA_KV_HEADS = 2
NSA_GROUP = 4
NSA_DH = 64
CMP_STRIDE = 16
SEL_BLOCK = 64
SEL_TOPK = 16
WINDOW = 512
Q_BLOCK = 64
PAGE = 128
CMP_PER_PAGE = PAGE // CMP_STRIDE
MLSTM_HEADS = 4
MLSTM_DH = 128
XA_HEADS = 4
XA_DH = 256
ALIBI = tuple(2.0 ** (-(h + 1)) for h in range(NSA_HEADS))

VMEM_LIMIT = 56 * 1024 * 1024


def _params(sem):
    return pltpu.CompilerParams(dimension_semantics=sem, vmem_limit_bytes=VMEM_LIMIT)


def _mm(a, b):
    return jnp.dot(a.astype(BF16), b.astype(BF16), preferred_element_type=F32)


def _mm_nt(a, b):
    return lax.dot_general(a.astype(BF16), b.astype(BF16), (((1,), (1,)), ((), ())),
                           preferred_element_type=F32)


def _mm_tn(a, b):
    return lax.dot_general(a.astype(BF16), b.astype(BF16), (((0,), (0,)), ((), ())),
                           preferred_element_type=F32)


def _mm_f32(a, b):
    return jnp.dot(a, b, precision=HIGHEST, preferred_element_type=F32)


def _rms(x, g):
    return x * lax.rsqrt(jnp.mean(x * x, axis=-1, keepdims=True) + EPS) * g


def _sigmoid(x):
    return 1.0 / (1.0 + jnp.exp(-x))


def _log_sigmoid(x):
    return jnp.minimum(x, 0.0) - jnp.log(1.0 + jnp.exp(-jnp.abs(x)))


def _gelu_tanh(x):
    return 0.5 * x * (1.0 + jnp.tanh(0.7978845608028654 * (x + 0.044715 * x * x * x)))


def _topk_mask(imp, k):
    col = lax.broadcasted_iota(jnp.int32, imp.shape, 1)
    sel = jnp.zeros(imp.shape, F32)
    work = imp
    for _ in range(k):
        m = jnp.max(work, axis=-1, keepdims=True)
        idx = jnp.min(jnp.where(work == m, col, jnp.int32(2 ** 30)), axis=-1, keepdims=True)
        hit = col == idx
        sel = jnp.where(hit, 1.0, sel)
        work = jnp.where(hit, TAKEN_SCORE, work)
    return sel


def _softmax_parts(s, valid):
    s = jnp.where(valid, s, NEG_INF)
    mx = jnp.max(s, axis=-1, keepdims=True)
    p = jnp.where(valid, jnp.exp(s - mx), 0.0)
    return p, mx, jnp.sum(p, axis=-1, keepdims=True)


def _q_rows_f32(q_ref, hk, nq):
    lane = lax.broadcasted_iota(jnp.int32, (1, LANES), 1)
    own = (lane >= hk * NSA_DH) & (lane < (hk + 1) * NSA_DH)
    parts = []
    for g in range(NSA_GROUP):
        h = hk * NSA_GROUP + g
        blk = q_ref[0, :, (h // 2) * LANES:(h // 2 + 1) * LANES]
        if (h % 2) != hk:
            blk = pltpu.roll(blk, NSA_DH, 1)
        parts.append(blk)
    qh = jnp.concatenate(parts, axis=0)
    return jnp.where(own, qh * (NSA_DH ** -0.5), 0.0), own


def _q_rows(q_ref, hk, nq):
    qh, own = _q_rows_f32(q_ref, hk, nq)
    return qh.astype(BF16), own


def _feature_lanes(hk):
    base = (1 - hk) * NSA_DH
    return base, base + 1, base + 2


def _select_t(imp_t, tq_lane, ns, n_top):
    blk = lax.broadcasted_iota(jnp.int32, (imp_t.shape[0], 1), 0)
    cur = tq_lane >> 6
    forced = (blk == 0) | (blk == cur) | (blk == cur - 1)
    work = jnp.where(forced, FORCE_SCORE, imp_t)
    work = jnp.where(blk * SEL_BLOCK <= tq_lane, work, NEG_INF)
    work = jnp.where(blk < ns, work, PAD_SCORE)
    sel = jnp.zeros(imp_t.shape, F32)
    for _ in range(n_top):
        m = jnp.max(work, axis=0, keepdims=True)
        idx = jnp.min(jnp.where(work == m, blk, jnp.int32(2 ** 30)), axis=0, keepdims=True)
        hit = blk == idx
        sel = jnp.where(hit, 1.0, sel)
        work = jnp.where(hit, TAKEN_SCORE, work)
    return sel


def _row_consts(hk, nq, t0):
    rows = NSA_GROUP * nq
    row = lax.broadcasted_iota(jnp.int32, (rows, 1), 0)
    slope = jnp.full((rows, 1), ALIBI[hk * NSA_GROUP + NSA_GROUP - 1], F32)
    for g in range(NSA_GROUP - 2, -1, -1):
        slope = jnp.where(row < (g + 1) * nq, ALIBI[hk * NSA_GROUP + g], slope)
    tq = t0 + (row & (nq - 1))
    return slope, tq


def _cmp_branch(qh, slope, tq, kc_ref, vc_ref, a_ref, nq):
    ncp = kc_ref.shape[1]
    s = _mm_nt(qh, kc_ref[0])
    cend = lax.broadcasted_iota(jnp.int32, (1, ncp), 1) * CMP_STRIDE + (2 * CMP_STRIDE - 1)
    d = tq - cend
    valid = d >= 0
    p, _, l = _softmax_parts(s - slope * d.astype(F32), valid)
    p = p / jnp.maximum(l, 1e-30)
    o_c = _mm(p, vc_ref[0])
    psum = p[0:nq]
    for g in range(1, NSA_GROUP):
        psum = psum + p[g * nq:(g + 1) * nq]
    return o_c, _mm_f32(psum, a_ref[...])


def _select(imp, tq_q, ns, n_top):
    nsp = imp.shape[1]
    blk = lax.broadcasted_iota(jnp.int32, (1, nsp), 1)
    cur = tq_q >> 6
    forced = (blk == 0) | (blk == cur) | (blk == cur - 1)
    imp = jnp.where(forced, FORCE_SCORE, imp)
    imp = jnp.where(blk * SEL_BLOCK <= tq_q, imp, NEG_INF)
    imp = jnp.where(blk < ns, imp, PAD_SCORE)
    return _topk_mask(imp, n_top)


def _flash_step(qh, kch, vch, pos, tq, slope, sel_rows, carry, feature_major=False):
    m, l, acc = carry
    nsp = sel_rows.shape[1]
    s = _mm(qh, kch) if feature_major else _mm_nt(qh, kch)
    d = tq - pos
    blk = lax.broadcasted_iota(jnp.int32, (nsp, 1), 0)
    expand = (blk == (pos >> 6)).astype(BF16)
    chosen = jnp.dot(sel_rows, expand, preferred_element_type=F32) > 0.5
    valid = (d >= 0) & chosen
    s = jnp.where(valid, s - slope * d.astype(F32), NEG_INF)
    m_new = jnp.maximum(m, jnp.max(s, axis=-1, keepdims=True))
    alpha = jnp.exp(m - m_new)
    p = jnp.where(valid, jnp.exp(s - m_new), 0.0)
    l = alpha * l + jnp.sum(p, axis=-1, keepdims=True)
    acc = alpha * acc + (_mm_nt(p, vch) if feature_major else _mm(p, vch))
    return m_new, l, acc


def _combine_heads(o_c, o_s, o_w, gates, gh_ref, own, hk, nq, out_pairs):
    for g in range(NSA_GROUP):
        h = hk * NSA_GROUP + g
        r = slice(g * nq, (g + 1) * nq)
        og = (gates[:, 3 * h:3 * h + 1] * o_c[r] + gates[:, 3 * h + 1:3 * h + 2] * o_s[r]
              + gates[:, 3 * h + 2:3 * h + 3] * o_w[r])
        og = jnp.where(own, og, 0.0)
        ms = jnp.sum(og * og, axis=-1, keepdims=True) * (1.0 / NSA_DH)
        y = og * lax.rsqrt(ms + EPS) * gh_ref[h:h + 1, :]
        if (h % 2) != hk:
            y = pltpu.roll(y, NSA_DH, 1)
        out_pairs[h // 2] = y if out_pairs[h // 2] is None else out_pairs[h // 2] + y


def _norm_proj_kernel(x_ref, g_ref, *refs, n, n_t):
    h = _rms(x_ref[...], g_ref[...]).astype(BF16)
    has_t = 1 if n_t else 0
    outs = refs[n + has_t:]
    for w_ref, o_ref in zip(refs[:n], outs[:n]):
        o_ref[...] = jnp.dot(h, w_ref[...], preferred_element_type=F32)
    if n_t:
        yt = lax.dot_general(refs[n][...], h, (((1,), (1,)), ((), ())), preferred_element_type=F32)
        for k, o_ref in enumerate(outs[n:]):
            o_ref[0] = yt[k * LANES:(k + 1) * LANES, :]


def _norm_proj(x, g, ws, tm, wt=None, t=None):
    m, d = x.shape
    n = len(ws)
    n_t = 0 if wt is None else wt.shape[0] // LANES
    in_specs = [pl.BlockSpec((tm, d), lambda i: (i, 0)), pl.BlockSpec((1, d), lambda i: (0, 0))]
    in_specs += [pl.BlockSpec(w.shape, lambda i: (0, 0)) for w in ws]
    out_specs = [pl.BlockSpec((tm, w.shape[1]), lambda i: (i, 0)) for w in ws]
    out_shape = [jax.ShapeDtypeStruct((m, w.shape[1]), F32) for w in ws]
    args = [x, g.reshape(1, d), *ws]
    if n_t:
        per_b = t // tm
        in_specs.append(pl.BlockSpec(wt.shape, lambda i: (0, 0)))
        out_specs += [pl.BlockSpec((1, LANES, tm), lambda i: (i // per_b, 0, i % per_b))] * n_t
        out_shape += [jax.ShapeDtypeStruct((m // t, LANES, t), F32)] * n_t
        args.append(wt)
    return pl.pallas_call(
        functools.partial(_norm_proj_kernel, n=n, n_t=n_t),
        grid=(m // tm,),
        in_specs=in_specs,
        out_specs=out_specs,
        out_shape=out_shape,
        compiler_params=_params(("parallel",)),
        name="norm_proj",
    )(*args)


def _compress_kernel(tbl_ref, pool_ref, last_ref, wr_ref, pe_ref, b1_ref, w2_ref, o_ref,
                     buf, xrow, acc, cst, sem, *, ch, nchunks, paged, feature_major):
    b = pl.program_id(0)
    j = pl.program_id(1)
    nblk = ch * CMP_PER_PAGE
    half = wr_ref.shape[2] // 2
    npairs = wr_ref.shape[0]

    n = b * nchunks + j
    cur = n % 2

    def page_copy(bb, k, buf_i, slot):
        if paged:
            src = pool_ref.at[tbl_ref[bb, k]]
        else:
            src = pool_ref.at[bb, :, pl.ds(pl.multiple_of(k * PAGE, PAGE), PAGE)]
        return pltpu.make_async_copy(src, buf.at[buf_i, slot], sem.at[buf_i, slot])

    def step_copies(bb, jj, buf_i, fn):
        for s in range(ch):
            fn(page_copy(bb, jj * ch + s, buf_i, s))

        @pl.when(jj < nchunks - 1)
        def _():
            fn(page_copy(bb, jnp.minimum((jj + 1) * ch, nchunks * ch - 1), buf_i, ch))

    @pl.when(n == 0)
    def _():
        step_copies(b, j, cur, lambda c: c.start())
        c = jnp.zeros((8, half), F32)
        for pr in range(npairs):
            c = c + _mm(jnp.broadcast_to(pe_ref[0, pr:pr + 1, :], (8, 2 * LANES)), wr_ref[pr, :, 0:half])
            c = c + _mm(jnp.broadcast_to(pe_ref[1, pr:pr + 1, :], (8, 2 * LANES)), wr_ref[pr, :, half:2 * half])
        cst[...] = c + b1_ref[...]

    @pl.when(n + 1 < pl.num_programs(0) * nchunks)
    def _():
        wrap = j + 1 == nchunks
        step_copies(jnp.where(wrap, b + 1, b), jnp.where(wrap, 0, j + 1), 1 - cur, lambda c: c.start())

    step_copies(b, j, cur, lambda c: c.wait())

    for s in range(ch):
        page = buf[cur, s]
        xrow[pl.ds(s * PAGE, PAGE), :] = page.T if feature_major else page

    @pl.when(j < nchunks - 1)
    def _():
        page = buf[cur, ch]
        xrow[pl.ds(ch * PAGE, PAGE), :] = page.T if feature_major else page

    @pl.when(j == nchunks - 1)
    def _():
        xrow[pl.ds(ch * PAGE, PAGE), :] = last_ref[0]

    for pr in range(npairs):
        x0 = xrow[pl.ds(2 * pr, nblk + 8, stride=CMP_STRIDE), :]
        x1 = xrow[pl.ds(2 * pr + 1, nblk + 8, stride=CMP_STRIDE), :]
        part = _mm(jnp.concatenate([x0, x1], axis=1), wr_ref[pr])
        if pr == 0:
            acc[...] = part
        else:
            acc[...] += part
    hid = acc[0:nblk, 0:half] + acc[pl.ds(1, nblk), half:2 * half] + cst[0:1, :]
    o_ref[0] = _mm(_gelu_tanh(hid), w2_ref[...])


def _compress(table, pool, last_next, pe, w1, b1, w2, ch, paged, feature_major):
    nb, npg = table.shape
    nchunks = npg // ch
    hid = w1.shape[1]
    dh = NSA_DH
    wa = w1[:CMP_STRIDE * dh].reshape(CMP_STRIDE, dh, hid)
    wb = w1[CMP_STRIDE * dh:].reshape(CMP_STRIDE, dh, hid)
    z = jnp.zeros_like(wa)
    wr = jnp.concatenate([jnp.concatenate([wa, z, wb, z], axis=2),
                          jnp.concatenate([z, wa, z, wb], axis=2)], axis=1).astype(BF16)
    wr = wr.reshape(CMP_STRIDE // 2, 2 * LANES, 4 * hid)
    pe1 = jnp.concatenate([pe[:CMP_STRIDE], pe[:CMP_STRIDE]], axis=1).reshape(CMP_STRIDE // 2, 2 * LANES)
    pe2 = jnp.concatenate([pe[CMP_STRIDE:], pe[CMP_STRIDE:]], axis=1).reshape(CMP_STRIDE // 2, 2 * LANES)
    pes = jnp.stack([pe1, pe2])
    b1t = jnp.concatenate([b1, b1]).reshape(1, 2 * hid)
    zz = jnp.zeros_like(w2)
    w2bd = jnp.concatenate([jnp.concatenate([w2, zz], axis=1),
                            jnp.concatenate([zz, w2], axis=1)], axis=0).astype(BF16)
    nblk = ch * CMP_PER_PAGE
    grid_spec = pltpu.PrefetchScalarGridSpec(
        num_scalar_prefetch=1,
        grid=(nb, nchunks),
        in_specs=[pl.BlockSpec(memory_space=pl.ANY),
                  pl.BlockSpec((1, PAGE, LANES), lambda b, j, t: (b, 0, 0)),
                  pl.BlockSpec(wr.shape, lambda b, j, t: (0, 0, 0)),
                  pl.BlockSpec(pes.shape, lambda b, j, t: (0, 0, 0)),
                  pl.BlockSpec(b1t.shape, lambda b, j, t: (0, 0)),
                  pl.BlockSpec(w2bd.shape, lambda b, j, t: (0, 0))],
        out_specs=pl.BlockSpec((1, nblk, LANES), lambda b, j, t: (b, j, 0)),
        scratch_shapes=[pltpu.VMEM((2, ch + 1, PAGE, LANES), F32),
                        pltpu.VMEM(((ch + 1) * PAGE, LANES), F32),
                        pltpu.VMEM((nblk + 8, 4 * hid), F32),
                        pltpu.VMEM((8, 2 * hid), F32),
                        pltpu.SemaphoreType.DMA((2, ch + 1))],
    )
    return pl.pallas_call(
        functools.partial(_compress_kernel, ch=ch, nchunks=nchunks, paged=paged, feature_major=feature_major),
        grid_spec=grid_spec,
        out_shape=jax.ShapeDtypeStruct((nb, npg * CMP_PER_PAGE, LANES), F32),
        compiler_params=_params(("arbitrary", "arbitrary")),
        name="compress",
    )(table, pool, last_next, wr, pes, b1t, w2bd)


def _sel_matrix(nc_rows, nc_valid, ns_cols):
    a = np.zeros((nc_rows, ns_cols), np.float32)
    for j in range(nc_valid // 4):
        for c, wgt in ((4 * j - 1, 0.5), (4 * j, 1.0), (4 * j + 1, 1.0), (4 * j + 2, 1.0), (4 * j + 3, 0.5)):
            if 0 <= c < nc_valid:
                a[c, j] += wgt
    return jnp.asarray(a)


def _nsa_prompt_kernel(qi_ref, ci_ref, q_ref, sm_ref, kc_ref, vc_ref, ks_ref, vs_ref, kw_ref, vw_ref, at_ref, bg_ref,
                       gh_ref, o_ref, ksa, vsa, kwa, vwa, kca, vca, qa_s, oc_s, ow_s, m_s, acc_s,
                       *, nq, ns, n_top, kchunk, wkeys, prep_rows):
    step = pl.program_id(1)
    i = qi_ref[step]
    c = ci_ref[step]
    t0 = i * nq
    rows = NSA_GROUP * nq
    t = ks_ref.shape[2]
    nc = kc_ref.shape[1]
    last = (t0 + nq + kchunk - 1) // kchunk - 1
    lane = lax.broadcasted_iota(jnp.int32, (1, LANES), 1)

    @pl.when(step == 0)
    def _():
        for hk in range(NSA_KV_HEADS):
            own = (lane >= hk * NSA_DH) & (lane < (hk + 1) * NSA_DH)
            f0, f1, f2 = _feature_lanes(hk)
            cidx = lax.broadcasted_iota(jnp.int32, (nc, 1), 0).astype(F32)
            kca[hk] = jnp.where(own, kc_ref[0], jnp.where(lane == f2, cidx, 0.0)).astype(BF16)
            vca[hk] = vc_ref[0].astype(BF16)

            row = lax.broadcasted_iota(jnp.int32, (LANES, 1), 0)
            own_r = (row >= hk * NSA_DH) & (row < (hk + 1) * NSA_DH)

            def prep(r, _):
                r0 = pl.multiple_of(r * prep_rows, prep_rows)
                cs = pl.ds(r0, prep_rows)
                pos = r0 + lax.broadcasted_iota(jnp.int32, (1, prep_rows), 1)
                blk = pos >> 6
                feat = jnp.where(row == f0, blk.astype(F32), jnp.where(row == f1, (pos & 63).astype(F32), 0.0))
                ksa[hk, 0:LANES, cs] = jnp.where(own_r, ks_ref[0, :, cs], feat).astype(BF16)
                ksa[hk, LANES:2 * LANES, cs] = (row == blk).astype(BF16)
                kwa[hk, :, cs] = jnp.where(own_r, kw_ref[0, :, cs], feat).astype(BF16)
                vsa[hk, :, cs] = jnp.where(own_r, vs_ref[0, :, cs], 1.0).astype(BF16)
                vwa[hk, :, cs] = jnp.where(own_r, vw_ref[0, :, cs], 1.0).astype(BF16)
                return 0

            lax.fori_loop(0, t // prep_rows, prep, 0)

    @pl.when(c == 0)
    def _():
        tq_lane = t0 + lax.broadcasted_iota(jnp.int32, (1, nq), 1)
        for hk in range(NSA_KV_HEADS):
            f0, f1, f2 = _feature_lanes(hk)
            qh, _ = _q_rows_f32(q_ref, hk, nq)
            slope, tq = _row_consts(hk, nq, t0)
            qlo = jnp.where(lane == f0, slope * SEL_BLOCK,
                            jnp.where(lane == f1, slope, jnp.where(lane == f2, slope * CMP_STRIDE, qh))).astype(BF16)
            s = lax.dot_general(qlo, kca[hk], (((1,), (1,)), ((), ())), preferred_element_type=F32)
            cend = lax.broadcasted_iota(jnp.int32, (1, nc), 1) * CMP_STRIDE + (2 * CMP_STRIDE - 1)
            p, _, l_c = _softmax_parts(s, cend <= tq)
            p = p / jnp.maximum(l_c, 1e-30)
            oc_s[hk] = jnp.dot(p.astype(BF16), vca[hk], preferred_element_type=F32)
            psum = p[0:nq]
            for g in range(1, NSA_GROUP):
                psum = psum + p[g * nq:(g + 1) * nq]
            imp_t = lax.dot_general(at_ref[...], psum, (((1,), (1,)), ((), ())), precision=HIGHEST,
                                    preferred_element_type=F32)
            sel_t = _select_t(imp_t, tq_lane, ns, n_top)
            bias = (jnp.transpose(sel_t) - 1.0) * 1e30
            qa_s[hk, :, 0:LANES] = qlo
            qa_s[hk, :, LANES:2 * LANES] = jnp.concatenate([bias] * NSA_GROUP, axis=0).astype(BF16)
            w0 = pl.multiple_of(jnp.maximum(t0 + nq - wkeys, 0), LANES)
            d = tq - (w0 + lax.broadcasted_iota(jnp.int32, (1, wkeys), 1))
            s = jnp.dot(qlo, kwa[hk, :, pl.ds(w0, wkeys)], preferred_element_type=F32)
            p, _, _ = _softmax_parts(s, (d >= 0) & (d < WINDOW))
            ow = lax.dot_general(p.astype(BF16), vwa[hk, :, pl.ds(w0, wkeys)], (((1,), (1,)), ((), ())),
                                 preferred_element_type=F32)
            ow_s[hk] = ow / jnp.maximum(pltpu.roll(ow, NSA_DH, 1), 1e-30)
            m_s[hk] = jnp.full((rows, LANES), NEG_INF, F32)
            acc_s[hk] = jnp.zeros((rows, LANES), F32)

    k0 = pl.multiple_of(c * kchunk, kchunk)

    def flash(causal):
        for hk in range(NSA_KV_HEADS):
            s = jnp.dot(qa_s[hk], ksa[hk, :, pl.ds(k0, kchunk)], preferred_element_type=F32)
            if causal:
                _, tq = _row_consts(hk, nq, t0)
                s = jnp.where(k0 + lax.broadcasted_iota(jnp.int32, (1, kchunk), 1) <= tq, s, NEG_INF)
            m_old = m_s[hk][:, 0:1]
            m_new = jnp.maximum(m_old, jnp.max(s, axis=-1, keepdims=True))
            p = jnp.exp(s - m_new).astype(BF16)
            acc_s[hk] = jnp.exp(m_old - m_new) * acc_s[hk] + lax.dot_general(
                p, vsa[hk, :, pl.ds(k0, kchunk)], (((1,), (1,)), ((), ())), preferred_element_type=F32)
            m_s[hk] = jnp.broadcast_to(m_new, (rows, LANES))

    @pl.when(c < last)
    def _():
        flash(False)

    @pl.when(c == last)
    def _():
        flash(True)
        gates = _sigmoid(sm_ref[0] + bg_ref[...])
        out_pairs = [None] * (NSA_HEADS // 2)
        for hk in range(NSA_KV_HEADS):
            own = (lane >= hk * NSA_DH) & (lane < (hk + 1) * NSA_DH)
            acc = acc_s[hk]
            o_s = acc / jnp.maximum(pltpu.roll(acc, NSA_DH, 1), 1e-30)
            _combine_heads(oc_s[hk], o_s, ow_s[hk], gates, gh_ref, own, hk, nq, out_pairs)
        for k, y in enumerate(out_pairs):
            o_ref[0, :, k * LANES:(k + 1) * LANES] = y


def _nsa_prompt(q, sm, k_cmp, v_cmp, ks, vs, kw, vw, bg, gh, nq):
    b, t, _ = q.shape
    nc = k_cmp.shape[1]
    ns = t // SEL_BLOCK
    assert nq == LANES and ns <= LANES and t % 512 == 0, "prompt NSA kernel: 128-query blocks, at most 128 blocks"
    a = _sel_matrix(nc, nc, LANES).T
    kchunk = min(512, t)
    wkeys = min(-(-(WINDOW + nq) // LANES) * LANES, t)
    pairs = [(i, c) for i in range(t // nq) for c in range(-(-((i + 1) * nq) // kchunk))]
    qi = jnp.asarray(np.array([p[0] for p in pairs], np.int32))
    ci = jnp.asarray(np.array([p[1] for p in pairs], np.int32))
    rows = NSA_GROUP * nq
    full = lambda n: pl.BlockSpec((1, n, LANES), lambda bi, s, qi, ci: (bi, 0, 0))
    full_t = pl.BlockSpec((1, LANES, t), lambda bi, s, qi, ci: (bi, 0, 0))
    const = lambda shape: pl.BlockSpec(shape, lambda bi, s, qi, ci: (0,) * len(shape))
    tok = lambda w: pl.BlockSpec((1, nq, w), lambda bi, s, qi, ci: (bi, qi[s], 0))
    grid_spec = pltpu.PrefetchScalarGridSpec(
        num_scalar_prefetch=2,
        grid=(b, len(pairs)),
        in_specs=[tok(4 * LANES), tok(LANES), full(nc), full(nc), full_t, full_t, full_t, full_t,
                  const(a.shape), const((1, LANES)), const((NSA_HEADS, LANES))],
        out_specs=tok(4 * LANES),
        scratch_shapes=[pltpu.VMEM((NSA_KV_HEADS, 2 * LANES, t), BF16)]
        + [pltpu.VMEM((NSA_KV_HEADS, LANES, t), BF16)] * 3
        + [pltpu.VMEM((NSA_KV_HEADS, nc, LANES), BF16)] * 2
        + [pltpu.VMEM((NSA_KV_HEADS, rows, 2 * LANES), BF16)]
        + [pltpu.VMEM((NSA_KV_HEADS, rows, LANES), F32)] * 4,
    )
    return pl.pallas_call(
        functools.partial(_nsa_prompt_kernel, nq=nq, ns=ns, n_top=min(SEL_TOPK, ns), kchunk=kchunk, wkeys=wkeys,
                          prep_rows=512),
        grid_spec=grid_spec,
        out_shape=jax.ShapeDtypeStruct((b, t, 4 * LANES), F32),
        compiler_params=_params(("arbitrary", "arbitrary")),
        name="nsa_prompt",
    )(qi, ci, q, sm, k_cmp, v_cmp, ks, vs, kw, vw, a, bg, gh)


def _nsa_sample_kernel(pt_ref, q_ref, sm_ref, kc_ref, vc_ref, ks_pool, vs_pool, ksn_ref, vsn_ref,
                       wk_ref, wv_ref, kwn_ref, vwn_ref, a_ref, bg_ref, gh_ref, o_ref,
                       kbuf, vbuf, sem, q_s, sel_s, oc_s, ow_s, m_s, l_s, acc_s,
                       *, ch, nchunks, offset, ns, n_top, kchunk):
    b = pl.program_id(0)
    j = pl.program_id(1)
    nq = q_ref.shape[1]
    rows = NSA_GROUP * nq
    w_pre = wk_ref.shape[2]

    def page_copies(s):
        page = pt_ref[b, j * ch + s]
        dst = pl.ds(s * PAGE, PAGE)
        return (pltpu.make_async_copy(ks_pool.at[page], kbuf.at[:, dst], sem.at[0]),
                pltpu.make_async_copy(vs_pool.at[page], vbuf.at[:, dst], sem.at[1]))

    for s in range(ch):
        ck, cv = page_copies(s)
        ck.start()
        cv.start()

    @pl.when(j == 0)
    def _():
        for hk in range(NSA_KV_HEADS):
            qh, _ = _q_rows(q_ref, hk, nq)
            slope, tq = _row_consts(hk, nq, offset)
            o_c, imp = _cmp_branch(qh, slope, tq, kc_ref, vc_ref, a_ref, nq)
            sel = _select(imp, tq[0:nq], ns, n_top)
            q_s[hk] = qh
            sel_s[hk] = jnp.concatenate([sel] * NSA_GROUP, axis=0).astype(BF16)
            oc_s[hk] = o_c
            pos1 = offset - w_pre + lax.broadcasted_iota(jnp.int32, (1, w_pre), 1)
            d1 = tq - pos1
            pos2 = offset + lax.broadcasted_iota(jnp.int32, (1, kwn_ref.shape[1]), 1)
            d2 = tq - pos2
            v1 = (d1 >= 0) & (d1 < WINDOW)
            v2 = (d2 >= 0) & (d2 < WINDOW)
            s1 = jnp.where(v1, _mm(qh, wk_ref[0]) - slope * d1.astype(F32), NEG_INF)
            s2 = jnp.where(v2, _mm_nt(qh, kwn_ref[0]) - slope * d2.astype(F32), NEG_INF)
            mx = jnp.maximum(jnp.max(s1, axis=-1, keepdims=True), jnp.max(s2, axis=-1, keepdims=True))
            p1 = jnp.where(v1, jnp.exp(s1 - mx), 0.0)
            p2 = jnp.where(v2, jnp.exp(s2 - mx), 0.0)
            l_w = jnp.sum(p1, axis=-1, keepdims=True) + jnp.sum(p2, axis=-1, keepdims=True)
            ow_s[hk] = (_mm_nt(p1, wv_ref[0]) + _mm(p2, vwn_ref[0])) / jnp.maximum(l_w, 1e-30)
            m_s[hk] = jnp.full((rows, LANES), NEG_INF, F32)
            l_s[hk] = jnp.zeros((rows, LANES), F32)
            acc_s[hk] = jnp.zeros((rows, LANES), F32)

    for s in range(ch):
        ck, cv = page_copies(s)
        ck.wait()
        cv.wait()

    base = j * (ch * PAGE)
    for hk in range(NSA_KV_HEADS):
        qh = q_s[hk]
        slope, tq = _row_consts(hk, nq, offset)
        carry = (m_s[hk][:, 0:1], l_s[hk][:, 0:1], acc_s[hk])
        for c in range(ch * PAGE // kchunk):
            pos = base + c * kchunk + lax.broadcasted_iota(jnp.int32, (1, kchunk), 1)
            carry = _flash_step(qh, kbuf[:, pl.ds(c * kchunk, kchunk)], vbuf[:, pl.ds(c * kchunk, kchunk)],
                                pos, tq, slope, sel_s[hk], carry, feature_major=True)
        m_s[hk] = jnp.broadcast_to(carry[0], (rows, LANES))
        l_s[hk] = jnp.broadcast_to(carry[1], (rows, LANES))
        acc_s[hk] = carry[2]

    @pl.when(j == nchunks - 1)
    def _():
        gates = _sigmoid(sm_ref[0] + bg_ref[...])
        lane = lax.broadcasted_iota(jnp.int32, (1, LANES), 1)
        out_pairs = [None] * (NSA_HEADS // 2)
        for hk in range(NSA_KV_HEADS):
            own = (lane >= hk * NSA_DH) & (lane < (hk + 1) * NSA_DH)
            qh = q_s[hk]
            slope, tq = _row_consts(hk, nq, offset)
            pos = offset + lax.broadcasted_iota(jnp.int32, (1, ksn_ref.shape[1]), 1)
            carry = (m_s[hk][:, 0:1], l_s[hk][:, 0:1], acc_s[hk])
            _, l_f, acc_f = _flash_step(qh, ksn_ref[0], vsn_ref[0], pos, tq, slope, sel_s[hk], carry)
            o_s = acc_f / jnp.maximum(l_f, 1e-30)
            _combine_heads(oc_s[hk], o_s, ow_s[hk], gates, gh_ref, own, hk, nq, out_pairs)
        for k, y in enumerate(out_pairs):
            o_ref[0, :, k * LANES:(k + 1) * LANES] = y


def _nsa_sample(page_table, q, sm, k_cmp, v_cmp, ks_pool, vs_pool, ks_new, vs_new, win_k, win_v, kw_new, vw_new,
                bg, gh, nc_valid, ch):
    b, nq, _ = q.shape
    npg = page_table.shape[1]
    nchunks = npg // ch
    offset = npg * PAGE
    ncp = k_cmp.shape[1]
    ns = nc_valid // 4
    nsp = -(-ns // LANES) * LANES
    a = _sel_matrix(ncp, nc_valid, nsp)
    rows = NSA_GROUP * nq
    kchunk = min(512, ch * PAGE)
    per_b = lambda n, w=LANES: pl.BlockSpec((1, n, w), lambda bi, j, t: (bi, 0, 0))
    const = lambda shape: pl.BlockSpec(shape, lambda bi, j, t: (0,) * len(shape))
    grid_spec = pltpu.PrefetchScalarGridSpec(
        num_scalar_prefetch=1,
        grid=(b, nchunks),
        in_specs=[per_b(nq, 4 * LANES), per_b(nq), per_b(ncp), per_b(ncp),
                  pl.BlockSpec(memory_space=pl.ANY), pl.BlockSpec(memory_space=pl.ANY),
                  per_b(PAGE), per_b(PAGE), per_b(LANES, win_k.shape[2]), per_b(LANES, win_v.shape[2]), per_b(PAGE), per_b(PAGE),
                  const(a.shape), const((1, LANES)), const((NSA_HEADS, LANES))],
        out_specs=per_b(nq, 4 * LANES),
        scratch_shapes=[pltpu.VMEM((LANES, ch * PAGE), F32), pltpu.VMEM((LANES, ch * PAGE), F32),
                        pltpu.SemaphoreType.DMA((2,)),
                        pltpu.VMEM((NSA_KV_HEADS, rows, LANES), BF16),
                        pltpu.VMEM((NSA_KV_HEADS, rows, nsp), BF16),
                        pltpu.VMEM((NSA_KV_HEADS, rows, LANES), F32),
                        pltpu.VMEM((NSA_KV_HEADS, rows, LANES), F32),
                        pltpu.VMEM((NSA_KV_HEADS, rows, LANES), F32),
                        pltpu.VMEM((NSA_KV_HEADS, rows, LANES), F32),
                        pltpu.VMEM((NSA_KV_HEADS, rows, LANES), F32)],
    )
    return pl.pallas_call(
        functools.partial(_nsa_sample_kernel, ch=ch, nchunks=nchunks, offset=offset, ns=ns,
                          n_top=min(SEL_TOPK, ns), kchunk=kchunk),
        grid_spec=grid_spec,
        out_shape=jax.ShapeDtypeStruct((b, nq, 4 * LANES), F32),
        compiler_params=_params(("arbitrary", "arbitrary")),
        name="nsa_sample",
    )(page_table, q, sm, k_cmp, v_cmp, ks_pool, vs_pool, ks_new, vs_new, win_k, win_v, kw_new, vw_new, a, bg, gh)


def _mlstm_kernel(xm_ref, vm_ref, om_ref, sm_ref, smt_ref, conv0_ref, cw_ref, cb_ref, wq_ref, wk_ref,
                  bcol_ref, brow_ref, gh_ref, c0_ref, n0_ref, m0_ref,
                  hm_ref, c_ref, n_ref, m_ref, xs_ref, *, L, t_valid):
    c = pl.program_id(1)
    dh = MLSTM_DH

    @pl.when(c == 0)
    def _():
        xs_ref[0:8, :] = jnp.zeros((8, xs_ref.shape[1]), F32)
        xs_ref[5:8, :] = conv0_ref[0]
        c_ref[...] = c0_ref[...]
        n_ref[...] = n0_ref[...]
        m_ref[...] = m0_ref[...]

    xs_ref[8:8 + L, :] = xm_ref[0]
    xc = cb_ref[...]
    for jj in range(4):
        xc = xc + cw_ref[jj:jj + 1, :] * xs_ref[pl.ds(5 + jj, L), :]
    xc = xc * _sigmoid(xc)
    xs_ref[0:8, :] = xs_ref[L:L + 8, :]

    pre_col = sm_ref[0] + brow_ref[...]
    pre_row = smt_ref[0, 0] + bcol_ref[...]
    lf_col = _log_sigmoid(pre_col)
    lf_row = _log_sigmoid(pre_row)
    if t_valid < L:
        rid = lax.broadcasted_iota(jnp.int32, (L, 1), 0) < t_valid
        cid = lax.broadcasted_iota(jnp.int32, (1, L), 1) < t_valid
        lf_col = jnp.where(rid, lf_col, 0.0)
        lf_row = jnp.where(cid, lf_row, 0.0)
        pre_col = jnp.where(rid, pre_col, NEG_INF)
        pre_row = jnp.where(cid, pre_row, NEG_INF)
    ri = lax.broadcasted_iota(jnp.int32, (L, L), 0)
    ci = lax.broadcasted_iota(jnp.int32, (L, L), 1)
    causal = ci <= ri
    bcum_col = _mm_f32(causal.astype(F32), lf_col)
    bcum_row = _mm_f32(lf_row, (ri <= ci).astype(F32))

    for h in range(MLSTM_HEADS):
        hs = slice(h * dh, (h + 1) * dh)
        xh = xc[:, hs]
        q = _mm(xh, wq_ref[h])
        k = _mm(xh, wk_ref[h]) * (dh ** -0.5)
        v = vm_ref[0, :, hs]
        bc = bcum_col[:, 28 + h:29 + h]
        ic = pre_col[:, 24 + h:25 + h]
        br = bcum_row[4 + h:5 + h, :]
        ir = pre_row[h:h + 1, :]
        mh = m_ref[0, :, h:h + 1]
        ch_ = c_ref[0, h]
        nh = n_ref[0, h:h + 1, :]
        dmat = jnp.where(causal, bc - br + ir, NEG_INF)
        inter = bc + mh
        mq = jnp.maximum(inter, jnp.max(dmat, axis=1, keepdims=True))
        a = jnp.exp(dmat - mq) * _mm_nt(q, k)
        wi = jnp.exp(inter - mq)
        num = _mm(a, v) + wi * _mm_nt(q, ch_)
        den = jnp.sum(a, axis=1, keepdims=True) + wi * jnp.sum(q * nh, axis=1, keepdims=True)
        hout = num / jnp.maximum(jnp.abs(den), jnp.exp(-mq))
        btot = bc[L - 1:L, :]
        dec_r = btot - br + ir
        dec_c = btot - bc + ic
        m_new = jnp.maximum(btot + mh, jnp.max(dec_r, axis=1, keepdims=True))
        ws_c = jnp.exp(dec_c - m_new)
        w_c = jnp.exp(btot + mh - m_new)
        c_ref[0, h] = w_c * ch_ + _mm_tn(v * ws_c, k)
        n_ref[0, h:h + 1, :] = w_c * nh + jnp.sum(k * ws_c, axis=0, keepdims=True)
        m_ref[0, :, h:h + 1] = m_new
        y = _rms(hout, gh_ref[h:h + 1, :]) * _sigmoid(om_ref[0, :, hs])
        hm_ref[0, :, hs] = y


def _mlstm(xm, vm, om, sm, conv0, conv_w, conv_b, w_qm, w_km, b_i, b_f, g_head, c0, n0, m0, L, t_valid):
    b, t, w = xm.shape
    nck = t // L
    smt = sm[:, :, 24:32].reshape(b, nck, L, 8).transpose(0, 1, 3, 2)
    brow = jnp.zeros((1, LANES), F32).at[0, 24:28].set(b_i).at[0, 28:32].set(b_f)
    bcol = jnp.concatenate([b_i, b_f]).reshape(8, 1)
    m0p = jnp.zeros((b, 1, LANES), F32).at[:, 0, :MLSTM_HEADS].set(m0)
    tok = lambda: pl.BlockSpec((1, L, w), lambda bi, c: (bi, c, 0))
    const = lambda shape: pl.BlockSpec(shape, lambda bi, c: (0,) * len(shape))
    state = lambda shape: pl.BlockSpec((1,) + shape, lambda bi, c: (bi,) + (0,) * len(shape))
    hm, c_out, n_out, m_out = pl.pallas_call(
        functools.partial(_mlstm_kernel, L=L, t_valid=t_valid),
        grid=(b, nck),
        in_specs=[tok(), tok(), tok(),
                  pl.BlockSpec((1, L, LANES), lambda bi, c: (bi, c, 0)),
                  pl.BlockSpec((1, 1, 8, L), lambda bi, c: (bi, c, 0, 0)),
                  state(conv0.shape[1:]),
                  const(conv_w.shape), const((1, w)), const(w_qm.shape), const(w_km.shape),
                  const((8, 1)), const((1, LANES)), const(g_head.shape),
                  state(c0.shape[1:]), state(n0.shape[1:]), state((1, LANES))],
        out_specs=[tok(), state(c0.shape[1:]), state(n0.shape[1:]), state((1, LANES))],
        out_shape=[jax.ShapeDtypeStruct((b, t, w), F32), jax.ShapeDtypeStruct(c0.shape, F32),
                   jax.ShapeDtypeStruct(n0.shape, F32), jax.ShapeDtypeStruct((b, 1, LANES), F32)],
        scratch_shapes=[pltpu.VMEM((L + 8, w), F32)],
        compiler_params=_params(("arbitrary", "arbitrary")),
        name="mlstm",
    )(xm, vm, om, sm, smt, conv0, conv_w, conv_b.reshape(1, w), w_qm.astype(BF16), w_km.astype(BF16),
      bcol, brow, g_head, c0, n0, m0p)
    return hm, c_out, n_out, m_out[:, 0, :MLSTM_HEADS]


def _mix_kernel(x_ref, on_ref, hm_ref, wo1_ref, wo2_ref, gxa_ref, wxq_ref, x1_ref, qx_ref):
    x1 = x_ref[...] + _mm(on_ref[...], wo1_ref[...]) + _mm(hm_ref[...], wo2_ref[...])
    x1_ref[...] = x1
    qx_ref[...] = _mm(_rms(x1, gxa_ref[...]), wxq_ref[...]) * (XA_DH ** -0.5)


def _mix(x, o_nsa, hm, w_out, g_xa, w_xq, tm):
    m, d = x.shape
    half = o_nsa.shape[1]
    row = lambda w: pl.BlockSpec((tm, w), lambda i: (i, 0))
    const = lambda shape: pl.BlockSpec(shape, lambda i: (0, 0))
    return pl.pallas_call(
        _mix_kernel,
        grid=(m // tm,),
        in_specs=[row(d), row(half), row(half), const((half, d)), const((half, d)), const((1, d)), const((d, d))],
        out_specs=[row(d), row(d)],
        out_shape=[jax.ShapeDtypeStruct((m, d), F32)] * 2,
        compiler_params=_params(("parallel",)),
        name="mix",
    )(x, o_nsa, hm, w_out[:half].astype(BF16), w_out[half:].astype(BF16), g_xa.reshape(1, d), w_xq.astype(BF16))


def _xattn_kernel(qx_ref, mk_ref, mv_ref, o_ref):
    for h in range(XA_HEADS):
        hs = slice(h * XA_DH, (h + 1) * XA_DH)
        s = _mm_nt(qx_ref[0, :, hs], mk_ref[0, :, hs])
        p = jnp.exp(s - jnp.max(s, axis=-1, keepdims=True))
        o_ref[0, :, hs] = _mm(p, mv_ref[0, :, hs]) / jnp.sum(p, axis=-1, keepdims=True)


def _xattn(qx, mem_k, mem_v, tq):
    b, t, d = qx.shape
    nm = mem_k.shape[1]
    return pl.pallas_call(
        _xattn_kernel,
        grid=(b, t // tq),
        in_specs=[pl.BlockSpec((1, tq, d), lambda bi, i: (bi, i, 0)),
                  pl.BlockSpec((1, nm, d), lambda bi, i: (bi, 0, 0)),
                  pl.BlockSpec((1, nm, d), lambda bi, i: (bi, 0, 0))],
        out_specs=pl.BlockSpec((1, tq, d), lambda bi, i: (bi, i, 0)),
        out_shape=jax.ShapeDtypeStruct((b, t, d), F32),
        compiler_params=_params(("parallel", "parallel")),
        name="xattn",
    )(qx, mem_k, mem_v)


def _ffn_kernel(x1_ref, ox_ref, wxo_ref, gf_ref, wg_ref, wu_ref, wd_ref, gfin_ref, y_ref, x2_s, h_s, acc_s):
    j = pl.program_id(1)

    @pl.when(j == 0)
    def _():
        x2 = x1_ref[...] + _mm(ox_ref[...], wxo_ref[...])
        x2_s[...] = x2
        h_s[...] = _rms(x2, gf_ref[...]).astype(BF16)
        acc_s[...] = jnp.zeros(acc_s.shape, F32)

    h = h_s[...]
    g = jnp.dot(h, wg_ref[...], preferred_element_type=F32)
    u = jnp.dot(h, wu_ref[...], preferred_element_type=F32)
    acc_s[...] += _mm(g * _sigmoid(g) * u, wd_ref[...])

    @pl.when(j == pl.num_programs(1) - 1)
    def _():
        y_ref[...] = _rms(x2_s[...] + acc_s[...], gfin_ref[...])


def _ffn(x1, ox, w_xo, g_ffn, w_gate, w_up, w_down, g_final, tm, tf):
    m, d = x1.shape
    dff = w_gate.shape[1]
    row = pl.BlockSpec((tm, d), lambda i, j: (i, 0))
    vec = pl.BlockSpec((1, d), lambda i, j: (0, 0))
    return pl.pallas_call(
        _ffn_kernel,
        grid=(m // tm, dff // tf),
        in_specs=[row, row, pl.BlockSpec((d, d), lambda i, j: (0, 0)), vec,
                  pl.BlockSpec((d, tf), lambda i, j: (0, j)), pl.BlockSpec((d, tf), lambda i, j: (0, j)),
                  pl.BlockSpec((tf, d), lambda i, j: (j, 0)), vec],
        out_specs=row,
        out_shape=jax.ShapeDtypeStruct((m, d), F32),
        scratch_shapes=[pltpu.VMEM((tm, d), F32), pltpu.VMEM((tm, d), BF16), pltpu.VMEM((tm, d), F32)],
        compiler_params=_params(("parallel", "arbitrary")),
        name="ffn",
    )(x1, ox, w_xo.astype(BF16), g_ffn.reshape(1, d), w_gate.astype(BF16), w_up.astype(BF16),
      w_down.astype(BF16), g_final.reshape(1, d))


def _split_w_in(w_in, nsa_w, kv_w, mlstm_w):
    cuts = np.cumsum([nsa_w] + [kv_w] * 6 + [3 * NSA_HEADS] + [mlstm_w] * 3 + [MLSTM_HEADS] * 2)
    parts = jnp.split(w_in, cuts[:-1].tolist(), axis=1)
    small = jnp.concatenate([parts[7], parts[11], parts[12]], axis=1)
    small = jnp.pad(small, ((0, 0), (0, LANES - small.shape[1])))
    ws = [parts[0]] + list(parts[1:7]) + [small] + list(parts[8:11])
    return [w.astype(BF16) for w in ws]


def _tail(x1, ox, w, tm, b, t):
    d = x1.shape[1]
    dff = w["w_gate"].shape[1]
    tf = dff // 2 if (dff // 2) % LANES == 0 else dff
    y = _ffn(x1, ox.reshape(-1, d), w["w_xo"], w["g_ffn"], w["w_gate"], w["w_up"], w["w_down"], w["g_final"], tm, tf)
    return y.reshape(b, t, d)


def _gate_consts(w):
    bg = jnp.pad(w["b_gate"], (0, LANES - w["b_gate"].shape[0])).reshape(1, LANES)
    gh = jnp.concatenate([w["g_head_nsa"], w["g_head_nsa"]], axis=1)
    return bg, gh


def _prompt_group(x, mem, w):
    b, t, d = x.shape
    m = b * t
    tm = 512
    wp = w["w_in_parts"]
    wt = jnp.concatenate(wp[1:7], axis=1).T
    q, sm, xm, vm, om, kc, vc, ks, vs, kw, vw = _norm_proj(x.reshape(m, d), w["g_mix"], [wp[0]] + wp[7:], tm,
                                                           wt=wt, t=t)
    r3 = lambda a: a.reshape(b, t, a.shape[-1])
    npg = t // PAGE
    table = jnp.zeros((b, npg), jnp.int32)
    zeros_next = jnp.zeros((b, PAGE, LANES), F32)
    k_cmp = _compress(table, kc, zeros_next, *w["cmp_k"], ch=npg, paged=False, feature_major=True)
    v_cmp = _compress(table, vc, zeros_next, *w["cmp_v"], ch=npg, paged=False, feature_major=True)
    bg, gh = _gate_consts(w)
    o_nsa = _nsa_prompt(r3(q), r3(sm), k_cmp, v_cmp, ks, vs, kw, vw, bg, gh, 128)
    L = next(c for c in (256, 128, 64) if t % c == 0)
    hm, c_out, n_out, m_out = _mlstm(
        r3(xm), r3(vm), r3(om), r3(sm), jnp.zeros((b, 3, xm.shape[1]), F32), w["conv_w"], w["conv_b"],
        w["w_qm"], w["w_km"], w["b_i"], w["b_f"], w["g_head_m"],
        jnp.zeros((b, MLSTM_HEADS, MLSTM_DH, MLSTM_DH), F32), jnp.zeros((b, MLSTM_HEADS, MLSTM_DH), F32),
        jnp.zeros((b, MLSTM_HEADS), F32), L, L)
    nm = mem.shape[1]
    mk, mv = _norm_proj(mem.reshape(b * nm, d), w["g_mem"], [w["w_xk"].astype(BF16), w["w_xv"].astype(BF16)],
                        min(512, b * nm))
    x1, qx = _mix(x.reshape(m, d), o_nsa.reshape(m, -1), hm.reshape(m, -1), w["w_out"], w["g_xa"], w["w_xq"], tm)
    ox = _xattn(qx.reshape(b, t, d), mk.reshape(b, nm, d), mv.reshape(b, nm, d), 512)
    y = _tail(x1, ox, w, tm, b, t)
    kv5 = lambda a: a.reshape(b, NSA_KV_HEADS, NSA_DH, a.shape[2]).transpose(0, 3, 1, 2)[None]
    keep = min(WINDOW, t)
    xm3 = r3(xm)
    states = (kv5(kc), kv5(vc), kv5(ks), kv5(vs), kv5(kw[:, :, t - keep:]), kv5(vw[:, :, t - keep:]),
              c_out[None], n_out[None], m_out[None], xm3[None, :, t - 3:],
              mk.reshape(1, b, nm, XA_HEADS, XA_DH), mv.reshape(1, b, nm, XA_HEADS, XA_DH))
    return y, states


def _sample_group(x, pools, page_table, win_k, win_v, conv0, c0, n0, m0, mem_k, mem_v, w):
    b, t, d = x.shape
    m = b * t
    tp = 8
    tm = min(m, 512)
    q, kc, vc, ks, vs, kw, vw, sm, xm, vm, om = _norm_proj(x.reshape(m, d), w["g_mix"], w["w_in_parts"], tm)
    r3 = lambda a: a.reshape(b, t, a.shape[-1])
    pad_t = lambda a: jnp.pad(r3(a), ((0, 0), (0, tp - t), (0, 0)))
    npg = page_table.shape[1]
    past = npg * PAGE
    ch = min(32, npg)
    fm = lambda a: a.transpose(0, 2, 3, 1).reshape(a.shape[0], LANES, a.shape[1])
    pool_kc, pool_vc, pool_ks, pool_vs = [fm(p) for p in pools]
    win_kt, win_vt = fm(win_k), fm(win_v)
    nc_valid = (past + (-(-t // SEL_BLOCK)) * SEL_BLOCK) // CMP_STRIDE
    ncp = -(-nc_valid // LANES) * LANES

    pad_page = lambda a: jnp.pad(r3(a), ((0, 0), (0, PAGE - t), (0, 0)))

    def compressed(pool, new_rows, cw):
        nxt = pad_page(new_rows)
        main = _compress(page_table, pool, nxt, *cw, ch=ch, paged=True, feature_major=True)
        tail = _compress(jnp.arange(b, dtype=jnp.int32).reshape(1, b), nxt, jnp.zeros((1, PAGE, LANES), F32),
                         *cw, ch=b, paged=True, feature_major=False)
        n_tail = nc_valid - npg * CMP_PER_PAGE
        full = jnp.concatenate([main, tail.reshape(b, CMP_PER_PAGE, LANES)[:, :n_tail]], axis=1)
        return jnp.pad(full, ((0, 0), (0, ncp - nc_valid), (0, 0)))

    k_cmp = compressed(pool_kc, kc, w["cmp_k"])
    v_cmp = compressed(pool_vc, vc, w["cmp_v"])
    bg, gh = _gate_consts(w)
    o_nsa = _nsa_sample(page_table, pad_t(q), pad_t(sm), k_cmp, v_cmp, pool_ks, pool_vs, pad_page(ks), pad_page(vs),
                        win_kt, win_vt, pad_page(kw), pad_page(vw),
                        bg, gh, nc_valid, ch)[:, :t]
    hm, c_out, n_out, m_out = _mlstm(pad_t(xm), pad_t(vm), pad_t(om), pad_t(sm), conv0, w["conv_w"], w["conv_b"],
                                     w["w_qm"], w["w_km"], w["b_i"], w["b_f"], w["g_head_m"], c0, n0, m0, tp, t)
    hm = hm[:, :t]
    x1, qx = _mix(x.reshape(m, d), o_nsa.reshape(m, -1), hm.reshape(m, -1), w["w_out"], w["g_xa"], w["w_xq"], tm)
    nm = mem_k.shape[1]
    ox = _xattn(pad_t(qx), mem_k.reshape(b, nm, d), mem_v.reshape(b, nm, d), tp)[:, :t]
    y = _tail(x1, ox, w, tm, b, t)
    kv5 = lambda a: a.reshape(1, b, t, NSA_KV_HEADS, NSA_DH)
    keep = min(WINDOW, past + t)
    unfm = lambda a: a.reshape(b, NSA_KV_HEADS, NSA_DH, a.shape[2]).transpose(0, 3, 1, 2)[None]
    win5 = lambda old_t, new: unfm(jnp.concatenate([old_t, r3(new).transpose(0, 2, 1)], axis=2)[:, :, -keep:])
    conv_all = jnp.concatenate([conv0, r3(xm)], axis=1)
    states = (kv5(kc), kv5(vc), kv5(ks), kv5(vs), win5(win_kt, kw), win5(win_vt, vw),
              c_out[None], n_out[None], m_out[None], conv_all[None, :, -3:])
    return y, states


def kernel(x_prompt, x_sample, cache_k_cmp, cache_v_cmp, cache_k_slc, cache_v_slc, state_k_win, state_v_win, state_conv, state_C, state_n, state_m, cache_mem_k, cache_mem_v, page_table, mem_prompt, g_mix, w_in, b_gate, cmp_pe_k, cmp_w1_k, cmp_b1_k, cmp_w2_k, cmp_pe_v, cmp_w1_v, cmp_b1_v, cmp_w2_v, g_head_nsa, conv_w, conv_b, w_qm, w_km, b_i, b_f, g_head_m, w_out, g_xa, g_mem, w_xq, w_xk, w_xv, w_xo, g_ffn, w_gate, w_up, w_down, g_final):
    assert w_in.shape[0] == 1, "single-layer decoder"
    l = 0
    w = dict(g_mix=g_mix[l], b_gate=b_gate[l],
             w_in_parts=_split_w_in(w_in[l], NSA_HEADS * NSA_DH, NSA_KV_HEADS * NSA_DH, MLSTM_HEADS * MLSTM_DH),
             cmp_k=(cmp_pe_k[l], cmp_w1_k[l], cmp_b1_k[l], cmp_w2_k[l]),
             cmp_v=(cmp_pe_v[l], cmp_w1_v[l], cmp_b1_v[l], cmp_w2_v[l]),
             g_head_nsa=g_head_nsa[l], conv_w=conv_w[l], conv_b=conv_b[l], w_qm=w_qm[l], w_km=w_km[l],
             b_i=b_i[l], b_f=b_f[l], g_head_m=g_head_m[l], w_out=w_out[l], g_xa=g_xa[l], g_mem=g_mem[l],
             w_xq=w_xq[l], w_xk=w_xk[l], w_xv=w_xv[l], w_xo=w_xo[l], g_ffn=g_ffn[l], w_gate=w_gate[l],
             w_up=w_up[l], w_down=w_down[l], g_final=g_final)
    y_p, st_p = _prompt_group(x_prompt, mem_prompt, w)
    pools = (cache_k_cmp[l], cache_v_cmp[l], cache_k_slc[l], cache_v_slc[l])
    y_s, st_s = _sample_group(x_sample, pools, page_table, state_k_win[l], state_v_win[l], state_conv[l],
                              state_C[l], state_n[l], state_m[l], cache_mem_k[l], cache_mem_v[l], w)
    return (y_p, y_s) + st_p + st_s
```

```python
import functools

import numpy as np
import jax
import jax.numpy as jnp
from jax import lax
from jax.experimental import pallas as pl
from jax.experimental.pallas import tpu as pltpu

F32 = jnp.float32
BF16 = jnp.bfloat16
HIGHEST = lax.Precision.HIGHEST

EPS = 1e-6
NEG_INF = -1e30
FORCE_SCORE = 1e9
PAD_SCORE = -2e38
TAKEN_SCORE = -3e38

LANES = 128
NSA_HEADS = 8
NSA_KV_HEADS = 2
NSA_GROUP = 4
NSA_DH = 64
CMP_STRIDE = 16
SEL_BLOCK = 64
SEL_TOPK = 16
WINDOW = 512
Q_BLOCK = 64
PAGE = 128
CMP_PER_PAGE = PAGE // CMP_STRIDE
MLSTM_HEADS = 4
MLSTM_DH = 128
XA_HEADS = 4
XA_DH = 256
ALIBI = tuple(2.0 ** (-(h + 1)) for h in range(NSA_HEADS))

VMEM_LIMIT = 56 * 1024 * 1024


def _params(sem):
    return pltpu.CompilerParams(dimension_semantics=sem, vmem_limit_bytes=VMEM_LIMIT)


def _mm(a, b):
    return jnp.dot(a.astype(BF16), b.astype(BF16), preferred_element_type=F32)


def _mm_nt(a, b):
    return lax.dot_general(a.astype(BF16), b.astype(BF16), (((1,), (1,)), ((), ())),
                           preferred_element_type=F32)


def _mm_tn(a, b):
    return lax.dot_general(a.astype(BF16), b.astype(BF16), (((0,), (0,)), ((), ())),
                           preferred_element_type=F32)


def _mm_f32(a, b):
    return jnp.dot(a, b, precision=HIGHEST, preferred_element_type=F32)


def _rms(x, g):
    return x * lax.rsqrt(jnp.mean(x * x, axis=-1, keepdims=True) + EPS) * g


def _sigmoid(x):
    return 1.0 / (1.0 + jnp.exp(-x))


def _log_sigmoid(x):
    return jnp.minimum(x, 0.0) - jnp.log(1.0 + jnp.exp(-jnp.abs(x)))


def _gelu_tanh(x):
    return 0.5 * x * (1.0 + jnp.tanh(0.7978845608028654 * (x + 0.044715 * x * x * x)))


def _topk_mask(imp, k):
    col = lax.broadcasted_iota(jnp.int32, imp.shape, 1)
    sel = jnp.zeros(imp.shape, F32)
    work = imp
    for _ in range(k):
        m = jnp.max(work, axis=-1, keepdims=True)
        idx = jnp.min(jnp.where(work == m, col, jnp.int32(2 ** 30)), axis=-1, keepdims=True)
        hit = col == idx
        sel = jnp.where(hit, 1.0, sel)
        work = jnp.where(hit, TAKEN_SCORE, work)
    return sel


def _softmax_parts(s, valid):
    s = jnp.where(valid, s, NEG_INF)
    mx = jnp.max(s, axis=-1, keepdims=True)
    p = jnp.where(valid, jnp.exp(s - mx), 0.0)
    return p, mx, jnp.sum(p, axis=-1, keepdims=True)


def _q_rows_f32(q_ref, hk, nq):
    lane = lax.broadcasted_iota(jnp.int32, (1, LANES), 1)
    own = (lane >= hk * NSA_DH) & (lane < (hk + 1) * NSA_DH)
    parts = []
    for g in range(NSA_GROUP):
        h = hk * NSA_GROUP + g
        blk = q_ref[0, :, (h // 2) * LANES:(h // 2 + 1) * LANES]
        if (h % 2) != hk:
            blk = pltpu.roll(blk, NSA_DH, 1)
        parts.append(blk)
    qh = jnp.concatenate(parts, axis=0)
    return jnp.where(own, qh * (NSA_DH ** -0.5), 0.0), own


def _q_rows(q_ref, hk, nq):
    qh, own = _q_rows_f32(q_ref, hk, nq)
    return qh.astype(BF16), own


def _feature_lanes(hk):
    base = (1 - hk) * NSA_DH
    return base, base + 1, base + 2


def _select_t(imp_t, tq_lane, ns, n_top):
    blk = lax.broadcasted_iota(jnp.int32, (imp_t.shape[0], 1), 0)
    cur = tq_lane >> 6
    forced = (blk == 0) | (blk == cur) | (blk == cur - 1)
    work = jnp.where(forced, FORCE_SCORE, imp_t)
    work = jnp.where(blk * SEL_BLOCK <= tq_lane, work, NEG_INF)
    work = jnp.where(blk < ns, work, PAD_SCORE)
    sel = jnp.zeros(imp_t.shape, F32)
    for _ in range(n_top):
        m = jnp.max(work, axis=0, keepdims=True)
        idx = jnp.min(jnp.where(work == m, blk, jnp.int32(2 ** 30)), axis=0, keepdims=True)
        hit = blk == idx
        sel = jnp.where(hit, 1.0, sel)
        work = jnp.where(hit, TAKEN_SCORE, work)
    return sel


def _row_consts(hk, nq, t0):
    rows = NSA_GROUP * nq
    row = lax.broadcasted_iota(jnp.int32, (rows, 1), 0)
    slope = jnp.full((rows, 1), ALIBI[hk * NSA_GROUP + NSA_GROUP - 1], F32)
    for g in range(NSA_GROUP - 2, -1, -1):
        slope = jnp.where(row < (g + 1) * nq, ALIBI[hk * NSA_GROUP + g], slope)
    tq = t0 + (row & (nq - 1))
    return slope, tq


def _cmp_branch(qh, slope, tq, kc_ref, vc_ref, a_ref, nq):
    ncp = kc_ref.shape[1]
    s = _mm_nt(qh, kc_ref[0])
    cend = lax.broadcasted_iota(jnp.int32, (1, ncp), 1) * CMP_STRIDE + (2 * CMP_STRIDE - 1)
    d = tq - cend
    valid = d >= 0
    p, _, l = _softmax_parts(s - slope * d.astype(F32), valid)
    p = p / jnp.maximum(l, 1e-30)
    o_c = _mm(p, vc_ref[0])
    psum = p[0:nq]
    for g in range(1, NSA_GROUP):
        psum = psum + p[g * nq:(g + 1) * nq]
    return o_c, _mm_f32(psum, a_ref[...])


def _select(imp, tq_q, ns, n_top):
    nsp = imp.shape[1]
    blk = lax.broadcasted_iota(jnp.int32, (1, nsp), 1)
    cur = tq_q >> 6
    forced = (blk == 0) | (blk == cur) | (blk == cur - 1)
    imp = jnp.where(forced, FORCE_SCORE, imp)
    imp = jnp.where(blk * SEL_BLOCK <= tq_q, imp, NEG_INF)
    imp = jnp.where(blk < ns, imp, PAD_SCORE)
    return _topk_mask(imp, n_top)


def _flash_step(qh, kch, vch, pos, tq, slope, sel_rows, carry, feature_major=False):
    m, l, acc = carry
    nsp = sel_rows.shape[1]
    s = _mm(qh, kch) if feature_major else _mm_nt(qh, kch)
    d = tq - pos
    blk = lax.broadcasted_iota(jnp.int32, (nsp, 1), 0)
    expand = (blk == (pos >> 6)).astype(BF16)
    chosen = jnp.dot(sel_rows, expand, preferred_element_type=F32) > 0.5
    valid = (d >= 0) & chosen
    s = jnp.where(valid, s - slope * d.astype(F32), NEG_INF)
    m_new = jnp.maximum(m, jnp.max(s, axis=-1, keepdims=True))
    alpha = jnp.exp(m - m_new)
    p = jnp.where(valid, jnp.exp(s - m_new), 0.0)
    l = alpha * l + jnp.sum(p, axis=-1, keepdims=True)
    acc = alpha * acc + (_mm_nt(p, vch) if feature_major else _mm(p, vch))
    return m_new, l, acc


def _combine_heads(o_c, o_s, o_w, gates, gh_ref, own, hk, nq, out_pairs):
    for g in range(NSA_GROUP):
        h = hk * NSA_GROUP + g
        r = slice(g * nq, (g + 1) * nq)
        og = (gates[:, 3 * h:3 * h + 1] * o_c[r] + gates[:, 3 * h + 1:3 * h + 2] * o_s[r]
              + gates[:, 3 * h + 2:3 * h + 3] * o_w[r])
        og = jnp.where(own, og, 0.0)
        ms = jnp.sum(og * og, axis=-1, keepdims=True) * (1.0 / NSA_DH)
        y = og * lax.rsqrt(ms + EPS) * gh_ref[h:h + 1, :]
        if (h % 2) != hk:
            y = pltpu.roll(y, NSA_DH, 1)
        out_pairs[h // 2] = y if out_pairs[h // 2] is None else out_pairs[h // 2] + y


def _norm_proj_kernel(x_ref, g_ref, *refs, n, n_t):
    h = _rms(x_ref[...], g_ref[...]).astype(BF16)
    has_t = 1 if n_t else 0
    outs = refs[n + has_t:]
    for w_ref, o_ref in zip(refs[:n], outs[:n]):
        o_ref[...] = jnp.dot(h, w_ref[...], preferred_element_type=F32)
    if n_t:
        yt = lax.dot_general(refs[n][...], h, (((1,), (1,)), ((), ())), preferred_element_type=F32)
        for k, o_ref in enumerate(outs[n:]):
            o_ref[0] = yt[k * LANES:(k + 1) * LANES, :]


def _norm_proj(x, g, ws, tm, wt=None, t=None):
    m, d = x.shape
    n = len(ws)
    n_t = 0 if wt is None else wt.shape[0] // LANES
    in_specs = [pl.BlockSpec((tm, d), lambda i: (i, 0)), pl.BlockSpec((1, d), lambda i: (0, 0))]
    in_specs += [pl.BlockSpec(w.shape, lambda i: (0, 0)) for w in ws]
    out_specs = [pl.BlockSpec((tm, w.shape[1]), lambda i: (i, 0)) for w in ws]
    out_shape = [jax.ShapeDtypeStruct((m, w.shape[1]), F32) for w in ws]
    args = [x, g.reshape(1, d), *ws]
    if n_t:
        per_b = t // tm
        in_specs.append(pl.BlockSpec(wt.shape, lambda i: (0, 0)))
        out_specs += [pl.BlockSpec((1, LANES, tm), lambda i: (i // per_b, 0, i % per_b))] * n_t
        out_shape += [jax.ShapeDtypeStruct((m // t, LANES, t), F32)] * n_t
        args.append(wt)
    return pl.pallas_call(
        functools.partial(_norm_proj_kernel, n=n, n_t=n_t),
        grid=(m // tm,),
        in_specs=in_specs,
        out_specs=out_specs,
        out_shape=out_shape,
        compiler_params=_params(("parallel",)),
        name="norm_proj",
    )(*args)


def _compress_kernel(tbl_ref, pool_ref, last_ref, wr_ref, pe_ref, b1_ref, w2_ref, o_ref,
                     buf, xrow, acc, cst, sem, *, ch, nchunks, paged, feature_major):
    b = pl.program_id(0)
    j = pl.program_id(1)
    nblk = ch * CMP_PER_PAGE
    half = wr_ref.shape[2] // 2
    npairs = wr_ref.shape[0]

    n = b * nchunks + j
    cur = n % 2

    def page_copy(bb, k, buf_i, slot):
        if paged:
            src = pool_ref.at[tbl_ref[bb, k]]
        else:
            src = pool_ref.at[bb, :, pl.ds(pl.multiple_of(k * PAGE, PAGE), PAGE)]
        return pltpu.make_async_copy(src, buf.at[buf_i, slot], sem.at[buf_i, slot])

    def step_copies(bb, jj, buf_i, fn):
        for s in range(ch):
            fn(page_copy(bb, jj * ch + s, buf_i, s))

        @pl.when(jj < nchunks - 1)
        def _():
            fn(page_copy(bb, jnp.minimum((jj + 1) * ch, nchunks * ch - 1), buf_i, ch))

    @pl.when(n == 0)
    def _():
        step_copies(b, j, cur, lambda c: c.start())
        c = jnp.zeros((8, half), F32)
        for pr in range(npairs):
            c = c + _mm(jnp.broadcast_to(pe_ref[0, pr:pr + 1, :], (8, 2 * LANES)), wr_ref[pr, :, 0:half])
            c = c + _mm(jnp.broadcast_to(pe_ref[1, pr:pr + 1, :], (8, 2 * LANES)), wr_ref[pr, :, half:2 * half])
        cst[...] = c + b1_ref[...]

    @pl.when(n + 1 < pl.num_programs(0) * nchunks)
    def _():
        wrap = j + 1 == nchunks
        step_copies(jnp.where(wrap, b + 1, b), jnp.where(wrap, 0, j + 1), 1 - cur, lambda c: c.start())

    step_copies(b, j, cur, lambda c: c.wait())

    for s in range(ch):
        page = buf[cur, s]
        xrow[pl.ds(s * PAGE, PAGE), :] = page.T if feature_major else page

    @pl.when(j < nchunks - 1)
    def _():
        page = buf[cur, ch]
        xrow[pl.ds(ch * PAGE, PAGE), :] = page.T if feature_major else page

    @pl.when(j == nchunks - 1)
    def _():
        xrow[pl.ds(ch * PAGE, PAGE), :] = last_ref[0]

    for pr in range(npairs):
        x0 = xrow[pl.ds(2 * pr, nblk + 8, stride=CMP_STRIDE), :]
        x1 = xrow[pl.ds(2 * pr + 1, nblk + 8, stride=CMP_STRIDE), :]
        part = _mm(jnp.concatenate([x0, x1], axis=1), wr_ref[pr])
        if pr == 0:
            acc[...] = part
        else:
            acc[...] += part
    hid = acc[0:nblk, 0:half] + acc[pl.ds(1, nblk), half:2 * half] + cst[0:1, :]
    o_ref[0] = _mm(_gelu_tanh(hid), w2_ref[...])


def _compress(table, pool, last_next, pe, w1, b1, w2, ch, paged, feature_major):
    nb, npg = table.shape
    nchunks = npg // ch
    hid = w1.shape[1]
    dh = NSA_DH
    wa = w1[:CMP_STRIDE * dh].reshape(CMP_STRIDE, dh, hid)
    wb = w1[CMP_STRIDE * dh:].reshape(CMP_STRIDE, dh, hid)
    z = jnp.zeros_like(wa)
    wr = jnp.concatenate([jnp.concatenate([wa, z, wb, z], axis=2),
                          jnp.concatenate([z, wa, z, wb], axis=2)], axis=1).astype(BF16)
    wr = wr.reshape(CMP_STRIDE // 2, 2 * LANES, 4 * hid)
    pe1 = jnp.concatenate([pe[:CMP_STRIDE], pe[:CMP_STRIDE]], axis=1).reshape(CMP_STRIDE // 2, 2 * LANES)
    pe2 = jnp.concatenate([pe[CMP_STRIDE:], pe[CMP_STRIDE:]], axis=1).reshape(CMP_STRIDE // 2, 2 * LANES)
    pes = jnp.stack([pe1, pe2])
    b1t = jnp.concatenate([b1, b1]).reshape(1, 2 * hid)
    zz = jnp.zeros_like(w2)
    w2bd = jnp.concatenate([jnp.concatenate([w2, zz], axis=1),
                            jnp.concatenate([zz, w2], axis=1)], axis=0).astype(BF16)
    nblk = ch * CMP_PER_PAGE
    grid_spec = pltpu.PrefetchScalarGridSpec(
        num_scalar_prefetch=1,
        grid=(nb, nchunks),
        in_specs=[pl.BlockSpec(memory_space=pl.ANY),
                  pl.BlockSpec((1, PAGE, LANES), lambda b, j, t: (b, 0, 0)),
                  pl.BlockSpec(wr.shape, lambda b, j, t: (0, 0, 0)),
                  pl.BlockSpec(pes.shape, lambda b, j, t: (0, 0, 0)),
                  pl.BlockSpec(b1t.shape, lambda b, j, t: (0, 0)),
                  pl.BlockSpec(w2bd.shape, lambda b, j, t: (0, 0))],
        out_specs=pl.BlockSpec((1, nblk, LANES), lambda b, j, t: (b, j, 0)),
        scratch_shapes=[pltpu.VMEM((2, ch + 1, PAGE, LANES), F32),
                        pltpu.VMEM(((ch + 1) * PAGE, LANES), F32),
                        pltpu.VMEM((nblk + 8, 4 * hid), F32),
                        pltpu.VMEM((8, 2 * hid), F32),
                        pltpu.SemaphoreType.DMA((2, ch + 1))],
    )
    return pl.pallas_call(
        functools.partial(_compress_kernel, ch=ch, nchunks=nchunks, paged=paged, feature_major=feature_major),
        grid_spec=grid_spec,
        out_shape=jax.ShapeDtypeStruct((nb, npg * CMP_PER_PAGE, LANES), F32),
        compiler_params=_params(("arbitrary", "arbitrary")),
        name="compress",
    )(table, pool, last_next, wr, pes, b1t, w2bd)


def _sel_matrix(nc_rows, nc_valid, ns_cols):
    a = np.zeros((nc_rows, ns_cols), np.float32)
    for j in range(nc_valid // 4):
        for c, wgt in ((4 * j - 1, 0.5), (4 * j, 1.0), (4 * j + 1, 1.0), (4 * j + 2, 1.0), (4 * j + 3, 0.5)):
            if 0 <= c < nc_valid:
                a[c, j] += wgt
    return jnp.asarray(a)


def _nsa_prompt_kernel(qi_ref, ci_ref, q_ref, sm_ref, kc_ref, vc_ref, ks_ref, vs_ref, kw_ref, vw_ref, at_ref, bg_ref,
                       gh_ref, o_ref, ksa, vsa, kwa, vwa, kca, vca, qa_s, oc_s, ow_s, m_s, acc_s,
                       *, nq, ns, n_top, kchunk, wkeys, prep_rows):
    step = pl.program_id(1)
    i = qi_ref[step]
    c = ci_ref[step]
    t0 = i * nq
    rows = NSA_GROUP * nq
    t = ks_ref.shape[2]
    nc = kc_ref.shape[1]
    last = (t0 + nq + kchunk - 1) // kchunk - 1
    lane = lax.broadcasted_iota(jnp.int32, (1, LANES), 1)

    @pl.when(step == 0)
    def _():
        for hk in range(NSA_KV_HEADS):
            own = (lane >= hk * NSA_DH) & (lane < (hk + 1) * NSA_DH)
            f0, f1, f2 = _feature_lanes(hk)
            cidx = lax.broadcasted_iota(jnp.int32, (nc, 1), 0).astype(F32)
            kca[hk] = jnp.where(own, kc_ref[0], jnp.where(lane == f2, cidx, 0.0)).astype(BF16)
            vca[hk] = vc_ref[0].astype(BF16)

            row = lax.broadcasted_iota(jnp.int32, (LANES, 1), 0)
            own_r = (row >= hk * NSA_DH) & (row < (hk + 1) * NSA_DH)

            def prep(r, _):
                r0 = pl.multiple_of(r * prep_rows, prep_rows)
                cs = pl.ds(r0, prep_rows)
                pos = r0 + lax.broadcasted_iota(jnp.int32, (1, prep_rows), 1)
                blk = pos >> 6
                feat = jnp.where(row == f0, blk.astype(F32), jnp.where(row == f1, (pos & 63).astype(F32), 0.0))
                ksa[hk, 0:LANES, cs] = jnp.where(own_r, ks_ref[0, :, cs], feat).astype(BF16)
                ksa[hk, LANES:2 * LANES, cs] = (row == blk).astype(BF16)
                kwa[hk, :, cs] = jnp.where(own_r, kw_ref[0, :, cs], feat).astype(BF16)
                vsa[hk, :, cs] = jnp.where(own_r, vs_ref[0, :, cs], 1.0).astype(BF16)
                vwa[hk, :, cs] = jnp.where(own_r, vw_ref[0, :, cs], 1.0).astype(BF16)
                return 0

            lax.fori_loop(0, t // prep_rows, prep, 0)

    @pl.when(c == 0)
    def _():
        tq_lane = t0 + lax.broadcasted_iota(jnp.int32, (1, nq), 1)
        for hk in range(NSA_KV_HEADS):
            f0, f1, f2 = _feature_lanes(hk)
            qh, _ = _q_rows_f32(q_ref, hk, nq)
            slope, tq = _row_consts(hk, nq, t0)
            qlo = jnp.where(lane == f0, slope * SEL_BLOCK,
                            jnp.where(lane == f1, slope, jnp.where(lane == f2, slope * CMP_STRIDE, qh))).astype(BF16)
            s = lax.dot_general(qlo, kca[hk], (((1,), (1,)), ((), ())), preferred_element_type=F32)
            cend = lax.broadcasted_iota(jnp.int32, (1, nc), 1) * CMP_STRIDE + (2 * CMP_STRIDE - 1)
            p, _, l_c = _softmax_parts(s, cend <= tq)
            p = p / jnp.maximum(l_c, 1e-30)
            oc_s[hk] = jnp.dot(p.astype(BF16), vca[hk], preferred_element_type=F32)
            psum = p[0:nq]
            for g in range(1, NSA_GROUP):
                psum = psum + p[g * nq:(g + 1) * nq]
            imp_t = lax.dot_general(at_ref[...], psum, (((1,), (1,)), ((), ())), precision=HIGHEST,
                                    preferred_element_type=F32)
            sel_t = _select_t(imp_t, tq_lane, ns, n_top)
            bias = (jnp.transpose(sel_t) - 1.0) * 1e30
            qa_s[hk, :, 0:LANES] = qlo
            qa_s[hk, :, LANES:2 * LANES] = jnp.concatenate([bias] * NSA_GROUP, axis=0).astype(BF16)
            w0 = pl.multiple_of(jnp.maximum(t0 + nq - wkeys, 0), LANES)
            d = tq - (w0 + lax.broadcasted_iota(jnp.int32, (1, wkeys), 1))
            s = jnp.dot(qlo, kwa[hk, :, pl.ds(w0, wkeys)], preferred_element_type=F32)
            p, _, _ = _softmax_parts(s, (d >= 0) & (d < WINDOW))
            ow = lax.dot_general(p.astype(BF16), vwa[hk, :, pl.ds(w0, wkeys)], (((1,), (1,)), ((), ())),
                                 preferred_element_type=F32)
            ow_s[hk] = ow / jnp.maximum(pltpu.roll(ow, NSA_DH, 1), 1e-30)
            m_s[hk] = jnp.full((rows, LANES), NEG_INF, F32)
            acc_s[hk] = jnp.zeros((rows, LANES), F32)

    k0 = pl.multiple_of(c * kchunk, kchunk)

    def flash(causal):
        for hk in range(NSA_KV_HEADS):
            s = jnp.dot(qa_s[hk], ksa[hk, :, pl.ds(k0, kchunk)], preferred_element_type=F32)
            if causal:
                _, tq = _row_consts(hk, nq, t0)
                s = jnp.where(k0 + lax.broadcasted_iota(jnp.int32, (1, kchunk), 1) <= tq, s, NEG_INF)
            m_old = m_s[hk][:, 0:1]
            m_new = jnp.maximum(m_old, jnp.max(s, axis=-1, keepdims=True))
            p = jnp.exp(s - m_new).astype(BF16)
            acc_s[hk] = jnp.exp(m_old - m_new) * acc_s[hk] + lax.dot_general(
                p, vsa[hk, :, pl.ds(k0, kchunk)], (((1,), (1,)), ((), ())), preferred_element_type=F32)
            m_s[hk] = jnp.broadcast_to(m_new, (rows, LANES))

    @pl.when(c < last)
    def _():
        flash(False)

    @pl.when(c == last)
    def _():
        flash(True)
        gates = _sigmoid(sm_ref[0] + bg_ref[...])
        out_pairs = [None] * (NSA_HEADS // 2)
        for hk in range(NSA_KV_HEADS):
            own = (lane >= hk * NSA_DH) & (lane < (hk + 1) * NSA_DH)
            acc = acc_s[hk]
            o_s = acc / jnp.maximum(pltpu.roll(acc, NSA_DH, 1), 1e-30)
            _combine_heads(oc_s[hk], o_s, ow_s[hk], gates, gh_ref, own, hk, nq, out_pairs)
        for k, y in enumerate(out_pairs):
            o_ref[0, :, k * LANES:(k + 1) * LANES] = y


def _nsa_prompt(q, sm, k_cmp, v_cmp, ks, vs, kw, vw, bg, gh, nq):
    b, t, _ = q.shape
    nc = k_cmp.shape[1]
    ns = t // SEL_BLOCK
    assert nq == LANES and ns <= LANES and t % 512 == 0, "prompt NSA kernel: 128-query blocks, at most 128 blocks"
    a = _sel_matrix(nc, nc, LANES).T
    kchunk = min(512, t)
    wkeys = min(-(-(WINDOW + nq) // LANES) * LANES, t)
    pairs = [(i, c) for i in range(t // nq) for c in range(-(-((i + 1) * nq) // kchunk))]
    qi = jnp.asarray(np.array([p[0] for p in pairs], np.int32))
    ci = jnp.asarray(np.array([p[1] for p in pairs], np.int32))
    rows = NSA_GROUP * nq
    full = lambda n: pl.BlockSpec((1, n, LANES), lambda bi, s, qi, ci: (bi, 0, 0))
    full_t = pl.BlockSpec((1, LANES, t), lambda bi, s, qi, ci: (bi, 0, 0))
    const = lambda shape: pl.BlockSpec(shape, lambda bi, s, qi, ci: (0,) * len(shape))
    tok = lambda w: pl.BlockSpec((1, nq, w), lambda bi, s, qi, ci: (bi, qi[s], 0))
    grid_spec = pltpu.PrefetchScalarGridSpec(
        num_scalar_prefetch=2,
        grid=(b, len(pairs)),
        in_specs=[tok(4 * LANES), tok(LANES), full(nc), full(nc), full_t, full_t, full_t, full_t,
                  const(a.shape), const((1, LANES)), const((NSA_HEADS, LANES))],
        out_specs=tok(4 * LANES),
        scratch_shapes=[pltpu.VMEM((NSA_KV_HEADS, 2 * LANES, t), BF16)]
        + [pltpu.VMEM((NSA_KV_HEADS, LANES, t), BF16)] * 3
        + [pltpu.VMEM((NSA_KV_HEADS, nc, LANES), BF16)] * 2
        + [pltpu.VMEM((NSA_KV_HEADS, rows, 2 * LANES), BF16)]
        + [pltpu.VMEM((NSA_KV_HEADS, rows, LANES), F32)] * 4,
    )
    return pl.pallas_call(
        functools.partial(_nsa_prompt_kernel, nq=nq, ns=ns, n_top=min(SEL_TOPK, ns), kchunk=kchunk, wkeys=wkeys,
                          prep_rows=512),
        grid_spec=grid_spec,
        out_shape=jax.ShapeDtypeStruct((b, t, 4 * LANES), F32),
        compiler_params=_params(("arbitrary", "arbitrary")),
        name="nsa_prompt",
    )(qi, ci, q, sm, k_cmp, v_cmp, ks, vs, kw, vw, a, bg, gh)


def _head_feature_rows(hk, g, n):
    slope = ALIBI[hk * NSA_GROUP + g]
    row = lax.broadcasted_iota(jnp.int32, (NSA_DH, n), 0)
    return jnp.where(row == 0, slope * SEL_BLOCK,
                     jnp.where(row == 1, slope, jnp.where(row == 2, slope * CMP_STRIDE, 0.0)))


def _swap_halves(x):
    return jnp.concatenate([x[NSA_DH:], x[:NSA_DH]], axis=0)


def _nsa_prompt_t_kernel(qi_ref, ci_ref, q_ref, sm_ref, kc_ref, vc_ref, ks_ref, vs_ref, kw_ref, vw_ref, at_ref,
                         bg_ref, ght_ref, o_ref, ksr, kwr, vsa, vwa, kca, vct, qa_s, oc_s, ow_s, m_s, acc_s,
                         *, nq, ns, n_top, kchunk, wpad):
    step = pl.program_id(1)
    i = qi_ref[step]
    c = ci_ref[step]
    t0 = pl.multiple_of(i * nq, nq)
    cols = NSA_GROUP * nq
    t = ks_ref.shape[2]
    nc = kc_ref.shape[1]
    nsr = at_ref.shape[0]
    wkeys = wpad + nq
    last = (t0 + nq + kchunk - 1) // kchunk - 1
    lane = lax.broadcasted_iota(jnp.int32, (1, LANES), 1)
    tq_q = t0 + lax.broadcasted_iota(jnp.int32, (1, nq), 1)
    tq = jnp.concatenate([tq_q] * NSA_GROUP, axis=1)

    @pl.when(step == 0)
    def _():
        srow = lax.broadcasted_iota(jnp.int32, (LANES, 1), 0)
        cidx = lax.broadcasted_iota(jnp.int32, (nc, 1), 0).astype(F32)
        vct_f = jnp.transpose(vc_ref[0]).astype(BF16)
        for hk in range(NSA_KV_HEADS):
            own = (lane >= hk * NSA_DH) & (lane < (hk + 1) * NSA_DH)
            f0, f1, f2 = _feature_lanes(hk)
            kca[hk] = jnp.where(own, kc_ref[0], jnp.where(lane == f2, cidx, 0.0)).astype(BF16)
            vct[hk] = vct_f
            kwr[hk, 0:wpad, :] = jnp.broadcast_to(jnp.where(lane == f0, -1e30, 0.0), (wpad, LANES)).astype(BF16)
            vwa[hk, :, 0:wpad] = jnp.zeros((LANES, wpad), BF16)

        def prep(r, _):
            r0 = pl.multiple_of(r * LANES, LANES)
            cs = pl.ds(r0, LANES)
            pos = r0 + lax.broadcasted_iota(jnp.int32, (LANES, 1), 0)
            blk = pos >> 6
            ks_rows = jnp.transpose(ks_ref[0, :, cs])
            kw_rows = jnp.transpose(kw_ref[0, :, cs])
            for hk in range(NSA_KV_HEADS):
                own = (lane >= hk * NSA_DH) & (lane < (hk + 1) * NSA_DH)
                own_r = (srow >= hk * NSA_DH) & (srow < (hk + 1) * NSA_DH)
                f0, f1, f2 = _feature_lanes(hk)
                feat = jnp.where(lane == f0, blk.astype(F32), jnp.where(lane == f1, (pos & 63).astype(F32), 0.0))
                ksr[hk, cs, 0:LANES] = jnp.where(own, ks_rows, feat).astype(BF16)
                ksr[hk, cs, LANES:2 * LANES] = (lane == blk).astype(BF16)
                kwr[hk, pl.ds(wpad + r0, LANES), :] = jnp.where(own, kw_rows, feat).astype(BF16)
                vsa[hk, :, cs] = jnp.where(own_r, vs_ref[0, :, cs], 1.0).astype(BF16)
                vwa[hk, :, pl.ds(wpad + r0, LANES)] = jnp.where(own_r, vw_ref[0, :, cs], 1.0).astype(BF16)
            return 0

        lax.fori_loop(0, t // LANES, prep, 0)

    @pl.when(c == 0)
    def _():
        qt = jnp.transpose(q_ref[0]) * (NSA_DH ** -0.5)
        cend = lax.broadcasted_iota(jnp.int32, (nc, 1), 0) * CMP_STRIDE + (2 * CMP_STRIDE - 1)
        wrow = lax.broadcasted_iota(jnp.int32, (LANES, 1), 0)
        for hk in range(NSA_KV_HEADS):
            slabs = []
            for g in range(NSA_GROUP):
                h = hk * NSA_GROUP + g
                halves = [qt[h * NSA_DH:(h + 1) * NSA_DH], _head_feature_rows(hk, g, nq)]
                slabs.append(jnp.concatenate(halves if hk == 0 else halves[::-1], axis=0))
            qlo = jnp.concatenate(slabs, axis=1).astype(BF16)
            s = jnp.dot(kca[hk], qlo, preferred_element_type=F32)
            valid = cend <= tq
            s = jnp.where(valid, s, NEG_INF)
            p = jnp.where(valid, jnp.exp(s - jnp.max(s, axis=0, keepdims=True)), 0.0)
            p = p / jnp.maximum(jnp.sum(p, axis=0, keepdims=True), 1e-30)
            oc_s[hk] = jnp.dot(vct[hk], p.astype(BF16), preferred_element_type=F32)
            psum = p[:, 0:nq]
            for g in range(1, NSA_GROUP):
                psum = psum + p[:, g * nq:(g + 1) * nq]
            imp_t = jnp.dot(at_ref[...], psum, precision=HIGHEST, preferred_element_type=F32)
            bias = (_select_t(imp_t, tq_q, ns, n_top) - 1.0) * 1e30
            bias = jnp.concatenate([bias, jnp.zeros((LANES - nsr, nq), F32)], axis=0) if nsr < LANES else bias
            qa_s[hk, 0:LANES, :] = qlo
            qa_s[hk, LANES:2 * LANES, :] = jnp.concatenate([bias] * NSA_GROUP, axis=1).astype(BF16)
            s = jnp.dot(kwr[hk, pl.ds(t0, wkeys), :], qlo, preferred_element_type=F32)
            old = jnp.where(t0 - wpad + wrow > tq - WINDOW, s[0:LANES], NEG_INF)
            new = jnp.where(t0 + wrow[0:nq] <= tq, s[wkeys - nq:wkeys], NEG_INF)
            s = jnp.concatenate([old, s[LANES:wkeys - nq], new], axis=0)
            p = jnp.exp(s - jnp.max(s, axis=0, keepdims=True)).astype(BF16)
            ow = jnp.dot(vwa[hk, :, pl.ds(t0, wkeys)], p, preferred_element_type=F32)
            l_w = _swap_halves(ow)
            ow_s[hk] = ow / jnp.maximum(l_w, 1e-30)
            m_s[hk] = jnp.full((8, cols), NEG_INF, F32)
            acc_s[hk] = jnp.zeros((LANES, cols), F32)

    k0 = pl.multiple_of(c * kchunk, kchunk)

    def flash(causal):
        for hk in range(NSA_KV_HEADS):
            s = jnp.dot(ksr[hk, pl.ds(k0, kchunk), :], qa_s[hk], preferred_element_type=F32)
            if causal:
                s = jnp.where(k0 + lax.broadcasted_iota(jnp.int32, (kchunk, 1), 0) <= tq, s, NEG_INF)
            m_old = m_s[hk][0:1, :]
            m_new = jnp.maximum(m_old, jnp.max(s, axis=0, keepdims=True))
            p = jnp.exp(s - m_new).astype(BF16)
            acc_s[hk] = jnp.exp(m_old - m_new) * acc_s[hk] + jnp.dot(vsa[hk, :, pl.ds(k0, kchunk)], p,
                                                                   preferred_element_type=F32)
            m_s[hk] = jnp.broadcast_to(m_new, (8, cols))

    @pl.when(c < last)
    def _():
        flash(False)

    @pl.when(c == last)
    def _():
        flash(True)
        gates = jnp.transpose(_sigmoid(sm_ref[0] + bg_ref[...]))
        srow = lax.broadcasted_iota(jnp.int32, (LANES, 1), 0)
        halves = [None] * NSA_HEADS
        for hk in range(NSA_KV_HEADS):
            own_r = (srow >= hk * NSA_DH) & (srow < (hk + 1) * NSA_DH)
            acc = acc_s[hk]
            o_s = acc / jnp.maximum(_swap_halves(acc), 1e-30)
            o_c = oc_s[hk]
            o_w = ow_s[hk]
            for g in range(NSA_GROUP):
                h = hk * NSA_GROUP + g
                cs = slice(g * nq, (g + 1) * nq)
                og = (gates[3 * h:3 * h + 1] * o_c[:, cs] + gates[3 * h + 1:3 * h + 2] * o_s[:, cs]
                      + gates[3 * h + 2:3 * h + 3] * o_w[:, cs])
                og = jnp.where(own_r, og, 0.0)
                ms = jnp.sum(og * og, axis=0, keepdims=True) * (1.0 / NSA_DH)
                y = og * lax.rsqrt(ms + EPS) * ght_ref[:, h:h + 1]
                halves[h] = y[hk * NSA_DH:(hk + 1) * NSA_DH]
        for k in range(NSA_HEADS // 2):
            o_ref[0, :, k * LANES:(k + 1) * LANES] = jnp.transpose(
                jnp.concatenate([halves[2 * k], halves[2 * k + 1]], axis=0))


def _nsa_prompt_t(q, sm, k_cmp, v_cmp, ks, vs, kw, vw, bg, gh, nq):
    b, t, _ = q.shape
    nc = k_cmp.shape[1]
    ns = t // SEL_BLOCK
    nsr = -(-ns // 8) * 8
    assert nq == LANES and nsr <= LANES and nc <= 256 and t % 512 == 0, "prompt NSA kernel shape limits"
    at = _sel_matrix(nc, nc, nsr).T
    kchunk = min(512, t)
    wpad = WINDOW
    pairs = [(i, c) for i in range(t // nq) for c in range(-(-((i + 1) * nq) // kchunk))]
    qi = jnp.asarray(np.array([p[0] for p in pairs], np.int32))
    ci = jnp.asarray(np.array([p[1] for p in pairs], np.int32))
    cols = NSA_GROUP * nq
    full = lambda n: pl.BlockSpec((1, n, LANES), lambda bi, s, qi, ci: (bi, 0, 0))
    full_t = pl.BlockSpec((1, LANES, t), lambda bi, s, qi, ci: (bi, 0, 0))
    const = lambda shape: pl.BlockSpec(shape, lambda bi, s, qi, ci: (0,) * len(shape))
    tok = lambda w: pl.BlockSpec((1, nq, w), lambda bi, s, qi, ci: (bi, qi[s], 0))
    grid_spec = pltpu.PrefetchScalarGridSpec(
        num_scalar_prefetch=2,
        grid=(b, len(pairs)),
        in_specs=[tok(4 * LANES), tok(LANES), full(nc), full(nc), full_t, full_t, full_t, full_t,
                  const(at.shape), const((1, LANES)), const((LANES, NSA_HEADS))],
        out_specs=tok(4 * LANES),
        scratch_shapes=[pltpu.VMEM((NSA_KV_HEADS, t, 2 * LANES), BF16),
                        pltpu.VMEM((NSA_KV_HEADS, wpad + t, LANES), BF16),
                        pltpu.VMEM((NSA_KV_HEADS, LANES, t), BF16),
                        pltpu.VMEM((NSA_KV_HEADS, LANES, wpad + t), BF16),
                        pltpu.VMEM((NSA_KV_HEADS, nc, LANES), BF16),
                        pltpu.VMEM((NSA_KV_HEADS, LANES, nc), BF16),
                        pltpu.VMEM((NSA_KV_HEADS, 2 * LANES, cols), BF16),
                        pltpu.VMEM((NSA_KV_HEADS, LANES, cols), F32),
                        pltpu.VMEM((NSA_KV_HEADS, LANES, cols), F32),
                        pltpu.VMEM((NSA_KV_HEADS, 8, cols), F32),
                        pltpu.VMEM((NSA_KV_HEADS, LANES, cols), F32)],
    )
    return pl.pallas_call(
        functools.partial(_nsa_prompt_t_kernel, nq=nq, ns=ns, n_top=min(SEL_TOPK, ns), kchunk=kchunk, wpad=wpad),
        grid_spec=grid_spec,
        out_shape=jax.ShapeDtypeStruct((b, t, 4 * LANES), F32),
        compiler_params=_params(("arbitrary", "arbitrary")),
        name="nsa_prompt",
    )(qi, ci, q, sm, k_cmp, v_cmp, ks, vs, kw, vw, at, bg, gh.T)


def _nsa_sample_kernel(pt_ref, q_ref, sm_ref, kc_ref, vc_ref, ks_pool, vs_pool, ksn_ref, vsn_ref,
                       wk_ref, wv_ref, kwn_ref, vwn_ref, a_ref, bg_ref, gh_ref, o_ref,
                       kbuf, vbuf, sem, q_s, sel_s, oc_s, ow_s, m_s, l_s, acc_s,
                       *, ch, nchunks, offset, ns, n_top, kchunk):
    b = pl.program_id(0)
    j = pl.program_id(1)
    nq = q_ref.shape[1]
    rows = NSA_GROUP * nq
    w_pre = wk_ref.shape[2]

    def page_copies(s):
        page = pt_ref[b, j * ch + s]
        dst = pl.ds(s * PAGE, PAGE)
        return (pltpu.make_async_copy(ks_pool.at[page], kbuf.at[:, dst], sem.at[0]),
                pltpu.make_async_copy(vs_pool.at[page], vbuf.at[:, dst], sem.at[1]))

    for s in range(ch):
        ck, cv = page_copies(s)
        ck.start()
        cv.start()

    @pl.when(j == 0)
    def _():
        for hk in range(NSA_KV_HEADS):
            qh, _ = _q_rows(q_ref, hk, nq)
            slope, tq = _row_consts(hk, nq, offset)
            o_c, imp = _cmp_branch(qh, slope, tq, kc_ref, vc_ref, a_ref, nq)
            sel = _select(imp, tq[0:nq], ns, n_top)
            q_s[hk] = qh
            sel_s[hk] = jnp.concatenate([sel] * NSA_GROUP, axis=0).astype(BF16)
            oc_s[hk] = o_c
            pos1 = offset - w_pre + lax.broadcasted_iota(jnp.int32, (1, w_pre), 1)
            d1 = tq - pos1
            pos2 = offset + lax.broadcasted_iota(jnp.int32, (1, kwn_ref.shape[1]), 1)
            d2 = tq - pos2
            v1 = (d1 >= 0) & (d1 < WINDOW)
            v2 = (d2 >= 0) & (d2 < WINDOW)
            s1 = jnp.where(v1, _mm(qh, wk_ref[0]) - slope * d1.astype(F32), NEG_INF)
            s2 = jnp.where(v2, _mm_nt(qh, kwn_ref[0]) - slope * d2.astype(F32), NEG_INF)
            mx = jnp.maximum(jnp.max(s1, axis=-1, keepdims=True), jnp.max(s2, axis=-1, keepdims=True))
            p1 = jnp.where(v1, jnp.exp(s1 - mx), 0.0)
            p2 = jnp.where(v2, jnp.exp(s2 - mx), 0.0)
            l_w = jnp.sum(p1, axis=-1, keepdims=True) + jnp.sum(p2, axis=-1, keepdims=True)
            ow_s[hk] = (_mm_nt(p1, wv_ref[0]) + _mm(p2, vwn_ref[0])) / jnp.maximum(l_w, 1e-30)
            m_s[hk] = jnp.full((rows, LANES), NEG_INF, F32)
            l_s[hk] = jnp.zeros((rows, LANES), F32)
            acc_s[hk] = jnp.zeros((rows, LANES), F32)

    for s in range(ch):
        ck, cv = page_copies(s)
        ck.wait()
        cv.wait()

    base = j * (ch * PAGE)
    for hk in range(NSA_KV_HEADS):
        qh = q_s[hk]
        slope, tq = _row_consts(hk, nq, offset)
        carry = (m_s[hk][:, 0:1], l_s[hk][:, 0:1], acc_s[hk])
        for c in range(ch * PAGE // kchunk):
            pos = base + c * kchunk + lax.broadcasted_iota(jnp.int32, (1, kchunk), 1)
            carry = _flash_step(qh, kbuf[:, pl.ds(c * kchunk, kchunk)], vbuf[:, pl.ds(c * kchunk, kchunk)],
                                pos, tq, slope, sel_s[hk], carry, feature_major=True)
        m_s[hk] = jnp.broadcast_to(carry[0], (rows, LANES))
        l_s[hk] = jnp.broadcast_to(carry[1], (rows, LANES))
        acc_s[hk] = carry[2]

    @pl.when(j == nchunks - 1)
    def _():
        gates = _sigmoid(sm_ref[0] + bg_ref[...])
        lane = lax.broadcasted_iota(jnp.int32, (1, LANES), 1)
        out_pairs = [None] * (NSA_HEADS // 2)
        for hk in range(NSA_KV_HEADS):
            own = (lane >= hk * NSA_DH) & (lane < (hk + 1) * NSA_DH)
            qh = q_s[hk]
            slope, tq = _row_consts(hk, nq, offset)
            pos = offset + lax.broadcasted_iota(jnp.int32, (1, ksn_ref.shape[1]), 1)
            carry = (m_s[hk][:, 0:1], l_s[hk][:, 0:1], acc_s[hk])
            _, l_f, acc_f = _flash_step(qh, ksn_ref[0], vsn_ref[0], pos, tq, slope, sel_s[hk], carry)
            o_s = acc_f / jnp.maximum(l_f, 1e-30)
            _combine_heads(oc_s[hk], o_s, ow_s[hk], gates, gh_ref, own, hk, nq, out_pairs)
        for k, y in enumerate(out_pairs):
            o_ref[0, :, k * LANES:(k + 1) * LANES] = y


def _nsa_sample(page_table, q, sm, k_cmp, v_cmp, ks_pool, vs_pool, ks_new, vs_new, win_k, win_v, kw_new, vw_new,
                bg, gh, nc_valid, ch):
    b, nq, _ = q.shape
    npg = page_table.shape[1]
    nchunks = npg // ch
    offset = npg * PAGE
    ncp = k_cmp.shape[1]
    ns = nc_valid // 4
    nsp = -(-ns // LANES) * LANES
    a = _sel_matrix(ncp, nc_valid, nsp)
    rows = NSA_GROUP * nq
    kchunk = min(512, ch * PAGE)
    per_b = lambda n, w=LANES: pl.BlockSpec((1, n, w), lambda bi, j, t: (bi, 0, 0))
    const = lambda shape: pl.BlockSpec(shape, lambda bi, j, t: (0,) * len(shape))
    grid_spec = pltpu.PrefetchScalarGridSpec(
        num_scalar_prefetch=1,
        grid=(b, nchunks),
        in_specs=[per_b(nq, 4 * LANES), per_b(nq), per_b(ncp), per_b(ncp),
                  pl.BlockSpec(memory_space=pl.ANY), pl.BlockSpec(memory_space=pl.ANY),
                  per_b(PAGE), per_b(PAGE), per_b(LANES, win_k.shape[2]), per_b(LANES, win_v.shape[2]), per_b(PAGE), per_b(PAGE),
                  const(a.shape), const((1, LANES)), const((NSA_HEADS, LANES))],
        out_specs=per_b(nq, 4 * LANES),
        scratch_shapes=[pltpu.VMEM((LANES, ch * PAGE), F32), pltpu.VMEM((LANES, ch * PAGE), F32),
                        pltpu.SemaphoreType.DMA((2,)),
                        pltpu.VMEM((NSA_KV_HEADS, rows, LANES), BF16),
                        pltpu.VMEM((NSA_KV_HEADS, rows, nsp), BF16),
                        pltpu.VMEM((NSA_KV_HEADS, rows, LANES), F32),
                        pltpu.VMEM((NSA_KV_HEADS, rows, LANES), F32),
                        pltpu.VMEM((NSA_KV_HEADS, rows, LANES), F32),
                        pltpu.VMEM((NSA_KV_HEADS, rows, LANES), F32),
                        pltpu.VMEM((NSA_KV_HEADS, rows, LANES), F32)],
    )
    return pl.pallas_call(
        functools.partial(_nsa_sample_kernel, ch=ch, nchunks=nchunks, offset=offset, ns=ns,
                          n_top=min(SEL_TOPK, ns), kchunk=kchunk),
        grid_spec=grid_spec,
        out_shape=jax.ShapeDtypeStruct((b, nq, 4 * LANES), F32),
        compiler_params=_params(("arbitrary", "arbitrary")),
        name="nsa_sample",
    )(page_table, q, sm, k_cmp, v_cmp, ks_pool, vs_pool, ks_new, vs_new, win_k, win_v, kw_new, vw_new, a, bg, gh)


def _mlstm_kernel(xm_ref, vm_ref, om_ref, sm_ref, smt_ref, conv0_ref, cw_ref, cb_ref, wq_ref, wk_ref,
                  bcol_ref, brow_ref, gh_ref, c0_ref, n0_ref, m0_ref,
                  hm_ref, c_ref, n_ref, m_ref, xs_ref, *, L, t_valid):
    c = pl.program_id(1)
    dh = MLSTM_DH

    @pl.when(c == 0)
    def _():
        xs_ref[0:8, :] = jnp.zeros((8, xs_ref.shape[1]), F32)
        xs_ref[5:8, :] = conv0_ref[0]
        c_ref[...] = c0_ref[...]
        n_ref[...] = n0_ref[...]
        m_ref[...] = m0_ref[...]

    xs_ref[8:8 + L, :] = xm_ref[0]
    xc = cb_ref[...]
    for jj in range(4):
        xc = xc + cw_ref[jj:jj + 1, :] * xs_ref[pl.ds(5 + jj, L), :]
    xc = xc * _sigmoid(xc)
    xs_ref[0:8, :] = xs_ref[L:L + 8, :]

    pre_col = sm_ref[0] + brow_ref[...]
    pre_row = smt_ref[0, 0] + bcol_ref[...]
    lf_col = _log_sigmoid(pre_col)
    lf_row = _log_sigmoid(pre_row)
    if t_valid < L:
        rid = lax.broadcasted_iota(jnp.int32, (L, 1), 0) < t_valid
        cid = lax.broadcasted_iota(jnp.int32, (1, L), 1) < t_valid
        lf_col = jnp.where(rid, lf_col, 0.0)
        lf_row = jnp.where(cid, lf_row, 0.0)
        pre_col = jnp.where(rid, pre_col, NEG_INF)
        pre_row = jnp.where(cid, pre_row, NEG_INF)
    ri = lax.broadcasted_iota(jnp.int32, (L, L), 0)
    ci = lax.broadcasted_iota(jnp.int32, (L, L), 1)
    causal = ci <= ri
    bcum_col = _mm_f32(causal.astype(F32), lf_col)
    bcum_row = _mm_f32(lf_row, (ri <= ci).astype(F32))

    for h in range(MLSTM_HEADS):
        hs = slice(h * dh, (h + 1) * dh)
        xh = xc[:, hs]
        q = _mm(xh, wq_ref[h])
        k = _mm(xh, wk_ref[h]) * (dh ** -0.5)
        v = vm_ref[0, :, hs]
        bc = bcum_col[:, 28 + h:29 + h]
        ic = pre_col[:, 24 + h:25 + h]
        br = bcum_row[4 + h:5 + h, :]
        ir = pre_row[h:h + 1, :]
        mh = m_ref[0, :, h:h + 1]
        ch_ = c_ref[0, h]
        nh = n_ref[0, h:h + 1, :]
        dmat = jnp.where(causal, bc - br + ir, NEG_INF)
        inter = bc + mh
        mq = jnp.maximum(inter, jnp.max(dmat, axis=1, keepdims=True))
        a = jnp.exp(dmat - mq) * _mm_nt(q, k)
        wi = jnp.exp(inter - mq)
        num = _mm(a, v) + wi * _mm_nt(q, ch_)
        den = jnp.sum(a, axis=1, keepdims=True) + wi * jnp.sum(q * nh, axis=1, keepdims=True)
        hout = num / jnp.maximum(jnp.abs(den), jnp.exp(-mq))
        btot = bc[L - 1:L, :]
        dec_r = btot - br + ir
        dec_c = btot - bc + ic
        m_new = jnp.maximum(btot + mh, jnp.max(dec_r, axis=1, keepdims=True))
        ws_c = jnp.exp(dec_c - m_new)
        w_c = jnp.exp(btot + mh - m_new)
        c_ref[0, h] = w_c * ch_ + _mm_tn(v * ws_c, k)
        n_ref[0, h:h + 1, :] = w_c * nh + jnp.sum(k * ws_c, axis=0, keepdims=True)
        m_ref[0, :, h:h + 1] = m_new
        y = _rms(hout, gh_ref[h:h + 1, :]) * _sigmoid(om_ref[0, :, hs])
        hm_ref[0, :, hs] = y


def _mlstm(xm, vm, om, sm, conv0, conv_w, conv_b, w_qm, w_km, b_i, b_f, g_head, c0, n0, m0, L, t_valid):
    b, t, w = xm.shape
    nck = t // L
    smt = sm[:, :, 24:32].reshape(b, nck, L, 8).transpose(0, 1, 3, 2)
    brow = jnp.zeros((1, LANES), F32).at[0, 24:28].set(b_i).at[0, 28:32].set(b_f)
    bcol = jnp.concatenate([b_i, b_f]).reshape(8, 1)
    m0p = jnp.zeros((b, 1, LANES), F32).at[:, 0, :MLSTM_HEADS].set(m0)
    tok = lambda: pl.BlockSpec((1, L, w), lambda bi, c: (bi, c, 0))
    const = lambda shape: pl.BlockSpec(shape, lambda bi, c: (0,) * len(shape))
    state = lambda shape: pl.BlockSpec((1,) + shape, lambda bi, c: (bi,) + (0,) * len(shape))
    hm, c_out, n_out, m_out = pl.pallas_call(
        functools.partial(_mlstm_kernel, L=L, t_valid=t_valid),
        grid=(b, nck),
        in_specs=[tok(), tok(), tok(),
                  pl.BlockSpec((1, L, LANES), lambda bi, c: (bi, c, 0)),
                  pl.BlockSpec((1, 1, 8, L), lambda bi, c: (bi, c, 0, 0)),
                  state(conv0.shape[1:]),
                  const(conv_w.shape), const((1, w)), const(w_qm.shape), const(w_km.shape),
                  const((8, 1)), const((1, LANES)), const(g_head.shape),
                  state(c0.shape[1:]), state(n0.shape[1:]), state((1, LANES))],
        out_specs=[tok(), state(c0.shape[1:]), state(n0.shape[1:]), state((1, LANES))],
        out_shape=[jax.ShapeDtypeStruct((b, t, w), F32), jax.ShapeDtypeStruct(c0.shape, F32),
                   jax.ShapeDtypeStruct(n0.shape, F32), jax.ShapeDtypeStruct((b, 1, LANES), F32)],
        scratch_shapes=[pltpu.VMEM((L + 8, w), F32)],
        compiler_params=_params(("arbitrary", "arbitrary")),
        name="mlstm",
    )(xm, vm, om, sm, smt, conv0, conv_w, conv_b.reshape(1, w), w_qm.astype(BF16), w_km.astype(BF16),
      bcol, brow, g_head, c0, n0, m0p)
    return hm, c_out, n_out, m_out[:, 0, :MLSTM_HEADS]


def _mix_kernel(x_ref, on_ref, hm_ref, wo1_ref, wo2_ref, gxa_ref, wxq_ref, x1_ref, qx_ref):
    x1 = x_ref[...] + _mm(on_ref[...], wo1_ref[...]) + _mm(hm_ref[...], wo2_ref[...])
    x1_ref[...] = x1
    qx_ref[...] = _mm(_rms(x1, gxa_ref[...]), wxq_ref[...]) * (XA_DH ** -0.5)


def _mix(x, o_nsa, hm, w_out, g_xa, w_xq, tm):
    m, d = x.shape
    half = o_nsa.shape[1]
    row = lambda w: pl.BlockSpec((tm, w), lambda i: (i, 0))
    const = lambda shape: pl.BlockSpec(shape, lambda i: (0, 0))
    return pl.pallas_call(
        _mix_kernel,
        grid=(m // tm,),
        in_specs=[row(d), row(half), row(half), const((half, d)), const((half, d)), const((1, d)), const((d, d))],
        out_specs=[row(d), row(d)],
        out_shape=[jax.ShapeDtypeStruct((m, d), F32)] * 2,
        compiler_params=_params(("parallel",)),
        name="mix",
    )(x, o_nsa, hm, w_out[:half].astype(BF16), w_out[half:].astype(BF16), g_xa.reshape(1, d), w_xq.astype(BF16))


def _xattn_kernel(qx_ref, mk_ref, mv_ref, o_ref):
    for h in range(XA_HEADS):
        hs = slice(h * XA_DH, (h + 1) * XA_DH)
        s = _mm_nt(qx_ref[0, :, hs], mk_ref[0, :, hs])
        p = jnp.exp(s - jnp.max(s, axis=-1, keepdims=True))
        o_ref[0, :, hs] = _mm(p, mv_ref[0, :, hs]) / jnp.sum(p, axis=-1, keepdims=True)


def _xattn(qx, mem_k, mem_v, tq):
    b, t, d = qx.shape
    nm = mem_k.shape[1]
    return pl.pallas_call(
        _xattn_kernel,
        grid=(b, t // tq),
        in_specs=[pl.BlockSpec((1, tq, d), lambda bi, i: (bi, i, 0)),
                  pl.BlockSpec((1, nm, d), lambda bi, i: (bi, 0, 0)),
                  pl.BlockSpec((1, nm, d), lambda bi, i: (bi, 0, 0))],
        out_specs=pl.BlockSpec((1, tq, d), lambda bi, i: (bi, i, 0)),
        out_shape=jax.ShapeDtypeStruct((b, t, d), F32),
        compiler_params=_params(("parallel", "parallel")),
        name="xattn",
    )(qx, mem_k, mem_v)


def _ffn_kernel(x1_ref, ox_ref, wxo_ref, gf_ref, wg_ref, wu_ref, wd_ref, gfin_ref, y_ref, x2_s, h_s, acc_s):
    j = pl.program_id(1)

    @pl.when(j == 0)
    def _():
        x2 = x1_ref[...] + _mm(ox_ref[...], wxo_ref[...])
        x2_s[...] = x2
        h_s[...] = _rms(x2, gf_ref[...]).astype(BF16)
        acc_s[...] = jnp.zeros(acc_s.shape, F32)

    h = h_s[...]
    g = jnp.dot(h, wg_ref[...], preferred_element_type=F32)
    u = jnp.dot(h, wu_ref[...], preferred_element_type=F32)
    acc_s[...] += _mm(g * _sigmoid(g) * u, wd_ref[...])

    @pl.when(j == pl.num_programs(1) - 1)
    def _():
        y_ref[...] = _rms(x2_s[...] + acc_s[...], gfin_ref[...])


def _ffn(x1, ox, w_xo, g_ffn, w_gate, w_up, w_down, g_final, tm, tf):
    m, d = x1.shape
    dff = w_gate.shape[1]
    row = pl.BlockSpec((tm, d), lambda i, j: (i, 0))
    vec = pl.BlockSpec((1, d), lambda i, j: (0, 0))
    return pl.pallas_call(
        _ffn_kernel,
        grid=(m // tm, dff // tf),
        in_specs=[row, row, pl.BlockSpec((d, d), lambda i, j: (0, 0)), vec,
                  pl.BlockSpec((d, tf), lambda i, j: (0, j)), pl.BlockSpec((d, tf), lambda i, j: (0, j)),
                  pl.BlockSpec((tf, d), lambda i, j: (j, 0)), vec],
        out_specs=row,
        out_shape=jax.ShapeDtypeStruct((m, d), F32),
        scratch_shapes=[pltpu.VMEM((tm, d), F32), pltpu.VMEM((tm, d), BF16), pltpu.VMEM((tm, d), F32)],
        compiler_params=_params(("parallel", "arbitrary")),
        name="ffn",
    )(x1, ox, w_xo.astype(BF16), g_ffn.reshape(1, d), w_gate.astype(BF16), w_up.astype(BF16),
      w_down.astype(BF16), g_final.reshape(1, d))


def _split_w_in(w_in, nsa_w, kv_w, mlstm_w):
    cuts = np.cumsum([nsa_w] + [kv_w] * 6 + [3 * NSA_HEADS] + [mlstm_w] * 3 + [MLSTM_HEADS] * 2)
    parts = jnp.split(w_in, cuts[:-1].tolist(), axis=1)
    small = jnp.concatenate([parts[7], parts[11], parts[12]], axis=1)
    small = jnp.pad(small, ((0, 0), (0, LANES - small.shape[1])))
    ws = [parts[0]] + list(parts[1:7]) + [small] + list(parts[8:11])
    return [w.astype(BF16) for w in ws]


def _tail(x1, ox, w, tm, b, t):
    d = x1.shape[1]
    dff = w["w_gate"].shape[1]
    tf = dff // 2 if (dff // 2) % LANES == 0 else dff
    y = _ffn(x1, ox.reshape(-1, d), w["w_xo"], w["g_ffn"], w["w_gate"], w["w_up"], w["w_down"], w["g_final"], tm, tf)
    return y.reshape(b, t, d)


def _gate_consts(w):
    bg = jnp.pad(w["b_gate"], (0, LANES - w["b_gate"].shape[0])).reshape(1, LANES)
    gh = jnp.concatenate([w["g_head_nsa"], w["g_head_nsa"]], axis=1)
    return bg, gh


def _prompt_group(x, mem, w):
    b, t, d = x.shape
    m = b * t
    tm = 512
    wp = w["w_in_parts"]
    wt = jnp.concatenate(wp[1:7], axis=1).T
    q, sm, xm, vm, om, kc, vc, ks, vs, kw, vw = _norm_proj(x.reshape(m, d), w["g_mix"], [wp[0]] + wp[7:], tm,
                                                           wt=wt, t=t)
    r3 = lambda a: a.reshape(b, t, a.shape[-1])
    npg = t // PAGE
    table = jnp.zeros((b, npg), jnp.int32)
    zeros_next = jnp.zeros((b, PAGE, LANES), F32)
    k_cmp = _compress(table, kc, zeros_next, *w["cmp_k"], ch=npg, paged=False, feature_major=True)
    v_cmp = _compress(table, vc, zeros_next, *w["cmp_v"], ch=npg, paged=False, feature_major=True)
    bg, gh = _gate_consts(w)
    o_nsa = _nsa_prompt_t(r3(q), r3(sm), k_cmp, v_cmp, ks, vs, kw, vw, bg, gh, LANES)
    L = next(c for c in (256, 128, 64) if t % c == 0)
    hm, c_out, n_out, m_out = _mlstm(
        r3(xm), r3(vm), r3(om), r3(sm), jnp.zeros((b, 3, xm.shape[1]), F32), w["conv_w"], w["conv_b"],
        w["w_qm"], w["w_km"], w["b_i"], w["b_f"], w["g_head_m"],
        jnp.zeros((b, MLSTM_HEADS, MLSTM_DH, MLSTM_DH), F32), jnp.zeros((b, MLSTM_HEADS, MLSTM_DH), F32),
        jnp.zeros((b, MLSTM_HEADS), F32), L, L)
    nm = mem.shape[1]
    mk, mv = _norm_proj(mem.reshape(b * nm, d), w["g_mem"], [w["w_xk"].astype(BF16), w["w_xv"].astype(BF16)],
                        min(512, b * nm))
    x1, qx = _mix(x.reshape(m, d), o_nsa.reshape(m, -1), hm.reshape(m, -1), w["w_out"], w["g_xa"], w["w_xq"], tm)
    ox = _xattn(qx.reshape(b, t, d), mk.reshape(b, nm, d), mv.reshape(b, nm, d), 512)
    y = _tail(x1, ox, w, tm, b, t)
    kv5 = lambda a: a.reshape(b, NSA_KV_HEADS, NSA_DH, a.shape[2]).transpose(0, 3, 1, 2)[None]
    keep = min(WINDOW, t)
    xm3 = r3(xm)
    states = (kv5(kc), kv5(vc), kv5(ks), kv5(vs), kv5(kw[:, :, t - keep:]), kv5(vw[:, :, t - keep:]),
              c_out[None], n_out[None], m_out[None], xm3[None, :, t - 3:],
              mk.reshape(1, b, nm, XA_HEADS, XA_DH), mv.reshape(1, b, nm, XA_HEADS, XA_DH))
    return y, states


def _sample_group(x, pools, page_table, win_k, win_v, conv0, c0, n0, m0, mem_k, mem_v, w):
    b, t, d = x.shape
    m = b * t
    tp = 8
    tm = min(m, 512)
    q, kc, vc, ks, vs, kw, vw, sm, xm, vm, om = _norm_proj(x.reshape(m, d), w["g_mix"], w["w_in_parts"], tm)
    r3 = lambda a: a.reshape(b, t, a.shape[-1])
    pad_t = lambda a: jnp.pad(r3(a), ((0, 0), (0, tp - t), (0, 0)))
    npg = page_table.shape[1]
    past = npg * PAGE
    ch = min(32, npg)
    fm = lambda a: a.transpose(0, 2, 3, 1).reshape(a.shape[0], LANES, a.shape[1])
    pool_kc, pool_vc, pool_ks, pool_vs = [fm(p) for p in pools]
    win_kt, win_vt = fm(win_k), fm(win_v)
    nc_valid = (past + (-(-t // SEL_BLOCK)) * SEL_BLOCK) // CMP_STRIDE
    ncp = -(-nc_valid // LANES) * LANES

    pad_page = lambda a: jnp.pad(r3(a), ((0, 0), (0, PAGE - t), (0, 0)))

    def compressed(pool, new_rows, cw):
        nxt = pad_page(new_rows)
        main = _compress(page_table, pool, nxt, *cw, ch=ch, paged=True, feature_major=True)
        tail = _compress(jnp.arange(b, dtype=jnp.int32).reshape(1, b), nxt, jnp.zeros((1, PAGE, LANES), F32),
                         *cw, ch=b, paged=True, feature_major=False)
        n_tail = nc_valid - npg * CMP_PER_PAGE
        full = jnp.concatenate([main, tail.reshape(b, CMP_PER_PAGE, LANES)[:, :n_tail]], axis=1)
        return jnp.pad(full, ((0, 0), (0, ncp - nc_valid), (0, 0)))

    k_cmp = compressed(pool_kc, kc, w["cmp_k"])
    v_cmp = compressed(pool_vc, vc, w["cmp_v"])
    bg, gh = _gate_consts(w)
    o_nsa = _nsa_sample(page_table, pad_t(q), pad_t(sm), k_cmp, v_cmp, pool_ks, pool_vs, pad_page(ks), pad_page(vs),
                        win_kt, win_vt, pad_page(kw), pad_page(vw),
                        bg, gh, nc_valid, ch)[:, :t]
    hm, c_out, n_out, m_out = _mlstm(pad_t(xm), pad_t(vm), pad_t(om), pad_t(sm), conv0, w["conv_w"], w["conv_b"],
                                     w["w_qm"], w["w_km"], w["b_i"], w["b_f"], w["g_head_m"], c0, n0, m0, tp, t)
    hm = hm[:, :t]
    x1, qx = _mix(x.reshape(m, d), o_nsa.reshape(m, -1), hm.reshape(m, -1), w["w_out"], w["g_xa"], w["w_xq"], tm)
    nm = mem_k.shape[1]
    ox = _xattn(pad_t(qx), mem_k.reshape(b, nm, d), mem_v.reshape(b, nm, d), tp)[:, :t]
    y = _tail(x1, ox, w, tm, b, t)
    kv5 = lambda a: a.reshape(1, b, t, NSA_KV_HEADS, NSA_DH)
    keep = min(WINDOW, past + t)
    unfm = lambda a: a.reshape(b, NSA_KV_HEADS, NSA_DH, a.shape[2]).transpose(0, 3, 1, 2)[None]
    win5 = lambda old_t, new: unfm(jnp.concatenate([old_t, r3(new).transpose(0, 2, 1)], axis=2)[:, :, -keep:])
    conv_all = jnp.concatenate([conv0, r3(xm)], axis=1)
    states = (kv5(kc), kv5(vc), kv5(ks), kv5(vs), win5(win_kt, kw), win5(win_vt, vw),
              c_out[None], n_out[None], m_out[None], conv_all[None, :, -3:])
    return y, states


def kernel(x_prompt, x_sample, cache_k_cmp, cache_v_cmp, cache_k_slc, cache_v_slc, state_k_win, state_v_win, state_conv, state_C, state_n, state_m, cache_mem_k, cache_mem_v, page_table, mem_prompt, g_mix, w_in, b_gate, cmp_pe_k, cmp_w1_k, cmp_b1_k, cmp_w2_k, cmp_pe_v, cmp_w1_v, cmp_b1_v, cmp_w2_v, g_head_nsa, conv_w, conv_b, w_qm, w_km, b_i, b_f, g_head_m, w_out, g_xa, g_mem, w_xq, w_xk, w_xv, w_xo, g_ffn, w_gate, w_up, w_down, g_final):
    assert w_in.shape[0] == 1, "single-layer decoder"
    l = 0
    w = dict(g_mix=g_mix[l], b_gate=b_gate[l],
             w_in_parts=_split_w_in(w_in[l], NSA_HEADS * NSA_DH, NSA_KV_HEADS * NSA_DH, MLSTM_HEADS * MLSTM_DH),
             cmp_k=(cmp_pe_k[l], cmp_w1_k[l], cmp_b1_k[l], cmp_w2_k[l]),
             cmp_v=(cmp_pe_v[l], cmp_w1_v[l], cmp_b1_v[l], cmp_w2_v[l]),
             g_head_nsa=g_head_nsa[l], conv_w=conv_w[l], conv_b=conv_b[l], w_qm=w_qm[l], w_km=w_km[l],
             b_i=b_i[l], b_f=b_f[l], g_head_m=g_head_m[l], w_out=w_out[l], g_xa=g_xa[l], g_mem=g_mem[l],
             w_xq=w_xq[l], w_xk=w_xk[l], w_xv=w_xv[l], w_xo=w_xo[l], g_ffn=g_ffn[l], w_gate=w_gate[l],
             w_up=w_up[l], w_down=w_down[l], g_final=g_final)
    y_p, st_p = _prompt_group(x_prompt, mem_prompt, w)
    pools = (cache_k_cmp[l], cache_v_cmp[l], cache_k_slc[l], cache_v_slc[l])
    y_s, st_s = _sample_group(x_sample, pools, page_table, state_k_win[l], state_v_win[l], state_conv[l],
                              state_C[l], state_n[l], state_m[l], cache_mem_k[l], cache_mem_v[l], w)
    return (y_p, y_s) + st_p + st_s
```

```python
import functools

import numpy as np
import jax
import jax.numpy as jnp
from jax import lax
from jax.experimental import pallas as pl
from jax.experimental.pallas import tpu as pltpu

F32 = jnp.float32
BF16 = jnp.bfloat16
HIGHEST = lax.Precision.HIGHEST

EPS = 1e-6
NEG_INF = -1e30
FORCE_SCORE = 1e9
PAD_SCORE = -2e38
TAKEN_SCORE = -3e38

LANES = 128
NSA_HEADS = 8
NSA_KV_HEADS = 2
NSA_GROUP = 4
NSA_DH = 64
CMP_STRIDE = 16
SEL_BLOCK = 64
SEL_TOPK = 16
WINDOW = 512
Q_BLOCK = 64
PAGE = 128
CMP_PER_PAGE = PAGE // CMP_STRIDE
MLSTM_HEADS = 4
MLSTM_DH = 128
XA_HEADS = 4
XA_DH = 256
ALIBI = tuple(2.0 ** (-(h + 1)) for h in range(NSA_HEADS))

VMEM_LIMIT = 56 * 1024 * 1024


def _params(sem):
    return pltpu.CompilerParams(dimension_semantics=sem, vmem_limit_bytes=VMEM_LIMIT)


def _mm(a, b):
    return jnp.dot(a.astype(BF16), b.astype(BF16), preferred_element_type=F32)


def _mm_nt(a, b):
    return lax.dot_general(a.astype(BF16), b.astype(BF16), (((1,), (1,)), ((), ())),
                           preferred_element_type=F32)


def _mm_tn(a, b):
    return lax.dot_general(a.astype(BF16), b.astype(BF16), (((0,), (0,)), ((), ())),
                           preferred_element_type=F32)


def _mm_split(a, b_bf16):
    hi = a.astype(BF16)
    lo = (a - hi.astype(F32)).astype(BF16)
    return (jnp.dot(hi, b_bf16, preferred_element_type=F32) + jnp.dot(lo, b_bf16, preferred_element_type=F32))


def _bf16_parts(x):
    hi = x.astype(BF16)
    r1 = x - hi.astype(F32)
    mid = r1.astype(BF16)
    lo = (r1 - mid.astype(F32)).astype(BF16)
    return hi, mid, lo


def _mm_f32(a, b):
    return jnp.dot(a, b, precision=HIGHEST, preferred_element_type=F32)


def _rms(x, g):
    return x * lax.rsqrt(jnp.mean(x * x, axis=-1, keepdims=True) + EPS) * g


def _sigmoid(x):
    return 1.0 / (1.0 + jnp.exp(-x))


def _log_sigmoid(x):
    return jnp.minimum(x, 0.0) - jnp.log(1.0 + jnp.exp(-jnp.abs(x)))


def _gelu_tanh(x):
    return 0.5 * x * (1.0 + jnp.tanh(0.7978845608028654 * (x + 0.044715 * x * x * x)))


def _topk_mask(imp, k):
    col = lax.broadcasted_iota(jnp.int32, imp.shape, 1)
    sel = jnp.zeros(imp.shape, F32)
    work = imp
    for _ in range(k):
        m = jnp.max(work, axis=-1, keepdims=True)
        idx = jnp.min(jnp.where(work == m, col, jnp.int32(2 ** 30)), axis=-1, keepdims=True)
        hit = col == idx
        sel = jnp.where(hit, 1.0, sel)
        work = jnp.where(hit, TAKEN_SCORE, work)
    return sel


def _softmax_parts(s, valid):
    s = jnp.where(valid, s, NEG_INF)
    mx = jnp.max(s, axis=-1, keepdims=True)
    p = jnp.where(valid, jnp.exp(s - mx), 0.0)
    return p, mx, jnp.sum(p, axis=-1, keepdims=True)


def _q_rows_f32(q_ref, hk, nq):
    lane = lax.broadcasted_iota(jnp.int32, (1, LANES), 1)
    own = (lane >= hk * NSA_DH) & (lane < (hk + 1) * NSA_DH)
    parts = []
    for g in range(NSA_GROUP):
        h = hk * NSA_GROUP + g
        blk = q_ref[0, :, (h // 2) * LANES:(h // 2 + 1) * LANES]
        if (h % 2) != hk:
            blk = pltpu.roll(blk, NSA_DH, 1)
        parts.append(blk)
    qh = jnp.concatenate(parts, axis=0)
    return jnp.where(own, qh * (NSA_DH ** -0.5), 0.0), own


def _q_rows(q_ref, hk, nq):
    qh, own = _q_rows_f32(q_ref, hk, nq)
    return qh.astype(BF16), own


def _feature_lanes(hk):
    base = (1 - hk) * NSA_DH
    return base, base + 1, base + 2


def _select_t(imp_t, tq_lane, ns, n_top):
    blk = lax.broadcasted_iota(jnp.int32, (imp_t.shape[0], 1), 0)
    cur = tq_lane >> 6
    forced = (blk == 0) | (blk == cur) | (blk == cur - 1)
    work = jnp.where(forced, FORCE_SCORE, imp_t)
    work = jnp.where(blk * SEL_BLOCK <= tq_lane, work, NEG_INF)
    work = jnp.where(blk < ns, work, PAD_SCORE)
    sel = jnp.zeros(imp_t.shape, F32)
    for _ in range(n_top):
        m = jnp.max(work, axis=0, keepdims=True)
        idx = jnp.min(jnp.where(work == m, blk, jnp.int32(2 ** 30)), axis=0, keepdims=True)
        hit = blk == idx
        sel = jnp.where(hit, 1.0, sel)
        work = jnp.where(hit, TAKEN_SCORE, work)
    return sel


def _row_consts(hk, nq, t0):
    rows = NSA_GROUP * nq
    row = lax.broadcasted_iota(jnp.int32, (rows, 1), 0)
    slope = jnp.full((rows, 1), ALIBI[hk * NSA_GROUP + NSA_GROUP - 1], F32)
    for g in range(NSA_GROUP - 2, -1, -1):
        slope = jnp.where(row < (g + 1) * nq, ALIBI[hk * NSA_GROUP + g], slope)
    tq = t0 + (row & (nq - 1))
    return slope, tq


def _cmp_branch(qh, slope, tq, kc_ref, vc_ref, a_ref, nq):
    ncp = kc_ref.shape[1]
    s = _mm_nt(qh, kc_ref[0])
    cend = lax.broadcasted_iota(jnp.int32, (1, ncp), 1) * CMP_STRIDE + (2 * CMP_STRIDE - 1)
    d = tq - cend
    valid = d >= 0
    p, _, l = _softmax_parts(s - slope * d.astype(F32), valid)
    p = p / jnp.maximum(l, 1e-30)
    o_c = _mm(p, vc_ref[0])
    psum = p[0:nq]
    for g in range(1, NSA_GROUP):
        psum = psum + p[g * nq:(g + 1) * nq]
    return o_c, sum(jnp.dot(part, a_ref[...], preferred_element_type=F32) for part in _bf16_parts(psum))


def _select(imp, tq_q, ns, n_top):
    nsp = imp.shape[1]
    blk = lax.broadcasted_iota(jnp.int32, (1, nsp), 1)
    cur = tq_q >> 6
    forced = (blk == 0) | (blk == cur) | (blk == cur - 1)
    imp = jnp.where(forced, FORCE_SCORE, imp)
    imp = jnp.where(blk * SEL_BLOCK <= tq_q, imp, NEG_INF)
    imp = jnp.where(blk < ns, imp, PAD_SCORE)
    return _topk_mask(imp, n_top)


def _flash_step(qh, kch, vch, pos, tq, slope, sel_rows, carry, feature_major=False):
    m, l, acc = carry
    nsp = sel_rows.shape[1]
    s = _mm(qh, kch) if feature_major else _mm_nt(qh, kch)
    d = tq - pos
    blk = lax.broadcasted_iota(jnp.int32, (nsp, 1), 0)
    expand = (blk == (pos >> 6)).astype(BF16)
    chosen = jnp.dot(sel_rows, expand, preferred_element_type=F32) > 0.5
    valid = (d >= 0) & chosen
    s = jnp.where(valid, s - slope * d.astype(F32), NEG_INF)
    m_new = jnp.maximum(m, jnp.max(s, axis=-1, keepdims=True))
    alpha = jnp.exp(m - m_new)
    p = jnp.where(valid, jnp.exp(s - m_new), 0.0)
    l = alpha * l + jnp.sum(p, axis=-1, keepdims=True)
    acc = alpha * acc + (_mm_nt(p, vch) if feature_major else _mm(p, vch))
    return m_new, l, acc


def _combine_heads(o_c, o_s, o_w, gates, gh_ref, own, hk, nq, out_pairs):
    for g in range(NSA_GROUP):
        h = hk * NSA_GROUP + g
        r = slice(g * nq, (g + 1) * nq)
        og = (gates[:, 3 * h:3 * h + 1] * o_c[r] + gates[:, 3 * h + 1:3 * h + 2] * o_s[r]
              + gates[:, 3 * h + 2:3 * h + 3] * o_w[r])
        og = jnp.where(own, og, 0.0)
        ms = jnp.sum(og * og, axis=-1, keepdims=True) * (1.0 / NSA_DH)
        y = og * lax.rsqrt(ms + EPS) * gh_ref[h:h + 1, :]
        if (h % 2) != hk:
            y = pltpu.roll(y, NSA_DH, 1)
        out_pairs[h // 2] = y if out_pairs[h // 2] is None else out_pairs[h // 2] + y


def _norm_proj_kernel(x_ref, g_ref, *refs, n, n_t):
    h = _rms(x_ref[...], g_ref[...]).astype(BF16)
    has_t = 1 if n_t else 0
    outs = refs[n + has_t:]
    for w_ref, o_ref in zip(refs[:n], outs[:n]):
        o_ref[...] = jnp.dot(h, w_ref[...], preferred_element_type=F32)
    if n_t:
        yt = lax.dot_general(refs[n][...], h, (((1,), (1,)), ((), ())), preferred_element_type=F32)
        for k, o_ref in enumerate(outs[n:]):
            o_ref[0] = yt[k * LANES:(k + 1) * LANES, :]


def _norm_proj(x, g, ws, tm, wt=None, t=None):
    m, d = x.shape
    n = len(ws)
    n_t = 0 if wt is None else wt.shape[0] // LANES
    in_specs = [pl.BlockSpec((tm, d), lambda i: (i, 0)), pl.BlockSpec((1, d), lambda i: (0, 0))]
    in_specs += [pl.BlockSpec(w.shape, lambda i: (0, 0)) for w in ws]
    out_specs = [pl.BlockSpec((tm, w.shape[1]), lambda i: (i, 0)) for w in ws]
    out_shape = [jax.ShapeDtypeStruct((m, w.shape[1]), F32) for w in ws]
    args = [x, g.reshape(1, d), *ws]
    if n_t:
        per_b = t // tm
        in_specs.append(pl.BlockSpec(wt.shape, lambda i: (0, 0)))
        out_specs += [pl.BlockSpec((1, LANES, tm), lambda i: (i // per_b, 0, i % per_b))] * n_t
        out_shape += [jax.ShapeDtypeStruct((m // t, LANES, t), F32)] * n_t
        args.append(wt)
    return pl.pallas_call(
        functools.partial(_norm_proj_kernel, n=n, n_t=n_t),
        grid=(m // tm,),
        in_specs=in_specs,
        out_specs=out_specs,
        out_shape=out_shape,
        compiler_params=_params(("parallel",)),
        name="norm_proj",
    )(*args)


def _compress_kernel(tbl_ref, pool_ref, last_ref, wr_ref, pe_ref, b1_ref, w2_ref, o_ref,
                     buf, xrow, acc, cst, sem, *, ch, nchunks, paged, feature_major):
    b = pl.program_id(0)
    j = pl.program_id(1)
    nblk = ch * CMP_PER_PAGE
    half = wr_ref.shape[2] // 2
    npairs = wr_ref.shape[0]

    n = b * nchunks + j
    cur = n % 2

    def page_copy(bb, k, buf_i, slot):
        if paged:
            src = pool_ref.at[tbl_ref[bb, k]]
        else:
            src = pool_ref.at[bb, :, pl.ds(pl.multiple_of(k * PAGE, PAGE), PAGE)]
        return pltpu.make_async_copy(src, buf.at[buf_i, slot], sem.at[buf_i, slot])

    def step_copies(bb, jj, buf_i, fn):
        for s in range(ch):
            fn(page_copy(bb, jj * ch + s, buf_i, s))

        @pl.when(jj < nchunks - 1)
        def _():
            fn(page_copy(bb, jnp.minimum((jj + 1) * ch, nchunks * ch - 1), buf_i, ch))

    @pl.when(n == 0)
    def _():
        step_copies(b, j, cur, lambda c: c.start())
        c = jnp.zeros((8, half), F32)
        for pr in range(npairs):
            c = c + _mm(jnp.broadcast_to(pe_ref[0, pr:pr + 1, :], (8, 2 * LANES)), wr_ref[pr, :, 0:half])
            c = c + _mm(jnp.broadcast_to(pe_ref[1, pr:pr + 1, :], (8, 2 * LANES)), wr_ref[pr, :, half:2 * half])
        cst[...] = c + b1_ref[...]

    @pl.when(n + 1 < pl.num_programs(0) * nchunks)
    def _():
        wrap = j + 1 == nchunks
        step_copies(jnp.where(wrap, b + 1, b), jnp.where(wrap, 0, j + 1), 1 - cur, lambda c: c.start())

    step_copies(b, j, cur, lambda c: c.wait())

    for s in range(ch):
        page = buf[cur, s]
        xrow[pl.ds(s * PAGE, PAGE), :] = page.T if feature_major else page

    @pl.when(j < nchunks - 1)
    def _():
        page = buf[cur, ch]
        xrow[pl.ds(ch * PAGE, PAGE), :] = page.T if feature_major else page

    @pl.when(j == nchunks - 1)
    def _():
        xrow[pl.ds(ch * PAGE, PAGE), :] = last_ref[0]

    for pr in range(npairs):
        x0 = xrow[pl.ds(2 * pr, nblk + 8, stride=CMP_STRIDE), :]
        x1 = xrow[pl.ds(2 * pr + 1, nblk + 8, stride=CMP_STRIDE), :]
        part = _mm(jnp.concatenate([x0, x1], axis=1), wr_ref[pr])
        if pr == 0:
            acc[...] = part
        else:
            acc[...] += part
    hid = acc[0:nblk, 0:half] + acc[pl.ds(1, nblk), half:2 * half] + cst[0:1, :]
    o_ref[0] = _mm(_gelu_tanh(hid), w2_ref[...])


def _compress(table, pool, last_next, pe, w1, b1, w2, ch, paged, feature_major):
    nb, npg = table.shape
    nchunks = npg // ch
    hid = w1.shape[1]
    dh = NSA_DH
    wa = w1[:CMP_STRIDE * dh].reshape(CMP_STRIDE, dh, hid)
    wb = w1[CMP_STRIDE * dh:].reshape(CMP_STRIDE, dh, hid)
    z = jnp.zeros_like(wa)
    wr = jnp.concatenate([jnp.concatenate([wa, z, wb, z], axis=2),
                          jnp.concatenate([z, wa, z, wb], axis=2)], axis=1).astype(BF16)
    wr = wr.reshape(CMP_STRIDE // 2, 2 * LANES, 4 * hid)
    pe1 = jnp.concatenate([pe[:CMP_STRIDE], pe[:CMP_STRIDE]], axis=1).reshape(CMP_STRIDE // 2, 2 * LANES)
    pe2 = jnp.concatenate([pe[CMP_STRIDE:], pe[CMP_STRIDE:]], axis=1).reshape(CMP_STRIDE // 2, 2 * LANES)
    pes = jnp.stack([pe1, pe2])
    b1t = jnp.concatenate([b1, b1]).reshape(1, 2 * hid)
    zz = jnp.zeros_like(w2)
    w2bd = jnp.concatenate([jnp.concatenate([w2, zz], axis=1),
                            jnp.concatenate([zz, w2], axis=1)], axis=0).astype(BF16)
    nblk = ch * CMP_PER_PAGE
    grid_spec = pltpu.PrefetchScalarGridSpec(
        num_scalar_prefetch=1,
        grid=(nb, nchunks),
        in_specs=[pl.BlockSpec(memory_space=pl.ANY),
                  pl.BlockSpec((1, PAGE, LANES), lambda b, j, t: (b, 0, 0)),
                  pl.BlockSpec(wr.shape, lambda b, j, t: (0, 0, 0)),
                  pl.BlockSpec(pes.shape, lambda b, j, t: (0, 0, 0)),
                  pl.BlockSpec(b1t.shape, lambda b, j, t: (0, 0)),
                  pl.BlockSpec(w2bd.shape, lambda b, j, t: (0, 0))],
        out_specs=pl.BlockSpec((1, nblk, LANES), lambda b, j, t: (b, j, 0)),
        scratch_shapes=[pltpu.VMEM((2, ch + 1, PAGE, LANES), F32),
                        pltpu.VMEM(((ch + 1) * PAGE, LANES), F32),
                        pltpu.VMEM((nblk + 8, 4 * hid), F32),
                        pltpu.VMEM((8, 2 * hid), F32),
                        pltpu.SemaphoreType.DMA((2, ch + 1))],
    )
    return pl.pallas_call(
        functools.partial(_compress_kernel, ch=ch, nchunks=nchunks, paged=paged, feature_major=feature_major),
        grid_spec=grid_spec,
        out_shape=jax.ShapeDtypeStruct((nb, npg * CMP_PER_PAGE, LANES), F32),
        compiler_params=_params(("arbitrary", "arbitrary")),
        name="compress",
    )(table, pool, last_next, wr, pes, b1t, w2bd)


def _sel_matrix(nc_rows, nc_valid, ns_cols):
    a = np.zeros((nc_rows, ns_cols), np.float32)
    for j in range(nc_valid // 4):
        for c, wgt in ((4 * j - 1, 0.5), (4 * j, 1.0), (4 * j + 1, 1.0), (4 * j + 2, 1.0), (4 * j + 3, 0.5)):
            if 0 <= c < nc_valid:
                a[c, j] += wgt
    return jnp.asarray(a)


def _nsa_prompt_kernel(qi_ref, ci_ref, q_ref, sm_ref, kc_ref, vc_ref, ks_ref, vs_ref, kw_ref, vw_ref, at_ref, bg_ref,
                       gh_ref, o_ref, ksa, vsa, kwa, vwa, kca, vca, qa_s, oc_s, ow_s, m_s, acc_s,
                       *, nq, ns, n_top, kchunk, wkeys, prep_rows):
    step = pl.program_id(1)
    i = qi_ref[step]
    c = ci_ref[step]
    t0 = i * nq
    rows = NSA_GROUP * nq
    t = ks_ref.shape[2]
    nc = kc_ref.shape[1]
    last = (t0 + nq + kchunk - 1) // kchunk - 1
    lane = lax.broadcasted_iota(jnp.int32, (1, LANES), 1)

    @pl.when(step == 0)
    def _():
        for hk in range(NSA_KV_HEADS):
            own = (lane >= hk * NSA_DH) & (lane < (hk + 1) * NSA_DH)
            f0, f1, f2 = _feature_lanes(hk)
            cidx = lax.broadcasted_iota(jnp.int32, (nc, 1), 0).astype(F32)
            kca[hk] = jnp.where(own, kc_ref[0], jnp.where(lane == f2, cidx, 0.0)).astype(BF16)
            vca[hk] = vc_ref[0].astype(BF16)

            row = lax.broadcasted_iota(jnp.int32, (LANES, 1), 0)
            own_r = (row >= hk * NSA_DH) & (row < (hk + 1) * NSA_DH)

            def prep(r, _):
                r0 = pl.multiple_of(r * prep_rows, prep_rows)
                cs = pl.ds(r0, prep_rows)
                pos = r0 + lax.broadcasted_iota(jnp.int32, (1, prep_rows), 1)
                blk = pos >> 6
                feat = jnp.where(row == f0, blk.astype(F32), jnp.where(row == f1, (pos & 63).astype(F32), 0.0))
                ksa[hk, 0:LANES, cs] = jnp.where(own_r, ks_ref[0, :, cs], feat).astype(BF16)
                ksa[hk, LANES:2 * LANES, cs] = (row == blk).astype(BF16)
                kwa[hk, :, cs] = jnp.where(own_r, kw_ref[0, :, cs], feat).astype(BF16)
                vsa[hk, :, cs] = jnp.where(own_r, vs_ref[0, :, cs], 1.0).astype(BF16)
                vwa[hk, :, cs] = jnp.where(own_r, vw_ref[0, :, cs], 1.0).astype(BF16)
                return 0

            lax.fori_loop(0, t // prep_rows, prep, 0)

    @pl.when(c == 0)
    def _():
        tq_lane = t0 + lax.broadcasted_iota(jnp.int32, (1, nq), 1)
        for hk in range(NSA_KV_HEADS):
            f0, f1, f2 = _feature_lanes(hk)
            qh, _ = _q_rows_f32(q_ref, hk, nq)
            slope, tq = _row_consts(hk, nq, t0)
            qlo = jnp.where(lane == f0, slope * SEL_BLOCK,
                            jnp.where(lane == f1, slope, jnp.where(lane == f2, slope * CMP_STRIDE, qh))).astype(BF16)
            s = lax.dot_general(qlo, kca[hk], (((1,), (1,)), ((), ())), preferred_element_type=F32)
            cend = lax.broadcasted_iota(jnp.int32, (1, nc), 1) * CMP_STRIDE + (2 * CMP_STRIDE - 1)
            p, _, l_c = _softmax_parts(s, cend <= tq)
            p = p / jnp.maximum(l_c, 1e-30)
            oc_s[hk] = jnp.dot(p.astype(BF16), vca[hk], preferred_element_type=F32)
            psum = p[0:nq]
            for g in range(1, NSA_GROUP):
                psum = psum + p[g * nq:(g + 1) * nq]
            imp_t = lax.dot_general(at_ref[...], psum, (((1,), (1,)), ((), ())), precision=HIGHEST,
                                    preferred_element_type=F32)
            sel_t = _select_t(imp_t, tq_lane, ns, n_top)
            bias = (jnp.transpose(sel_t) - 1.0) * 1e30
            qa_s[hk, :, 0:LANES] = qlo
            qa_s[hk, :, LANES:2 * LANES] = jnp.concatenate([bias] * NSA_GROUP, axis=0).astype(BF16)
            w0 = pl.multiple_of(jnp.maximum(t0 + nq - wkeys, 0), LANES)
            d = tq - (w0 + lax.broadcasted_iota(jnp.int32, (1, wkeys), 1))
            s = jnp.dot(qlo, kwa[hk, :, pl.ds(w0, wkeys)], preferred_element_type=F32)
            p, _, _ = _softmax_parts(s, (d >= 0) & (d < WINDOW))
            ow = lax.dot_general(p.astype(BF16), vwa[hk, :, pl.ds(w0, wkeys)], (((1,), (1,)), ((), ())),
                                 preferred_element_type=F32)
            ow_s[hk] = ow / jnp.maximum(pltpu.roll(ow, NSA_DH, 1), 1e-30)
            m_s[hk] = jnp.full((rows, LANES), NEG_INF, F32)
            acc_s[hk] = jnp.zeros((rows, LANES), F32)

    k0 = pl.multiple_of(c * kchunk, kchunk)

    def flash(causal):
        for hk in range(NSA_KV_HEADS):
            s = jnp.dot(qa_s[hk], ksa[hk, :, pl.ds(k0, kchunk)], preferred_element_type=F32)
            if causal:
                _, tq = _row_consts(hk, nq, t0)
                s = jnp.where(k0 + lax.broadcasted_iota(jnp.int32, (1, kchunk), 1) <= tq, s, NEG_INF)
            m_old = m_s[hk][:, 0:1]
            m_new = jnp.maximum(m_old, jnp.max(s, axis=-1, keepdims=True))
            p = jnp.exp(s - m_new).astype(BF16)
            acc_s[hk] = jnp.exp(m_old - m_new) * acc_s[hk] + lax.dot_general(
                p, vsa[hk, :, pl.ds(k0, kchunk)], (((1,), (1,)), ((), ())), preferred_element_type=F32)
            m_s[hk] = jnp.broadcast_to(m_new, (rows, LANES))

    @pl.when(c < last)
    def _():
        flash(False)

    @pl.when(c == last)
    def _():
        flash(True)
        gates = _sigmoid(sm_ref[0] + bg_ref[...])
        out_pairs = [None] * (NSA_HEADS // 2)
        for hk in range(NSA_KV_HEADS):
            own = (lane >= hk * NSA_DH) & (lane < (hk + 1) * NSA_DH)
            acc = acc_s[hk]
            o_s = acc / jnp.maximum(pltpu.roll(acc, NSA_DH, 1), 1e-30)
            _combine_heads(oc_s[hk], o_s, ow_s[hk], gates, gh_ref, own, hk, nq, out_pairs)
        for k, y in enumerate(out_pairs):
            o_ref[0, :, k * LANES:(k + 1) * LANES] = y


def _nsa_prompt(q, sm, k_cmp, v_cmp, ks, vs, kw, vw, bg, gh, nq):
    b, t, _ = q.shape
    nc = k_cmp.shape[1]
    ns = t // SEL_BLOCK
    assert nq == LANES and ns <= LANES and t % 512 == 0, "prompt NSA kernel: 128-query blocks, at most 128 blocks"
    a = _sel_matrix(nc, nc, LANES).T
    kchunk = min(512, t)
    wkeys = min(-(-(WINDOW + nq) // LANES) * LANES, t)
    pairs = [(i, c) for i in range(t // nq) for c in range(-(-((i + 1) * nq) // kchunk))]
    qi = jnp.asarray(np.array([p[0] for p in pairs], np.int32))
    ci = jnp.asarray(np.array([p[1] for p in pairs], np.int32))
    rows = NSA_GROUP * nq
    full = lambda n: pl.BlockSpec((1, n, LANES), lambda bi, s, qi, ci: (bi, 0, 0))
    full_t = pl.BlockSpec((1, LANES, t), lambda bi, s, qi, ci: (bi, 0, 0))
    const = lambda shape: pl.BlockSpec(shape, lambda bi, s, qi, ci: (0,) * len(shape))
    tok = lambda w: pl.BlockSpec((1, nq, w), lambda bi, s, qi, ci: (bi, qi[s], 0))
    grid_spec = pltpu.PrefetchScalarGridSpec(
        num_scalar_prefetch=2,
        grid=(b, len(pairs)),
        in_specs=[tok(4 * LANES), tok(LANES), full(nc), full(nc), full_t, full_t, full_t, full_t,
                  const(a.shape), const((1, LANES)), const((NSA_HEADS, LANES))],
        out_specs=tok(4 * LANES),
        scratch_shapes=[pltpu.VMEM((NSA_KV_HEADS, 2 * LANES, t), BF16)]
        + [pltpu.VMEM((NSA_KV_HEADS, LANES, t), BF16)] * 3
        + [pltpu.VMEM((NSA_KV_HEADS, nc, LANES), BF16)] * 2
        + [pltpu.VMEM((NSA_KV_HEADS, rows, 2 * LANES), BF16)]
        + [pltpu.VMEM((NSA_KV_HEADS, rows, LANES), F32)] * 4,
    )
    return pl.pallas_call(
        functools.partial(_nsa_prompt_kernel, nq=nq, ns=ns, n_top=min(SEL_TOPK, ns), kchunk=kchunk, wkeys=wkeys,
                          prep_rows=512),
        grid_spec=grid_spec,
        out_shape=jax.ShapeDtypeStruct((b, t, 4 * LANES), F32),
        compiler_params=_params(("arbitrary", "arbitrary")),
        name="nsa_prompt",
    )(qi, ci, q, sm, k_cmp, v_cmp, ks, vs, kw, vw, a, bg, gh)


def _head_feature_rows(hk, g, n):
    slope = ALIBI[hk * NSA_GROUP + g]
    row = lax.broadcasted_iota(jnp.int32, (NSA_DH, n), 0)
    return jnp.where(row == 0, slope * SEL_BLOCK,
                     jnp.where(row == 1, slope, jnp.where(row == 2, slope * CMP_STRIDE, 0.0)))


def _swap_halves(x):
    return jnp.concatenate([x[NSA_DH:], x[:NSA_DH]], axis=0)


def _nsa_prompt_t_kernel(qi_ref, ci_ref, q_ref, sm_ref, kc_ref, vc_ref, ks_ref, vs_ref, kw_ref, vw_ref, at_ref,
                         bg_ref, ght_ref, o_ref, ksr, kwr, vsa, vwa, kca, vct, qa_s, oc_s, ow_s, m_s, acc_s,
                         *, nq, ns, n_top, kchunk, wpad):
    step = pl.program_id(1)
    i = qi_ref[step]
    c = ci_ref[step]
    t0 = pl.multiple_of(i * nq, nq)
    cols = NSA_GROUP * nq
    t = ks_ref.shape[2]
    nc = kc_ref.shape[1]
    nsr = at_ref.shape[0]
    wkeys = wpad + nq
    last = (t0 + nq + kchunk - 1) // kchunk - 1
    lane = lax.broadcasted_iota(jnp.int32, (1, LANES), 1)
    tq_q = t0 + lax.broadcasted_iota(jnp.int32, (1, nq), 1)
    tq = jnp.concatenate([tq_q] * NSA_GROUP, axis=1)

    @pl.when(step == 0)
    def _():
        srow = lax.broadcasted_iota(jnp.int32, (LANES, 1), 0)
        cidx = lax.broadcasted_iota(jnp.int32, (nc, 1), 0).astype(F32)
        vct_f = jnp.transpose(vc_ref[0]).astype(BF16)
        for hk in range(NSA_KV_HEADS):
            own = (lane >= hk * NSA_DH) & (lane < (hk + 1) * NSA_DH)
            f0, f1, f2 = _feature_lanes(hk)
            kca[hk] = jnp.where(own, kc_ref[0], jnp.where(lane == f2, cidx, 0.0)).astype(BF16)
            vct[hk] = vct_f
            kwr[hk, 0:wpad, :] = jnp.broadcast_to(jnp.where(lane == f0, -1e30, 0.0), (wpad, LANES)).astype(BF16)
            vwa[hk, :, 0:wpad] = jnp.zeros((LANES, wpad), BF16)

        def prep(r, _):
            r0 = pl.multiple_of(r * LANES, LANES)
            cs = pl.ds(r0, LANES)
            pos = r0 + lax.broadcasted_iota(jnp.int32, (LANES, 1), 0)
            blk = pos >> 6
            ks_rows = jnp.transpose(ks_ref[0, :, cs])
            kw_rows = jnp.transpose(kw_ref[0, :, cs])
            for hk in range(NSA_KV_HEADS):
                own = (lane >= hk * NSA_DH) & (lane < (hk + 1) * NSA_DH)
                own_r = (srow >= hk * NSA_DH) & (srow < (hk + 1) * NSA_DH)
                f0, f1, f2 = _feature_lanes(hk)
                feat = jnp.where(lane == f0, blk.astype(F32), jnp.where(lane == f1, (pos & 63).astype(F32), 0.0))
                ksr[hk, cs, 0:LANES] = jnp.where(own, ks_rows, feat).astype(BF16)
                ksr[hk, cs, LANES:2 * LANES] = (lane == blk).astype(BF16)
                kwr[hk, pl.ds(wpad + r0, LANES), :] = jnp.where(own, kw_rows, feat).astype(BF16)
                vsa[hk, :, cs] = jnp.where(own_r, vs_ref[0, :, cs], 1.0).astype(BF16)
                vwa[hk, :, pl.ds(wpad + r0, LANES)] = jnp.where(own_r, vw_ref[0, :, cs], 1.0).astype(BF16)
            return 0

        lax.fori_loop(0, t // LANES, prep, 0)

    @pl.when(c == 0)
    def _():
        qt = jnp.transpose(q_ref[0]) * (NSA_DH ** -0.5)
        cend = lax.broadcasted_iota(jnp.int32, (nc, 1), 0) * CMP_STRIDE + (2 * CMP_STRIDE - 1)
        wrow = lax.broadcasted_iota(jnp.int32, (LANES, 1), 0)
        for hk in range(NSA_KV_HEADS):
            slabs = []
            for g in range(NSA_GROUP):
                h = hk * NSA_GROUP + g
                halves = [qt[h * NSA_DH:(h + 1) * NSA_DH], _head_feature_rows(hk, g, nq)]
                slabs.append(jnp.concatenate(halves if hk == 0 else halves[::-1], axis=0))
            qlo = jnp.concatenate(slabs, axis=1).astype(BF16)
            s = jnp.dot(kca[hk], qlo, preferred_element_type=F32)
            valid = cend <= tq
            s = jnp.where(valid, s, NEG_INF)
            p = jnp.where(valid, jnp.exp(s - jnp.max(s, axis=0, keepdims=True)), 0.0)
            p = p / jnp.maximum(jnp.sum(p, axis=0, keepdims=True), 1e-30)
            oc_s[hk] = jnp.dot(vct[hk], p.astype(BF16), preferred_element_type=F32)
            psum = p[:, 0:nq]
            for g in range(1, NSA_GROUP):
                psum = psum + p[:, g * nq:(g + 1) * nq]
            imp_t = sum(jnp.dot(at_ref[...], part, preferred_element_type=F32)
                        for part in _bf16_parts(psum))
            bias = (_select_t(imp_t, tq_q, ns, n_top) - 1.0) * 1e30
            bias = jnp.concatenate([bias, jnp.zeros((LANES - nsr, nq), F32)], axis=0) if nsr < LANES else bias
            qa_s[hk, 0:LANES, :] = qlo
            qa_s[hk, LANES:2 * LANES, :] = jnp.concatenate([bias] * NSA_GROUP, axis=1).astype(BF16)
            s = jnp.dot(kwr[hk, pl.ds(t0, wkeys), :], qlo, preferred_element_type=F32)
            old = jnp.where(t0 - wpad + wrow > tq - WINDOW, s[0:LANES], NEG_INF)
            new = jnp.where(t0 + wrow[0:nq] <= tq, s[wkeys - nq:wkeys], NEG_INF)
            s = jnp.concatenate([old, s[LANES:wkeys - nq], new], axis=0)
            p = jnp.exp(s - jnp.max(s, axis=0, keepdims=True)).astype(BF16)
            ow = jnp.dot(vwa[hk, :, pl.ds(t0, wkeys)], p, preferred_element_type=F32)
            l_w = _swap_halves(ow)
            ow_s[hk] = ow / jnp.maximum(l_w, 1e-30)
            m_s[hk] = jnp.full((8, cols), NEG_INF, F32)
            acc_s[hk] = jnp.zeros((LANES, cols), F32)

    k0 = pl.multiple_of(c * kchunk, kchunk)

    def flash(causal):
        for hk in range(NSA_KV_HEADS):
            s = jnp.dot(ksr[hk, pl.ds(k0, kchunk), :], qa_s[hk], preferred_element_type=F32)
            if causal:
                s = jnp.where(k0 + lax.broadcasted_iota(jnp.int32, (kchunk, 1), 0) <= tq, s, NEG_INF)
            m_old = m_s[hk][0:1, :]
            m_new = jnp.maximum(m_old, jnp.max(s, axis=0, keepdims=True))
            p = jnp.exp(s - m_new).astype(BF16)
            acc_s[hk] = jnp.exp(m_old - m_new) * acc_s[hk] + jnp.dot(vsa[hk, :, pl.ds(k0, kchunk)], p,
                                                                   preferred_element_type=F32)
            m_s[hk] = jnp.broadcast_to(m_new, (8, cols))

    @pl.when(c < last)
    def _():
        flash(False)

    @pl.when(c == last)
    def _():
        flash(True)
        gates = jnp.transpose(_sigmoid(sm_ref[0] + bg_ref[...]))
        srow = lax.broadcasted_iota(jnp.int32, (LANES, 1), 0)
        halves = [None] * NSA_HEADS
        for hk in range(NSA_KV_HEADS):
            own_r = (srow >= hk * NSA_DH) & (srow < (hk + 1) * NSA_DH)
            acc = acc_s[hk]
            o_s = acc / jnp.maximum(_swap_halves(acc), 1e-30)
            o_c = oc_s[hk]
            o_w = ow_s[hk]
            for g in range(NSA_GROUP):
                h = hk * NSA_GROUP + g
                cs = slice(g * nq, (g + 1) * nq)
                og = (gates[3 * h:3 * h + 1] * o_c[:, cs] + gates[3 * h + 1:3 * h + 2] * o_s[:, cs]
                      + gates[3 * h + 2:3 * h + 3] * o_w[:, cs])
                og = jnp.where(own_r, og, 0.0)
                ms = jnp.sum(og * og, axis=0, keepdims=True) * (1.0 / NSA_DH)
                y = og * lax.rsqrt(ms + EPS) * ght_ref[:, h:h + 1]
                halves[h] = y[hk * NSA_DH:(hk + 1) * NSA_DH]
        for k in range(NSA_HEADS // 2):
            o_ref[0, :, k * LANES:(k + 1) * LANES] = jnp.transpose(
                jnp.concatenate([halves[2 * k], halves[2 * k + 1]], axis=0))


def _nsa_prompt_t(q, sm, k_cmp, v_cmp, ks, vs, kw, vw, bg, gh, nq):
    b, t, _ = q.shape
    nc = k_cmp.shape[1]
    ns = t // SEL_BLOCK
    nsr = -(-ns // 8) * 8
    assert nq == LANES and nsr <= LANES and nc <= 256 and t % 512 == 0, "prompt NSA kernel shape limits"
    at = _sel_matrix(nc, nc, nsr).T.astype(BF16)
    kchunk = min(512, t)
    wpad = WINDOW
    pairs = [(i, c) for i in range(t // nq) for c in range(-(-((i + 1) * nq) // kchunk))]
    qi = jnp.asarray(np.array([p[0] for p in pairs], np.int32))
    ci = jnp.asarray(np.array([p[1] for p in pairs], np.int32))
    cols = NSA_GROUP * nq
    full = lambda n: pl.BlockSpec((1, n, LANES), lambda bi, s, qi, ci: (bi, 0, 0))
    full_t = pl.BlockSpec((1, LANES, t), lambda bi, s, qi, ci: (bi, 0, 0))
    const = lambda shape: pl.BlockSpec(shape, lambda bi, s, qi, ci: (0,) * len(shape))
    tok = lambda w: pl.BlockSpec((1, nq, w), lambda bi, s, qi, ci: (bi, qi[s], 0))
    grid_spec = pltpu.PrefetchScalarGridSpec(
        num_scalar_prefetch=2,
        grid=(b, len(pairs)),
        in_specs=[tok(4 * LANES), tok(LANES), full(nc), full(nc), full_t, full_t, full_t, full_t,
                  const(at.shape), const((1, LANES)), const((LANES, NSA_HEADS))],
        out_specs=tok(4 * LANES),
        scratch_shapes=[pltpu.VMEM((NSA_KV_HEADS, t, 2 * LANES), BF16),
                        pltpu.VMEM((NSA_KV_HEADS, wpad + t, LANES), BF16),
                        pltpu.VMEM((NSA_KV_HEADS, LANES, t), BF16),
                        pltpu.VMEM((NSA_KV_HEADS, LANES, wpad + t), BF16),
                        pltpu.VMEM((NSA_KV_HEADS, nc, LANES), BF16),
                        pltpu.VMEM((NSA_KV_HEADS, LANES, nc), BF16),
                        pltpu.VMEM((NSA_KV_HEADS, 2 * LANES, cols), BF16),
                        pltpu.VMEM((NSA_KV_HEADS, LANES, cols), F32),
                        pltpu.VMEM((NSA_KV_HEADS, LANES, cols), F32),
                        pltpu.VMEM((NSA_KV_HEADS, 8, cols), F32),
                        pltpu.VMEM((NSA_KV_HEADS, LANES, cols), F32)],
    )
    return pl.pallas_call(
        functools.partial(_nsa_prompt_t_kernel, nq=nq, ns=ns, n_top=min(SEL_TOPK, ns), kchunk=kchunk, wpad=wpad),
        grid_spec=grid_spec,
        out_shape=jax.ShapeDtypeStruct((b, t, 4 * LANES), F32),
        compiler_params=_params(("arbitrary", "arbitrary")),
        name="nsa_prompt",
    )(qi, ci, q, sm, k_cmp, v_cmp, ks, vs, kw, vw, at, bg, gh.T)


def _nsa_sample_kernel(pt_ref, q_ref, sm_ref, kc_ref, vc_ref, ks_pool, vs_pool, ksn_ref, vsn_ref,
                       wk_ref, wv_ref, kwn_ref, vwn_ref, a_ref, bg_ref, gh_ref, o_ref,
                       kbuf, vbuf, sem, q_s, sel_s, oc_s, ow_s, m_s, l_s, acc_s,
                       *, ch, nchunks, offset, ns, n_top, kchunk):
    b = pl.program_id(0)
    j = pl.program_id(1)
    nq = q_ref.shape[1]
    rows = NSA_GROUP * nq
    w_pre = wk_ref.shape[2]

    def page_copies(s):
        page = pt_ref[b, j * ch + s]
        dst = pl.ds(s * PAGE, PAGE)
        return (pltpu.make_async_copy(ks_pool.at[page], kbuf.at[:, dst], sem.at[0]),
                pltpu.make_async_copy(vs_pool.at[page], vbuf.at[:, dst], sem.at[1]))

    for s in range(ch):
        ck, cv = page_copies(s)
        ck.start()
        cv.start()

    @pl.when(j == 0)
    def _():
        for hk in range(NSA_KV_HEADS):
            qh, _ = _q_rows(q_ref, hk, nq)
            slope, tq = _row_consts(hk, nq, offset)
            o_c, imp = _cmp_branch(qh, slope, tq, kc_ref, vc_ref, a_ref, nq)
            sel = _select(imp, tq[0:nq], ns, n_top)
            q_s[hk] = qh
            sel_s[hk] = jnp.concatenate([sel] * NSA_GROUP, axis=0).astype(BF16)
            oc_s[hk] = o_c
            pos1 = offset - w_pre + lax.broadcasted_iota(jnp.int32, (1, w_pre), 1)
            d1 = tq - pos1
            pos2 = offset + lax.broadcasted_iota(jnp.int32, (1, kwn_ref.shape[1]), 1)
            d2 = tq - pos2
            v1 = (d1 >= 0) & (d1 < WINDOW)
            v2 = (d2 >= 0) & (d2 < WINDOW)
            s1 = jnp.where(v1, _mm(qh, wk_ref[0]) - slope * d1.astype(F32), NEG_INF)
            s2 = jnp.where(v2, _mm_nt(qh, kwn_ref[0]) - slope * d2.astype(F32), NEG_INF)
            mx = jnp.maximum(jnp.max(s1, axis=-1, keepdims=True), jnp.max(s2, axis=-1, keepdims=True))
            p1 = jnp.where(v1, jnp.exp(s1 - mx), 0.0)
            p2 = jnp.where(v2, jnp.exp(s2 - mx), 0.0)
            l_w = jnp.sum(p1, axis=-1, keepdims=True) + jnp.sum(p2, axis=-1, keepdims=True)
            ow_s[hk] = (_mm_nt(p1, wv_ref[0]) + _mm(p2, vwn_ref[0])) / jnp.maximum(l_w, 1e-30)
            m_s[hk] = jnp.full((rows, LANES), NEG_INF, F32)
            l_s[hk] = jnp.zeros((rows, LANES), F32)
            acc_s[hk] = jnp.zeros((rows, LANES), F32)

    for s in range(ch):
        ck, cv = page_copies(s)
        ck.wait()
        cv.wait()

    base = j * (ch * PAGE)
    for hk in range(NSA_KV_HEADS):
        qh = q_s[hk]
        slope, tq = _row_consts(hk, nq, offset)
        carry = (m_s[hk][:, 0:1], l_s[hk][:, 0:1], acc_s[hk])
        for c in range(ch * PAGE // kchunk):
            pos = base + c * kchunk + lax.broadcasted_iota(jnp.int32, (1, kchunk), 1)
            carry = _flash_step(qh, kbuf[:, pl.ds(c * kchunk, kchunk)], vbuf[:, pl.ds(c * kchunk, kchunk)],
                                pos, tq, slope, sel_s[hk], carry, feature_major=True)
        m_s[hk] = jnp.broadcast_to(carry[0], (rows, LANES))
        l_s[hk] = jnp.broadcast_to(carry[1], (rows, LANES))
        acc_s[hk] = carry[2]

    @pl.when(j == nchunks - 1)
    def _():
        gates = _sigmoid(sm_ref[0] + bg_ref[...])
        lane = lax.broadcasted_iota(jnp.int32, (1, LANES), 1)
        out_pairs = [None] * (NSA_HEADS // 2)
        for hk in range(NSA_KV_HEADS):
            own = (lane >= hk * NSA_DH) & (lane < (hk + 1) * NSA_DH)
            qh = q_s[hk]
            slope, tq = _row_consts(hk, nq, offset)
            pos = offset + lax.broadcasted_iota(jnp.int32, (1, ksn_ref.shape[1]), 1)
            carry = (m_s[hk][:, 0:1], l_s[hk][:, 0:1], acc_s[hk])
            _, l_f, acc_f = _flash_step(qh, ksn_ref[0], vsn_ref[0], pos, tq, slope, sel_s[hk], carry)
            o_s = acc_f / jnp.maximum(l_f, 1e-30)
            _combine_heads(oc_s[hk], o_s, ow_s[hk], gates, gh_ref, own, hk, nq, out_pairs)
        for k, y in enumerate(out_pairs):
            o_ref[0, :, k * LANES:(k + 1) * LANES] = y


def _nsa_sample(page_table, q, sm, k_cmp, v_cmp, ks_pool, vs_pool, ks_new, vs_new, win_k, win_v, kw_new, vw_new,
                bg, gh, nc_valid, ch):
    b, nq, _ = q.shape
    npg = page_table.shape[1]
    nchunks = npg // ch
    offset = npg * PAGE
    ncp = k_cmp.shape[1]
    ns = nc_valid // 4
    nsp = -(-ns // LANES) * LANES
    a = _sel_matrix(ncp, nc_valid, nsp)
    rows = NSA_GROUP * nq
    kchunk = min(512, ch * PAGE)
    per_b = lambda n, w=LANES: pl.BlockSpec((1, n, w), lambda bi, j, t: (bi, 0, 0))
    const = lambda shape: pl.BlockSpec(shape, lambda bi, j, t: (0,) * len(shape))
    grid_spec = pltpu.PrefetchScalarGridSpec(
        num_scalar_prefetch=1,
        grid=(b, nchunks),
        in_specs=[per_b(nq, 4 * LANES), per_b(nq), per_b(ncp), per_b(ncp),
                  pl.BlockSpec(memory_space=pl.ANY), pl.BlockSpec(memory_space=pl.ANY),
                  per_b(PAGE), per_b(PAGE), per_b(LANES, win_k.shape[2]), per_b(LANES, win_v.shape[2]), per_b(PAGE), per_b(PAGE),
                  const(a.shape), const((1, LANES)), const((NSA_HEADS, LANES))],
        out_specs=per_b(nq, 4 * LANES),
        scratch_shapes=[pltpu.VMEM((LANES, ch * PAGE), F32), pltpu.VMEM((LANES, ch * PAGE), F32),
                        pltpu.SemaphoreType.DMA((2,)),
                        pltpu.VMEM((NSA_KV_HEADS, rows, LANES), BF16),
                        pltpu.VMEM((NSA_KV_HEADS, rows, nsp), BF16),
                        pltpu.VMEM((NSA_KV_HEADS, rows, LANES), F32),
                        pltpu.VMEM((NSA_KV_HEADS, rows, LANES), F32),
                        pltpu.VMEM((NSA_KV_HEADS, rows, LANES), F32),
                        pltpu.VMEM((NSA_KV_HEADS, rows, LANES), F32),
                        pltpu.VMEM((NSA_KV_HEADS, rows, LANES), F32)],
    )
    return pl.pallas_call(
        functools.partial(_nsa_sample_kernel, ch=ch, nchunks=nchunks, offset=offset, ns=ns,
                          n_top=min(SEL_TOPK, ns), kchunk=kchunk),
        grid_spec=grid_spec,
        out_shape=jax.ShapeDtypeStruct((b, nq, 4 * LANES), F32),
        compiler_params=_params(("arbitrary", "arbitrary")),
        name="nsa_sample",
    )(page_table, q, sm, k_cmp, v_cmp, ks_pool, vs_pool, ks_new, vs_new, win_k, win_v, kw_new, vw_new, a, bg, gh)


def _chunk_features(nkeys):
    key = np.arange(nkeys)
    c = np.zeros((LANES, nkeys), np.float32)
    c[key // SEL_BLOCK, key] = 1.0
    c[NSA_DH, :] = key // SEL_BLOCK
    c[NSA_DH + 1, :] = key % SEL_BLOCK
    c[NSA_DH + 2, :] = 1.0
    return jnp.asarray(c, BF16)


def _nsa_sample2_kernel(pt_ref, q_ref, sm_ref, kc_ref, vc_ref, ks_pool, vs_pool, ksn_ref, vsn_ref,
                        wk_ref, wv_ref, kwn_ref, vwn_ref, a_ref, cf_ref, bg_ref, gh_ref, o_ref,
                        kbuf, vbuf, sem, q_s, bias_s, selnew_s, oc_s, ow_s, m_s, l_s, acc_s,
                        *, ch, nchunks, offset, ns, n_top, kchunk):
    b = pl.program_id(0)
    j = pl.program_id(1)
    nq = q_ref.shape[1]
    rows = NSA_GROUP * nq
    w_pre = wk_ref.shape[2]
    nkeys = ch * PAGE
    n = b * nchunks + j
    cur = n % 2
    lane = lax.broadcasted_iota(jnp.int32, (1, LANES), 1)

    def step_copies(bb, jj, buf_i, fn):
        for s in range(ch):
            page = pt_ref[bb, jj * ch + s]
            dst = pl.ds(s * PAGE, PAGE)
            fn(pltpu.make_async_copy(ks_pool.at[page], kbuf.at[buf_i, :, dst], sem.at[0, buf_i]))
            fn(pltpu.make_async_copy(vs_pool.at[page], vbuf.at[buf_i, :, dst], sem.at[1, buf_i]))

    @pl.when(n == 0)
    def _():
        step_copies(b, j, cur, lambda c: c.start())

    @pl.when(n + 1 < pl.num_programs(0) * nchunks)
    def _():
        wrap = j + 1 == nchunks
        step_copies(jnp.where(wrap, b + 1, b), jnp.where(wrap, 0, j + 1), 1 - cur, lambda c: c.start())

    @pl.when(j == 0)
    def _():
        qhs, imps = [], []
        for hk in range(NSA_KV_HEADS):
            rs = slice(hk * rows, (hk + 1) * rows)
            qh, _ = _q_rows(q_ref, hk, nq)
            slope, tq = _row_consts(hk, nq, offset)
            o_c, imp = _cmp_branch(qh, slope, tq, kc_ref, vc_ref, a_ref, nq)
            q_s[rs, :] = qh
            oc_s[rs, :] = o_c
            qhs.append(qh)
            imps.append(imp)
        _, tq = _row_consts(0, nq, offset)
        sel_all = _select(jnp.concatenate(imps, axis=0), jnp.concatenate([tq[0:nq]] * NSA_KV_HEADS, axis=0),
                          ns, n_top)
        for hk in range(NSA_KV_HEADS):
            rs = slice(hk * rows, (hk + 1) * rows)
            qh = qhs[hk]
            slope, tq = _row_consts(hk, nq, offset)
            sel_rows = jnp.concatenate([sel_all[hk * nq:(hk + 1) * nq]] * NSA_GROUP, axis=0)
            for jj in range(nchunks):
                nb_c = nkeys // SEL_BLOCK
                blocks = sel_rows[:, jj * nb_c:jj * nb_c + LANES]
                upper = jnp.where(lane < nb_c, (blocks - 1.0) * 1e30,
                                  jnp.where(lane == NSA_DH, slope * SEL_BLOCK,
                                            jnp.where(lane == NSA_DH + 1, slope,
                                                      jnp.where(lane == NSA_DH + 2,
                                                                slope * float(jj * nkeys - offset), 0.0))))
                bias_s[jj, rs, :] = upper.astype(BF16)
            new_blk = offset // SEL_BLOCK
            selnew_s[rs, :] = jnp.broadcast_to(sel_rows[:, new_blk:new_blk + 1], (rows, LANES))
            pos1 = offset - w_pre + lax.broadcasted_iota(jnp.int32, (1, w_pre), 1)
            d1 = tq - pos1
            pos2 = offset + lax.broadcasted_iota(jnp.int32, (1, kwn_ref.shape[1]), 1)
            d2 = tq - pos2
            v1 = (d1 >= 0) & (d1 < WINDOW)
            v2 = (d2 >= 0) & (d2 < WINDOW)
            s1 = jnp.where(v1, _mm(qh, wk_ref[0]) - slope * d1.astype(F32), NEG_INF)
            s2 = jnp.where(v2, _mm_nt(qh, kwn_ref[0]) - slope * d2.astype(F32), NEG_INF)
            mx = jnp.maximum(jnp.max(s1, axis=-1, keepdims=True), jnp.max(s2, axis=-1, keepdims=True))
            p1 = jnp.where(v1, jnp.exp(s1 - mx), 0.0)
            p2 = jnp.where(v2, jnp.exp(s2 - mx), 0.0)
            l_w = jnp.sum(p1, axis=-1, keepdims=True) + jnp.sum(p2, axis=-1, keepdims=True)
            ow_s[rs, :] = (_mm_nt(p1, wv_ref[0]) + _mm(p2, vwn_ref[0])) / jnp.maximum(l_w, 1e-30)
        m_s[...] = jnp.full(m_s.shape, NEG_INF, F32)
        l_s[...] = jnp.zeros(l_s.shape, F32)
        acc_s[...] = jnp.zeros(acc_s.shape, F32)

    step_copies(b, j, cur, lambda c: c.wait())

    qa = jnp.concatenate([q_s[...], bias_s[j]], axis=1)
    m, l, acc = m_s[:, 0:1], l_s[:, 0:1], acc_s[...]
    for c in range(nkeys // kchunk):
        cs = pl.ds(c * kchunk, kchunk)
        ka = jnp.concatenate([kbuf[cur, :, cs].astype(BF16), cf_ref[:, cs]], axis=0)
        s = jnp.dot(qa, ka, preferred_element_type=F32)
        m_new = jnp.maximum(m, jnp.max(s, axis=-1, keepdims=True))
        alpha = jnp.exp(m - m_new)
        p = jnp.exp(s - m_new)
        l = alpha * l + jnp.sum(p, axis=-1, keepdims=True)
        acc = alpha * acc + _mm_nt(p, vbuf[cur, :, cs])
        m = m_new
    m_s[...] = jnp.broadcast_to(m, m_s.shape)
    l_s[...] = jnp.broadcast_to(l, l_s.shape)
    acc_s[...] = acc

    @pl.when(j == nchunks - 1)
    def _():
        gates = _sigmoid(sm_ref[0] + bg_ref[...])
        out_pairs = [None] * (NSA_HEADS // 2)
        nnew = ksn_ref.shape[1]
        r_new = lax.broadcasted_iota(jnp.int32, (1, nnew), 1)
        for hk in range(NSA_KV_HEADS):
            rs = slice(hk * rows, (hk + 1) * rows)
            own = (lane >= hk * NSA_DH) & (lane < (hk + 1) * NSA_DH)
            slope, tq = _row_consts(hk, nq, 0)
            valid = (r_new <= tq) & (selnew_s[rs, 0:1] > 0.5)
            s = jnp.where(valid, _mm_nt(q_s[rs, :], ksn_ref[0]) + slope * r_new.astype(F32), NEG_INF)
            m_o = m_s[rs, 0:1]
            m_f = jnp.maximum(m_o, jnp.max(s, axis=-1, keepdims=True))
            alpha = jnp.exp(m_o - m_f)
            p = jnp.where(valid, jnp.exp(s - m_f), 0.0)
            l_f = alpha * l_s[rs, 0:1] + jnp.sum(p, axis=-1, keepdims=True)
            o_s = (alpha * acc_s[rs, :] + _mm(p, vsn_ref[0])) / jnp.maximum(l_f, 1e-30)
            _combine_heads(oc_s[rs, :], o_s, ow_s[rs, :], gates, gh_ref, own, hk, nq, out_pairs)
        for k, y in enumerate(out_pairs):
            o_ref[0, :, k * LANES:(k + 1) * LANES] = y


def _nsa_sample2(page_table, q, sm, k_cmp, v_cmp, ks_pool, vs_pool, ks_new, vs_new, win_k, win_v, kw_new, vw_new,
                 bg, gh, nc_valid, ch):
    b, nq, _ = q.shape
    npg = page_table.shape[1]
    nchunks = npg // ch
    offset = npg * PAGE
    ncp = k_cmp.shape[1]
    ns = nc_valid // 4
    nsp = -(-(ns + LANES) // LANES) * LANES
    assert ch * PAGE // SEL_BLOCK <= NSA_DH, "a chunk's blocks must fit the 64 mask lanes"
    a = _sel_matrix(ncp, nc_valid, nsp).astype(BF16)
    rows2 = NSA_KV_HEADS * NSA_GROUP * nq
    kchunk = min(512, ch * PAGE)
    cf = _chunk_features(ch * PAGE)
    per_b = lambda n, w=LANES: pl.BlockSpec((1, n, w), lambda bi, j, t: (bi, 0, 0))
    const = lambda shape: pl.BlockSpec(shape, lambda bi, j, t: (0,) * len(shape))
    grid_spec = pltpu.PrefetchScalarGridSpec(
        num_scalar_prefetch=1,
        grid=(b, nchunks),
        in_specs=[per_b(nq, 4 * LANES), per_b(nq), per_b(ncp), per_b(ncp),
                  pl.BlockSpec(memory_space=pl.ANY), pl.BlockSpec(memory_space=pl.ANY),
                  per_b(PAGE), per_b(PAGE), per_b(LANES, win_k.shape[2]), per_b(LANES, win_v.shape[2]),
                  per_b(PAGE), per_b(PAGE),
                  const(a.shape), const(cf.shape), const((1, LANES)), const((NSA_HEADS, LANES))],
        out_specs=per_b(nq, 4 * LANES),
        scratch_shapes=[pltpu.VMEM((2, LANES, ch * PAGE), F32), pltpu.VMEM((2, LANES, ch * PAGE), F32),
                        pltpu.SemaphoreType.DMA((2, 2)),
                        pltpu.VMEM((rows2, LANES), BF16),
                        pltpu.VMEM((nchunks, rows2, LANES), BF16)]
        + [pltpu.VMEM((rows2, LANES), F32)] * 6,
    )
    return pl.pallas_call(
        functools.partial(_nsa_sample2_kernel, ch=ch, nchunks=nchunks, offset=offset, ns=ns,
                          n_top=min(SEL_TOPK, ns), kchunk=kchunk),
        grid_spec=grid_spec,
        out_shape=jax.ShapeDtypeStruct((b, nq, 4 * LANES), F32),
        compiler_params=_params(("arbitrary", "arbitrary")),
        name="nsa_sample",
    )(page_table, q, sm, k_cmp, v_cmp, ks_pool, vs_pool, ks_new, vs_new, win_k, win_v, kw_new, vw_new, a, cf, bg, gh)


def _mlstm_kernel(xm_ref, vm_ref, om_ref, sm_ref, smt_ref, conv0_ref, cw_ref, cb_ref, wq_ref, wk_ref,
                  bcol_ref, brow_ref, gh_ref, c0_ref, n0_ref, m0_ref,
                  hm_ref, c_ref, n_ref, m_ref, xs_ref, *, L, t_valid):
    c = pl.program_id(1)
    dh = MLSTM_DH

    @pl.when(c == 0)
    def _():
        xs_ref[0:8, :] = jnp.zeros((8, xs_ref.shape[1]), F32)
        xs_ref[5:8, :] = conv0_ref[0]
        c_ref[...] = c0_ref[...]
        n_ref[...] = n0_ref[...]
        m_ref[...] = m0_ref[...]

    xs_ref[8:8 + L, :] = xm_ref[0]
    xc = cb_ref[...]
    for jj in range(4):
        xc = xc + cw_ref[jj:jj + 1, :] * xs_ref[pl.ds(5 + jj, L), :]
    xc = xc * _sigmoid(xc)
    xs_ref[0:8, :] = xs_ref[L:L + 8, :]

    pre_col = sm_ref[0] + brow_ref[...]
    pre_row = smt_ref[0, 0] + bcol_ref[...]
    lf_col = _log_sigmoid(pre_col)
    lf_row = _log_sigmoid(pre_row)
    if t_valid < L:
        rid = lax.broadcasted_iota(jnp.int32, (L, 1), 0) < t_valid
        cid = lax.broadcasted_iota(jnp.int32, (1, L), 1) < t_valid
        lf_col = jnp.where(rid, lf_col, 0.0)
        lf_row = jnp.where(cid, lf_row, 0.0)
        pre_col = jnp.where(rid, pre_col, NEG_INF)
        pre_row = jnp.where(cid, pre_row, NEG_INF)
    ri = lax.broadcasted_iota(jnp.int32, (L, L), 0)
    ci = lax.broadcasted_iota(jnp.int32, (L, L), 1)
    causal = ci <= ri
    tril = causal.astype(BF16)
    triu = (ri <= ci).astype(BF16)
    bcum_col = sum(jnp.dot(tril, part, preferred_element_type=F32) for part in _bf16_parts(lf_col))
    bcum_row = sum(jnp.dot(part, triu, preferred_element_type=F32) for part in _bf16_parts(lf_row))

    for h in range(MLSTM_HEADS):
        hs = slice(h * dh, (h + 1) * dh)
        xh = xc[:, hs]
        q = _mm(xh, wq_ref[h])
        k = _mm(xh, wk_ref[h]) * (dh ** -0.5)
        v = vm_ref[0, :, hs]
        bc = bcum_col[:, 28 + h:29 + h]
        ic = pre_col[:, 24 + h:25 + h]
        br = bcum_row[4 + h:5 + h, :]
        ir = pre_row[h:h + 1, :]
        mh = m_ref[0, :, h:h + 1]
        ch_ = c_ref[0, h]
        nh = n_ref[0, h:h + 1, :]
        dmat = jnp.where(causal, bc - br + ir, NEG_INF)
        inter = bc + mh
        mq = jnp.maximum(inter, jnp.max(dmat, axis=1, keepdims=True))
        a = jnp.exp(dmat - mq) * _mm_nt(q, k)
        wi = jnp.exp(inter - mq)
        num = _mm(a, v) + wi * _mm_nt(q, ch_)
        den = jnp.sum(a, axis=1, keepdims=True) + wi * jnp.sum(q * nh, axis=1, keepdims=True)
        hout = num / jnp.maximum(jnp.abs(den), jnp.exp(-mq))
        btot = bc[L - 1:L, :]
        dec_r = btot - br + ir
        dec_c = btot - bc + ic
        m_new = jnp.maximum(btot + mh, jnp.max(dec_r, axis=1, keepdims=True))
        ws_c = jnp.exp(dec_c - m_new)
        w_c = jnp.exp(btot + mh - m_new)
        c_ref[0, h] = w_c * ch_ + _mm_tn(v * ws_c, k)
        n_ref[0, h:h + 1, :] = w_c * nh + jnp.sum(k * ws_c, axis=0, keepdims=True)
        m_ref[0, :, h:h + 1] = m_new
        y = _rms(hout, gh_ref[h:h + 1, :]) * _sigmoid(om_ref[0, :, hs])
        hm_ref[0, :, hs] = y


def _mlstm(xm, vm, om, sm, conv0, conv_w, conv_b, w_qm, w_km, b_i, b_f, g_head, c0, n0, m0, L, t_valid):
    b, t, w = xm.shape
    nck = t // L
    smt = sm[:, :, 24:32].reshape(b, nck, L, 8).transpose(0, 1, 3, 2)
    brow = jnp.zeros((1, LANES), F32).at[0, 24:28].set(b_i).at[0, 28:32].set(b_f)
    bcol = jnp.concatenate([b_i, b_f]).reshape(8, 1)
    m0p = jnp.zeros((b, 1, LANES), F32).at[:, 0, :MLSTM_HEADS].set(m0)
    tok = lambda: pl.BlockSpec((1, L, w), lambda bi, c: (bi, c, 0))
    const = lambda shape: pl.BlockSpec(shape, lambda bi, c: (0,) * len(shape))
    state = lambda shape: pl.BlockSpec((1,) + shape, lambda bi, c: (bi,) + (0,) * len(shape))
    hm, c_out, n_out, m_out = pl.pallas_call(
        functools.partial(_mlstm_kernel, L=L, t_valid=t_valid),
        grid=(b, nck),
        in_specs=[tok(), tok(), tok(),
                  pl.BlockSpec((1, L, LANES), lambda bi, c: (bi, c, 0)),
                  pl.BlockSpec((1, 1, 8, L), lambda bi, c: (bi, c, 0, 0)),
                  state(conv0.shape[1:]),
                  const(conv_w.shape), const((1, w)), const(w_qm.shape), const(w_km.shape),
                  const((8, 1)), const((1, LANES)), const(g_head.shape),
                  state(c0.shape[1:]), state(n0.shape[1:]), state((1, LANES))],
        out_specs=[tok(), state(c0.shape[1:]), state(n0.shape[1:]), state((1, LANES))],
        out_shape=[jax.ShapeDtypeStruct((b, t, w), F32), jax.ShapeDtypeStruct(c0.shape, F32),
                   jax.ShapeDtypeStruct(n0.shape, F32), jax.ShapeDtypeStruct((b, 1, LANES), F32)],
        scratch_shapes=[pltpu.VMEM((L + 8, w), F32)],
        compiler_params=_params(("arbitrary", "arbitrary")),
        name="mlstm",
    )(xm, vm, om, sm, smt, conv0, conv_w, conv_b.reshape(1, w), w_qm.astype(BF16), w_km.astype(BF16),
      bcol, brow, g_head, c0, n0, m0p)
    return hm, c_out, n_out, m_out[:, 0, :MLSTM_HEADS]


def _mix_kernel(x_ref, on_ref, hm_ref, wo1_ref, wo2_ref, gxa_ref, wxq_ref, x1_ref, qx_ref):
    x1 = x_ref[...] + _mm(on_ref[...], wo1_ref[...]) + _mm(hm_ref[...], wo2_ref[...])
    x1_ref[...] = x1
    qx_ref[...] = _mm(_rms(x1, gxa_ref[...]), wxq_ref[...]) * (XA_DH ** -0.5)


def _mix(x, o_nsa, hm, w_out, g_xa, w_xq, tm):
    m, d = x.shape
    half = o_nsa.shape[1]
    row = lambda w: pl.BlockSpec((tm, w), lambda i: (i, 0))
    const = lambda shape: pl.BlockSpec(shape, lambda i: (0, 0))
    return pl.pallas_call(
        _mix_kernel,
        grid=(m // tm,),
        in_specs=[row(d), row(half), row(half), const((half, d)), const((half, d)), const((1, d)), const((d, d))],
        out_specs=[row(d), row(d)],
        out_shape=[jax.ShapeDtypeStruct((m, d), F32)] * 2,
        compiler_params=_params(("parallel",)),
        name="mix",
    )(x, o_nsa, hm, w_out[:half].astype(BF16), w_out[half:].astype(BF16), g_xa.reshape(1, d), w_xq.astype(BF16))


def _xattn_kernel(qx_ref, mk_ref, mv_ref, o_ref):
    for h in range(XA_HEADS):
        hs = slice(h * XA_DH, (h + 1) * XA_DH)
        s = _mm_nt(qx_ref[0, :, hs], mk_ref[0, :, hs])
        p = jnp.exp(s - jnp.max(s, axis=-1, keepdims=True))
        o_ref[0, :, hs] = _mm(p, mv_ref[0, :, hs]) / jnp.sum(p, axis=-1, keepdims=True)


def _xattn(qx, mem_k, mem_v, tq):
    b, t, d = qx.shape
    nm = mem_k.shape[1]
    return pl.pallas_call(
        _xattn_kernel,
        grid=(b, t // tq),
        in_specs=[pl.BlockSpec((1, tq, d), lambda bi, i: (bi, i, 0)),
                  pl.BlockSpec((1, nm, d), lambda bi, i: (bi, 0, 0)),
                  pl.BlockSpec((1, nm, d), lambda bi, i: (bi, 0, 0))],
        out_specs=pl.BlockSpec((1, tq, d), lambda bi, i: (bi, i, 0)),
        out_shape=jax.ShapeDtypeStruct((b, t, d), F32),
        compiler_params=_params(("parallel", "parallel")),
        name="xattn",
    )(qx, mem_k, mem_v)


def _ffn_kernel(x1_ref, ox_ref, wxo_ref, gf_ref, wg_ref, wu_ref, wd_ref, gfin_ref, y_ref, x2_s, h_s, acc_s):
    j = pl.program_id(1)

    @pl.when(j == 0)
    def _():
        x2 = x1_ref[...] + _mm(ox_ref[...], wxo_ref[...])
        x2_s[...] = x2
        h_s[...] = _rms(x2, gf_ref[...]).astype(BF16)
        acc_s[...] = jnp.zeros(acc_s.shape, F32)

    h = h_s[...]
    g = jnp.dot(h, wg_ref[...], preferred_element_type=F32)
    u = jnp.dot(h, wu_ref[...], preferred_element_type=F32)
    acc_s[...] += _mm(g * _sigmoid(g) * u, wd_ref[...])

    @pl.when(j == pl.num_programs(1) - 1)
    def _():
        y_ref[...] = _rms(x2_s[...] + acc_s[...], gfin_ref[...])


def _ffn(x1, ox, w_xo, g_ffn, w_gate, w_up, w_down, g_final, tm, tf):
    m, d = x1.shape
    dff = w_gate.shape[1]
    row = pl.BlockSpec((tm, d), lambda i, j: (i, 0))
    vec = pl.BlockSpec((1, d), lambda i, j: (0, 0))
    return pl.pallas_call(
        _ffn_kernel,
        grid=(m // tm, dff // tf),
        in_specs=[row, row, pl.BlockSpec((d, d), lambda i, j: (0, 0)), vec,
                  pl.BlockSpec((d, tf), lambda i, j: (0, j)), pl.BlockSpec((d, tf), lambda i, j: (0, j)),
                  pl.BlockSpec((tf, d), lambda i, j: (j, 0)), vec],
        out_specs=row,
        out_shape=jax.ShapeDtypeStruct((m, d), F32),
        scratch_shapes=[pltpu.VMEM((tm, d), F32), pltpu.VMEM((tm, d), BF16), pltpu.VMEM((tm, d), F32)],
        compiler_params=_params(("parallel", "arbitrary")),
        name="ffn",
    )(x1, ox, w_xo.astype(BF16), g_ffn.reshape(1, d), w_gate.astype(BF16), w_up.astype(BF16),
      w_down.astype(BF16), g_final.reshape(1, d))


def _split_w_in(w_in, nsa_w, kv_w, mlstm_w):
    cuts = np.cumsum([nsa_w] + [kv_w] * 6 + [3 * NSA_HEADS] + [mlstm_w] * 3 + [MLSTM_HEADS] * 2)
    parts = jnp.split(w_in, cuts[:-1].tolist(), axis=1)
    small = jnp.concatenate([parts[7], parts[11], parts[12]], axis=1)
    small = jnp.pad(small, ((0, 0), (0, LANES - small.shape[1])))
    ws = [parts[0]] + list(parts[1:7]) + [small] + list(parts[8:11])
    return [w.astype(BF16) for w in ws]


def _tail(x1, ox, w, tm, b, t):
    d = x1.shape[1]
    dff = w["w_gate"].shape[1]
    tf = dff // 2 if (dff // 2) % LANES == 0 else dff
    y = _ffn(x1, ox.reshape(-1, d), w["w_xo"], w["g_ffn"], w["w_gate"], w["w_up"], w["w_down"], w["g_final"], tm, tf)
    return y.reshape(b, t, d)


def _gate_consts(w):
    bg = jnp.pad(w["b_gate"], (0, LANES - w["b_gate"].shape[0])).reshape(1, LANES)
    gh = jnp.concatenate([w["g_head_nsa"], w["g_head_nsa"]], axis=1)
    return bg, gh


def _prompt_group(x, mem, w):
    b, t, d = x.shape
    m = b * t
    tm = 512
    wp = w["w_in_parts"]
    wt = jnp.concatenate(wp[1:7], axis=1).T
    q, sm, xm, vm, om, kc, vc, ks, vs, kw, vw = _norm_proj(x.reshape(m, d), w["g_mix"], [wp[0]] + wp[7:], tm,
                                                           wt=wt, t=t)
    r3 = lambda a: a.reshape(b, t, a.shape[-1])
    npg = t // PAGE
    table = jnp.zeros((b, npg), jnp.int32)
    zeros_next = jnp.zeros((b, PAGE, LANES), F32)
    k_cmp = _compress(table, kc, zeros_next, *w["cmp_k"], ch=npg, paged=False, feature_major=True)
    v_cmp = _compress(table, vc, zeros_next, *w["cmp_v"], ch=npg, paged=False, feature_major=True)
    bg, gh = _gate_consts(w)
    o_nsa = _nsa_prompt_t(r3(q), r3(sm), k_cmp, v_cmp, ks, vs, kw, vw, bg, gh, LANES)
    L = next(c for c in (256, 128, 64) if t % c == 0)
    hm, c_out, n_out, m_out = _mlstm(
        r3(xm), r3(vm), r3(om), r3(sm), jnp.zeros((b, 3, xm.shape[1]), F32), w["conv_w"], w["conv_b"],
        w["w_qm"], w["w_km"], w["b_i"], w["b_f"], w["g_head_m"],
        jnp.zeros((b, MLSTM_HEADS, MLSTM_DH, MLSTM_DH), F32), jnp.zeros((b, MLSTM_HEADS, MLSTM_DH), F32),
        jnp.zeros((b, MLSTM_HEADS), F32), L, L)
    nm = mem.shape[1]
    mk, mv = _norm_proj(mem.reshape(b * nm, d), w["g_mem"], [w["w_xk"].astype(BF16), w["w_xv"].astype(BF16)],
                        min(512, b * nm))
    x1, qx = _mix(x.reshape(m, d), o_nsa.reshape(m, -1), hm.reshape(m, -1), w["w_out"], w["g_xa"], w["w_xq"], tm)
    ox = _xattn(qx.reshape(b, t, d), mk.reshape(b, nm, d), mv.reshape(b, nm, d), 512)
    y = _tail(x1, ox, w, tm, b, t)
    kv5 = lambda a: a.reshape(b, NSA_KV_HEADS, NSA_DH, a.shape[2]).transpose(0, 3, 1, 2)[None]
    keep = min(WINDOW, t)
    xm3 = r3(xm)
    states = (kv5(kc), kv5(vc), kv5(ks), kv5(vs), kv5(kw[:, :, t - keep:]), kv5(vw[:, :, t - keep:]),
              c_out[None], n_out[None], m_out[None], xm3[None, :, t - 3:],
              mk.reshape(1, b, nm, XA_HEADS, XA_DH), mv.reshape(1, b, nm, XA_HEADS, XA_DH))
    return y, states


def _sample_group(x, pools, page_table, win_k, win_v, conv0, c0, n0, m0, mem_k, mem_v, w):
    b, t, d = x.shape
    m = b * t
    tp = 8
    tm = min(m, 512)
    q, kc, vc, ks, vs, kw, vw, sm, xm, vm, om = _norm_proj(x.reshape(m, d), w["g_mix"], w["w_in_parts"], tm)
    r3 = lambda a: a.reshape(b, t, a.shape[-1])
    pad_t = lambda a: jnp.pad(r3(a), ((0, 0), (0, tp - t), (0, 0)))
    npg = page_table.shape[1]
    past = npg * PAGE
    ch = min(32, npg)
    fm = lambda a: a.transpose(0, 2, 3, 1).reshape(a.shape[0], LANES, a.shape[1])
    pool_kc, pool_vc, pool_ks, pool_vs = [fm(p) for p in pools]
    win_kt, win_vt = fm(win_k), fm(win_v)
    nc_valid = (past + (-(-t // SEL_BLOCK)) * SEL_BLOCK) // CMP_STRIDE
    ncp = -(-nc_valid // LANES) * LANES

    pad_page = lambda a: jnp.pad(r3(a), ((0, 0), (0, PAGE - t), (0, 0)))

    def compressed(pool, new_rows, cw):
        nxt = pad_page(new_rows)
        main = _compress(page_table, pool, nxt, *cw, ch=ch, paged=True, feature_major=True)
        tail = _compress(jnp.arange(b, dtype=jnp.int32).reshape(1, b), nxt, jnp.zeros((1, PAGE, LANES), F32),
                         *cw, ch=b, paged=True, feature_major=False)
        n_tail = nc_valid - npg * CMP_PER_PAGE
        full = jnp.concatenate([main, tail.reshape(b, CMP_PER_PAGE, LANES)[:, :n_tail]], axis=1)
        return jnp.pad(full, ((0, 0), (0, ncp - nc_valid), (0, 0)))

    k_cmp = compressed(pool_kc, kc, w["cmp_k"])
    v_cmp = compressed(pool_vc, vc, w["cmp_v"])
    bg, gh = _gate_consts(w)
    o_nsa = _nsa_sample2(page_table, pad_t(q), pad_t(sm), k_cmp, v_cmp, pool_ks, pool_vs, pad_page(ks), pad_page(vs),
                        win_kt, win_vt, pad_page(kw), pad_page(vw),
                        bg, gh, nc_valid, ch)[:, :t]
    hm, c_out, n_out, m_out = _mlstm(pad_t(xm), pad_t(vm), pad_t(om), pad_t(sm), conv0, w["conv_w"], w["conv_b"],
                                     w["w_qm"], w["w_km"], w["b_i"], w["b_f"], w["g_head_m"], c0, n0, m0, tp, t)
    hm = hm[:, :t]
    x1, qx = _mix(x.reshape(m, d), o_nsa.reshape(m, -1), hm.reshape(m, -1), w["w_out"], w["g_xa"], w["w_xq"], tm)
    nm = mem_k.shape[1]
    ox = _xattn(pad_t(qx), mem_k.reshape(b, nm, d), mem_v.reshape(b, nm, d), tp)[:, :t]
    y = _tail(x1, ox, w, tm, b, t)
    kv5 = lambda a: a.reshape(1, b, t, NSA_KV_HEADS, NSA_DH)
    keep = min(WINDOW, past + t)
    unfm = lambda a: a.reshape(b, NSA_KV_HEADS, NSA_DH, a.shape[2]).transpose(0, 3, 1, 2)[None]
    win5 = lambda old_t, new: unfm(jnp.concatenate([old_t, r3(new).transpose(0, 2, 1)], axis=2)[:, :, -keep:])
    conv_all = jnp.concatenate([conv0, r3(xm)], axis=1)
    states = (kv5(kc), kv5(vc), kv5(ks), kv5(vs), win5(win_kt, kw), win5(win_vt, vw),
              c_out[None], n_out[None], m_out[None], conv_all[None, :, -3:])
    return y, states


def kernel(x_prompt, x_sample, cache_k_cmp, cache_v_cmp, cache_k_slc, cache_v_slc, state_k_win, state_v_win, state_conv, state_C, state_n, state_m, cache_mem_k, cache_mem_v, page_table, mem_prompt, g_mix, w_in, b_gate, cmp_pe_k, cmp_w1_k, cmp_b1_k, cmp_w2_k, cmp_pe_v, cmp_w1_v, cmp_b1_v, cmp_w2_v, g_head_nsa, conv_w, conv_b, w_qm, w_km, b_i, b_f, g_head_m, w_out, g_xa, g_mem, w_xq, w_xk, w_xv, w_xo, g_ffn, w_gate, w_up, w_down, g_final):
    assert w_in.shape[0] == 1, "single-layer decoder"
    l = 0
    w = dict(g_mix=g_mix[l], b_gate=b_gate[l],
             w_in_parts=_split_w_in(w_in[l], NSA_HEADS * NSA_DH, NSA_KV_HEADS * NSA_DH, MLSTM_HEADS * MLSTM_DH),
             cmp_k=(cmp_pe_k[l], cmp_w1_k[l], cmp_b1_k[l], cmp_w2_k[l]),
             cmp_v=(cmp_pe_v[l], cmp_w1_v[l], cmp_b1_v[l], cmp_w2_v[l]),
             g_head_nsa=g_head_nsa[l], conv_w=conv_w[l], conv_b=conv_b[l], w_qm=w_qm[l], w_km=w_km[l],
             b_i=b_i[l], b_f=b_f[l], g_head_m=g_head_m[l], w_out=w_out[l], g_xa=g_xa[l], g_mem=g_mem[l],
             w_xq=w_xq[l], w_xk=w_xk[l], w_xv=w_xv[l], w_xo=w_xo[l], g_ffn=g_ffn[l], w_gate=w_gate[l],
             w_up=w_up[l], w_down=w_down[l], g_final=g_final)
    y_p, st_p = _prompt_group(x_prompt, mem_prompt, w)
    pools = (cache_k_cmp[l], cache_v_cmp[l], cache_k_slc[l], cache_v_slc[l])
    y_s, st_s = _sample_group(x_sample, pools, page_table, state_k_win[l], state_v_win[l], state_conv[l],
                              state_C[l], state_n[l], state_m[l], cache_mem_k[l], cache_mem_v[l], w)
    return (y_p, y_s) + st_p + st_s
```

```python
import functools

import numpy as np
import jax
import jax.numpy as jnp
from jax import lax
from jax.experimental import pallas as pl
from jax.experimental.pallas import tpu as pltpu

F32 = jnp.float32
BF16 = jnp.bfloat16
HIGHEST = lax.Precision.HIGHEST

EPS = 1e-6
NEG_INF = -1e30
FORCE_SCORE = 1e9
PAD_SCORE = -2e38
TAKEN_SCORE = -3e38

LANES = 128
NSA_HEADS = 8
NSA_KV_HEADS = 2
NSA_GROUP = 4
NSA_DH = 64
CMP_STRIDE = 16
SEL_BLOCK = 64
SEL_TOPK = 16
WINDOW = 512
Q_BLOCK = 64
PAGE = 128
CMP_PER_PAGE = PAGE // CMP_STRIDE
MLSTM_HEADS = 4
MLSTM_DH = 128
XA_HEADS = 4
XA_DH = 256
ALIBI = tuple(2.0 ** (-(h + 1)) for h in range(NSA_HEADS))

VMEM_LIMIT = 56 * 1024 * 1024


def _params(sem):
    return pltpu.CompilerParams(dimension_semantics=sem, vmem_limit_bytes=VMEM_LIMIT)


def _mm(a, b):
    return jnp.dot(a.astype(BF16), b.astype(BF16), preferred_element_type=F32)


def _mm_nt(a, b):
    return lax.dot_general(a.astype(BF16), b.astype(BF16), (((1,), (1,)), ((), ())),
                           preferred_element_type=F32)


def _mm_tn(a, b):
    return lax.dot_general(a.astype(BF16), b.astype(BF16), (((0,), (0,)), ((), ())),
                           preferred_element_type=F32)


def _mm_split(a, b_bf16):
    hi = a.astype(BF16)
    lo = (a - hi.astype(F32)).astype(BF16)
    return (jnp.dot(hi, b_bf16, preferred_element_type=F32) + jnp.dot(lo, b_bf16, preferred_element_type=F32))


def _bf16_parts(x):
    hi = x.astype(BF16)
    r1 = x - hi.astype(F32)
    mid = r1.astype(BF16)
    lo = (r1 - mid.astype(F32)).astype(BF16)
    return hi, mid, lo


def _mm_f32(a, b):
    return jnp.dot(a, b, precision=HIGHEST, preferred_element_type=F32)


def _rms(x, g):
    return x * lax.rsqrt(jnp.mean(x * x, axis=-1, keepdims=True) + EPS) * g


def _sigmoid(x):
    return 1.0 / (1.0 + jnp.exp(-x))


def _log_sigmoid(x):
    return jnp.minimum(x, 0.0) - jnp.log(1.0 + jnp.exp(-jnp.abs(x)))


def _gelu_tanh(x):
    return 0.5 * x * (1.0 + jnp.tanh(0.7978845608028654 * (x + 0.044715 * x * x * x)))


def _topk_mask(imp, k):
    col = lax.broadcasted_iota(jnp.int32, imp.shape, 1)
    sel = jnp.zeros(imp.shape, F32)
    work = imp
    for _ in range(k):
        m = jnp.max(work, axis=-1, keepdims=True)
        idx = jnp.min(jnp.where(work == m, col, jnp.int32(2 ** 30)), axis=-1, keepdims=True)
        hit = col == idx
        sel = jnp.where(hit, 1.0, sel)
        work = jnp.where(hit, TAKEN_SCORE, work)
    return sel


def _softmax_parts(s, valid):
    s = jnp.where(valid, s, NEG_INF)
    mx = jnp.max(s, axis=-1, keepdims=True)
    p = jnp.where(valid, jnp.exp(s - mx), 0.0)
    return p, mx, jnp.sum(p, axis=-1, keepdims=True)


def _q_rows_f32(q_ref, hk, nq):
    lane = lax.broadcasted_iota(jnp.int32, (1, LANES), 1)
    own = (lane >= hk * NSA_DH) & (lane < (hk + 1) * NSA_DH)
    parts = []
    for g in range(NSA_GROUP):
        h = hk * NSA_GROUP + g
        blk = q_ref[0, :, (h // 2) * LANES:(h // 2 + 1) * LANES]
        if (h % 2) != hk:
            blk = pltpu.roll(blk, NSA_DH, 1)
        parts.append(blk)
    qh = jnp.concatenate(parts, axis=0)
    return jnp.where(own, qh * (NSA_DH ** -0.5), 0.0), own


def _q_rows(q_ref, hk, nq):
    qh, own = _q_rows_f32(q_ref, hk, nq)
    return qh.astype(BF16), own


def _feature_lanes(hk):
    base = (1 - hk) * NSA_DH
    return base, base + 1, base + 2


def _select_t(imp_t, tq_lane, ns, n_top):
    blk = lax.broadcasted_iota(jnp.int32, (imp_t.shape[0], 1), 0)
    cur = tq_lane >> 6
    forced = (blk == 0) | (blk == cur) | (blk == cur - 1)
    work = jnp.where(forced, FORCE_SCORE, imp_t)
    work = jnp.where(blk * SEL_BLOCK <= tq_lane, work, NEG_INF)
    work = jnp.where(blk < ns, work, PAD_SCORE)
    sel = jnp.zeros(imp_t.shape, F32)
    for _ in range(n_top):
        m = jnp.max(work, axis=0, keepdims=True)
        idx = jnp.min(jnp.where(work == m, blk, jnp.int32(2 ** 30)), axis=0, keepdims=True)
        hit = blk == idx
        sel = jnp.where(hit, 1.0, sel)
        work = jnp.where(hit, TAKEN_SCORE, work)
    return sel


def _row_consts(hk, nq, t0):
    rows = NSA_GROUP * nq
    row = lax.broadcasted_iota(jnp.int32, (rows, 1), 0)
    slope = jnp.full((rows, 1), ALIBI[hk * NSA_GROUP + NSA_GROUP - 1], F32)
    for g in range(NSA_GROUP - 2, -1, -1):
        slope = jnp.where(row < (g + 1) * nq, ALIBI[hk * NSA_GROUP + g], slope)
    tq = t0 + (row & (nq - 1))
    return slope, tq


def _cmp_branch(qh, slope, tq, kc_ref, vc_ref, a_ref, nq):
    ncp = kc_ref.shape[1]
    s = _mm_nt(qh, kc_ref[0])
    cend = lax.broadcasted_iota(jnp.int32, (1, ncp), 1) * CMP_STRIDE + (2 * CMP_STRIDE - 1)
    d = tq - cend
    valid = d >= 0
    p, _, l = _softmax_parts(s - slope * d.astype(F32), valid)
    p = p / jnp.maximum(l, 1e-30)
    o_c = _mm(p, vc_ref[0])
    psum = p[0:nq]
    for g in range(1, NSA_GROUP):
        psum = psum + p[g * nq:(g + 1) * nq]
    return o_c, sum(jnp.dot(part, a_ref[...], preferred_element_type=F32) for part in _bf16_parts(psum))


def _select(imp, tq_q, ns, n_top):
    nsp = imp.shape[1]
    blk = lax.broadcasted_iota(jnp.int32, (1, nsp), 1)
    cur = tq_q >> 6
    forced = (blk == 0) | (blk == cur) | (blk == cur - 1)
    imp = jnp.where(forced, FORCE_SCORE, imp)
    imp = jnp.where(blk * SEL_BLOCK <= tq_q, imp, NEG_INF)
    imp = jnp.where(blk < ns, imp, PAD_SCORE)
    return _topk_mask(imp, n_top)


def _flash_step(qh, kch, vch, pos, tq, slope, sel_rows, carry, feature_major=False):
    m, l, acc = carry
    nsp = sel_rows.shape[1]
    s = _mm(qh, kch) if feature_major else _mm_nt(qh, kch)
    d = tq - pos
    blk = lax.broadcasted_iota(jnp.int32, (nsp, 1), 0)
    expand = (blk == (pos >> 6)).astype(BF16)
    chosen = jnp.dot(sel_rows, expand, preferred_element_type=F32) > 0.5
    valid = (d >= 0) & chosen
    s = jnp.where(valid, s - slope * d.astype(F32), NEG_INF)
    m_new = jnp.maximum(m, jnp.max(s, axis=-1, keepdims=True))
    alpha = jnp.exp(m - m_new)
    p = jnp.where(valid, jnp.exp(s - m_new), 0.0)
    l = alpha * l + jnp.sum(p, axis=-1, keepdims=True)
    acc = alpha * acc + (_mm_nt(p, vch) if feature_major else _mm(p, vch))
    return m_new, l, acc


def _combine_heads(o_c, o_s, o_w, gates, gh_ref, own, hk, nq, out_pairs):
    for g in range(NSA_GROUP):
        h = hk * NSA_GROUP + g
        r = slice(g * nq, (g + 1) * nq)
        og = (gates[:, 3 * h:3 * h + 1] * o_c[r] + gates[:, 3 * h + 1:3 * h + 2] * o_s[r]
              + gates[:, 3 * h + 2:3 * h + 3] * o_w[r])
        og = jnp.where(own, og, 0.0)
        ms = jnp.sum(og * og, axis=-1, keepdims=True) * (1.0 / NSA_DH)
        y = og * lax.rsqrt(ms + EPS) * gh_ref[h:h + 1, :]
        if (h % 2) != hk:
            y = pltpu.roll(y, NSA_DH, 1)
        out_pairs[h // 2] = y if out_pairs[h // 2] is None else out_pairs[h // 2] + y


def _norm_proj_kernel(x_ref, g_ref, *refs, n, n_t):
    h = _rms(x_ref[...], g_ref[...]).astype(BF16)
    has_t = 1 if n_t else 0
    outs = refs[n + has_t:]
    for w_ref, o_ref in zip(refs[:n], outs[:n]):
        o_ref[...] = jnp.dot(h, w_ref[...], preferred_element_type=F32)
    if n_t:
        yt = lax.dot_general(refs[n][...], h, (((1,), (1,)), ((), ())), preferred_element_type=F32)
        for k, o_ref in enumerate(outs[n:]):
            o_ref[0] = yt[k * LANES:(k + 1) * LANES, :]


def _norm_proj(x, g, ws, tm, wt=None, t=None):
    m, d = x.shape
    n = len(ws)
    n_t = 0 if wt is None else wt.shape[0] // LANES
    in_specs = [pl.BlockSpec((tm, d), lambda i: (i, 0)), pl.BlockSpec((1, d), lambda i: (0, 0))]
    in_specs += [pl.BlockSpec(w.shape, lambda i: (0, 0)) for w in ws]
    out_specs = [pl.BlockSpec((tm, w.shape[1]), lambda i: (i, 0)) for w in ws]
    out_shape = [jax.ShapeDtypeStruct((m, w.shape[1]), F32) for w in ws]
    args = [x, g.reshape(1, d), *ws]
    if n_t:
        per_b = t // tm
        in_specs.append(pl.BlockSpec(wt.shape, lambda i: (0, 0)))
        out_specs += [pl.BlockSpec((1, LANES, tm), lambda i: (i // per_b, 0, i % per_b))] * n_t
        out_shape += [jax.ShapeDtypeStruct((m // t, LANES, t), F32)] * n_t
        args.append(wt)
    return pl.pallas_call(
        functools.partial(_norm_proj_kernel, n=n, n_t=n_t),
        grid=(m // tm,),
        in_specs=in_specs,
        out_specs=out_specs,
        out_shape=out_shape,
        compiler_params=_params(("parallel",)),
        name="norm_proj",
    )(*args)


def _compress_kernel(tbl_ref, pool_ref, last_ref, wr_ref, pe_ref, b1_ref, w2_ref, o_ref,
                     buf, xrow, acc, cst, sem, *, ch, nchunks, paged, feature_major):
    b = pl.program_id(0)
    j = pl.program_id(1)
    nblk = ch * CMP_PER_PAGE
    half = wr_ref.shape[2] // 2
    npairs = wr_ref.shape[0]

    n = b * nchunks + j
    cur = n % 2

    def page_copy(bb, k, buf_i, slot):
        if paged:
            src = pool_ref.at[tbl_ref[bb, k]]
        else:
            src = pool_ref.at[bb, :, pl.ds(pl.multiple_of(k * PAGE, PAGE), PAGE)]
        return pltpu.make_async_copy(src, buf.at[buf_i, slot], sem.at[buf_i, slot])

    def step_copies(bb, jj, buf_i, fn):
        for s in range(ch):
            fn(page_copy(bb, jj * ch + s, buf_i, s))

        @pl.when(jj < nchunks - 1)
        def _():
            fn(page_copy(bb, jnp.minimum((jj + 1) * ch, nchunks * ch - 1), buf_i, ch))

    @pl.when(n == 0)
    def _():
        step_copies(b, j, cur, lambda c: c.start())
        c = jnp.zeros((8, half), F32)
        for pr in range(npairs):
            c = c + _mm(jnp.broadcast_to(pe_ref[0, pr:pr + 1, :], (8, 2 * LANES)), wr_ref[pr, :, 0:half])
            c = c + _mm(jnp.broadcast_to(pe_ref[1, pr:pr + 1, :], (8, 2 * LANES)), wr_ref[pr, :, half:2 * half])
        cst[...] = c + b1_ref[...]

    @pl.when(n + 1 < pl.num_programs(0) * nchunks)
    def _():
        wrap = j + 1 == nchunks
        step_copies(jnp.where(wrap, b + 1, b), jnp.where(wrap, 0, j + 1), 1 - cur, lambda c: c.start())

    step_copies(b, j, cur, lambda c: c.wait())

    for s in range(ch):
        page = buf[cur, s]
        xrow[pl.ds(s * PAGE, PAGE), :] = page.T if feature_major else page

    @pl.when(j < nchunks - 1)
    def _():
        page = buf[cur, ch]
        xrow[pl.ds(ch * PAGE, PAGE), :] = page.T if feature_major else page

    @pl.when(j == nchunks - 1)
    def _():
        xrow[pl.ds(ch * PAGE, PAGE), :] = last_ref[0]

    total = None
    for pr in range(npairs):
        x0 = xrow[pl.ds(2 * pr, nblk + 8, stride=CMP_STRIDE), :]
        x1 = xrow[pl.ds(2 * pr + 1, nblk + 8, stride=CMP_STRIDE), :]
        prod = _mm(jnp.concatenate([x0, x1], axis=1), wr_ref[pr])
        total = prod if total is None else total + prod
    acc[...] = total
    hid = acc[0:nblk, 0:half] + acc[pl.ds(1, nblk), half:2 * half] + cst[0:1, :]
    o_ref[0] = _mm(_gelu_tanh(hid), w2_ref[...])


def _compress(table, pool, last_next, pe, w1, b1, w2, ch, paged, feature_major):
    nb, npg = table.shape
    nchunks = npg // ch
    hid = w1.shape[1]
    dh = NSA_DH
    wa = w1[:CMP_STRIDE * dh].reshape(CMP_STRIDE, dh, hid)
    wb = w1[CMP_STRIDE * dh:].reshape(CMP_STRIDE, dh, hid)
    z = jnp.zeros_like(wa)
    wr = jnp.concatenate([jnp.concatenate([wa, z, wb, z], axis=2),
                          jnp.concatenate([z, wa, z, wb], axis=2)], axis=1).astype(BF16)
    wr = wr.reshape(CMP_STRIDE // 2, 2 * LANES, 4 * hid)
    pe1 = jnp.concatenate([pe[:CMP_STRIDE], pe[:CMP_STRIDE]], axis=1).reshape(CMP_STRIDE // 2, 2 * LANES)
    pe2 = jnp.concatenate([pe[CMP_STRIDE:], pe[CMP_STRIDE:]], axis=1).reshape(CMP_STRIDE // 2, 2 * LANES)
    pes = jnp.stack([pe1, pe2])
    b1t = jnp.concatenate([b1, b1]).reshape(1, 2 * hid)
    zz = jnp.zeros_like(w2)
    w2bd = jnp.concatenate([jnp.concatenate([w2, zz], axis=1),
                            jnp.concatenate([zz, w2], axis=1)], axis=0).astype(BF16)
    nblk = ch * CMP_PER_PAGE
    grid_spec = pltpu.PrefetchScalarGridSpec(
        num_scalar_prefetch=1,
        grid=(nb, nchunks),
        in_specs=[pl.BlockSpec(memory_space=pl.ANY),
                  pl.BlockSpec((1, PAGE, LANES), lambda b, j, t: (b, 0, 0)),
                  pl.BlockSpec(wr.shape, lambda b, j, t: (0, 0, 0)),
                  pl.BlockSpec(pes.shape, lambda b, j, t: (0, 0, 0)),
                  pl.BlockSpec(b1t.shape, lambda b, j, t: (0, 0)),
                  pl.BlockSpec(w2bd.shape, lambda b, j, t: (0, 0))],
        out_specs=pl.BlockSpec((1, nblk, LANES), lambda b, j, t: (b, j, 0)),
        scratch_shapes=[pltpu.VMEM((2, ch + 1, PAGE, LANES), F32),
                        pltpu.VMEM(((ch + 1) * PAGE, LANES), F32),
                        pltpu.VMEM((nblk + 8, 4 * hid), F32),
                        pltpu.VMEM((8, 2 * hid), F32),
                        pltpu.SemaphoreType.DMA((2, ch + 1))],
    )
    return pl.pallas_call(
        functools.partial(_compress_kernel, ch=ch, nchunks=nchunks, paged=paged, feature_major=feature_major),
        grid_spec=grid_spec,
        out_shape=jax.ShapeDtypeStruct((nb, npg * CMP_PER_PAGE, LANES), F32),
        compiler_params=_params(("arbitrary", "arbitrary")),
        name="compress",
    )(table, pool, last_next, wr, pes, b1t, w2bd)


def _sel_matrix(nc_rows, nc_valid, ns_cols):
    a = np.zeros((nc_rows, ns_cols), np.float32)
    for j in range(nc_valid // 4):
        for c, wgt in ((4 * j - 1, 0.5), (4 * j, 1.0), (4 * j + 1, 1.0), (4 * j + 2, 1.0), (4 * j + 3, 0.5)):
            if 0 <= c < nc_valid:
                a[c, j] += wgt
    return jnp.asarray(a)


def _nsa_prompt_kernel(qi_ref, ci_ref, q_ref, sm_ref, kc_ref, vc_ref, ks_ref, vs_ref, kw_ref, vw_ref, at_ref, bg_ref,
                       gh_ref, o_ref, ksa, vsa, kwa, vwa, kca, vca, qa_s, oc_s, ow_s, m_s, acc_s,
                       *, nq, ns, n_top, kchunk, wkeys, prep_rows):
    step = pl.program_id(1)
    i = qi_ref[step]
    c = ci_ref[step]
    t0 = i * nq
    rows = NSA_GROUP * nq
    t = ks_ref.shape[2]
    nc = kc_ref.shape[1]
    last = (t0 + nq + kchunk - 1) // kchunk - 1
    lane = lax.broadcasted_iota(jnp.int32, (1, LANES), 1)

    @pl.when(step == 0)
    def _():
        for hk in range(NSA_KV_HEADS):
            own = (lane >= hk * NSA_DH) & (lane < (hk + 1) * NSA_DH)
            f0, f1, f2 = _feature_lanes(hk)
            cidx = lax.broadcasted_iota(jnp.int32, (nc, 1), 0).astype(F32)
            kca[hk] = jnp.where(own, kc_ref[0], jnp.where(lane == f2, cidx, 0.0)).astype(BF16)
            vca[hk] = vc_ref[0].astype(BF16)

            row = lax.broadcasted_iota(jnp.int32, (LANES, 1), 0)
            own_r = (row >= hk * NSA_DH) & (row < (hk + 1) * NSA_DH)

            def prep(r, _):
                r0 = pl.multiple_of(r * prep_rows, prep_rows)
                cs = pl.ds(r0, prep_rows)
                pos = r0 + lax.broadcasted_iota(jnp.int32, (1, prep_rows), 1)
                blk = pos >> 6
                feat = jnp.where(row == f0, blk.astype(F32), jnp.where(row == f1, (pos & 63).astype(F32), 0.0))
                ksa[hk, 0:LANES, cs] = jnp.where(own_r, ks_ref[0, :, cs], feat).astype(BF16)
                ksa[hk, LANES:2 * LANES, cs] = (row == blk).astype(BF16)
                kwa[hk, :, cs] = jnp.where(own_r, kw_ref[0, :, cs], feat).astype(BF16)
                vsa[hk, :, cs] = jnp.where(own_r, vs_ref[0, :, cs], 1.0).astype(BF16)
                vwa[hk, :, cs] = jnp.where(own_r, vw_ref[0, :, cs], 1.0).astype(BF16)
                return 0

            lax.fori_loop(0, t // prep_rows, prep, 0)

    @pl.when(c == 0)
    def _():
        tq_lane = t0 + lax.broadcasted_iota(jnp.int32, (1, nq), 1)
        for hk in range(NSA_KV_HEADS):
            f0, f1, f2 = _feature_lanes(hk)
            qh, _ = _q_rows_f32(q_ref, hk, nq)
            slope, tq = _row_consts(hk, nq, t0)
            qlo = jnp.where(lane == f0, slope * SEL_BLOCK,
                            jnp.where(lane == f1, slope, jnp.where(lane == f2, slope * CMP_STRIDE, qh))).astype(BF16)
            s = lax.dot_general(qlo, kca[hk], (((1,), (1,)), ((), ())), preferred_element_type=F32)
            cend = lax.broadcasted_iota(jnp.int32, (1, nc), 1) * CMP_STRIDE + (2 * CMP_STRIDE - 1)
            p, _, l_c = _softmax_parts(s, cend <= tq)
            p = p / jnp.maximum(l_c, 1e-30)
            oc_s[hk] = jnp.dot(p.astype(BF16), vca[hk], preferred_element_type=F32)
            psum = p[0:nq]
            for g in range(1, NSA_GROUP):
                psum = psum + p[g * nq:(g + 1) * nq]
            imp_t = lax.dot_general(at_ref[...], psum, (((1,), (1,)), ((), ())), precision=HIGHEST,
                                    preferred_element_type=F32)
            sel_t = _select_t(imp_t, tq_lane, ns, n_top)
            bias = (jnp.transpose(sel_t) - 1.0) * 1e30
            qa_s[hk, :, 0:LANES] = qlo
            qa_s[hk, :, LANES:2 * LANES] = jnp.concatenate([bias] * NSA_GROUP, axis=0).astype(BF16)
            w0 = pl.multiple_of(jnp.maximum(t0 + nq - wkeys, 0), LANES)
            d = tq - (w0 + lax.broadcasted_iota(jnp.int32, (1, wkeys), 1))
            s = jnp.dot(qlo, kwa[hk, :, pl.ds(w0, wkeys)], preferred_element_type=F32)
            p, _, _ = _softmax_parts(s, (d >= 0) & (d < WINDOW))
            ow = lax.dot_general(p.astype(BF16), vwa[hk, :, pl.ds(w0, wkeys)], (((1,), (1,)), ((), ())),
                                 preferred_element_type=F32)
            ow_s[hk] = ow / jnp.maximum(pltpu.roll(ow, NSA_DH, 1), 1e-30)
            m_s[hk] = jnp.full((rows, LANES), NEG_INF, F32)
            acc_s[hk] = jnp.zeros((rows, LANES), F32)

    k0 = pl.multiple_of(c * kchunk, kchunk)

    def flash(causal):
        for hk in range(NSA_KV_HEADS):
            s = jnp.dot(qa_s[hk], ksa[hk, :, pl.ds(k0, kchunk)], preferred_element_type=F32)
            if causal:
                _, tq = _row_consts(hk, nq, t0)
                s = jnp.where(k0 + lax.broadcasted_iota(jnp.int32, (1, kchunk), 1) <= tq, s, NEG_INF)
            m_old = m_s[hk][:, 0:1]
            m_new = jnp.maximum(m_old, jnp.max(s, axis=-1, keepdims=True))
            p = jnp.exp(s - m_new).astype(BF16)
            acc_s[hk] = jnp.exp(m_old - m_new) * acc_s[hk] + lax.dot_general(
                p, vsa[hk, :, pl.ds(k0, kchunk)], (((1,), (1,)), ((), ())), preferred_element_type=F32)
            m_s[hk] = jnp.broadcast_to(m_new, (rows, LANES))

    @pl.when(c < last)
    def _():
        flash(False)

    @pl.when(c == last)
    def _():
        flash(True)
        gates = _sigmoid(sm_ref[0] + bg_ref[...])
        out_pairs = [None] * (NSA_HEADS // 2)
        for hk in range(NSA_KV_HEADS):
            own = (lane >= hk * NSA_DH) & (lane < (hk + 1) * NSA_DH)
            acc = acc_s[hk]
            o_s = acc / jnp.maximum(pltpu.roll(acc, NSA_DH, 1), 1e-30)
            _combine_heads(oc_s[hk], o_s, ow_s[hk], gates, gh_ref, own, hk, nq, out_pairs)
        for k, y in enumerate(out_pairs):
            o_ref[0, :, k * LANES:(k + 1) * LANES] = y


def _nsa_prompt(q, sm, k_cmp, v_cmp, ks, vs, kw, vw, bg, gh, nq):
    b, t, _ = q.shape
    nc = k_cmp.shape[1]
    ns = t // SEL_BLOCK
    assert nq == LANES and ns <= LANES and t % 512 == 0, "prompt NSA kernel: 128-query blocks, at most 128 blocks"
    a = _sel_matrix(nc, nc, LANES).T
    kchunk = min(512, t)
    wkeys = min(-(-(WINDOW + nq) // LANES) * LANES, t)
    pairs = [(i, c) for i in range(t // nq) for c in range(-(-((i + 1) * nq) // kchunk))]
    qi = jnp.asarray(np.array([p[0] for p in pairs], np.int32))
    ci = jnp.asarray(np.array([p[1] for p in pairs], np.int32))
    rows = NSA_GROUP * nq
    full = lambda n: pl.BlockSpec((1, n, LANES), lambda bi, s, qi, ci: (bi, 0, 0))
    full_t = pl.BlockSpec((1, LANES, t), lambda bi, s, qi, ci: (bi, 0, 0))
    const = lambda shape: pl.BlockSpec(shape, lambda bi, s, qi, ci: (0,) * len(shape))
    tok = lambda w: pl.BlockSpec((1, nq, w), lambda bi, s, qi, ci: (bi, qi[s], 0))
    grid_spec = pltpu.PrefetchScalarGridSpec(
        num_scalar_prefetch=2,
        grid=(b, len(pairs)),
        in_specs=[tok(4 * LANES), tok(LANES), full(nc), full(nc), full_t, full_t, full_t, full_t,
                  const(a.shape), const((1, LANES)), const((NSA_HEADS, LANES))],
        out_specs=tok(4 * LANES),
        scratch_shapes=[pltpu.VMEM((NSA_KV_HEADS, 2 * LANES, t), BF16)]
        + [pltpu.VMEM((NSA_KV_HEADS, LANES, t), BF16)] * 3
        + [pltpu.VMEM((NSA_KV_HEADS, nc, LANES), BF16)] * 2
        + [pltpu.VMEM((NSA_KV_HEADS, rows, 2 * LANES), BF16)]
        + [pltpu.VMEM((NSA_KV_HEADS, rows, LANES), F32)] * 4,
    )
    return pl.pallas_call(
        functools.partial(_nsa_prompt_kernel, nq=nq, ns=ns, n_top=min(SEL_TOPK, ns), kchunk=kchunk, wkeys=wkeys,
                          prep_rows=512),
        grid_spec=grid_spec,
        out_shape=jax.ShapeDtypeStruct((b, t, 4 * LANES), F32),
        compiler_params=_params(("arbitrary", "arbitrary")),
        name="nsa_prompt",
    )(qi, ci, q, sm, k_cmp, v_cmp, ks, vs, kw, vw, a, bg, gh)


def _head_feature_rows(hk, g, n):
    slope = ALIBI[hk * NSA_GROUP + g]
    row = lax.broadcasted_iota(jnp.int32, (NSA_DH, n), 0)
    return jnp.where(row == 0, slope * SEL_BLOCK,
                     jnp.where(row == 1, slope, jnp.where(row == 2, slope * CMP_STRIDE, 0.0)))


def _swap_halves(x):
    return jnp.concatenate([x[NSA_DH:], x[:NSA_DH]], axis=0)


def _nsa_prompt_t_kernel(qi_ref, ci_ref, q_ref, sm_ref, kc_ref, vc_ref, ks_ref, vs_ref, kw_ref, vw_ref, at_ref,
                         bg_ref, ght_ref, o_ref, ksr, kwr, vsa, vwa, kca, vct, qa_s, oc_s, ow_s, m_s, acc_s, used_ref,
                         *, nq, ns, n_top, kchunk, wpad):
    step = pl.program_id(1)
    i = qi_ref[step]
    c = ci_ref[step]
    t0 = pl.multiple_of(i * nq, nq)
    cols = NSA_GROUP * nq
    t = ks_ref.shape[2]
    nc = kc_ref.shape[1]
    nsr = at_ref.shape[0]
    wkeys = wpad + nq
    last = (t0 + nq + kchunk - 1) // kchunk - 1
    lane = lax.broadcasted_iota(jnp.int32, (1, LANES), 1)
    tq_q = t0 + lax.broadcasted_iota(jnp.int32, (1, nq), 1)
    tq = jnp.concatenate([tq_q] * NSA_GROUP, axis=1)

    @pl.when(step == 0)
    def _():
        srow = lax.broadcasted_iota(jnp.int32, (LANES, 1), 0)
        cidx = lax.broadcasted_iota(jnp.int32, (nc, 1), 0).astype(F32)
        vct_f = jnp.transpose(vc_ref[0]).astype(BF16)
        for hk in range(NSA_KV_HEADS):
            own = (lane >= hk * NSA_DH) & (lane < (hk + 1) * NSA_DH)
            f0, f1, f2 = _feature_lanes(hk)
            kca[hk] = jnp.where(own, kc_ref[0], jnp.where(lane == f2, cidx, 0.0)).astype(BF16)
            vct[hk] = vct_f
            kwr[hk, 0:wpad, :] = jnp.broadcast_to(jnp.where(lane == f0, -1e30, 0.0), (wpad, LANES)).astype(BF16)
            vwa[hk, :, 0:wpad] = jnp.zeros((LANES, wpad), BF16)

        def prep(r, _):
            r0 = pl.multiple_of(r * LANES, LANES)
            cs = pl.ds(r0, LANES)
            pos = r0 + lax.broadcasted_iota(jnp.int32, (LANES, 1), 0)
            blk = pos >> 6
            ks_rows = jnp.transpose(ks_ref[0, :, cs])
            kw_rows = jnp.transpose(kw_ref[0, :, cs])
            for hk in range(NSA_KV_HEADS):
                own = (lane >= hk * NSA_DH) & (lane < (hk + 1) * NSA_DH)
                own_r = (srow >= hk * NSA_DH) & (srow < (hk + 1) * NSA_DH)
                f0, f1, f2 = _feature_lanes(hk)
                feat = jnp.where(lane == f0, blk.astype(F32), jnp.where(lane == f1, (pos & 63).astype(F32), 0.0))
                ksr[hk, cs, 0:LANES] = jnp.where(own, ks_rows, feat).astype(BF16)
                ksr[hk, cs, LANES:2 * LANES] = (lane == blk).astype(BF16)
                kwr[hk, pl.ds(wpad + r0, LANES), :] = jnp.where(own, kw_rows, feat).astype(BF16)
                vsa[hk, :, cs] = jnp.where(own_r, vs_ref[0, :, cs], 1.0).astype(BF16)
                vwa[hk, :, pl.ds(wpad + r0, LANES)] = jnp.where(own_r, vw_ref[0, :, cs], 1.0).astype(BF16)
            return 0

        lax.fori_loop(0, t // LANES, prep, 0)

    @pl.when(c == 0)
    def _():
        qt = jnp.transpose(q_ref[0]) * (NSA_DH ** -0.5)
        cend = lax.broadcasted_iota(jnp.int32, (nc, 1), 0) * CMP_STRIDE + (2 * CMP_STRIDE - 1)
        wrow = lax.broadcasted_iota(jnp.int32, (LANES, 1), 0)
        sel_heads = []
        for hk in range(NSA_KV_HEADS):
            slabs = []
            for g in range(NSA_GROUP):
                h = hk * NSA_GROUP + g
                halves = [qt[h * NSA_DH:(h + 1) * NSA_DH], _head_feature_rows(hk, g, nq)]
                slabs.append(jnp.concatenate(halves if hk == 0 else halves[::-1], axis=0))
            qlo = jnp.concatenate(slabs, axis=1).astype(BF16)
            s = jnp.dot(kca[hk], qlo, preferred_element_type=F32)
            valid = cend <= tq
            s = jnp.where(valid, s, NEG_INF)
            p = jnp.where(valid, jnp.exp(s - jnp.max(s, axis=0, keepdims=True)), 0.0)
            p = p / jnp.maximum(jnp.sum(p, axis=0, keepdims=True), 1e-30)
            oc_s[hk] = jnp.dot(vct[hk], p.astype(BF16), preferred_element_type=F32)
            psum = p[:, 0:nq]
            for g in range(1, NSA_GROUP):
                psum = psum + p[:, g * nq:(g + 1) * nq]
            imp_t = sum(jnp.dot(at_ref[...], part, preferred_element_type=F32)
                        for part in _bf16_parts(psum))
            sel_t = _select_t(imp_t, tq_q, ns, n_top)
            sel_heads.append(sel_t)
            bias = (sel_t - 1.0) * 1e30
            bias = jnp.concatenate([bias, jnp.zeros((LANES - nsr, nq), F32)], axis=0) if nsr < LANES else bias
            qa_s[hk, 0:LANES, :] = qlo
            qa_s[hk, LANES:2 * LANES, :] = jnp.concatenate([bias] * NSA_GROUP, axis=1).astype(BF16)
            s = jnp.dot(kwr[hk, pl.ds(t0, wkeys), :], qlo, preferred_element_type=F32)
            old = jnp.where(t0 - wpad + wrow > tq - WINDOW, s[0:LANES], NEG_INF)
            new = jnp.where(t0 + wrow[0:nq] <= tq, s[wkeys - nq:wkeys], NEG_INF)
            s = jnp.concatenate([old, s[LANES:wkeys - nq], new], axis=0)
            p = jnp.exp(s - jnp.max(s, axis=0, keepdims=True)).astype(BF16)
            ow = jnp.dot(vwa[hk, :, pl.ds(t0, wkeys)], p, preferred_element_type=F32)
            l_w = _swap_halves(ow)
            ow_s[hk] = ow / jnp.maximum(l_w, 1e-30)
            m_s[hk] = jnp.full((8, cols), NEG_INF, F32)
            acc_s[hk] = jnp.zeros((LANES, cols), F32)
        any_sel = sel_heads[0]
        for s_t in sel_heads[1:]:
            any_sel = jnp.maximum(any_sel, s_t)
        per_chunk = kchunk // SEL_BLOCK
        for cc in range(t // kchunk):
            used_ref[cc] = (jnp.max(any_sel[cc * per_chunk:(cc + 1) * per_chunk, :]) > 0.5).astype(jnp.int32)

    k0 = pl.multiple_of(c * kchunk, kchunk)

    def flash(causal):
        for hk in range(NSA_KV_HEADS):
            s = jnp.dot(ksr[hk, pl.ds(k0, kchunk), :], qa_s[hk], preferred_element_type=F32)
            if causal:
                s = jnp.where(k0 + lax.broadcasted_iota(jnp.int32, (kchunk, 1), 0) <= tq, s, NEG_INF)
            m_old = m_s[hk][0:1, :]
            m_new = jnp.maximum(m_old, jnp.max(s, axis=0, keepdims=True))
            p = jnp.exp(s - m_new).astype(BF16)
            acc_s[hk] = jnp.exp(m_old - m_new) * acc_s[hk] + jnp.dot(vsa[hk, :, pl.ds(k0, kchunk)], p,
                                                                   preferred_element_type=F32)
            m_s[hk] = jnp.broadcast_to(m_new, (8, cols))

    @pl.when((c < last) & (used_ref[c] > 0))
    def _():
        flash(False)

    @pl.when(c == last)
    def _():
        flash(True)
        gates = jnp.transpose(_sigmoid(sm_ref[0] + bg_ref[...]))
        srow = lax.broadcasted_iota(jnp.int32, (LANES, 1), 0)
        halves = [None] * NSA_HEADS
        for hk in range(NSA_KV_HEADS):
            own_r = (srow >= hk * NSA_DH) & (srow < (hk + 1) * NSA_DH)
            acc = acc_s[hk]
            o_s = acc / jnp.maximum(_swap_halves(acc), 1e-30)
            o_c = oc_s[hk]
            o_w = ow_s[hk]
            for g in range(NSA_GROUP):
                h = hk * NSA_GROUP + g
                cs = slice(g * nq, (g + 1) * nq)
                og = (gates[3 * h:3 * h + 1] * o_c[:, cs] + gates[3 * h + 1:3 * h + 2] * o_s[:, cs]
                      + gates[3 * h + 2:3 * h + 3] * o_w[:, cs])
                og = jnp.where(own_r, og, 0.0)
                ms = jnp.sum(og * og, axis=0, keepdims=True) * (1.0 / NSA_DH)
                y = og * lax.rsqrt(ms + EPS) * ght_ref[:, h:h + 1]
                halves[h] = y[hk * NSA_DH:(hk + 1) * NSA_DH]
        for k in range(NSA_HEADS // 2):
            o_ref[0, :, k * LANES:(k + 1) * LANES] = jnp.transpose(
                jnp.concatenate([halves[2 * k], halves[2 * k + 1]], axis=0))


def _nsa_prompt_t(q, sm, k_cmp, v_cmp, ks, vs, kw, vw, bg, gh, nq):
    b, t, _ = q.shape
    nc = k_cmp.shape[1]
    ns = t // SEL_BLOCK
    nsr = -(-ns // 8) * 8
    assert nq == LANES and nsr <= LANES and nc <= 256 and t % 512 == 0, "prompt NSA kernel shape limits"
    at = _sel_matrix(nc, nc, nsr).T.astype(BF16)
    kchunk = min(512, t)
    wpad = WINDOW
    pairs = [(i, c) for i in range(t // nq) for c in range(-(-((i + 1) * nq) // kchunk))]
    qi = jnp.asarray(np.array([p[0] for p in pairs], np.int32))
    ci = jnp.asarray(np.array([p[1] for p in pairs], np.int32))
    cols = NSA_GROUP * nq
    full = lambda n: pl.BlockSpec((1, n, LANES), lambda bi, s, qi, ci: (bi, 0, 0))
    full_t = pl.BlockSpec((1, LANES, t), lambda bi, s, qi, ci: (bi, 0, 0))
    const = lambda shape: pl.BlockSpec(shape, lambda bi, s, qi, ci: (0,) * len(shape))
    tok = lambda w: pl.BlockSpec((1, nq, w), lambda bi, s, qi, ci: (bi, qi[s], 0))
    grid_spec = pltpu.PrefetchScalarGridSpec(
        num_scalar_prefetch=2,
        grid=(b, len(pairs)),
        in_specs=[tok(4 * LANES), tok(LANES), full(nc), full(nc), full_t, full_t, full_t, full_t,
                  const(at.shape), const((1, LANES)), const((LANES, NSA_HEADS))],
        out_specs=tok(4 * LANES),
        scratch_shapes=[pltpu.VMEM((NSA_KV_HEADS, t, 2 * LANES), BF16),
                        pltpu.VMEM((NSA_KV_HEADS, wpad + t, LANES), BF16),
                        pltpu.VMEM((NSA_KV_HEADS, LANES, t), BF16),
                        pltpu.VMEM((NSA_KV_HEADS, LANES, wpad + t), BF16),
                        pltpu.VMEM((NSA_KV_HEADS, nc, LANES), BF16),
                        pltpu.VMEM((NSA_KV_HEADS, LANES, nc), BF16),
                        pltpu.VMEM((NSA_KV_HEADS, 2 * LANES, cols), BF16),
                        pltpu.VMEM((NSA_KV_HEADS, LANES, cols), F32),
                        pltpu.VMEM((NSA_KV_HEADS, LANES, cols), F32),
                        pltpu.VMEM((NSA_KV_HEADS, 8, cols), F32),
                        pltpu.VMEM((NSA_KV_HEADS, LANES, cols), F32),
                        pltpu.SMEM((t // kchunk,), jnp.int32)],
    )
    return pl.pallas_call(
        functools.partial(_nsa_prompt_t_kernel, nq=nq, ns=ns, n_top=min(SEL_TOPK, ns), kchunk=kchunk, wpad=wpad),
        grid_spec=grid_spec,
        out_shape=jax.ShapeDtypeStruct((b, t, 4 * LANES), F32),
        compiler_params=_params(("arbitrary", "arbitrary")),
        name="nsa_prompt",
    )(qi, ci, q, sm, k_cmp, v_cmp, ks, vs, kw, vw, at, bg, gh.T)


def _nsa_sample_kernel(pt_ref, q_ref, sm_ref, kc_ref, vc_ref, ks_pool, vs_pool, ksn_ref, vsn_ref,
                       wk_ref, wv_ref, kwn_ref, vwn_ref, a_ref, bg_ref, gh_ref, o_ref,
                       kbuf, vbuf, sem, q_s, sel_s, oc_s, ow_s, m_s, l_s, acc_s,
                       *, ch, nchunks, offset, ns, n_top, kchunk):
    b = pl.program_id(0)
    j = pl.program_id(1)
    nq = q_ref.shape[1]
    rows = NSA_GROUP * nq
    w_pre = wk_ref.shape[2]

    def page_copies(s):
        page = pt_ref[b, j * ch + s]
        dst = pl.ds(s * PAGE, PAGE)
        return (pltpu.make_async_copy(ks_pool.at[page], kbuf.at[:, dst], sem.at[0]),
                pltpu.make_async_copy(vs_pool.at[page], vbuf.at[:, dst], sem.at[1]))

    for s in range(ch):
        ck, cv = page_copies(s)
        ck.start()
        cv.start()

    @pl.when(j == 0)
    def _():
        for hk in range(NSA_KV_HEADS):
            qh, _ = _q_rows(q_ref, hk, nq)
            slope, tq = _row_consts(hk, nq, offset)
            o_c, imp = _cmp_branch(qh, slope, tq, kc_ref, vc_ref, a_ref, nq)
            sel = _select(imp, tq[0:nq], ns, n_top)
            q_s[hk] = qh
            sel_s[hk] = jnp.concatenate([sel] * NSA_GROUP, axis=0).astype(BF16)
            oc_s[hk] = o_c
            pos1 = offset - w_pre + lax.broadcasted_iota(jnp.int32, (1, w_pre), 1)
            d1 = tq - pos1
            pos2 = offset + lax.broadcasted_iota(jnp.int32, (1, kwn_ref.shape[1]), 1)
            d2 = tq - pos2
            v1 = (d1 >= 0) & (d1 < WINDOW)
            v2 = (d2 >= 0) & (d2 < WINDOW)
            s1 = jnp.where(v1, _mm(qh, wk_ref[0]) - slope * d1.astype(F32), NEG_INF)
            s2 = jnp.where(v2, _mm_nt(qh, kwn_ref[0]) - slope * d2.astype(F32), NEG_INF)
            mx = jnp.maximum(jnp.max(s1, axis=-1, keepdims=True), jnp.max(s2, axis=-1, keepdims=True))
            p1 = jnp.where(v1, jnp.exp(s1 - mx), 0.0)
            p2 = jnp.where(v2, jnp.exp(s2 - mx), 0.0)
            l_w = jnp.sum(p1, axis=-1, keepdims=True) + jnp.sum(p2, axis=-1, keepdims=True)
            ow_s[hk] = (_mm_nt(p1, wv_ref[0]) + _mm(p2, vwn_ref[0])) / jnp.maximum(l_w, 1e-30)
            m_s[hk] = jnp.full((rows, LANES), NEG_INF, F32)
            l_s[hk] = jnp.zeros((rows, LANES), F32)
            acc_s[hk] = jnp.zeros((rows, LANES), F32)

    for s in range(ch):
        ck, cv = page_copies(s)
        ck.wait()
        cv.wait()

    base = j * (ch * PAGE)
    for hk in range(NSA_KV_HEADS):
        qh = q_s[hk]
        slope, tq = _row_consts(hk, nq, offset)
        carry = (m_s[hk][:, 0:1], l_s[hk][:, 0:1], acc_s[hk])
        for c in range(ch * PAGE // kchunk):
            pos = base + c * kchunk + lax.broadcasted_iota(jnp.int32, (1, kchunk), 1)
            carry = _flash_step(qh, kbuf[:, pl.ds(c * kchunk, kchunk)], vbuf[:, pl.ds(c * kchunk, kchunk)],
                                pos, tq, slope, sel_s[hk], carry, feature_major=True)
        m_s[hk] = jnp.broadcast_to(carry[0], (rows, LANES))
        l_s[hk] = jnp.broadcast_to(carry[1], (rows, LANES))
        acc_s[hk] = carry[2]

    @pl.when(j == nchunks - 1)
    def _():
        gates = _sigmoid(sm_ref[0] + bg_ref[...])
        lane = lax.broadcasted_iota(jnp.int32, (1, LANES), 1)
        out_pairs = [None] * (NSA_HEADS // 2)
        for hk in range(NSA_KV_HEADS):
            own = (lane >= hk * NSA_DH) & (lane < (hk + 1) * NSA_DH)
            qh = q_s[hk]
            slope, tq = _row_consts(hk, nq, offset)
            pos = offset + lax.broadcasted_iota(jnp.int32, (1, ksn_ref.shape[1]), 1)
            carry = (m_s[hk][:, 0:1], l_s[hk][:, 0:1], acc_s[hk])
            _, l_f, acc_f = _flash_step(qh, ksn_ref[0], vsn_ref[0], pos, tq, slope, sel_s[hk], carry)
            o_s = acc_f / jnp.maximum(l_f, 1e-30)
            _combine_heads(oc_s[hk], o_s, ow_s[hk], gates, gh_ref, own, hk, nq, out_pairs)
        for k, y in enumerate(out_pairs):
            o_ref[0, :, k * LANES:(k + 1) * LANES] = y


def _nsa_sample(page_table, q, sm, k_cmp, v_cmp, ks_pool, vs_pool, ks_new, vs_new, win_k, win_v, kw_new, vw_new,
                bg, gh, nc_valid, ch):
    b, nq, _ = q.shape
    npg = page_table.shape[1]
    nchunks = npg // ch
    offset = npg * PAGE
    ncp = k_cmp.shape[1]
    ns = nc_valid // 4
    nsp = -(-ns // LANES) * LANES
    a = _sel_matrix(ncp, nc_valid, nsp)
    rows = NSA_GROUP * nq
    kchunk = min(512, ch * PAGE)
    per_b = lambda n, w=LANES: pl.BlockSpec((1, n, w), lambda bi, j, t: (bi, 0, 0))
    const = lambda shape: pl.BlockSpec(shape, lambda bi, j, t: (0,) * len(shape))
    grid_spec = pltpu.PrefetchScalarGridSpec(
        num_scalar_prefetch=1,
        grid=(b, nchunks),
        in_specs=[per_b(nq, 4 * LANES), per_b(nq), per_b(ncp), per_b(ncp),
                  pl.BlockSpec(memory_space=pl.ANY), pl.BlockSpec(memory_space=pl.ANY),
                  per_b(PAGE), per_b(PAGE), per_b(LANES, win_k.shape[2]), per_b(LANES, win_v.shape[2]), per_b(PAGE), per_b(PAGE),
                  const(a.shape), const((1, LANES)), const((NSA_HEADS, LANES))],
        out_specs=per_b(nq, 4 * LANES),
        scratch_shapes=[pltpu.VMEM((LANES, ch * PAGE), F32), pltpu.VMEM((LANES, ch * PAGE), F32),
                        pltpu.SemaphoreType.DMA((2,)),
                        pltpu.VMEM((NSA_KV_HEADS, rows, LANES), BF16),
                        pltpu.VMEM((NSA_KV_HEADS, rows, nsp), BF16),
                        pltpu.VMEM((NSA_KV_HEADS, rows, LANES), F32),
                        pltpu.VMEM((NSA_KV_HEADS, rows, LANES), F32),
                        pltpu.VMEM((NSA_KV_HEADS, rows, LANES), F32),
                        pltpu.VMEM((NSA_KV_HEADS, rows, LANES), F32),
                        pltpu.VMEM((NSA_KV_HEADS, rows, LANES), F32)],
    )
    return pl.pallas_call(
        functools.partial(_nsa_sample_kernel, ch=ch, nchunks=nchunks, offset=offset, ns=ns,
                          n_top=min(SEL_TOPK, ns), kchunk=kchunk),
        grid_spec=grid_spec,
        out_shape=jax.ShapeDtypeStruct((b, nq, 4 * LANES), F32),
        compiler_params=_params(("arbitrary", "arbitrary")),
        name="nsa_sample",
    )(page_table, q, sm, k_cmp, v_cmp, ks_pool, vs_pool, ks_new, vs_new, win_k, win_v, kw_new, vw_new, a, bg, gh)


def _chunk_features(nkeys):
    key = np.arange(nkeys)
    c = np.zeros((LANES, nkeys), np.float32)
    c[key // SEL_BLOCK, key] = 1.0
    c[NSA_DH, :] = key // SEL_BLOCK
    c[NSA_DH + 1, :] = key % SEL_BLOCK
    c[NSA_DH + 2, :] = 1.0
    return jnp.asarray(c, BF16)


def _nsa_sample2_kernel(pt_ref, q_ref, sm_ref, kc_ref, vc_ref, ks_pool, vs_pool, ksn_ref, vsn_ref,
                        wk_ref, wv_ref, kwn_ref, vwn_ref, a_ref, cf_ref, bg_ref, gh_ref, o_ref,
                        kbuf, vbuf, sem, q_s, bias_s, selnew_s, oc_s, ow_s, m_s, l_s, acc_s,
                        *, ch, nchunks, offset, ns, n_top, kchunk):
    b = pl.program_id(0)
    j = pl.program_id(1)
    nq = q_ref.shape[1]
    rows = NSA_GROUP * nq
    w_pre = wk_ref.shape[2]
    nkeys = ch * PAGE
    n = b * nchunks + j
    cur = n % 2
    lane = lax.broadcasted_iota(jnp.int32, (1, LANES), 1)

    def step_copies(bb, jj, buf_i, fn):
        for s in range(ch):
            page = pt_ref[bb, jj * ch + s]
            dst = pl.ds(s * PAGE, PAGE)
            fn(pltpu.make_async_copy(ks_pool.at[page], kbuf.at[buf_i, :, dst], sem.at[0, buf_i]))
            fn(pltpu.make_async_copy(vs_pool.at[page], vbuf.at[buf_i, :, dst], sem.at[1, buf_i]))

    @pl.when(n == 0)
    def _():
        step_copies(b, j, cur, lambda c: c.start())

    @pl.when(n + 1 < pl.num_programs(0) * nchunks)
    def _():
        wrap = j + 1 == nchunks
        step_copies(jnp.where(wrap, b + 1, b), jnp.where(wrap, 0, j + 1), 1 - cur, lambda c: c.start())

    @pl.when(j == 0)
    def _():
        qhs, imps = [], []
        for hk in range(NSA_KV_HEADS):
            rs = slice(hk * rows, (hk + 1) * rows)
            qh, _ = _q_rows(q_ref, hk, nq)
            slope, tq = _row_consts(hk, nq, offset)
            o_c, imp = _cmp_branch(qh, slope, tq, kc_ref, vc_ref, a_ref, nq)
            q_s[rs, :] = qh
            oc_s[rs, :] = o_c
            qhs.append(qh)
            imps.append(imp)
        _, tq = _row_consts(0, nq, offset)
        sel_all = _select(jnp.concatenate(imps, axis=0), jnp.concatenate([tq[0:nq]] * NSA_KV_HEADS, axis=0),
                          ns, n_top)
        for hk in range(NSA_KV_HEADS):
            rs = slice(hk * rows, (hk + 1) * rows)
            qh = qhs[hk]
            slope, tq = _row_consts(hk, nq, offset)
            sel_rows = jnp.concatenate([sel_all[hk * nq:(hk + 1) * nq]] * NSA_GROUP, axis=0)
            for jj in range(nchunks):
                nb_c = nkeys // SEL_BLOCK
                blocks = sel_rows[:, jj * nb_c:jj * nb_c + LANES]
                upper = jnp.where(lane < nb_c, (blocks - 1.0) * 1e30,
                                  jnp.where(lane == NSA_DH, slope * SEL_BLOCK,
                                            jnp.where(lane == NSA_DH + 1, slope,
                                                      jnp.where(lane == NSA_DH + 2,
                                                                slope * float(jj * nkeys - offset), 0.0))))
                bias_s[jj, rs, :] = upper.astype(BF16)
            new_blk = offset // SEL_BLOCK
            selnew_s[rs, :] = jnp.broadcast_to(sel_rows[:, new_blk:new_blk + 1], (rows, LANES))
            pos1 = offset - w_pre + lax.broadcasted_iota(jnp.int32, (1, w_pre), 1)
            d1 = tq - pos1
            pos2 = offset + lax.broadcasted_iota(jnp.int32, (1, kwn_ref.shape[1]), 1)
            d2 = tq - pos2
            v1 = (d1 >= 0) & (d1 < WINDOW)
            v2 = (d2 >= 0) & (d2 < WINDOW)
            s1 = jnp.where(v1, _mm(qh, wk_ref[0]) - slope * d1.astype(F32), NEG_INF)
            s2 = jnp.where(v2, _mm_nt(qh, kwn_ref[0]) - slope * d2.astype(F32), NEG_INF)
            mx = jnp.maximum(jnp.max(s1, axis=-1, keepdims=True), jnp.max(s2, axis=-1, keepdims=True))
            p1 = jnp.where(v1, jnp.exp(s1 - mx), 0.0)
            p2 = jnp.where(v2, jnp.exp(s2 - mx), 0.0)
            l_w = jnp.sum(p1, axis=-1, keepdims=True) + jnp.sum(p2, axis=-1, keepdims=True)
            ow_s[rs, :] = (_mm_nt(p1, wv_ref[0]) + _mm(p2, vwn_ref[0])) / jnp.maximum(l_w, 1e-30)
        m_s[...] = jnp.full(m_s.shape, NEG_INF, F32)
        l_s[...] = jnp.zeros(l_s.shape, F32)
        acc_s[...] = jnp.zeros(acc_s.shape, F32)

    step_copies(b, j, cur, lambda c: c.wait())

    qa = jnp.concatenate([q_s[...], bias_s[j]], axis=1)
    m, l, acc = m_s[:, 0:1], l_s[:, 0:1], acc_s[...]
    for c in range(nkeys // kchunk):
        cs = pl.ds(c * kchunk, kchunk)
        ka = jnp.concatenate([kbuf[cur, :, cs].astype(BF16), cf_ref[:, cs]], axis=0)
        s = jnp.dot(qa, ka, preferred_element_type=F32)
        m_new = jnp.maximum(m, jnp.max(s, axis=-1, keepdims=True))
        alpha = jnp.exp(m - m_new)
        p = jnp.exp(s - m_new)
        l = alpha * l + jnp.sum(p, axis=-1, keepdims=True)
        acc = alpha * acc + _mm_nt(p, vbuf[cur, :, cs])
        m = m_new
    m_s[...] = jnp.broadcast_to(m, m_s.shape)
    l_s[...] = jnp.broadcast_to(l, l_s.shape)
    acc_s[...] = acc

    @pl.when(j == nchunks - 1)
    def _():
        gates = _sigmoid(sm_ref[0] + bg_ref[...])
        out_pairs = [None] * (NSA_HEADS // 2)
        nnew = ksn_ref.shape[1]
        r_new = lax.broadcasted_iota(jnp.int32, (1, nnew), 1)
        for hk in range(NSA_KV_HEADS):
            rs = slice(hk * rows, (hk + 1) * rows)
            own = (lane >= hk * NSA_DH) & (lane < (hk + 1) * NSA_DH)
            slope, tq = _row_consts(hk, nq, 0)
            valid = (r_new <= tq) & (selnew_s[rs, 0:1] > 0.5)
            s = jnp.where(valid, _mm_nt(q_s[rs, :], ksn_ref[0]) + slope * r_new.astype(F32), NEG_INF)
            m_o = m_s[rs, 0:1]
            m_f = jnp.maximum(m_o, jnp.max(s, axis=-1, keepdims=True))
            alpha = jnp.exp(m_o - m_f)
            p = jnp.where(valid, jnp.exp(s - m_f), 0.0)
            l_f = alpha * l_s[rs, 0:1] + jnp.sum(p, axis=-1, keepdims=True)
            o_s = (alpha * acc_s[rs, :] + _mm(p, vsn_ref[0])) / jnp.maximum(l_f, 1e-30)
            _combine_heads(oc_s[rs, :], o_s, ow_s[rs, :], gates, gh_ref, own, hk, nq, out_pairs)
        for k, y in enumerate(out_pairs):
            o_ref[0, :, k * LANES:(k + 1) * LANES] = y


def _nsa_sample2(page_table, q, sm, k_cmp, v_cmp, ks_pool, vs_pool, ks_new, vs_new, win_k, win_v, kw_new, vw_new,
                 bg, gh, nc_valid, ch):
    b, nq, _ = q.shape
    npg = page_table.shape[1]
    nchunks = npg // ch
    offset = npg * PAGE
    ncp = k_cmp.shape[1]
    ns = nc_valid // 4
    nsp = -(-(ns + LANES) // LANES) * LANES
    assert ch * PAGE // SEL_BLOCK <= NSA_DH, "a chunk's blocks must fit the 64 mask lanes"
    a = _sel_matrix(ncp, nc_valid, nsp).astype(BF16)
    rows2 = NSA_KV_HEADS * NSA_GROUP * nq
    kchunk = min(512, ch * PAGE)
    cf = _chunk_features(ch * PAGE)
    per_b = lambda n, w=LANES: pl.BlockSpec((1, n, w), lambda bi, j, t: (bi, 0, 0))
    const = lambda shape: pl.BlockSpec(shape, lambda bi, j, t: (0,) * len(shape))
    grid_spec = pltpu.PrefetchScalarGridSpec(
        num_scalar_prefetch=1,
        grid=(b, nchunks),
        in_specs=[per_b(nq, 4 * LANES), per_b(nq), per_b(ncp), per_b(ncp),
                  pl.BlockSpec(memory_space=pl.ANY), pl.BlockSpec(memory_space=pl.ANY),
                  per_b(PAGE), per_b(PAGE), per_b(LANES, win_k.shape[2]), per_b(LANES, win_v.shape[2]),
                  per_b(PAGE), per_b(PAGE),
                  const(a.shape), const(cf.shape), const((1, LANES)), const((NSA_HEADS, LANES))],
        out_specs=per_b(nq, 4 * LANES),
        scratch_shapes=[pltpu.VMEM((2, LANES, ch * PAGE), F32), pltpu.VMEM((2, LANES, ch * PAGE), F32),
                        pltpu.SemaphoreType.DMA((2, 2)),
                        pltpu.VMEM((rows2, LANES), BF16),
                        pltpu.VMEM((nchunks, rows2, LANES), BF16)]
        + [pltpu.VMEM((rows2, LANES), F32)] * 6,
    )
    return pl.pallas_call(
        functools.partial(_nsa_sample2_kernel, ch=ch, nchunks=nchunks, offset=offset, ns=ns,
                          n_top=min(SEL_TOPK, ns), kchunk=kchunk),
        grid_spec=grid_spec,
        out_shape=jax.ShapeDtypeStruct((b, nq, 4 * LANES), F32),
        compiler_params=_params(("arbitrary", "arbitrary")),
        name="nsa_sample",
    )(page_table, q, sm, k_cmp, v_cmp, ks_pool, vs_pool, ks_new, vs_new, win_k, win_v, kw_new, vw_new, a, cf, bg, gh)


def _mlstm_kernel(xm_ref, vm_ref, om_ref, sm_ref, smt_ref, conv0_ref, cw_ref, cb_ref, wq_ref, wk_ref,
                  bcol_ref, brow_ref, gh_ref, c0_ref, n0_ref, m0_ref,
                  hm_ref, c_ref, n_ref, m_ref, xs_ref, *, L, t_valid):
    c = pl.program_id(1)
    dh = MLSTM_DH

    @pl.when(c == 0)
    def _():
        xs_ref[0:8, :] = jnp.zeros((8, xs_ref.shape[1]), F32)
        xs_ref[5:8, :] = conv0_ref[0]
        c_ref[...] = c0_ref[...]
        n_ref[...] = n0_ref[...]
        m_ref[...] = m0_ref[...]

    xs_ref[8:8 + L, :] = xm_ref[0]
    xc = cb_ref[...]
    for jj in range(4):
        xc = xc + cw_ref[jj:jj + 1, :] * xs_ref[pl.ds(5 + jj, L), :]
    xc = xc * _sigmoid(xc)
    xs_ref[0:8, :] = xs_ref[L:L + 8, :]

    pre_col = sm_ref[0] + brow_ref[...]
    pre_row = smt_ref[0, 0] + bcol_ref[...]
    lf_col = _log_sigmoid(pre_col)
    lf_row = _log_sigmoid(pre_row)
    if t_valid < L:
        rid = lax.broadcasted_iota(jnp.int32, (L, 1), 0) < t_valid
        cid = lax.broadcasted_iota(jnp.int32, (1, L), 1) < t_valid
        lf_col = jnp.where(rid, lf_col, 0.0)
        lf_row = jnp.where(cid, lf_row, 0.0)
        pre_col = jnp.where(rid, pre_col, NEG_INF)
        pre_row = jnp.where(cid, pre_row, NEG_INF)
    ri = lax.broadcasted_iota(jnp.int32, (L, L), 0)
    ci = lax.broadcasted_iota(jnp.int32, (L, L), 1)
    causal = ci <= ri
    tril = causal.astype(BF16)
    triu = (ri <= ci).astype(BF16)
    bcum_col = sum(jnp.dot(tril, part, preferred_element_type=F32) for part in _bf16_parts(lf_col))
    bcum_row = sum(jnp.dot(part, triu, preferred_element_type=F32) for part in _bf16_parts(lf_row))

    for h in range(MLSTM_HEADS):
        hs = slice(h * dh, (h + 1) * dh)
        xh = xc[:, hs]
        q = _mm(xh, wq_ref[h])
        k = _mm(xh, wk_ref[h]) * (dh ** -0.5)
        v = vm_ref[0, :, hs]
        bc = bcum_col[:, 28 + h:29 + h]
        ic = pre_col[:, 24 + h:25 + h]
        br = bcum_row[4 + h:5 + h, :]
        ir = pre_row[h:h + 1, :]
        mh = m_ref[0, :, h:h + 1]
        ch_ = c_ref[0, h]
        nh = n_ref[0, h:h + 1, :]
        dmat = jnp.where(causal, bc - br + ir, NEG_INF)
        inter = bc + mh
        mq = jnp.maximum(inter, jnp.max(dmat, axis=1, keepdims=True))
        a = jnp.exp(dmat - mq) * _mm_nt(q, k)
        wi = jnp.exp(inter - mq)
        num = _mm(a, v) + wi * _mm_nt(q, ch_)
        den = jnp.sum(a, axis=1, keepdims=True) + wi * jnp.sum(q * nh, axis=1, keepdims=True)
        hout = num / jnp.maximum(jnp.abs(den), jnp.exp(-mq))
        btot = bc[L - 1:L, :]
        dec_r = btot - br + ir
        dec_c = btot - bc + ic
        m_new = jnp.maximum(btot + mh, jnp.max(dec_r, axis=1, keepdims=True))
        ws_c = jnp.exp(dec_c - m_new)
        w_c = jnp.exp(btot + mh - m_new)
        c_ref[0, h] = w_c * ch_ + _mm_tn(v * ws_c, k)
        n_ref[0, h:h + 1, :] = w_c * nh + jnp.sum(k * ws_c, axis=0, keepdims=True)
        m_ref[0, :, h:h + 1] = m_new
        y = _rms(hout, gh_ref[h:h + 1, :]) * _sigmoid(om_ref[0, :, hs])
        hm_ref[0, :, hs] = y


def _mlstm(xm, vm, om, sm, conv0, conv_w, conv_b, w_qm, w_km, b_i, b_f, g_head, c0, n0, m0, L, t_valid):
    b, t, w = xm.shape
    nck = t // L
    smt = sm[:, :, 24:32].reshape(b, nck, L, 8).transpose(0, 1, 3, 2)
    brow = jnp.zeros((1, LANES), F32).at[0, 24:28].set(b_i).at[0, 28:32].set(b_f)
    bcol = jnp.concatenate([b_i, b_f]).reshape(8, 1)
    m0p = jnp.zeros((b, 1, LANES), F32).at[:, 0, :MLSTM_HEADS].set(m0)
    tok = lambda: pl.BlockSpec((1, L, w), lambda bi, c: (bi, c, 0))
    const = lambda shape: pl.BlockSpec(shape, lambda bi, c: (0,) * len(shape))
    state = lambda shape: pl.BlockSpec((1,) + shape, lambda bi, c: (bi,) + (0,) * len(shape))
    hm, c_out, n_out, m_out = pl.pallas_call(
        functools.partial(_mlstm_kernel, L=L, t_valid=t_valid),
        grid=(b, nck),
        in_specs=[tok(), tok(), tok(),
                  pl.BlockSpec((1, L, LANES), lambda bi, c: (bi, c, 0)),
                  pl.BlockSpec((1, 1, 8, L), lambda bi, c: (bi, c, 0, 0)),
                  state(conv0.shape[1:]),
                  const(conv_w.shape), const((1, w)), const(w_qm.shape), const(w_km.shape),
                  const((8, 1)), const((1, LANES)), const(g_head.shape),
                  state(c0.shape[1:]), state(n0.shape[1:]), state((1, LANES))],
        out_specs=[tok(), state(c0.shape[1:]), state(n0.shape[1:]), state((1, LANES))],
        out_shape=[jax.ShapeDtypeStruct((b, t, w), F32), jax.ShapeDtypeStruct(c0.shape, F32),
                   jax.ShapeDtypeStruct(n0.shape, F32), jax.ShapeDtypeStruct((b, 1, LANES), F32)],
        scratch_shapes=[pltpu.VMEM((L + 8, w), F32)],
        compiler_params=_params(("arbitrary", "arbitrary")),
        name="mlstm",
    )(xm, vm, om, sm, smt, conv0, conv_w, conv_b.reshape(1, w), w_qm.astype(BF16), w_km.astype(BF16),
      bcol, brow, g_head, c0, n0, m0p)
    return hm, c_out, n_out, m_out[:, 0, :MLSTM_HEADS]


def _mix_kernel(x_ref, on_ref, hm_ref, wo1_ref, wo2_ref, gxa_ref, wxq_ref, x1_ref, qx_ref):
    x1 = x_ref[...] + _mm(on_ref[...], wo1_ref[...]) + _mm(hm_ref[...], wo2_ref[...])
    x1_ref[...] = x1
    qx_ref[...] = _mm(_rms(x1, gxa_ref[...]), wxq_ref[...]) * (XA_DH ** -0.5)


def _mix(x, o_nsa, hm, w_out, g_xa, w_xq, tm):
    m, d = x.shape
    half = o_nsa.shape[1]
    row = lambda w: pl.BlockSpec((tm, w), lambda i: (i, 0))
    const = lambda shape: pl.BlockSpec(shape, lambda i: (0, 0))
    return pl.pallas_call(
        _mix_kernel,
        grid=(m // tm,),
        in_specs=[row(d), row(half), row(half), const((half, d)), const((half, d)), const((1, d)), const((d, d))],
        out_specs=[row(d), row(d)],
        out_shape=[jax.ShapeDtypeStruct((m, d), F32)] * 2,
        compiler_params=_params(("parallel",)),
        name="mix",
    )(x, o_nsa, hm, w_out[:half].astype(BF16), w_out[half:].astype(BF16), g_xa.reshape(1, d), w_xq.astype(BF16))


def _xattn_kernel(qx_ref, mk_ref, mv_ref, o_ref, *, per_head):
    for h in range(XA_HEADS):
        hs = slice(h * XA_DH, (h + 1) * XA_DH)
        mk = mk_ref[0, :, h, :] if per_head else mk_ref[0, :, hs]
        mv = mv_ref[0, :, h, :] if per_head else mv_ref[0, :, hs]
        s = _mm_nt(qx_ref[0, :, hs], mk)
        p = jnp.exp(s - jnp.max(s, axis=-1, keepdims=True))
        o_ref[0, :, hs] = _mm(p, mv) / jnp.sum(p, axis=-1, keepdims=True)


def _xattn(qx, mem_k, mem_v, tq):
    b, t, d = qx.shape
    nm = mem_k.shape[1]
    per_head = mem_k.ndim == 4
    mem_spec = pl.BlockSpec((1,) + mem_k.shape[1:], lambda bi, i: (bi,) + (0,) * (mem_k.ndim - 1))
    return pl.pallas_call(
        functools.partial(_xattn_kernel, per_head=per_head),
        grid=(b, t // tq),
        in_specs=[pl.BlockSpec((1, tq, d), lambda bi, i: (bi, i, 0)), mem_spec, mem_spec],
        out_specs=pl.BlockSpec((1, tq, d), lambda bi, i: (bi, i, 0)),
        out_shape=jax.ShapeDtypeStruct((b, t, d), F32),
        compiler_params=_params(("parallel", "parallel")),
        name="xattn",
    )(qx, mem_k, mem_v)


def _ffn_kernel(x1_ref, ox_ref, wxo_ref, gf_ref, wg_ref, wu_ref, wd_ref, gfin_ref, y_ref, x2_s, h_s, acc_s):
    j = pl.program_id(1)

    @pl.when(j == 0)
    def _():
        x2 = x1_ref[...] + _mm(ox_ref[...], wxo_ref[...])
        x2_s[...] = x2
        h_s[...] = _rms(x2, gf_ref[...]).astype(BF16)
        acc_s[...] = jnp.zeros(acc_s.shape, F32)

    h = h_s[...]
    g = jnp.dot(h, wg_ref[...], preferred_element_type=F32)
    u = jnp.dot(h, wu_ref[...], preferred_element_type=F32)
    acc_s[...] += _mm(g * _sigmoid(g) * u, wd_ref[...])

    @pl.when(j == pl.num_programs(1) - 1)
    def _():
        y_ref[...] = _rms(x2_s[...] + acc_s[...], gfin_ref[...])


def _ffn(x1, ox, w_xo, g_ffn, w_gate, w_up, w_down, g_final, tm, tf):
    m, d = x1.shape
    dff = w_gate.shape[1]
    row = pl.BlockSpec((tm, d), lambda i, j: (i, 0))
    vec = pl.BlockSpec((1, d), lambda i, j: (0, 0))
    return pl.pallas_call(
        _ffn_kernel,
        grid=(m // tm, dff // tf),
        in_specs=[row, row, pl.BlockSpec((d, d), lambda i, j: (0, 0)), vec,
                  pl.BlockSpec((d, tf), lambda i, j: (0, j)), pl.BlockSpec((d, tf), lambda i, j: (0, j)),
                  pl.BlockSpec((tf, d), lambda i, j: (j, 0)), vec],
        out_specs=row,
        out_shape=jax.ShapeDtypeStruct((m, d), F32),
        scratch_shapes=[pltpu.VMEM((tm, d), F32), pltpu.VMEM((tm, d), BF16), pltpu.VMEM((tm, d), F32)],
        compiler_params=_params(("parallel", "arbitrary")),
        name="ffn",
    )(x1, ox, w_xo.astype(BF16), g_ffn.reshape(1, d), w_gate.astype(BF16), w_up.astype(BF16),
      w_down.astype(BF16), g_final.reshape(1, d))


def _split_w_in(w_in, nsa_w, kv_w, mlstm_w):
    cuts = np.cumsum([nsa_w] + [kv_w] * 6 + [3 * NSA_HEADS] + [mlstm_w] * 3 + [MLSTM_HEADS] * 2)
    parts = jnp.split(w_in, cuts[:-1].tolist(), axis=1)
    small = jnp.concatenate([parts[7], parts[11], parts[12]], axis=1)
    small = jnp.pad(small, ((0, 0), (0, LANES - small.shape[1])))
    ws = [parts[0]] + list(parts[1:7]) + [small] + list(parts[8:11])
    return [w.astype(BF16) for w in ws]


def _tail(x1, ox, w, tm, b, t):
    d = x1.shape[1]
    dff = w["w_gate"].shape[1]
    tf = dff // 2 if (dff // 2) % LANES == 0 else dff
    y = _ffn(x1, ox.reshape(-1, d), w["w_xo"], w["g_ffn"], w["w_gate"], w["w_up"], w["w_down"], w["g_final"], tm, tf)
    return y.reshape(b, t, d)


def _gate_consts(w):
    bg = jnp.pad(w["b_gate"], (0, LANES - w["b_gate"].shape[0])).reshape(1, LANES)
    gh = jnp.concatenate([w["g_head_nsa"], w["g_head_nsa"]], axis=1)
    return bg, gh


def _prompt_group(x, mem, w):
    b, t, d = x.shape
    m = b * t
    tm = 512
    wp = w["w_in_parts"]
    wt = jnp.concatenate(wp[1:7], axis=1).T
    q, sm, xm, vm, om, kc, vc, ks, vs, kw, vw = _norm_proj(x.reshape(m, d), w["g_mix"], [wp[0]] + wp[7:], tm,
                                                           wt=wt, t=t)
    r3 = lambda a: a.reshape(b, t, a.shape[-1])
    npg = t // PAGE
    table = jnp.zeros((b, npg), jnp.int32)
    zeros_next = jnp.zeros((b, PAGE, LANES), F32)
    k_cmp = _compress(table, kc, zeros_next, *w["cmp_k"], ch=npg, paged=False, feature_major=True)
    v_cmp = _compress(table, vc, zeros_next, *w["cmp_v"], ch=npg, paged=False, feature_major=True)
    bg, gh = _gate_consts(w)
    o_nsa = _nsa_prompt_t(r3(q), r3(sm), k_cmp, v_cmp, ks, vs, kw, vw, bg, gh, LANES)
    L = next(c for c in (256, 128, 64) if t % c == 0)
    hm, c_out, n_out, m_out = _mlstm(
        r3(xm), r3(vm), r3(om), r3(sm), jnp.zeros((b, 3, xm.shape[1]), F32), w["conv_w"], w["conv_b"],
        w["w_qm"], w["w_km"], w["b_i"], w["b_f"], w["g_head_m"],
        jnp.zeros((b, MLSTM_HEADS, MLSTM_DH, MLSTM_DH), F32), jnp.zeros((b, MLSTM_HEADS, MLSTM_DH), F32),
        jnp.zeros((b, MLSTM_HEADS), F32), L, L)
    nm = mem.shape[1]
    mk, mv = _norm_proj(mem.reshape(b * nm, d), w["g_mem"], [w["w_xk"].astype(BF16), w["w_xv"].astype(BF16)],
                        min(512, b * nm))
    x1, qx = _mix(x.reshape(m, d), o_nsa.reshape(m, -1), hm.reshape(m, -1), w["w_out"], w["g_xa"], w["w_xq"], tm)
    ox = _xattn(qx.reshape(b, t, d), mk.reshape(b, nm, d), mv.reshape(b, nm, d), 512)
    y = _tail(x1, ox, w, tm, b, t)
    kv5 = lambda a: a.reshape(b, NSA_KV_HEADS, NSA_DH, a.shape[2]).transpose(0, 3, 1, 2)[None]
    keep = min(WINDOW, t)
    xm3 = r3(xm)
    states = (kv5(kc), kv5(vc), kv5(ks), kv5(vs), kv5(kw[:, :, t - keep:]), kv5(vw[:, :, t - keep:]),
              c_out[None], n_out[None], m_out[None], xm3[None, :, t - 3:],
              mk.reshape(1, b, nm, XA_HEADS, XA_DH), mv.reshape(1, b, nm, XA_HEADS, XA_DH))
    return y, states


def _sample_group(x, pools, page_table, win_k, win_v, conv0, c0, n0, m0, mem_k, mem_v, w):
    b, t, d = x.shape
    m = b * t
    tp = 8
    tm = min(m, 512)
    q, kc, vc, ks, vs, kw, vw, sm, xm, vm, om = _norm_proj(x.reshape(m, d), w["g_mix"], w["w_in_parts"], tm)
    r3 = lambda a: a.reshape(b, t, a.shape[-1])
    pad_t = lambda a: jnp.pad(r3(a), ((0, 0), (0, tp - t), (0, 0)))
    npg = page_table.shape[1]
    past = npg * PAGE
    ch = min(32, npg)
    fm = lambda a: a.transpose(0, 2, 3, 1).reshape(a.shape[0], LANES, a.shape[1])
    pool_kc, pool_vc, pool_ks, pool_vs = [fm(p) for p in pools]
    win_kt, win_vt = fm(win_k), fm(win_v)
    nc_valid = (past + (-(-t // SEL_BLOCK)) * SEL_BLOCK) // CMP_STRIDE
    ncp = -(-nc_valid // LANES) * LANES

    pad_page = lambda a: jnp.pad(r3(a), ((0, 0), (0, PAGE - t), (0, 0)))

    def compressed(pool, new_rows, cw):
        nxt = pad_page(new_rows)
        main = _compress(page_table, pool, nxt, *cw, ch=ch, paged=True, feature_major=True)
        tail = _compress(jnp.arange(b, dtype=jnp.int32).reshape(1, b), nxt, jnp.zeros((1, PAGE, LANES), F32),
                         *cw, ch=b, paged=True, feature_major=False)
        n_tail = nc_valid - npg * CMP_PER_PAGE
        full = jnp.concatenate([main, tail.reshape(b, CMP_PER_PAGE, LANES)[:, :n_tail]], axis=1)
        return jnp.pad(full, ((0, 0), (0, ncp - nc_valid), (0, 0)))

    k_cmp = compressed(pool_kc, kc, w["cmp_k"])
    v_cmp = compressed(pool_vc, vc, w["cmp_v"])
    bg, gh = _gate_consts(w)
    o_nsa = _nsa_sample2(page_table, pad_t(q), pad_t(sm), k_cmp, v_cmp, pool_ks, pool_vs, pad_page(ks), pad_page(vs),
                        win_kt, win_vt, pad_page(kw), pad_page(vw),
                        bg, gh, nc_valid, ch)[:, :t]
    hm, c_out, n_out, m_out = _mlstm(pad_t(xm), pad_t(vm), pad_t(om), pad_t(sm), conv0, w["conv_w"], w["conv_b"],
                                     w["w_qm"], w["w_km"], w["b_i"], w["b_f"], w["g_head_m"], c0, n0, m0, tp, t)
    hm = hm[:, :t]
    x1, qx = _mix(x.reshape(m, d), o_nsa.reshape(m, -1), hm.reshape(m, -1), w["w_out"], w["g_xa"], w["w_xq"], tm)
    nm = mem_k.shape[1]
    ox = _xattn(pad_t(qx), mem_k, mem_v, tp)[:, :t]
    y = _tail(x1, ox, w, tm, b, t)
    kv5 = lambda a: a.reshape(1, b, t, NSA_KV_HEADS, NSA_DH)
    keep = min(WINDOW, past + t)
    unfm = lambda a: a.reshape(b, NSA_KV_HEADS, NSA_DH, a.shape[2]).transpose(0, 3, 1, 2)[None]
    win5 = lambda old_t, new: unfm(jnp.concatenate([old_t, r3(new).transpose(0, 2, 1)], axis=2)[:, :, -keep:])
    conv_all = jnp.concatenate([conv0, r3(xm)], axis=1)
    states = (kv5(kc), kv5(vc), kv5(ks), kv5(vs), win5(win_kt, kw), win5(win_vt, vw),
              c_out[None], n_out[None], m_out[None], conv_all[None, :, -3:])
    return y, states


def kernel(x_prompt, x_sample, cache_k_cmp, cache_v_cmp, cache_k_slc, cache_v_slc, state_k_win, state_v_win, state_conv, state_C, state_n, state_m, cache_mem_k, cache_mem_v, page_table, mem_prompt, g_mix, w_in, b_gate, cmp_pe_k, cmp_w1_k, cmp_b1_k, cmp_w2_k, cmp_pe_v, cmp_w1_v, cmp_b1_v, cmp_w2_v, g_head_nsa, conv_w, conv_b, w_qm, w_km, b_i, b_f, g_head_m, w_out, g_xa, g_mem, w_xq, w_xk, w_xv, w_xo, g_ffn, w_gate, w_up, w_down, g_final):
    assert w_in.shape[0] == 1, "single-layer decoder"
    l = 0
    w = dict(g_mix=g_mix[l], b_gate=b_gate[l],
             w_in_parts=_split_w_in(w_in[l], NSA_HEADS * NSA_DH, NSA_KV_HEADS * NSA_DH, MLSTM_HEADS * MLSTM_DH),
             cmp_k=(cmp_pe_k[l], cmp_w1_k[l], cmp_b1_k[l], cmp_w2_k[l]),
             cmp_v=(cmp_pe_v[l], cmp_w1_v[l], cmp_b1_v[l], cmp_w2_v[l]),
             g_head_nsa=g_head_nsa[l], conv_w=conv_w[l], conv_b=conv_b[l], w_qm=w_qm[l], w_km=w_km[l],
             b_i=b_i[l], b_f=b_f[l], g_head_m=g_head_m[l], w_out=w_out[l], g_xa=g_xa[l], g_mem=g_mem[l],
             w_xq=w_xq[l], w_xk=w_xk[l], w_xv=w_xv[l], w_xo=w_xo[l], g_ffn=g_ffn[l], w_gate=w_gate[l],
             w_up=w_up[l], w_down=w_down[l], g_final=g_final)
    y_p, st_p = _prompt_group(x_prompt, mem_prompt, w)
    pools = (cache_k_cmp[l], cache_v_cmp[l], cache_k_slc[l], cache_v_slc[l])
    y_s, st_s = _sample_group(x_sample, pools, page_table, state_k_win[l], state_v_win[l], state_conv[l],
                              state_C[l], state_n[l], state_m[l], cache_mem_k[l], cache_mem_v[l], w)
    return (y_p, y_s) + st_p + st_s
```

```python
import functools

import numpy as np
import jax
import jax.numpy as jnp
from jax import lax
from jax.experimental import pallas as pl
from jax.experimental.pallas import tpu as pltpu

F32 = jnp.float32
BF16 = jnp.bfloat16

EPS = 1e-6
NEG_INF = -1e30
FORCE_SCORE = 1e9
PAD_SCORE = -2e38
TAKEN_SCORE = -3e38

LANES = 128
NSA_HEADS = 8
NSA_KV_HEADS = 2
NSA_GROUP = 4
NSA_DH = 64
CMP_STRIDE = 16
SEL_BLOCK = 64
SEL_TOPK = 16
WINDOW = 512
Q_BLOCK = 64
PAGE = 128
CMP_PER_PAGE = PAGE // CMP_STRIDE
MLSTM_HEADS = 4
MLSTM_DH = 128
XA_HEADS = 4
XA_DH = 256
ALIBI = tuple(2.0 ** (-(h + 1)) for h in range(NSA_HEADS))

VMEM_LIMIT = 56 * 1024 * 1024


def _params(sem):
    return pltpu.CompilerParams(dimension_semantics=sem, vmem_limit_bytes=VMEM_LIMIT)


def _mm(a, b):
    return jnp.dot(a.astype(BF16), b.astype(BF16), preferred_element_type=F32)


def _mm_nt(a, b):
    return lax.dot_general(a.astype(BF16), b.astype(BF16), (((1,), (1,)), ((), ())),
                           preferred_element_type=F32)


def _mm_tn(a, b):
    return lax.dot_general(a.astype(BF16), b.astype(BF16), (((0,), (0,)), ((), ())),
                           preferred_element_type=F32)


def _bf16_parts(x):
    hi = x.astype(BF16)
    r1 = x - hi.astype(F32)
    mid = r1.astype(BF16)
    lo = (r1 - mid.astype(F32)).astype(BF16)
    return hi, mid, lo


def _rms(x, g):
    return x * lax.rsqrt(jnp.mean(x * x, axis=-1, keepdims=True) + EPS) * g


def _sigmoid(x):
    return 1.0 / (1.0 + jnp.exp(-x))


def _log_sigmoid(x):
    return jnp.minimum(x, 0.0) - jnp.log(1.0 + jnp.exp(-jnp.abs(x)))


def _gelu_tanh(x):
    return 0.5 * x * (1.0 + jnp.tanh(0.7978845608028654 * (x + 0.044715 * x * x * x)))


def _topk_mask(imp, k):
    col = lax.broadcasted_iota(jnp.int32, imp.shape, 1)
    sel = jnp.zeros(imp.shape, F32)
    work = imp
    for _ in range(k):
        m = jnp.max(work, axis=-1, keepdims=True)
        idx = jnp.min(jnp.where(work == m, col, jnp.int32(2 ** 30)), axis=-1, keepdims=True)
        hit = col == idx
        sel = jnp.where(hit, 1.0, sel)
        work = jnp.where(hit, TAKEN_SCORE, work)
    return sel


def _softmax_parts(s, valid):
    s = jnp.where(valid, s, NEG_INF)
    mx = jnp.max(s, axis=-1, keepdims=True)
    p = jnp.where(valid, jnp.exp(s - mx), 0.0)
    return p, mx, jnp.sum(p, axis=-1, keepdims=True)


def _q_rows_f32(q_ref, hk, nq):
    lane = lax.broadcasted_iota(jnp.int32, (1, LANES), 1)
    own = (lane >= hk * NSA_DH) & (lane < (hk + 1) * NSA_DH)
    parts = []
    for g in range(NSA_GROUP):
        h = hk * NSA_GROUP + g
        blk = q_ref[0, :, (h // 2) * LANES:(h // 2 + 1) * LANES]
        if (h % 2) != hk:
            blk = pltpu.roll(blk, NSA_DH, 1)
        parts.append(blk)
    qh = jnp.concatenate(parts, axis=0)
    return jnp.where(own, qh * (NSA_DH ** -0.5), 0.0), own


def _q_rows(q_ref, hk, nq):
    qh, own = _q_rows_f32(q_ref, hk, nq)
    return qh.astype(BF16), own


def _feature_lanes(hk):
    base = (1 - hk) * NSA_DH
    return base, base + 1, base + 2


def _select_t(imp_t, tq_lane, ns, n_top):
    blk = lax.broadcasted_iota(jnp.int32, (imp_t.shape[0], 1), 0)
    cur = tq_lane >> 6
    forced = (blk == 0) | (blk == cur) | (blk == cur - 1)
    work = jnp.where(forced, FORCE_SCORE, imp_t)
    work = jnp.where(blk * SEL_BLOCK <= tq_lane, work, NEG_INF)
    work = jnp.where(blk < ns, work, PAD_SCORE)
    sel = jnp.zeros(imp_t.shape, F32)
    for _ in range(n_top):
        m = jnp.max(work, axis=0, keepdims=True)
        idx = jnp.min(jnp.where(work == m, blk, jnp.int32(2 ** 30)), axis=0, keepdims=True)
        hit = blk == idx
        sel = jnp.where(hit, 1.0, sel)
        work = jnp.where(hit, TAKEN_SCORE, work)
    return sel


def _row_consts(hk, nq, t0):
    rows = NSA_GROUP * nq
    row = lax.broadcasted_iota(jnp.int32, (rows, 1), 0)
    slope = jnp.full((rows, 1), ALIBI[hk * NSA_GROUP + NSA_GROUP - 1], F32)
    for g in range(NSA_GROUP - 2, -1, -1):
        slope = jnp.where(row < (g + 1) * nq, ALIBI[hk * NSA_GROUP + g], slope)
    tq = t0 + (row & (nq - 1))
    return slope, tq


def _cmp_branch(qh, slope, tq, kc_ref, vc_ref, a_ref, nq):
    ncp = kc_ref.shape[1]
    s = _mm_nt(qh, kc_ref[0])
    cend = lax.broadcasted_iota(jnp.int32, (1, ncp), 1) * CMP_STRIDE + (2 * CMP_STRIDE - 1)
    d = tq - cend
    valid = d >= 0
    p, _, l = _softmax_parts(s - slope * d.astype(F32), valid)
    p = p / jnp.maximum(l, 1e-30)
    o_c = _mm(p, vc_ref[0])
    psum = p[0:nq]
    for g in range(1, NSA_GROUP):
        psum = psum + p[g * nq:(g + 1) * nq]
    return o_c, sum(jnp.dot(part, a_ref[...], preferred_element_type=F32) for part in _bf16_parts(psum))


def _select(imp, tq_q, ns, n_top):
    nsp = imp.shape[1]
    blk = lax.broadcasted_iota(jnp.int32, (1, nsp), 1)
    cur = tq_q >> 6
    forced = (blk == 0) | (blk == cur) | (blk == cur - 1)
    imp = jnp.where(forced, FORCE_SCORE, imp)
    imp = jnp.where(blk * SEL_BLOCK <= tq_q, imp, NEG_INF)
    imp = jnp.where(blk < ns, imp, PAD_SCORE)
    return _topk_mask(imp, n_top)


def _combine_heads(o_c, o_s, o_w, gates, gh_ref, own, hk, nq, out_pairs):
    for g in range(NSA_GROUP):
        h = hk * NSA_GROUP + g
        r = slice(g * nq, (g + 1) * nq)
        og = (gates[:, 3 * h:3 * h + 1] * o_c[r] + gates[:, 3 * h + 1:3 * h + 2] * o_s[r]
              + gates[:, 3 * h + 2:3 * h + 3] * o_w[r])
        og = jnp.where(own, og, 0.0)
        ms = jnp.sum(og * og, axis=-1, keepdims=True) * (1.0 / NSA_DH)
        y = og * lax.rsqrt(ms + EPS) * gh_ref[h:h + 1, :]
        if (h % 2) != hk:
            y = pltpu.roll(y, NSA_DH, 1)
        out_pairs[h // 2] = y if out_pairs[h // 2] is None else out_pairs[h // 2] + y


def _norm_proj_kernel(x_ref, g_ref, *refs, n, n_t):
    h = _rms(x_ref[...], g_ref[...]).astype(BF16)
    has_t = 1 if n_t else 0
    outs = refs[n + has_t:]
    for w_ref, o_ref in zip(refs[:n], outs[:n]):
        o_ref[...] = jnp.dot(h, w_ref[...], preferred_element_type=F32)
    if n_t:
        yt = lax.dot_general(refs[n][...], h, (((1,), (1,)), ((), ())), preferred_element_type=F32)
        for k, o_ref in enumerate(outs[n:]):
            o_ref[0] = yt[k * LANES:(k + 1) * LANES, :]


def _norm_proj(x, g, ws, tm, wt=None, t=None):
    m, d = x.shape
    n = len(ws)
    n_t = 0 if wt is None else wt.shape[0] // LANES
    in_specs = [pl.BlockSpec((tm, d), lambda i: (i, 0)), pl.BlockSpec((1, d), lambda i: (0, 0))]
    in_specs += [pl.BlockSpec(w.shape, lambda i: (0, 0)) for w in ws]
    out_specs = [pl.BlockSpec((tm, w.shape[1]), lambda i: (i, 0)) for w in ws]
    out_shape = [jax.ShapeDtypeStruct((m, w.shape[1]), F32) for w in ws]
    args = [x, g.reshape(1, d), *ws]
    if n_t:
        per_b = t // tm
        in_specs.append(pl.BlockSpec(wt.shape, lambda i: (0, 0)))
        out_specs += [pl.BlockSpec((1, LANES, tm), lambda i: (i // per_b, 0, i % per_b))] * n_t
        out_shape += [jax.ShapeDtypeStruct((m // t, LANES, t), F32)] * n_t
        args.append(wt)
    return pl.pallas_call(
        functools.partial(_norm_proj_kernel, n=n, n_t=n_t),
        grid=(m // tm,),
        in_specs=in_specs,
        out_specs=out_specs,
        out_shape=out_shape,
        compiler_params=_params(("parallel",)),
        name="norm_proj",
    )(*args)


def _compress_kernel(tbl_ref, pool_ref, last_ref, wr_ref, pe_ref, b1_ref, w2_ref, o_ref,
                     buf, xrow, acc, cst, sem, *, ch, nchunks, nbatch, paged, feature_major):
    b = pl.program_id(0)
    j = pl.program_id(1)
    nblk = ch * CMP_PER_PAGE
    half = wr_ref.shape[2] // 2
    npairs = wr_ref.shape[0]

    n = b * nchunks + j
    n_steps = nbatch * nchunks
    cur = n % 2

    def page_copy(bb, k, buf_i, slot):
        if paged:
            src = pool_ref.at[tbl_ref[bb, k]]
        else:
            src = pool_ref.at[bb, :, pl.ds(pl.multiple_of(k * PAGE, PAGE), PAGE)]
        return pltpu.make_async_copy(src, buf.at[buf_i, slot], sem.at[buf_i, slot])

    def step_copies(m, fn):
        bb, jj, buf_i = m // nchunks, m % nchunks, m % 2
        for s in range(ch):
            fn(page_copy(bb, jj * ch + s, buf_i, s))

        @pl.when(jj < nchunks - 1)
        def _():
            fn(page_copy(bb, jnp.minimum((jj + 1) * ch, nchunks * ch - 1), buf_i, ch))

    def to_rows(m, i):
        bb, jj = jnp.minimum(m // nchunks, nbatch - 1), m % nchunks
        for s in range(ch + 1):
            page = buf[i, s]
            rows = page.T if feature_major else page
            if s == ch:
                rows = jnp.where(jj < nchunks - 1, rows, last_ref[bb])
            xrow[i, pl.ds(s * PAGE, PAGE), :] = rows

    @pl.when(n == 0)
    def _():
        if n_steps == 1:
            buf[1] = jnp.zeros(buf.shape[1:], F32)
        buf[0, ch] = jnp.zeros((PAGE, LANES), F32)
        buf[1, ch] = jnp.zeros((PAGE, LANES), F32)
        step_copies(0, lambda c: c.start())
        if n_steps > 1:
            step_copies(1, lambda c: c.start())
        c = jnp.zeros((8, half), F32)
        for pr in range(npairs):
            c = c + _mm(jnp.broadcast_to(pe_ref[0, pr:pr + 1, :], (8, 2 * LANES)), wr_ref[pr, :, 0:half])
            c = c + _mm(jnp.broadcast_to(pe_ref[1, pr:pr + 1, :], (8, 2 * LANES)), wr_ref[pr, :, half:2 * half])
        cst[...] = c + b1_ref[...]
        step_copies(0, lambda c: c.wait())
        to_rows(0, 0)

    @pl.when(n + 2 < n_steps)
    def _():
        step_copies(n + 2, lambda c: c.start())

    @pl.when(n + 1 < n_steps)
    def _():
        step_copies(n + 1, lambda c: c.wait())

    def compute(par):
        to_rows(n + 1, 1 - par)
        total = None
        for pr in range(npairs):
            x0 = xrow[par, pl.ds(2 * pr, nblk + 8, stride=CMP_STRIDE), :]
            x1 = xrow[par, pl.ds(2 * pr + 1, nblk + 8, stride=CMP_STRIDE), :]
            prod = _mm(jnp.concatenate([x0, x1], axis=1), wr_ref[pr])
            total = prod if total is None else total + prod
        acc[...] = total
        hid = acc[0:nblk, 0:half] + acc[pl.ds(1, nblk), half:2 * half] + cst[0:1, :]
        o_ref[0] = _mm(_gelu_tanh(hid), w2_ref[...])

    for par in range(2):
        pl.when(cur == par)(functools.partial(compute, par))


def _compress(table, pool, last_next, pe, w1, b1, w2, ch, paged, feature_major):
    nb, npg = table.shape
    nchunks = npg // ch
    hid = w1.shape[1]
    dh = NSA_DH
    wa = w1[:CMP_STRIDE * dh].reshape(CMP_STRIDE, dh, hid)
    wb = w1[CMP_STRIDE * dh:].reshape(CMP_STRIDE, dh, hid)
    z = jnp.zeros_like(wa)
    wr = jnp.concatenate([jnp.concatenate([wa, z, wb, z], axis=2),
                          jnp.concatenate([z, wa, z, wb], axis=2)], axis=1).astype(BF16)
    wr = wr.reshape(CMP_STRIDE // 2, 2 * LANES, 4 * hid)
    pe1 = jnp.concatenate([pe[:CMP_STRIDE], pe[:CMP_STRIDE]], axis=1).reshape(CMP_STRIDE // 2, 2 * LANES)
    pe2 = jnp.concatenate([pe[CMP_STRIDE:], pe[CMP_STRIDE:]], axis=1).reshape(CMP_STRIDE // 2, 2 * LANES)
    pes = jnp.stack([pe1, pe2])
    b1t = jnp.concatenate([b1, b1]).reshape(1, 2 * hid)
    zz = jnp.zeros_like(w2)
    w2bd = jnp.concatenate([jnp.concatenate([w2, zz], axis=1),
                            jnp.concatenate([zz, w2], axis=1)], axis=0).astype(BF16)
    nblk = ch * CMP_PER_PAGE
    grid_spec = pltpu.PrefetchScalarGridSpec(
        num_scalar_prefetch=1,
        grid=(nb, nchunks),
        in_specs=[pl.BlockSpec(memory_space=pl.ANY),
                  pl.BlockSpec(last_next.shape, lambda b, j, t: (0, 0, 0)),
                  pl.BlockSpec(wr.shape, lambda b, j, t: (0, 0, 0)),
                  pl.BlockSpec(pes.shape, lambda b, j, t: (0, 0, 0)),
                  pl.BlockSpec(b1t.shape, lambda b, j, t: (0, 0)),
                  pl.BlockSpec(w2bd.shape, lambda b, j, t: (0, 0))],
        out_specs=pl.BlockSpec((1, nblk, LANES), lambda b, j, t: (b, j, 0)),
        scratch_shapes=[pltpu.VMEM((2, ch + 1, PAGE, LANES), F32),
                        pltpu.VMEM((2, (ch + 1) * PAGE, LANES), F32),
                        pltpu.VMEM((nblk + 8, 4 * hid), F32),
                        pltpu.VMEM((8, 2 * hid), F32),
                        pltpu.SemaphoreType.DMA((2, ch + 1))],
    )
    return pl.pallas_call(
        functools.partial(_compress_kernel, ch=ch, nchunks=nchunks, nbatch=nb, paged=paged,
                          feature_major=feature_major),
        grid_spec=grid_spec,
        out_shape=jax.ShapeDtypeStruct((nb, npg * CMP_PER_PAGE, LANES), F32),
        compiler_params=_params(("arbitrary", "arbitrary")),
        name="compress",
    )(table, pool, last_next, wr, pes, b1t, w2bd)


def _sel_matrix(nc_rows, nc_valid, ns_cols):
    a = np.zeros((nc_rows, ns_cols), np.float32)
    for j in range(nc_valid // 4):
        for c, wgt in ((4 * j - 1, 0.5), (4 * j, 1.0), (4 * j + 1, 1.0), (4 * j + 2, 1.0), (4 * j + 3, 0.5)):
            if 0 <= c < nc_valid:
                a[c, j] += wgt
    return jnp.asarray(a)


def _head_feature_rows(hk, g, n):
    slope = ALIBI[hk * NSA_GROUP + g]
    row = lax.broadcasted_iota(jnp.int32, (NSA_DH, n), 0)
    return jnp.where(row == 0, slope * SEL_BLOCK,
                     jnp.where(row == 1, slope, jnp.where(row == 2, slope * CMP_STRIDE, 0.0)))


def _swap_halves(x):
    return jnp.concatenate([x[NSA_DH:], x[:NSA_DH]], axis=0)


def _nsa_prompt_t_kernel(qi_ref, ci_ref, q_ref, sm_ref, kc_ref, vc_ref, ks_ref, vs_ref, kw_ref, vw_ref, at_ref,
                         bg_ref, ght_ref, o_ref, ksr, kwr, vsa, vwa, kca, vct, qa_s, oc_s, ow_s, m_s, acc_s, used_ref,
                         *, nq, ns, n_top, kchunk, wpad):
    step = pl.program_id(1)
    i = qi_ref[step]
    c = ci_ref[step]
    t0 = pl.multiple_of(i * nq, nq)
    cols = NSA_GROUP * nq
    t = ks_ref.shape[2]
    nc = kc_ref.shape[1]
    nsr = at_ref.shape[0]
    wkeys = wpad + nq
    last = (t0 + nq + kchunk - 1) // kchunk - 1
    lane = lax.broadcasted_iota(jnp.int32, (1, LANES), 1)
    tq_q = t0 + lax.broadcasted_iota(jnp.int32, (1, nq), 1)
    tq = jnp.concatenate([tq_q] * NSA_GROUP, axis=1)

    @pl.when(step == 0)
    def _():
        srow = lax.broadcasted_iota(jnp.int32, (LANES, 1), 0)
        cidx = lax.broadcasted_iota(jnp.int32, (nc, 1), 0).astype(F32)
        vct_f = jnp.transpose(vc_ref[0]).astype(BF16)
        for hk in range(NSA_KV_HEADS):
            own = (lane >= hk * NSA_DH) & (lane < (hk + 1) * NSA_DH)
            f0, f1, f2 = _feature_lanes(hk)
            kca[hk] = jnp.where(own, kc_ref[0], jnp.where(lane == f2, cidx, 0.0)).astype(BF16)
            vct[hk] = vct_f
            kwr[hk, 0:wpad, :] = jnp.broadcast_to(jnp.where(lane == f0, -1e30, 0.0), (wpad, LANES)).astype(BF16)
            vwa[hk, :, 0:wpad] = jnp.zeros((LANES, wpad), BF16)

        def prep(r, _):
            r0 = pl.multiple_of(r * LANES, LANES)
            cs = pl.ds(r0, LANES)
            pos = r0 + lax.broadcasted_iota(jnp.int32, (LANES, 1), 0)
            blk = pos >> 6
            ks_rows = jnp.transpose(ks_ref[0, :, cs])
            kw_rows = jnp.transpose(kw_ref[0, :, cs])
            for hk in range(NSA_KV_HEADS):
                own = (lane >= hk * NSA_DH) & (lane < (hk + 1) * NSA_DH)
                own_r = (srow >= hk * NSA_DH) & (srow < (hk + 1) * NSA_DH)
                f0, f1, f2 = _feature_lanes(hk)
                feat = jnp.where(lane == f0, blk.astype(F32), jnp.where(lane == f1, (pos & 63).astype(F32), 0.0))
                ksr[hk, cs, 0:LANES] = jnp.where(own, ks_rows, feat).astype(BF16)
                ksr[hk, cs, LANES:2 * LANES] = (lane == blk).astype(BF16)
                kwr[hk, pl.ds(wpad + r0, LANES), :] = jnp.where(own, kw_rows, feat).astype(BF16)
                vsa[hk, :, cs] = jnp.where(own_r, vs_ref[0, :, cs], 1.0).astype(BF16)
                vwa[hk, :, pl.ds(wpad + r0, LANES)] = jnp.where(own_r, vw_ref[0, :, cs], 1.0).astype(BF16)
            return 0

        lax.fori_loop(0, t // LANES, prep, 0)

    @pl.when(c == 0)
    def _():
        qt = jnp.transpose(q_ref[0]) * (NSA_DH ** -0.5)
        cend = lax.broadcasted_iota(jnp.int32, (nc, 1), 0) * CMP_STRIDE + (2 * CMP_STRIDE - 1)
        wrow = lax.broadcasted_iota(jnp.int32, (LANES, 1), 0)
        sel_heads = []
        for hk in range(NSA_KV_HEADS):
            slabs = []
            for g in range(NSA_GROUP):
                h = hk * NSA_GROUP + g
                halves = [qt[h * NSA_DH:(h + 1) * NSA_DH], _head_feature_rows(hk, g, nq)]
                slabs.append(jnp.concatenate(halves if hk == 0 else halves[::-1], axis=0))
            qlo = jnp.concatenate(slabs, axis=1).astype(BF16)
            s = jnp.dot(kca[hk], qlo, preferred_element_type=F32)
            valid = cend <= tq
            s = jnp.where(valid, s, NEG_INF)
            p = jnp.where(valid, jnp.exp(s - jnp.max(s, axis=0, keepdims=True)), 0.0)
            p = p / jnp.maximum(jnp.sum(p, axis=0, keepdims=True), 1e-30)
            oc_s[hk] = jnp.dot(vct[hk], p.astype(BF16), preferred_element_type=F32)
            psum = p[:, 0:nq]
            for g in range(1, NSA_GROUP):
                psum = psum + p[:, g * nq:(g + 1) * nq]
            imp_t = sum(jnp.dot(at_ref[...], part, preferred_element_type=F32)
                        for part in _bf16_parts(psum))
            sel_t = _select_t(imp_t, tq_q, ns, n_top)
            sel_heads.append(sel_t)
            bias = (sel_t - 1.0) * 1e30
            bias = jnp.concatenate([bias, jnp.zeros((LANES - nsr, nq), F32)], axis=0) if nsr < LANES else bias
            qa_s[hk, 0:LANES, :] = qlo
            qa_s[hk, LANES:2 * LANES, :] = jnp.concatenate([bias] * NSA_GROUP, axis=1).astype(BF16)
            s = jnp.dot(kwr[hk, pl.ds(t0, wkeys), :], qlo, preferred_element_type=F32)
            old = jnp.where(t0 - wpad + wrow > tq - WINDOW, s[0:LANES], NEG_INF)
            new = jnp.where(t0 + wrow[0:nq] <= tq, s[wkeys - nq:wkeys], NEG_INF)
            s = jnp.concatenate([old, s[LANES:wkeys - nq], new], axis=0)
            p = jnp.exp(s - jnp.max(s, axis=0, keepdims=True)).astype(BF16)
            ow = jnp.dot(vwa[hk, :, pl.ds(t0, wkeys)], p, preferred_element_type=F32)
            l_w = _swap_halves(ow)
            ow_s[hk] = ow / jnp.maximum(l_w, 1e-30)
            m_s[hk] = jnp.full((8, cols), NEG_INF, F32)
            acc_s[hk] = jnp.zeros((LANES, cols), F32)
        any_sel = sel_heads[0]
        for s_t in sel_heads[1:]:
            any_sel = jnp.maximum(any_sel, s_t)
        per_chunk = kchunk // SEL_BLOCK
        for cc in range(t // kchunk):
            used_ref[cc] = (jnp.max(any_sel[cc * per_chunk:(cc + 1) * per_chunk, :]) > 0.5).astype(jnp.int32)

    k0 = pl.multiple_of(c * kchunk, kchunk)

    def flash(causal):
        for hk in range(NSA_KV_HEADS):
            s = jnp.dot(ksr[hk, pl.ds(k0, kchunk), :], qa_s[hk], preferred_element_type=F32)
            if causal:
                s = jnp.where(k0 + lax.broadcasted_iota(jnp.int32, (kchunk, 1), 0) <= tq, s, NEG_INF)
            m_old = m_s[hk][0:1, :]
            m_new = jnp.maximum(m_old, jnp.max(s, axis=0, keepdims=True))
            p = jnp.exp(s - m_new).astype(BF16)
            acc_s[hk] = jnp.exp(m_old - m_new) * acc_s[hk] + jnp.dot(vsa[hk, :, pl.ds(k0, kchunk)], p,
                                                                   preferred_element_type=F32)
            m_s[hk] = jnp.broadcast_to(m_new, (8, cols))

    @pl.when((c < last) & (used_ref[c] > 0))
    def _():
        flash(False)

    @pl.when(c == last)
    def _():
        flash(True)
        gates = jnp.transpose(_sigmoid(sm_ref[0] + bg_ref[...]))
        srow = lax.broadcasted_iota(jnp.int32, (LANES, 1), 0)
        halves = [None] * NSA_HEADS
        for hk in range(NSA_KV_HEADS):
            own_r = (srow >= hk * NSA_DH) & (srow < (hk + 1) * NSA_DH)
            acc = acc_s[hk]
            o_s = acc / jnp.maximum(_swap_halves(acc), 1e-30)
            o_c = oc_s[hk]
            o_w = ow_s[hk]
            for g in range(NSA_GROUP):
                h = hk * NSA_GROUP + g
                cs = slice(g * nq, (g + 1) * nq)
                og = (gates[3 * h:3 * h + 1] * o_c[:, cs] + gates[3 * h + 1:3 * h + 2] * o_s[:, cs]
                      + gates[3 * h + 2:3 * h + 3] * o_w[:, cs])
                og = jnp.where(own_r, og, 0.0)
                ms = jnp.sum(og * og, axis=0, keepdims=True) * (1.0 / NSA_DH)
                y = og * lax.rsqrt(ms + EPS) * ght_ref[:, h:h + 1]
                halves[h] = y[hk * NSA_DH:(hk + 1) * NSA_DH]
        for k in range(NSA_HEADS // 2):
            o_ref[0, :, k * LANES:(k + 1) * LANES] = jnp.transpose(
                jnp.concatenate([halves[2 * k], halves[2 * k + 1]], axis=0))


def _nsa_prompt_t(q, sm, k_cmp, v_cmp, ks, vs, kw, vw, bg, gh, nq):
    b, t, _ = q.shape
    nc = k_cmp.shape[1]
    ns = t // SEL_BLOCK
    nsr = -(-ns // 8) * 8
    assert nq == LANES and nsr <= LANES and nc <= 256 and t % 512 == 0, "prompt NSA kernel shape limits"
    at = _sel_matrix(nc, nc, nsr).T.astype(BF16)
    kchunk = min(512, t)
    wpad = WINDOW
    pairs = [(i, c) for i in range(t // nq) for c in range(-(-((i + 1) * nq) // kchunk))]
    qi = jnp.asarray(np.array([p[0] for p in pairs], np.int32))
    ci = jnp.asarray(np.array([p[1] for p in pairs], np.int32))
    cols = NSA_GROUP * nq
    full = lambda n: pl.BlockSpec((1, n, LANES), lambda bi, s, qi, ci: (bi, 0, 0))
    full_t = pl.BlockSpec((1, LANES, t), lambda bi, s, qi, ci: (bi, 0, 0))
    const = lambda shape: pl.BlockSpec(shape, lambda bi, s, qi, ci: (0,) * len(shape))
    tok = lambda w: pl.BlockSpec((1, nq, w), lambda bi, s, qi, ci: (bi, qi[s], 0))
    grid_spec = pltpu.PrefetchScalarGridSpec(
        num_scalar_prefetch=2,
        grid=(b, len(pairs)),
        in_specs=[tok(4 * LANES), tok(LANES), full(nc), full(nc), full_t, full_t, full_t, full_t,
                  const(at.shape), const((1, LANES)), const((LANES, NSA_HEADS))],
        out_specs=tok(4 * LANES),
        scratch_shapes=[pltpu.VMEM((NSA_KV_HEADS, t, 2 * LANES), BF16),
                        pltpu.VMEM((NSA_KV_HEADS, wpad + t, LANES), BF16),
                        pltpu.VMEM((NSA_KV_HEADS, LANES, t), BF16),
                        pltpu.VMEM((NSA_KV_HEADS, LANES, wpad + t), BF16),
                        pltpu.VMEM((NSA_KV_HEADS, nc, LANES), BF16),
                        pltpu.VMEM((NSA_KV_HEADS, LANES, nc), BF16),
                        pltpu.VMEM((NSA_KV_HEADS, 2 * LANES, cols), BF16),
                        pltpu.VMEM((NSA_KV_HEADS, LANES, cols), F32),
                        pltpu.VMEM((NSA_KV_HEADS, LANES, cols), F32),
                        pltpu.VMEM((NSA_KV_HEADS, 8, cols), F32),
                        pltpu.VMEM((NSA_KV_HEADS, LANES, cols), F32),
                        pltpu.SMEM((t // kchunk,), jnp.int32)],
    )
    return pl.pallas_call(
        functools.partial(_nsa_prompt_t_kernel, nq=nq, ns=ns, n_top=min(SEL_TOPK, ns), kchunk=kchunk, wpad=wpad),
        grid_spec=grid_spec,
        out_shape=jax.ShapeDtypeStruct((b, t, 4 * LANES), F32),
        compiler_params=_params(("arbitrary", "arbitrary")),
        name="nsa_prompt",
    )(qi, ci, q, sm, k_cmp, v_cmp, ks, vs, kw, vw, at, bg, gh.T)


def _chunk_features(nkeys):
    key = np.arange(nkeys)
    c = np.zeros((LANES, nkeys), np.float32)
    c[key // SEL_BLOCK, key] = 1.0
    c[NSA_DH, :] = key // SEL_BLOCK
    c[NSA_DH + 1, :] = key % SEL_BLOCK
    c[NSA_DH + 2, :] = 1.0
    return jnp.asarray(c, BF16)


def _nsa_sample2_kernel(pt_ref, q_ref, sm_ref, kc_ref, vc_ref, ks_pool, vs_pool, ksn_ref, vsn_ref,
                        wk_ref, wv_ref, kwn_ref, vwn_ref, a_ref, cf_ref, bg_ref, gh_ref, o_ref,
                        kbuf, vbuf, sem, q_s, bias_s, selnew_s, oc_s, ow_s, m_s, l_s, acc_s,
                        *, ch, nchunks, offset, ns, n_top, kchunk):
    b = pl.program_id(0)
    j = pl.program_id(1)
    nq = q_ref.shape[1]
    rows = NSA_GROUP * nq
    w_pre = wk_ref.shape[2]
    nkeys = ch * PAGE
    n = b * nchunks + j
    cur = n % 2
    lane = lax.broadcasted_iota(jnp.int32, (1, LANES), 1)

    def step_copies(bb, jj, buf_i, fn):
        for s in range(ch):
            page = pt_ref[bb, jj * ch + s]
            dst = pl.ds(s * PAGE, PAGE)
            fn(pltpu.make_async_copy(ks_pool.at[page], kbuf.at[buf_i, :, dst], sem.at[0, buf_i]))
            fn(pltpu.make_async_copy(vs_pool.at[page], vbuf.at[buf_i, :, dst], sem.at[1, buf_i]))

    @pl.when(n == 0)
    def _():
        step_copies(b, j, cur, lambda c: c.start())

    @pl.when(n + 1 < pl.num_programs(0) * nchunks)
    def _():
        wrap = j + 1 == nchunks
        step_copies(jnp.where(wrap, b + 1, b), jnp.where(wrap, 0, j + 1), 1 - cur, lambda c: c.start())

    @pl.when(j == 0)
    def _():
        qhs, imps = [], []
        for hk in range(NSA_KV_HEADS):
            rs = slice(hk * rows, (hk + 1) * rows)
            qh, _ = _q_rows(q_ref, hk, nq)
            slope, tq = _row_consts(hk, nq, offset)
            o_c, imp = _cmp_branch(qh, slope, tq, kc_ref, vc_ref, a_ref, nq)
            q_s[rs, :] = qh
            oc_s[rs, :] = o_c
            qhs.append(qh)
            imps.append(imp)
        _, tq = _row_consts(0, nq, offset)
        sel_all = _select(jnp.concatenate(imps, axis=0), jnp.concatenate([tq[0:nq]] * NSA_KV_HEADS, axis=0),
                          ns, n_top)
        for hk in range(NSA_KV_HEADS):
            rs = slice(hk * rows, (hk + 1) * rows)
            qh = qhs[hk]
            slope, tq = _row_consts(hk, nq, offset)
            sel_rows = jnp.concatenate([sel_all[hk * nq:(hk + 1) * nq]] * NSA_GROUP, axis=0)
            for jj in range(nchunks):
                nb_c = nkeys // SEL_BLOCK
                blocks = sel_rows[:, jj * nb_c:jj * nb_c + LANES]
                upper = jnp.where(lane < nb_c, (blocks - 1.0) * 1e30,
                                  jnp.where(lane == NSA_DH, slope * SEL_BLOCK,
                                            jnp.where(lane == NSA_DH + 1, slope,
                                                      jnp.where(lane == NSA_DH + 2,
                                                                slope * float(jj * nkeys - offset), 0.0))))
                bias_s[jj, rs, :] = upper.astype(BF16)
            new_blk = offset // SEL_BLOCK
            selnew_s[rs, :] = jnp.broadcast_to(sel_rows[:, new_blk:new_blk + 1], (rows, LANES))
            pos1 = offset - w_pre + lax.broadcasted_iota(jnp.int32, (1, w_pre), 1)
            d1 = tq - pos1
            pos2 = offset + lax.broadcasted_iota(jnp.int32, (1, kwn_ref.shape[1]), 1)
            d2 = tq - pos2
            v1 = (d1 >= 0) & (d1 < WINDOW)
            v2 = (d2 >= 0) & (d2 < WINDOW)
            s1 = jnp.where(v1, _mm(qh, wk_ref[0]) - slope * d1.astype(F32), NEG_INF)
            s2 = jnp.where(v2, _mm_nt(qh, kwn_ref[0]) - slope * d2.astype(F32), NEG_INF)
            mx = jnp.maximum(jnp.max(s1, axis=-1, keepdims=True), jnp.max(s2, axis=-1, keepdims=True))
            p1 = jnp.where(v1, jnp.exp(s1 - mx), 0.0)
            p2 = jnp.where(v2, jnp.exp(s2 - mx), 0.0)
            l_w = jnp.sum(p1, axis=-1, keepdims=True) + jnp.sum(p2, axis=-1, keepdims=True)
            ow_s[rs, :] = (_mm_nt(p1, wv_ref[0]) + _mm(p2, vwn_ref[0])) / jnp.maximum(l_w, 1e-30)
        m_s[...] = jnp.full(m_s.shape, NEG_INF, F32)
        l_s[...] = jnp.zeros(l_s.shape, F32)
        acc_s[...] = jnp.zeros(acc_s.shape, F32)

    step_copies(b, j, cur, lambda c: c.wait())

    qa = jnp.concatenate([q_s[...], bias_s[j]], axis=1)
    m, l, acc = m_s[:, 0:1], l_s[:, 0:1], acc_s[...]
    for c in range(nkeys // kchunk):
        cs = pl.ds(c * kchunk, kchunk)
        ka = jnp.concatenate([kbuf[cur, :, cs].astype(BF16), cf_ref[:, cs]], axis=0)
        s = jnp.dot(qa, ka, preferred_element_type=F32)
        m_new = jnp.maximum(m, jnp.max(s, axis=-1, keepdims=True))
        alpha = jnp.exp(m - m_new)
        p = jnp.exp(s - m_new)
        l = alpha * l + jnp.sum(p, axis=-1, keepdims=True)
        acc = alpha * acc + _mm_nt(p, vbuf[cur, :, cs])
        m = m_new
    m_s[...] = jnp.broadcast_to(m, m_s.shape)
    l_s[...] = jnp.broadcast_to(l, l_s.shape)
    acc_s[...] = acc

    @pl.when(j == nchunks - 1)
    def _():
        gates = _sigmoid(sm_ref[0] + bg_ref[...])
        out_pairs = [None] * (NSA_HEADS // 2)
        nnew = ksn_ref.shape[1]
        r_new = lax.broadcasted_iota(jnp.int32, (1, nnew), 1)
        for hk in range(NSA_KV_HEADS):
            rs = slice(hk * rows, (hk + 1) * rows)
            own = (lane >= hk * NSA_DH) & (lane < (hk + 1) * NSA_DH)
            slope, tq = _row_consts(hk, nq, 0)
            valid = (r_new <= tq) & (selnew_s[rs, 0:1] > 0.5)
            s = jnp.where(valid, _mm_nt(q_s[rs, :], ksn_ref[0]) + slope * r_new.astype(F32), NEG_INF)
            m_o = m_s[rs, 0:1]
            m_f = jnp.maximum(m_o, jnp.max(s, axis=-1, keepdims=True))
            alpha = jnp.exp(m_o - m_f)
            p = jnp.where(valid, jnp.exp(s - m_f), 0.0)
            l_f = alpha * l_s[rs, 0:1] + jnp.sum(p, axis=-1, keepdims=True)
            o_s = (alpha * acc_s[rs, :] + _mm(p, vsn_ref[0])) / jnp.maximum(l_f, 1e-30)
            _combine_heads(oc_s[rs, :], o_s, ow_s[rs, :], gates, gh_ref, own, hk, nq, out_pairs)
        for k, y in enumerate(out_pairs):
            o_ref[0, :, k * LANES:(k + 1) * LANES] = y


def _nsa_sample2(page_table, q, sm, k_cmp, v_cmp, ks_pool, vs_pool, ks_new, vs_new, win_k, win_v, kw_new, vw_new,
                 bg, gh, nc_valid, ch):
    b, nq, _ = q.shape
    npg = page_table.shape[1]
    nchunks = npg // ch
    offset = npg * PAGE
    ncp = k_cmp.shape[1]
    ns = nc_valid // 4
    nsp = -(-(ns + LANES) // LANES) * LANES
    assert ch * PAGE // SEL_BLOCK <= NSA_DH, "a chunk's blocks must fit the 64 mask lanes"
    a = _sel_matrix(ncp, nc_valid, nsp).astype(BF16)
    rows2 = NSA_KV_HEADS * NSA_GROUP * nq
    kchunk = min(512, ch * PAGE)
    cf = _chunk_features(ch * PAGE)
    per_b = lambda n, w=LANES: pl.BlockSpec((1, n, w), lambda bi, j, t: (bi, 0, 0))
    const = lambda shape: pl.BlockSpec(shape, lambda bi, j, t: (0,) * len(shape))
    grid_spec = pltpu.PrefetchScalarGridSpec(
        num_scalar_prefetch=1,
        grid=(b, nchunks),
        in_specs=[per_b(nq, 4 * LANES), per_b(nq), per_b(ncp), per_b(ncp),
                  pl.BlockSpec(memory_space=pl.ANY), pl.BlockSpec(memory_space=pl.ANY),
                  per_b(PAGE), per_b(PAGE), per_b(LANES, win_k.shape[2]), per_b(LANES, win_v.shape[2]),
                  per_b(PAGE), per_b(PAGE),
                  const(a.shape), const(cf.shape), const((1, LANES)), const((NSA_HEADS, LANES))],
        out_specs=per_b(nq, 4 * LANES),
        scratch_shapes=[pltpu.VMEM((2, LANES, ch * PAGE), F32), pltpu.VMEM((2, LANES, ch * PAGE), F32),
                        pltpu.SemaphoreType.DMA((2, 2)),
                        pltpu.VMEM((rows2, LANES), BF16),
                        pltpu.VMEM((nchunks, rows2, LANES), BF16)]
        + [pltpu.VMEM((rows2, LANES), F32)] * 6,
    )
    return pl.pallas_call(
        functools.partial(_nsa_sample2_kernel, ch=ch, nchunks=nchunks, offset=offset, ns=ns,
                          n_top=min(SEL_TOPK, ns), kchunk=kchunk),
        grid_spec=grid_spec,
        out_shape=jax.ShapeDtypeStruct((b, nq, 4 * LANES), F32),
        compiler_params=_params(("arbitrary", "arbitrary")),
        name="nsa_sample",
    )(page_table, q, sm, k_cmp, v_cmp, ks_pool, vs_pool, ks_new, vs_new, win_k, win_v, kw_new, vw_new, a, cf, bg, gh)


def _mlstm_kernel(xm_ref, vm_ref, om_ref, sm_ref, smt_ref, conv0_ref, cw_ref, cb_ref, wq_ref, wk_ref,
                  bcol_ref, brow_ref, gh_ref, c0_ref, n0_ref, m0_ref,
                  hm_ref, c_ref, n_ref, m_ref, xs_ref, *, L, t_valid):
    c = pl.program_id(1)
    dh = MLSTM_DH

    @pl.when(c == 0)
    def _():
        xs_ref[0:8, :] = jnp.zeros((8, xs_ref.shape[1]), F32)
        xs_ref[5:8, :] = conv0_ref[0]
        c_ref[...] = c0_ref[...]
        n_ref[...] = n0_ref[...]
        m_ref[...] = m0_ref[...]

    xs_ref[8:8 + L, :] = xm_ref[0]
    xc = cb_ref[...]
    for jj in range(4):
        xc = xc + cw_ref[jj:jj + 1, :] * xs_ref[pl.ds(5 + jj, L), :]
    xc = xc * _sigmoid(xc)
    xs_ref[0:8, :] = xs_ref[L:L + 8, :]

    pre_col = sm_ref[0] + brow_ref[...]
    pre_row = smt_ref[0, 0] + bcol_ref[...]
    lf_col = _log_sigmoid(pre_col)
    lf_row = _log_sigmoid(pre_row)
    if t_valid < L:
        rid = lax.broadcasted_iota(jnp.int32, (L, 1), 0) < t_valid
        cid = lax.broadcasted_iota(jnp.int32, (1, L), 1) < t_valid
        lf_col = jnp.where(rid, lf_col, 0.0)
        lf_row = jnp.where(cid, lf_row, 0.0)
        pre_col = jnp.where(rid, pre_col, NEG_INF)
        pre_row = jnp.where(cid, pre_row, NEG_INF)
    ri = lax.broadcasted_iota(jnp.int32, (L, L), 0)
    ci = lax.broadcasted_iota(jnp.int32, (L, L), 1)
    causal = ci <= ri
    tril = causal.astype(BF16)
    triu = (ri <= ci).astype(BF16)
    bcum_col = sum(jnp.dot(tril, part, preferred_element_type=F32) for part in _bf16_parts(lf_col))
    bcum_row = sum(jnp.dot(part, triu, preferred_element_type=F32) for part in _bf16_parts(lf_row))

    for h in range(MLSTM_HEADS):
        hs = slice(h * dh, (h + 1) * dh)
        xh = xc[:, hs]
        q = _mm(xh, wq_ref[h])
        k = _mm(xh, wk_ref[h]) * (dh ** -0.5)
        v = vm_ref[0, :, hs]
        bc = bcum_col[:, 28 + h:29 + h]
        ic = pre_col[:, 24 + h:25 + h]
        br = bcum_row[4 + h:5 + h, :]
        ir = pre_row[h:h + 1, :]
        mh = m_ref[0, :, h:h + 1]
        ch_ = c_ref[0, h]
        nh = n_ref[0, h:h + 1, :]
        dmat = jnp.where(causal, bc - br + ir, NEG_INF)
        inter = bc + mh
        mq = jnp.maximum(inter, jnp.max(dmat, axis=1, keepdims=True))
        a = jnp.exp(dmat - mq) * _mm_nt(q, k)
        wi = jnp.exp(inter - mq)
        num = _mm(a, v) + wi * _mm_nt(q, ch_)
        den = jnp.sum(a, axis=1, keepdims=True) + wi * jnp.sum(q * nh, axis=1, keepdims=True)
        hout = num / jnp.maximum(jnp.abs(den), jnp.exp(-mq))
        btot = bc[L - 1:L, :]
        dec_r = btot - br + ir
        dec_c = btot - bc + ic
        m_new = jnp.maximum(btot + mh, jnp.max(dec_r, axis=1, keepdims=True))
        ws_c = jnp.exp(dec_c - m_new)
        w_c = jnp.exp(btot + mh - m_new)
        c_ref[0, h] = w_c * ch_ + _mm_tn(v * ws_c, k)
        n_ref[0, h:h + 1, :] = w_c * nh + jnp.sum(k * ws_c, axis=0, keepdims=True)
        m_ref[0, :, h:h + 1] = m_new
        y = _rms(hout, gh_ref[h:h + 1, :]) * _sigmoid(om_ref[0, :, hs])
        hm_ref[0, :, hs] = y


def _mlstm(xm, vm, om, sm, conv0, conv_w, conv_b, w_qm, w_km, b_i, b_f, g_head, c0, n0, m0, L, t_valid):
    b, t, w = xm.shape
    nck = t // L
    smt = sm[:, :, 24:32].reshape(b, nck, L, 8).transpose(0, 1, 3, 2)
    brow = jnp.zeros((1, LANES), F32).at[0, 24:28].set(b_i).at[0, 28:32].set(b_f)
    bcol = jnp.concatenate([b_i, b_f]).reshape(8, 1)
    m0p = jnp.zeros((b, 1, LANES), F32).at[:, 0, :MLSTM_HEADS].set(m0)
    tok = lambda: pl.BlockSpec((1, L, w), lambda bi, c: (bi, c, 0))
    const = lambda shape: pl.BlockSpec(shape, lambda bi, c: (0,) * len(shape))
    state = lambda shape: pl.BlockSpec((1,) + shape, lambda bi, c: (bi,) + (0,) * len(shape))
    hm, c_out, n_out, m_out = pl.pallas_call(
        functools.partial(_mlstm_kernel, L=L, t_valid=t_valid),
        grid=(b, nck),
        in_specs=[tok(), tok(), tok(),
                  pl.BlockSpec((1, L, LANES), lambda bi, c: (bi, c, 0)),
                  pl.BlockSpec((1, 1, 8, L), lambda bi, c: (bi, c, 0, 0)),
                  state(conv0.shape[1:]),
                  const(conv_w.shape), const((1, w)), const(w_qm.shape), const(w_km.shape),
                  const((8, 1)), const((1, LANES)), const(g_head.shape),
                  state(c0.shape[1:]), state(n0.shape[1:]), state((1, LANES))],
        out_specs=[tok(), state(c0.shape[1:]), state(n0.shape[1:]), state((1, LANES))],
        out_shape=[jax.ShapeDtypeStruct((b, t, w), F32), jax.ShapeDtypeStruct(c0.shape, F32),
                   jax.ShapeDtypeStruct(n0.shape, F32), jax.ShapeDtypeStruct((b, 1, LANES), F32)],
        scratch_shapes=[pltpu.VMEM((L + 8, w), F32)],
        compiler_params=_params(("arbitrary", "arbitrary")),
        name="mlstm",
    )(xm, vm, om, sm, smt, conv0, conv_w, conv_b.reshape(1, w), w_qm.astype(BF16), w_km.astype(BF16),
      bcol, brow, g_head, c0, n0, m0p)
    return hm, c_out, n_out, m_out[:, 0, :MLSTM_HEADS]


def _mix_kernel(x_ref, on_ref, hm_ref, wo1_ref, wo2_ref, gxa_ref, wxq_ref, x1_ref, qx_ref):
    x1 = x_ref[...] + _mm(on_ref[...], wo1_ref[...]) + _mm(hm_ref[...], wo2_ref[...])
    x1_ref[...] = x1
    qx_ref[...] = _mm(_rms(x1, gxa_ref[...]), wxq_ref[...]) * (XA_DH ** -0.5)


def _mix(x, o_nsa, hm, w_out, g_xa, w_xq, tm):
    m, d = x.shape
    half = o_nsa.shape[1]
    row = lambda w: pl.BlockSpec((tm, w), lambda i: (i, 0))
    const = lambda shape: pl.BlockSpec(shape, lambda i: (0, 0))
    return pl.pallas_call(
        _mix_kernel,
        grid=(m // tm,),
        in_specs=[row(d), row(half), row(half), const((half, d)), const((half, d)), const((1, d)), const((d, d))],
        out_specs=[row(d), row(d)],
        out_shape=[jax.ShapeDtypeStruct((m, d), F32)] * 2,
        compiler_params=_params(("parallel",)),
        name="mix",
    )(x, o_nsa, hm, w_out[:half].astype(BF16), w_out[half:].astype(BF16), g_xa.reshape(1, d), w_xq.astype(BF16))


def _xattn_kernel(qx_ref, mk_ref, mv_ref, o_ref):
    for h in range(XA_HEADS):
        hs = slice(h * XA_DH, (h + 1) * XA_DH)
        s = _mm_nt(qx_ref[0, :, hs], mk_ref[0, :, hs])
        p = jnp.exp(s - jnp.max(s, axis=-1, keepdims=True))
        o_ref[0, :, hs] = _mm(p, mv_ref[0, :, hs]) / jnp.sum(p, axis=-1, keepdims=True)


def _xattn(qx, mem_k, mem_v, tq):
    b, t, d = qx.shape
    nm = mem_k.shape[1]
    mem_spec = pl.BlockSpec((1, nm, d), lambda bi, i: (bi, 0, 0))
    return pl.pallas_call(
        _xattn_kernel,
        grid=(b, t // tq),
        in_specs=[pl.BlockSpec((1, tq, d), lambda bi, i: (bi, i, 0)), mem_spec, mem_spec],
        out_specs=pl.BlockSpec((1, tq, d), lambda bi, i: (bi, i, 0)),
        out_shape=jax.ShapeDtypeStruct((b, t, d), F32),
        compiler_params=_params(("parallel", "parallel")),
        name="xattn",
    )(qx, mem_k, mem_v)


def _ffn_kernel(x1_ref, ox_ref, wxo_ref, gf_ref, wg_ref, wu_ref, wd_ref, gfin_ref, y_ref, x2_s, h_s, acc_s):
    j = pl.program_id(1)

    @pl.when(j == 0)
    def _():
        x2 = x1_ref[...] + _mm(ox_ref[...], wxo_ref[...])
        x2_s[...] = x2
        h_s[...] = _rms(x2, gf_ref[...]).astype(BF16)
        acc_s[...] = jnp.zeros(acc_s.shape, F32)

    h = h_s[...]
    g = jnp.dot(h, wg_ref[...], preferred_element_type=F32)
    u = jnp.dot(h, wu_ref[...], preferred_element_type=F32)
    acc_s[...] += _mm(g * _sigmoid(g) * u, wd_ref[...])

    @pl.when(j == pl.num_programs(1) - 1)
    def _():
        y_ref[...] = _rms(x2_s[...] + acc_s[...], gfin_ref[...])


def _ffn(x1, ox, w_xo, g_ffn, w_gate, w_up, w_down, g_final, tm, tf):
    m, d = x1.shape
    dff = w_gate.shape[1]
    row = pl.BlockSpec((tm, d), lambda i, j: (i, 0))
    vec = pl.BlockSpec((1, d), lambda i, j: (0, 0))
    return pl.pallas_call(
        _ffn_kernel,
        grid=(m // tm, dff // tf),
        in_specs=[row, row, pl.BlockSpec((d, d), lambda i, j: (0, 0)), vec,
                  pl.BlockSpec((d, tf), lambda i, j: (0, j)), pl.BlockSpec((d, tf), lambda i, j: (0, j)),
                  pl.BlockSpec((tf, d), lambda i, j: (j, 0)), vec],
        out_specs=row,
        out_shape=jax.ShapeDtypeStruct((m, d), F32),
        scratch_shapes=[pltpu.VMEM((tm, d), F32), pltpu.VMEM((tm, d), BF16), pltpu.VMEM((tm, d), F32)],
        compiler_params=_params(("parallel", "arbitrary")),
        name="ffn",
    )(x1, ox, w_xo.astype(BF16), g_ffn.reshape(1, d), w_gate.astype(BF16), w_up.astype(BF16),
      w_down.astype(BF16), g_final.reshape(1, d))


def _split_w_in(w_in, nsa_w, kv_w, mlstm_w):
    cuts = np.cumsum([nsa_w] + [kv_w] * 6 + [3 * NSA_HEADS] + [mlstm_w] * 3 + [MLSTM_HEADS] * 2)
    parts = jnp.split(w_in, cuts[:-1].tolist(), axis=1)
    small = jnp.concatenate([parts[7], parts[11], parts[12]], axis=1)
    small = jnp.pad(small, ((0, 0), (0, LANES - small.shape[1])))
    ws = [parts[0]] + list(parts[1:7]) + [small] + list(parts[8:11])
    return [w.astype(BF16) for w in ws]


def _tail(x1, ox, w, tm, b, t):
    d = x1.shape[1]
    dff = w["w_gate"].shape[1]
    tf = dff // 2 if (dff // 2) % LANES == 0 else dff
    y = _ffn(x1, ox.reshape(-1, d), w["w_xo"], w["g_ffn"], w["w_gate"], w["w_up"], w["w_down"], w["g_final"], tm, tf)
    return y.reshape(b, t, d)


def _gate_consts(w):
    bg = jnp.pad(w["b_gate"], (0, LANES - w["b_gate"].shape[0])).reshape(1, LANES)
    gh = jnp.concatenate([w["g_head_nsa"], w["g_head_nsa"]], axis=1)
    return bg, gh


def _prompt_group(x, mem, w):
    b, t, d = x.shape
    m = b * t
    tm = 512
    wp = w["w_in_parts"]
    wt = jnp.concatenate(wp[1:7], axis=1).T
    q, sm, xm, vm, om, kc, vc, ks, vs, kw, vw = _norm_proj(x.reshape(m, d), w["g_mix"], [wp[0]] + wp[7:], tm,
                                                           wt=wt, t=t)
    r3 = lambda a: a.reshape(b, t, a.shape[-1])
    npg = t // PAGE
    table = jnp.zeros((b, npg), jnp.int32)
    zeros_next = jnp.zeros((b, PAGE, LANES), F32)
    k_cmp = _compress(table, kc, zeros_next, *w["cmp_k"], ch=npg, paged=False, feature_major=True)
    v_cmp = _compress(table, vc, zeros_next, *w["cmp_v"], ch=npg, paged=False, feature_major=True)
    bg, gh = _gate_consts(w)
    o_nsa = _nsa_prompt_t(r3(q), r3(sm), k_cmp, v_cmp, ks, vs, kw, vw, bg, gh, LANES)
    L = next(c for c in (256, 128, 64) if t % c == 0)
    hm, c_out, n_out, m_out = _mlstm(
        r3(xm), r3(vm), r3(om), r3(sm), jnp.zeros((b, 3, xm.shape[1]), F32), w["conv_w"], w["conv_b"],
        w["w_qm"], w["w_km"], w["b_i"], w["b_f"], w["g_head_m"],
        jnp.zeros((b, MLSTM_HEADS, MLSTM_DH, MLSTM_DH), F32), jnp.zeros((b, MLSTM_HEADS, MLSTM_DH), F32),
        jnp.zeros((b, MLSTM_HEADS), F32), L, L)
    nm = mem.shape[1]
    mk, mv = _norm_proj(mem.reshape(b * nm, d), w["g_mem"], [w["w_xk"].astype(BF16), w["w_xv"].astype(BF16)],
                        min(512, b * nm))
    x1, qx = _mix(x.reshape(m, d), o_nsa.reshape(m, -1), hm.reshape(m, -1), w["w_out"], w["g_xa"], w["w_xq"], tm)
    ox = _xattn(qx.reshape(b, t, d), mk.reshape(b, nm, d), mv.reshape(b, nm, d), 512)
    y = _tail(x1, ox, w, tm, b, t)
    kv5 = lambda a: a.reshape(b, NSA_KV_HEADS, NSA_DH, a.shape[2]).transpose(0, 3, 1, 2)[None]
    keep = min(WINDOW, t)
    xm3 = r3(xm)
    states = (kv5(kc), kv5(vc), kv5(ks), kv5(vs), kv5(kw[:, :, t - keep:]), kv5(vw[:, :, t - keep:]),
              c_out[None], n_out[None], m_out[None], xm3[None, :, t - 3:],
              mk.reshape(1, b, nm, XA_HEADS, XA_DH), mv.reshape(1, b, nm, XA_HEADS, XA_DH))
    return y, states


def _sample_group(x, pools, page_table, win_k, win_v, conv0, c0, n0, m0, mem_k, mem_v, w):
    b, t, d = x.shape
    m = b * t
    tp = 8
    tm = min(m, 512)
    q, kc, vc, ks, vs, kw, vw, sm, xm, vm, om = _norm_proj(x.reshape(m, d), w["g_mix"], w["w_in_parts"], tm)
    r3 = lambda a: a.reshape(b, t, a.shape[-1])
    pad_t = lambda a: jnp.pad(r3(a), ((0, 0), (0, tp - t), (0, 0)))
    npg = page_table.shape[1]
    past = npg * PAGE
    ch = min(32, npg)
    fm = lambda a: a.transpose(0, 2, 3, 1).reshape(a.shape[0], LANES, a.shape[1])
    pool_kc, pool_vc, pool_ks, pool_vs = [fm(p) for p in pools]
    win_kt, win_vt = fm(win_k), fm(win_v)
    nc_valid = (past + (-(-t // SEL_BLOCK)) * SEL_BLOCK) // CMP_STRIDE
    ncp = -(-nc_valid // LANES) * LANES

    pad_page = lambda a: jnp.pad(r3(a), ((0, 0), (0, PAGE - t), (0, 0)))

    def compressed(pool, new_rows, cw):
        nxt = pad_page(new_rows)
        main = _compress(page_table, pool, nxt, *cw, ch=ch, paged=True, feature_major=True)
        tail = _compress(jnp.arange(b, dtype=jnp.int32).reshape(1, b), nxt, jnp.zeros((1, PAGE, LANES), F32),
                         *cw, ch=b, paged=True, feature_major=False)
        n_tail = nc_valid - npg * CMP_PER_PAGE
        full = jnp.concatenate([main, tail.reshape(b, CMP_PER_PAGE, LANES)[:, :n_tail]], axis=1)
        return jnp.pad(full, ((0, 0), (0, ncp - nc_valid), (0, 0)))

    k_cmp = compressed(pool_kc, kc, w["cmp_k"])
    v_cmp = compressed(pool_vc, vc, w["cmp_v"])
    bg, gh = _gate_consts(w)
    o_nsa = _nsa_sample2(page_table, pad_t(q), pad_t(sm), k_cmp, v_cmp, pool_ks, pool_vs, pad_page(ks), pad_page(vs),
                        win_kt, win_vt, pad_page(kw), pad_page(vw),
                        bg, gh, nc_valid, ch)[:, :t]
    hm, c_out, n_out, m_out = _mlstm(pad_t(xm), pad_t(vm), pad_t(om), pad_t(sm), conv0, w["conv_w"], w["conv_b"],
                                     w["w_qm"], w["w_km"], w["b_i"], w["b_f"], w["g_head_m"], c0, n0, m0, tp, t)
    hm = hm[:, :t]
    x1, qx = _mix(x.reshape(m, d), o_nsa.reshape(m, -1), hm.reshape(m, -1), w["w_out"], w["g_xa"], w["w_xq"], tm)
    nm = mem_k.shape[1]
    ox = _xattn(pad_t(qx), mem_k.reshape(b, nm, d), mem_v.reshape(b, nm, d), tp)[:, :t]
    y = _tail(x1, ox, w, tm, b, t)
    kv5 = lambda a: a.reshape(1, b, t, NSA_KV_HEADS, NSA_DH)
    keep = min(WINDOW, past + t)
    unfm = lambda a: a.reshape(b, NSA_KV_HEADS, NSA_DH, a.shape[2]).transpose(0, 3, 1, 2)[None]
    win5 = lambda old_t, new: unfm(jnp.concatenate([old_t, r3(new).transpose(0, 2, 1)], axis=2)[:, :, -keep:])
    conv_all = jnp.concatenate([conv0, r3(xm)], axis=1)
    states = (kv5(kc), kv5(vc), kv5(ks), kv5(vs), win5(win_kt, kw), win5(win_vt, vw),
              c_out[None], n_out[None], m_out[None], conv_all[None, :, -3:])
    return y, states


def kernel(x_prompt, x_sample, cache_k_cmp, cache_v_cmp, cache_k_slc, cache_v_slc, state_k_win, state_v_win, state_conv, state_C, state_n, state_m, cache_mem_k, cache_mem_v, page_table, mem_prompt, g_mix, w_in, b_gate, cmp_pe_k, cmp_w1_k, cmp_b1_k, cmp_w2_k, cmp_pe_v, cmp_w1_v, cmp_b1_v, cmp_w2_v, g_head_nsa, conv_w, conv_b, w_qm, w_km, b_i, b_f, g_head_m, w_out, g_xa, g_mem, w_xq, w_xk, w_xv, w_xo, g_ffn, w_gate, w_up, w_down, g_final):
    assert w_in.shape[0] == 1, "single-layer decoder"
    l = 0
    w = dict(g_mix=g_mix[l], b_gate=b_gate[l],
             w_in_parts=_split_w_in(w_in[l], NSA_HEADS * NSA_DH, NSA_KV_HEADS * NSA_DH, MLSTM_HEADS * MLSTM_DH),
             cmp_k=(cmp_pe_k[l], cmp_w1_k[l], cmp_b1_k[l], cmp_w2_k[l]),
             cmp_v=(cmp_pe_v[l], cmp_w1_v[l], cmp_b1_v[l], cmp_w2_v[l]),
             g_head_nsa=g_head_nsa[l], conv_w=conv_w[l], conv_b=conv_b[l], w_qm=w_qm[l], w_km=w_km[l],
             b_i=b_i[l], b_f=b_f[l], g_head_m=g_head_m[l], w_out=w_out[l], g_xa=g_xa[l], g_mem=g_mem[l],
             w_xq=w_xq[l], w_xk=w_xk[l], w_xv=w_xv[l], w_xo=w_xo[l], g_ffn=g_ffn[l], w_gate=w_gate[l],
             w_up=w_up[l], w_down=w_down[l], g_final=g_final)
    y_p, st_p = _prompt_group(x_prompt, mem_prompt, w)
    pools = (cache_k_cmp[l], cache_v_cmp[l], cache_k_slc[l], cache_v_slc[l])
    y_s, st_s = _sample_group(x_sample, pools, page_table, state_k_win[l], state_v_win[l], state_conv[l],
                              state_C[l], state_n[l], state_m[l], cache_mem_k[l], cache_mem_v[l], w)
    return (y_p, y_s) + st_p + st_s
```

```python
import functools

import numpy as np
import jax
import jax.numpy as jnp
from jax import lax
from jax.experimental import pallas as pl
from jax.experimental.pallas import tpu as pltpu

F32 = jnp.float32
BF16 = jnp.bfloat16

EPS = 1e-6
NEG_INF = -1e30
FORCE_SCORE = 1e9
PAD_SCORE = -2e38
TAKEN_SCORE = -3e38

LANES = 128
NSA_HEADS = 8
NSA_KV_HEADS = 2
NSA_GROUP = 4
NSA_DH = 64
CMP_STRIDE = 16
SEL_BLOCK = 64
SEL_TOPK = 16
WINDOW = 512
Q_BLOCK = 64
PAGE = 128
CMP_PER_PAGE = PAGE // CMP_STRIDE
MLSTM_HEADS = 4
MLSTM_DH = 128
XA_HEADS = 4
XA_DH = 256
ALIBI = tuple(2.0 ** (-(h + 1)) for h in range(NSA_HEADS))

VMEM_LIMIT = 56 * 1024 * 1024


def _params(sem):
    return pltpu.CompilerParams(dimension_semantics=sem, vmem_limit_bytes=VMEM_LIMIT)


def _mm(a, b):
    return jnp.dot(a.astype(BF16), b.astype(BF16), preferred_element_type=F32)


def _mm_nt(a, b):
    return lax.dot_general(a.astype(BF16), b.astype(BF16), (((1,), (1,)), ((), ())),
                           preferred_element_type=F32)


def _mm_tn(a, b):
    return lax.dot_general(a.astype(BF16), b.astype(BF16), (((0,), (0,)), ((), ())),
                           preferred_element_type=F32)


def _bf16_parts(x):
    hi = x.astype(BF16)
    r1 = x - hi.astype(F32)
    mid = r1.astype(BF16)
    lo = (r1 - mid.astype(F32)).astype(BF16)
    return hi, mid, lo


def _rms(x, g):
    return x * lax.rsqrt(jnp.mean(x * x, axis=-1, keepdims=True) + EPS) * g


def _sigmoid(x):
    return 1.0 / (1.0 + jnp.exp(-x))


def _log_sigmoid(x):
    return jnp.minimum(x, 0.0) - jnp.log(1.0 + jnp.exp(-jnp.abs(x)))


def _gelu_tanh(x):
    return 0.5 * x * (1.0 + jnp.tanh(0.7978845608028654 * (x + 0.044715 * x * x * x)))


def _topk_mask(imp, k):
    col = lax.broadcasted_iota(jnp.int32, imp.shape, 1)
    sel = jnp.zeros(imp.shape, F32)
    work = imp
    for _ in range(k):
        m = jnp.max(work, axis=-1, keepdims=True)
        idx = jnp.min(jnp.where(work == m, col, jnp.int32(2 ** 30)), axis=-1, keepdims=True)
        hit = col == idx
        sel = jnp.where(hit, 1.0, sel)
        work = jnp.where(hit, TAKEN_SCORE, work)
    return sel


def _softmax_parts(s, valid):
    s = jnp.where(valid, s, NEG_INF)
    mx = jnp.max(s, axis=-1, keepdims=True)
    p = jnp.where(valid, jnp.exp(s - mx), 0.0)
    return p, mx, jnp.sum(p, axis=-1, keepdims=True)


def _q_rows_f32(q_ref, hk, nq):
    lane = lax.broadcasted_iota(jnp.int32, (1, LANES), 1)
    own = (lane >= hk * NSA_DH) & (lane < (hk + 1) * NSA_DH)
    parts = []
    for g in range(NSA_GROUP):
        h = hk * NSA_GROUP + g
        blk = q_ref[0, :, (h // 2) * LANES:(h // 2 + 1) * LANES]
        if (h % 2) != hk:
            blk = pltpu.roll(blk, NSA_DH, 1)
        parts.append(blk)
    qh = jnp.concatenate(parts, axis=0)
    return jnp.where(own, qh * (NSA_DH ** -0.5), 0.0), own


def _q_rows(q_ref, hk, nq):
    qh, own = _q_rows_f32(q_ref, hk, nq)
    return qh.astype(BF16), own


def _feature_lanes(hk):
    base = (1 - hk) * NSA_DH
    return base, base + 1, base + 2


def _select_t(imp_t, tq_lane, ns, n_top):
    blk = lax.broadcasted_iota(jnp.int32, (imp_t.shape[0], 1), 0)
    cur = tq_lane >> 6
    forced = (blk == 0) | (blk == cur) | (blk == cur - 1)
    work = jnp.where(forced, FORCE_SCORE, imp_t)
    work = jnp.where(blk * SEL_BLOCK <= tq_lane, work, NEG_INF)
    work = jnp.where(blk < ns, work, PAD_SCORE)
    sel = jnp.zeros(imp_t.shape, F32)
    for _ in range(n_top):
        m = jnp.max(work, axis=0, keepdims=True)
        idx = jnp.min(jnp.where(work == m, blk, jnp.int32(2 ** 30)), axis=0, keepdims=True)
        hit = blk == idx
        sel = jnp.where(hit, 1.0, sel)
        work = jnp.where(hit, TAKEN_SCORE, work)
    return sel


def _row_consts(hk, nq, t0):
    rows = NSA_GROUP * nq
    row = lax.broadcasted_iota(jnp.int32, (rows, 1), 0)
    slope = jnp.full((rows, 1), ALIBI[hk * NSA_GROUP + NSA_GROUP - 1], F32)
    for g in range(NSA_GROUP - 2, -1, -1):
        slope = jnp.where(row < (g + 1) * nq, ALIBI[hk * NSA_GROUP + g], slope)
    tq = t0 + (row & (nq - 1))
    return slope, tq


def _cmp_branch(qh, slope, tq, kc_ref, vc_ref, a_ref, nq):
    ncp = kc_ref.shape[1]
    s = _mm_nt(qh, kc_ref[0])
    cend = lax.broadcasted_iota(jnp.int32, (1, ncp), 1) * CMP_STRIDE + (2 * CMP_STRIDE - 1)
    d = tq - cend
    valid = d >= 0
    p, _, l = _softmax_parts(s - slope * d.astype(F32), valid)
    p = p / jnp.maximum(l, 1e-30)
    o_c = _mm(p, vc_ref[0])
    psum = p[0:nq]
    for g in range(1, NSA_GROUP):
        psum = psum + p[g * nq:(g + 1) * nq]
    return o_c, sum(jnp.dot(part, a_ref[...], preferred_element_type=F32) for part in _bf16_parts(psum))


def _select(imp, tq_q, ns, n_top):
    nsp = imp.shape[1]
    blk = lax.broadcasted_iota(jnp.int32, (1, nsp), 1)
    cur = tq_q >> 6
    forced = (blk == 0) | (blk == cur) | (blk == cur - 1)
    imp = jnp.where(forced, FORCE_SCORE, imp)
    imp = jnp.where(blk * SEL_BLOCK <= tq_q, imp, NEG_INF)
    imp = jnp.where(blk < ns, imp, PAD_SCORE)
    return _topk_mask(imp, n_top)


def _combine_heads(o_c, o_s, o_w, gates, gh_ref, own, hk, nq, out_pairs):
    for g in range(NSA_GROUP):
        h = hk * NSA_GROUP + g
        r = slice(g * nq, (g + 1) * nq)
        og = (gates[:, 3 * h:3 * h + 1] * o_c[r] + gates[:, 3 * h + 1:3 * h + 2] * o_s[r]
              + gates[:, 3 * h + 2:3 * h + 3] * o_w[r])
        og = jnp.where(own, og, 0.0)
        ms = jnp.sum(og * og, axis=-1, keepdims=True) * (1.0 / NSA_DH)
        y = og * lax.rsqrt(ms + EPS) * gh_ref[h:h + 1, :]
        if (h % 2) != hk:
            y = pltpu.roll(y, NSA_DH, 1)
        out_pairs[h // 2] = y if out_pairs[h // 2] is None else out_pairs[h // 2] + y


def _norm_proj_kernel(x_ref, g_ref, *refs, n, n_t):
    h = _rms(x_ref[...], g_ref[...]).astype(BF16)
    has_t = 1 if n_t else 0
    outs = refs[n + has_t:]
    for w_ref, o_ref in zip(refs[:n], outs[:n]):
        o_ref[...] = jnp.dot(h, w_ref[...], preferred_element_type=F32)
    if n_t:
        yt = lax.dot_general(refs[n][...], h, (((1,), (1,)), ((), ())), preferred_element_type=F32)
        for k, o_ref in enumerate(outs[n:]):
            o_ref[0] = yt[k * LANES:(k + 1) * LANES, :]


def _norm_proj(x, g, ws, tm, wt=None, t=None):
    m, d = x.shape
    n = len(ws)
    n_t = 0 if wt is None else wt.shape[0] // LANES
    in_specs = [pl.BlockSpec((tm, d), lambda i: (i, 0)), pl.BlockSpec((1, d), lambda i: (0, 0))]
    in_specs += [pl.BlockSpec(w.shape, lambda i: (0, 0)) for w in ws]
    out_specs = [pl.BlockSpec((tm, w.shape[1]), lambda i: (i, 0)) for w in ws]
    out_shape = [jax.ShapeDtypeStruct((m, w.shape[1]), F32) for w in ws]
    args = [x, g.reshape(1, d), *ws]
    if n_t:
        per_b = t // tm
        in_specs.append(pl.BlockSpec(wt.shape, lambda i: (0, 0)))
        out_specs += [pl.BlockSpec((1, LANES, tm), lambda i: (i // per_b, 0, i % per_b))] * n_t
        out_shape += [jax.ShapeDtypeStruct((m // t, LANES, t), F32)] * n_t
        args.append(wt)
    return pl.pallas_call(
        functools.partial(_norm_proj_kernel, n=n, n_t=n_t),
        grid=(m // tm,),
        in_specs=in_specs,
        out_specs=out_specs,
        out_shape=out_shape,
        compiler_params=_params(("parallel",)),
        name="norm_proj",
    )(*args)


def _compress_kernel(tbl_ref, pool_ref, last_ref, wr_ref, pe_ref, b1_ref, w2_ref, o_ref,
                     buf, xrow, acc, cst, sem, *, ch, nchunks, nbatch, paged, feature_major):
    b = pl.program_id(0)
    j = pl.program_id(1)
    nblk = ch * CMP_PER_PAGE
    half = wr_ref.shape[2] // 2
    npairs = wr_ref.shape[0]

    n = b * nchunks + j
    n_steps = nbatch * nchunks
    cur = n % 2

    def page_copy(bb, k, buf_i, slot):
        if paged:
            src = pool_ref.at[tbl_ref[bb, k]]
        else:
            src = pool_ref.at[bb, :, pl.ds(pl.multiple_of(k * PAGE, PAGE), PAGE)]
        return pltpu.make_async_copy(src, buf.at[buf_i, slot], sem.at[buf_i, slot])

    def step_copies(m, fn):
        bb, jj, buf_i = m // nchunks, m % nchunks, m % 2
        for s in range(ch):
            fn(page_copy(bb, jj * ch + s, buf_i, s))

        @pl.when(jj < nchunks - 1)
        def _():
            fn(page_copy(bb, jnp.minimum((jj + 1) * ch, nchunks * ch - 1), buf_i, ch))

    def to_rows(m, i):
        bb, jj = jnp.minimum(m // nchunks, nbatch - 1), m % nchunks
        for s in range(ch + 1):
            page = buf[i, s]
            rows = page.T if feature_major else page
            if s == ch:
                rows = jnp.where(jj < nchunks - 1, rows, last_ref[bb])
            xrow[i, pl.ds(s * PAGE, PAGE), :] = rows

    @pl.when(n == 0)
    def _():
        if n_steps == 1:
            buf[1] = jnp.zeros(buf.shape[1:], F32)
        buf[0, ch] = jnp.zeros((PAGE, LANES), F32)
        buf[1, ch] = jnp.zeros((PAGE, LANES), F32)
        step_copies(0, lambda c: c.start())
        if n_steps > 1:
            step_copies(1, lambda c: c.start())
        c = jnp.zeros((8, half), F32)
        for pr in range(npairs):
            c = c + _mm(jnp.broadcast_to(pe_ref[0, pr:pr + 1, :], (8, 2 * LANES)), wr_ref[pr, :, 0:half])
            c = c + _mm(jnp.broadcast_to(pe_ref[1, pr:pr + 1, :], (8, 2 * LANES)), wr_ref[pr, :, half:2 * half])
        cst[...] = c + b1_ref[...]
        step_copies(0, lambda c: c.wait())
        to_rows(0, 0)

    @pl.when(n + 2 < n_steps)
    def _():
        step_copies(n + 2, lambda c: c.start())

    @pl.when(n + 1 < n_steps)
    def _():
        step_copies(n + 1, lambda c: c.wait())

    def compute(par):
        to_rows(n + 1, 1 - par)
        total = None
        for pr in range(npairs):
            x0 = xrow[par, pl.ds(2 * pr, nblk + 8, stride=CMP_STRIDE), :]
            x1 = xrow[par, pl.ds(2 * pr + 1, nblk + 8, stride=CMP_STRIDE), :]
            prod = _mm(jnp.concatenate([x0, x1], axis=1), wr_ref[pr])
            total = prod if total is None else total + prod
        acc[...] = total
        hid = acc[0:nblk, 0:half] + acc[pl.ds(1, nblk), half:2 * half] + cst[0:1, :]
        o_ref[0] = _mm(_gelu_tanh(hid), w2_ref[...])

    for par in range(2):
        pl.when(cur == par)(functools.partial(compute, par))


def _compress(table, pool, last_next, pe, w1, b1, w2, ch, paged, feature_major):
    nb, npg = table.shape
    nchunks = npg // ch
    hid = w1.shape[1]
    dh = NSA_DH
    wa = w1[:CMP_STRIDE * dh].reshape(CMP_STRIDE, dh, hid)
    wb = w1[CMP_STRIDE * dh:].reshape(CMP_STRIDE, dh, hid)
    z = jnp.zeros_like(wa)
    wr = jnp.concatenate([jnp.concatenate([wa, z, wb, z], axis=2),
                          jnp.concatenate([z, wa, z, wb], axis=2)], axis=1).astype(BF16)
    wr = wr.reshape(CMP_STRIDE // 2, 2 * LANES, 4 * hid)
    pe1 = jnp.concatenate([pe[:CMP_STRIDE], pe[:CMP_STRIDE]], axis=1).reshape(CMP_STRIDE // 2, 2 * LANES)
    pe2 = jnp.concatenate([pe[CMP_STRIDE:], pe[CMP_STRIDE:]], axis=1).reshape(CMP_STRIDE // 2, 2 * LANES)
    pes = jnp.stack([pe1, pe2])
    b1t = jnp.concatenate([b1, b1]).reshape(1, 2 * hid)
    zz = jnp.zeros_like(w2)
    w2bd = jnp.concatenate([jnp.concatenate([w2, zz], axis=1),
                            jnp.concatenate([zz, w2], axis=1)], axis=0).astype(BF16)
    nblk = ch * CMP_PER_PAGE
    grid_spec = pltpu.PrefetchScalarGridSpec(
        num_scalar_prefetch=1,
        grid=(nb, nchunks),
        in_specs=[pl.BlockSpec(memory_space=pl.ANY),
                  pl.BlockSpec(last_next.shape, lambda b, j, t: (0, 0, 0)),
                  pl.BlockSpec(wr.shape, lambda b, j, t: (0, 0, 0)),
                  pl.BlockSpec(pes.shape, lambda b, j, t: (0, 0, 0)),
                  pl.BlockSpec(b1t.shape, lambda b, j, t: (0, 0)),
                  pl.BlockSpec(w2bd.shape, lambda b, j, t: (0, 0))],
        out_specs=pl.BlockSpec((1, nblk, LANES), lambda b, j, t: (b, j, 0)),
        scratch_shapes=[pltpu.VMEM((2, ch + 1, PAGE, LANES), F32),
                        pltpu.VMEM((2, (ch + 1) * PAGE, LANES), F32),
                        pltpu.VMEM((nblk + 8, 4 * hid), F32),
                        pltpu.VMEM((8, 2 * hid), F32),
                        pltpu.SemaphoreType.DMA((2, ch + 1))],
    )
    return pl.pallas_call(
        functools.partial(_compress_kernel, ch=ch, nchunks=nchunks, nbatch=nb, paged=paged,
                          feature_major=feature_major),
        grid_spec=grid_spec,
        out_shape=jax.ShapeDtypeStruct((nb, npg * CMP_PER_PAGE, LANES), F32),
        compiler_params=_params(("arbitrary", "arbitrary")),
        name="compress",
    )(table, pool, last_next, wr, pes, b1t, w2bd)


def _sel_matrix(nc_rows, nc_valid, ns_cols):
    a = np.zeros((nc_rows, ns_cols), np.float32)
    for j in range(nc_valid // 4):
        for c, wgt in ((4 * j - 1, 0.5), (4 * j, 1.0), (4 * j + 1, 1.0), (4 * j + 2, 1.0), (4 * j + 3, 0.5)):
            if 0 <= c < nc_valid:
                a[c, j] += wgt
    return jnp.asarray(a)


def _head_feature_rows(hk, g, n):
    slope = ALIBI[hk * NSA_GROUP + g]
    row = lax.broadcasted_iota(jnp.int32, (NSA_DH, n), 0)
    return jnp.where(row == 0, slope * SEL_BLOCK,
                     jnp.where(row == 1, slope, jnp.where(row == 2, slope * CMP_STRIDE, 0.0)))


def _swap_halves(x):
    return jnp.concatenate([x[NSA_DH:], x[:NSA_DH]], axis=0)


def _nsa_prompt_t_kernel(qi_ref, ci_ref, q_ref, sm_ref, kc_ref, vc_ref, ks_ref, vs_ref, kw_ref, vw_ref, at_ref,
                         bg_ref, ght_ref, o_ref, ksr, kwr, vsa, vwa, kca, vct, qa_s, oc_s, ow_s, m_s, acc_s, used_ref,
                         *, nq, ns, n_top, kchunk, wpad):
    step = pl.program_id(1)
    i = qi_ref[step]
    c = ci_ref[step]
    t0 = pl.multiple_of(i * nq, nq)
    cols = NSA_GROUP * nq
    t = ks_ref.shape[2]
    nc = kc_ref.shape[1]
    nsr = at_ref.shape[0]
    wkeys = wpad + nq
    last = (t0 + nq + kchunk - 1) // kchunk - 1
    lane = lax.broadcasted_iota(jnp.int32, (1, LANES), 1)
    tq_q = t0 + lax.broadcasted_iota(jnp.int32, (1, nq), 1)
    tq = jnp.concatenate([tq_q] * NSA_GROUP, axis=1)

    @pl.when(step == 0)
    def _():
        srow = lax.broadcasted_iota(jnp.int32, (LANES, 1), 0)
        cidx = lax.broadcasted_iota(jnp.int32, (nc, 1), 0).astype(F32)
        vct_f = jnp.transpose(vc_ref[0]).astype(BF16)
        for hk in range(NSA_KV_HEADS):
            own = (lane >= hk * NSA_DH) & (lane < (hk + 1) * NSA_DH)
            f0, f1, f2 = _feature_lanes(hk)
            kca[hk] = jnp.where(own, kc_ref[0], jnp.where(lane == f2, cidx, 0.0)).astype(BF16)
            vct[hk] = vct_f
            kwr[hk, 0:wpad, :] = jnp.broadcast_to(jnp.where(lane == f0, -1e30, 0.0), (wpad, LANES)).astype(BF16)
            vwa[hk, :, 0:wpad] = jnp.zeros((LANES, wpad), BF16)

        def prep(r, _):
            r0 = pl.multiple_of(r * LANES, LANES)
            cs = pl.ds(r0, LANES)
            pos = r0 + lax.broadcasted_iota(jnp.int32, (LANES, 1), 0)
            blk = pos >> 6
            ks_rows = jnp.transpose(ks_ref[0, :, cs])
            kw_rows = jnp.transpose(kw_ref[0, :, cs])
            for hk in range(NSA_KV_HEADS):
                own = (lane >= hk * NSA_DH) & (lane < (hk + 1) * NSA_DH)
                own_r = (srow >= hk * NSA_DH) & (srow < (hk + 1) * NSA_DH)
                f0, f1, f2 = _feature_lanes(hk)
                feat = jnp.where(lane == f0, blk.astype(F32), jnp.where(lane == f1, (pos & 63).astype(F32), 0.0))
                ksr[hk, cs, 0:LANES] = jnp.where(own, ks_rows, feat).astype(BF16)
                ksr[hk, cs, LANES:2 * LANES] = (lane == blk).astype(BF16)
                kwr[hk, pl.ds(wpad + r0, LANES), :] = jnp.where(own, kw_rows, feat).astype(BF16)
                vsa[hk, :, cs] = jnp.where(own_r, vs_ref[0, :, cs], 1.0).astype(BF16)
                vwa[hk, :, pl.ds(wpad + r0, LANES)] = jnp.where(own_r, vw_ref[0, :, cs], 1.0).astype(BF16)
            return 0

        lax.fori_loop(0, t // LANES, prep, 0)

    @pl.when(c == 0)
    def _():
        qt = jnp.transpose(q_ref[0]) * (NSA_DH ** -0.5)
        cend = lax.broadcasted_iota(jnp.int32, (nc, 1), 0) * CMP_STRIDE + (2 * CMP_STRIDE - 1)
        wrow = lax.broadcasted_iota(jnp.int32, (LANES, 1), 0)
        sel_heads = []
        for hk in range(NSA_KV_HEADS):
            slabs = []
            for g in range(NSA_GROUP):
                h = hk * NSA_GROUP + g
                halves = [qt[h * NSA_DH:(h + 1) * NSA_DH], _head_feature_rows(hk, g, nq)]
                slabs.append(jnp.concatenate(halves if hk == 0 else halves[::-1], axis=0))
            qlo = jnp.concatenate(slabs, axis=1).astype(BF16)
            s = jnp.dot(kca[hk], qlo, preferred_element_type=F32)
            valid = cend <= tq
            s = jnp.where(valid, s, NEG_INF)
            p = jnp.where(valid, jnp.exp(s - jnp.max(s, axis=0, keepdims=True)), 0.0)
            p = p / jnp.maximum(jnp.sum(p, axis=0, keepdims=True), 1e-30)
            oc_s[hk] = jnp.dot(vct[hk], p.astype(BF16), preferred_element_type=F32)
            psum = p[:, 0:nq]
            for g in range(1, NSA_GROUP):
                psum = psum + p[:, g * nq:(g + 1) * nq]
            imp_t = sum(jnp.dot(at_ref[...], part, preferred_element_type=F32)
                        for part in _bf16_parts(psum))
            sel_t = _select_t(imp_t, tq_q, ns, n_top)
            sel_heads.append(sel_t)
            bias = (sel_t - 1.0) * 1e30
            bias = jnp.concatenate([bias, jnp.zeros((LANES - nsr, nq), F32)], axis=0) if nsr < LANES else bias
            qa_s[hk, 0:LANES, :] = qlo
            qa_s[hk, LANES:2 * LANES, :] = jnp.concatenate([bias] * NSA_GROUP, axis=1).astype(BF16)
            s = jnp.dot(kwr[hk, pl.ds(t0, wkeys), :], qlo, preferred_element_type=F32)
            old = jnp.where(t0 - wpad + wrow > tq - WINDOW, s[0:LANES], NEG_INF)
            new = jnp.where(t0 + wrow[0:nq] <= tq, s[wkeys - nq:wkeys], NEG_INF)
            s = jnp.concatenate([old, s[LANES:wkeys - nq], new], axis=0)
            p = jnp.exp(s - jnp.max(s, axis=0, keepdims=True)).astype(BF16)
            ow = jnp.dot(vwa[hk, :, pl.ds(t0, wkeys)], p, preferred_element_type=F32)
            l_w = _swap_halves(ow)
            ow_s[hk] = ow / jnp.maximum(l_w, 1e-30)
            m_s[hk] = jnp.full((8, cols), NEG_INF, F32)
            acc_s[hk] = jnp.zeros((LANES, cols), F32)
        any_sel = sel_heads[0]
        for s_t in sel_heads[1:]:
            any_sel = jnp.maximum(any_sel, s_t)
        per_chunk = kchunk // SEL_BLOCK
        for cc in range(t // kchunk):
            used_ref[cc] = (jnp.max(any_sel[cc * per_chunk:(cc + 1) * per_chunk, :]) > 0.5).astype(jnp.int32)

    k0 = pl.multiple_of(c * kchunk, kchunk)

    def flash(causal):
        for hk in range(NSA_KV_HEADS):
            s = jnp.dot(ksr[hk, pl.ds(k0, kchunk), :], qa_s[hk], preferred_element_type=F32)
            if causal:
                s = jnp.where(k0 + lax.broadcasted_iota(jnp.int32, (kchunk, 1), 0) <= tq, s, NEG_INF)
            m_old = m_s[hk][0:1, :]
            m_new = jnp.maximum(m_old, jnp.max(s, axis=0, keepdims=True))
            p = jnp.exp(s - m_new).astype(BF16)
            acc_s[hk] = jnp.exp(m_old - m_new) * acc_s[hk] + jnp.dot(vsa[hk, :, pl.ds(k0, kchunk)], p,
                                                                   preferred_element_type=F32)
            m_s[hk] = jnp.broadcast_to(m_new, (8, cols))

    @pl.when((c < last) & (used_ref[c] > 0))
    def _():
        flash(False)

    @pl.when(c == last)
    def _():
        flash(True)
        gates = jnp.transpose(_sigmoid(sm_ref[0] + bg_ref[...]))
        srow = lax.broadcasted_iota(jnp.int32, (LANES, 1), 0)
        halves = [None] * NSA_HEADS
        for hk in range(NSA_KV_HEADS):
            own_r = (srow >= hk * NSA_DH) & (srow < (hk + 1) * NSA_DH)
            acc = acc_s[hk]
            o_s = acc / jnp.maximum(_swap_halves(acc), 1e-30)
            o_c = oc_s[hk]
            o_w = ow_s[hk]
            for g in range(NSA_GROUP):
                h = hk * NSA_GROUP + g
                cs = slice(g * nq, (g + 1) * nq)
                og = (gates[3 * h:3 * h + 1] * o_c[:, cs] + gates[3 * h + 1:3 * h + 2] * o_s[:, cs]
                      + gates[3 * h + 2:3 * h + 3] * o_w[:, cs])
                og = jnp.where(own_r, og, 0.0)
                ms = jnp.sum(og * og, axis=0, keepdims=True) * (1.0 / NSA_DH)
                y = og * lax.rsqrt(ms + EPS) * ght_ref[:, h:h + 1]
                halves[h] = y[hk * NSA_DH:(hk + 1) * NSA_DH]
        for k in range(NSA_HEADS // 2):
            o_ref[0, :, k * LANES:(k + 1) * LANES] = jnp.transpose(
                jnp.concatenate([halves[2 * k], halves[2 * k + 1]], axis=0))


def _nsa_prompt_t(q, sm, k_cmp, v_cmp, ks, vs, kw, vw, bg, gh, nq):
    b, t, _ = q.shape
    nc = k_cmp.shape[1]
    ns = t // SEL_BLOCK
    nsr = -(-ns // 8) * 8
    assert nq == LANES and nsr <= LANES and nc <= 256 and t % 512 == 0, "prompt NSA kernel shape limits"
    at = _sel_matrix(nc, nc, nsr).T.astype(BF16)
    kchunk = min(512, t)
    wpad = WINDOW
    pairs = [(i, c) for i in range(t // nq) for c in range(-(-((i + 1) * nq) // kchunk))]
    qi = jnp.asarray(np.array([p[0] for p in pairs], np.int32))
    ci = jnp.asarray(np.array([p[1] for p in pairs], np.int32))
    cols = NSA_GROUP * nq
    full = lambda n: pl.BlockSpec((1, n, LANES), lambda bi, s, qi, ci: (bi, 0, 0))
    full_t = pl.BlockSpec((1, LANES, t), lambda bi, s, qi, ci: (bi, 0, 0))
    const = lambda shape: pl.BlockSpec(shape, lambda bi, s, qi, ci: (0,) * len(shape))
    tok = lambda w: pl.BlockSpec((1, nq, w), lambda bi, s, qi, ci: (bi, qi[s], 0))
    grid_spec = pltpu.PrefetchScalarGridSpec(
        num_scalar_prefetch=2,
        grid=(b, len(pairs)),
        in_specs=[tok(4 * LANES), tok(LANES), full(nc), full(nc), full_t, full_t, full_t, full_t,
                  const(at.shape), const((1, LANES)), const((LANES, NSA_HEADS))],
        out_specs=tok(4 * LANES),
        scratch_shapes=[pltpu.VMEM((NSA_KV_HEADS, t, 2 * LANES), BF16),
                        pltpu.VMEM((NSA_KV_HEADS, wpad + t, LANES), BF16),
                        pltpu.VMEM((NSA_KV_HEADS, LANES, t), BF16),
                        pltpu.VMEM((NSA_KV_HEADS, LANES, wpad + t), BF16),
                        pltpu.VMEM((NSA_KV_HEADS, nc, LANES), BF16),
                        pltpu.VMEM((NSA_KV_HEADS, LANES, nc), BF16),
                        pltpu.VMEM((NSA_KV_HEADS, 2 * LANES, cols), BF16),
                        pltpu.VMEM((NSA_KV_HEADS, LANES, cols), F32),
                        pltpu.VMEM((NSA_KV_HEADS, LANES, cols), F32),
                        pltpu.VMEM((NSA_KV_HEADS, 8, cols), F32),
                        pltpu.VMEM((NSA_KV_HEADS, LANES, cols), F32),
                        pltpu.SMEM((t // kchunk,), jnp.int32)],
    )
    return pl.pallas_call(
        functools.partial(_nsa_prompt_t_kernel, nq=nq, ns=ns, n_top=min(SEL_TOPK, ns), kchunk=kchunk, wpad=wpad),
        grid_spec=grid_spec,
        out_shape=jax.ShapeDtypeStruct((b, t, 4 * LANES), F32),
        compiler_params=_params(("arbitrary", "arbitrary")),
        name="nsa_prompt",
    )(qi, ci, q, sm, k_cmp, v_cmp, ks, vs, kw, vw, at, bg, gh.T)


def _chunk_features(nkeys):
    key = np.arange(nkeys)
    c = np.zeros((LANES, nkeys), np.float32)
    c[key // SEL_BLOCK, key] = 1.0
    c[NSA_DH, :] = key // SEL_BLOCK
    c[NSA_DH + 1, :] = key % SEL_BLOCK
    c[NSA_DH + 2, :] = 1.0
    return jnp.asarray(c, BF16)


def _nsa_sample2_kernel(pt_ref, q_ref, sm_ref, kc_ref, vc_ref, ks_pool, vs_pool, ksn_ref, vsn_ref,
                        wk_ref, wv_ref, kwn_ref, vwn_ref, a_ref, cf_ref, bg_ref, gh_ref, o_ref,
                        kbuf, vbuf, sem, q_s, bias_s, selnew_s, oc_s, ow_s, m_s, l_s, acc_s,
                        *, ch, nchunks, offset, ns, n_top, kchunk):
    b = pl.program_id(0)
    j = pl.program_id(1)
    nq = q_ref.shape[1]
    rows = NSA_GROUP * nq
    w_pre = wk_ref.shape[2]
    nkeys = ch * PAGE
    n = b * nchunks + j
    cur = n % 2
    lane = lax.broadcasted_iota(jnp.int32, (1, LANES), 1)

    def step_copies(bb, jj, buf_i, fn):
        for s in range(ch):
            page = pt_ref[bb, jj * ch + s]
            dst = pl.ds(s * PAGE, PAGE)
            fn(pltpu.make_async_copy(ks_pool.at[page], kbuf.at[buf_i, :, dst], sem.at[0, buf_i]))
            fn(pltpu.make_async_copy(vs_pool.at[page], vbuf.at[buf_i, :, dst], sem.at[1, buf_i]))

    @pl.when(n == 0)
    def _():
        step_copies(b, j, cur, lambda c: c.start())

    @pl.when(n + 1 < pl.num_programs(0) * nchunks)
    def _():
        wrap = j + 1 == nchunks
        step_copies(jnp.where(wrap, b + 1, b), jnp.where(wrap, 0, j + 1), 1 - cur, lambda c: c.start())

    @pl.when(j == 0)
    def _():
        qhs, imps = [], []
        for hk in range(NSA_KV_HEADS):
            rs = slice(hk * rows, (hk + 1) * rows)
            qh, _ = _q_rows(q_ref, hk, nq)
            slope, tq = _row_consts(hk, nq, offset)
            o_c, imp = _cmp_branch(qh, slope, tq, kc_ref, vc_ref, a_ref, nq)
            q_s[rs, :] = qh
            oc_s[rs, :] = o_c
            qhs.append(qh)
            imps.append(imp)
        _, tq = _row_consts(0, nq, offset)
        sel_all = _select(jnp.concatenate(imps, axis=0), jnp.concatenate([tq[0:nq]] * NSA_KV_HEADS, axis=0),
                          ns, n_top)
        for hk in range(NSA_KV_HEADS):
            rs = slice(hk * rows, (hk + 1) * rows)
            qh = qhs[hk]
            slope, tq = _row_consts(hk, nq, offset)
            sel_rows = jnp.concatenate([sel_all[hk * nq:(hk + 1) * nq]] * NSA_GROUP, axis=0)
            for jj in range(nchunks):
                nb_c = nkeys // SEL_BLOCK
                blocks = sel_rows[:, jj * nb_c:jj * nb_c + LANES]
                upper = jnp.where(lane < nb_c, (blocks - 1.0) * 1e30,
                                  jnp.where(lane == NSA_DH, slope * SEL_BLOCK,
                                            jnp.where(lane == NSA_DH + 1, slope,
                                                      jnp.where(lane == NSA_DH + 2,
                                                                slope * float(jj * nkeys - offset), 0.0))))
                bias_s[jj, rs, :] = upper.astype(BF16)
            new_blk = offset // SEL_BLOCK
            selnew_s[rs, :] = jnp.broadcast_to(sel_rows[:, new_blk:new_blk + 1], (rows, LANES))
            pos1 = offset - w_pre + lax.broadcasted_iota(jnp.int32, (1, w_pre), 1)
            d1 = tq - pos1
            pos2 = offset + lax.broadcasted_iota(jnp.int32, (1, kwn_ref.shape[1]), 1)
            d2 = tq - pos2
            v1 = (d1 >= 0) & (d1 < WINDOW)
            v2 = (d2 >= 0) & (d2 < WINDOW)
            s1 = jnp.where(v1, _mm(qh, wk_ref[0]) - slope * d1.astype(F32), NEG_INF)
            s2 = jnp.where(v2, _mm_nt(qh, kwn_ref[0]) - slope * d2.astype(F32), NEG_INF)
            mx = jnp.maximum(jnp.max(s1, axis=-1, keepdims=True), jnp.max(s2, axis=-1, keepdims=True))
            p1 = jnp.where(v1, jnp.exp(s1 - mx), 0.0)
            p2 = jnp.where(v2, jnp.exp(s2 - mx), 0.0)
            l_w = jnp.sum(p1, axis=-1, keepdims=True) + jnp.sum(p2, axis=-1, keepdims=True)
            ow_s[rs, :] = (_mm_nt(p1, wv_ref[0]) + _mm(p2, vwn_ref[0])) / jnp.maximum(l_w, 1e-30)
        m_s[...] = jnp.full(m_s.shape, NEG_INF, F32)
        l_s[...] = jnp.zeros(l_s.shape, F32)
        acc_s[...] = jnp.zeros(acc_s.shape, F32)

    step_copies(b, j, cur, lambda c: c.wait())

    qa = jnp.concatenate([q_s[...], bias_s[j]], axis=1)
    tiles = []
    for c in range(nkeys // kchunk):
        cs = pl.ds(c * kchunk, kchunk)
        ka = jnp.concatenate([kbuf[cur, :, cs].astype(BF16), cf_ref[:, cs]], axis=0)
        tiles.append(jnp.dot(qa, ka, preferred_element_type=F32))
    s = jnp.concatenate(tiles, axis=1)
    m = m_s[:, 0:1]
    m_new = jnp.maximum(m, jnp.max(s, axis=-1, keepdims=True))
    alpha = jnp.exp(m - m_new)
    p = jnp.exp(s - m_new)
    m_s[...] = jnp.broadcast_to(m_new, m_s.shape)
    l_s[...] = jnp.broadcast_to(alpha * l_s[:, 0:1] + jnp.sum(p, axis=-1, keepdims=True), l_s.shape)
    acc_s[...] = alpha * acc_s[...] + _mm_nt(p, vbuf[cur])

    @pl.when(j == nchunks - 1)
    def _():
        gates = _sigmoid(sm_ref[0] + bg_ref[...])
        out_pairs = [None] * (NSA_HEADS // 2)
        nnew = ksn_ref.shape[1]
        r_new = lax.broadcasted_iota(jnp.int32, (1, nnew), 1)
        for hk in range(NSA_KV_HEADS):
            rs = slice(hk * rows, (hk + 1) * rows)
            own = (lane >= hk * NSA_DH) & (lane < (hk + 1) * NSA_DH)
            slope, tq = _row_consts(hk, nq, 0)
            valid = (r_new <= tq) & (selnew_s[rs, 0:1] > 0.5)
            s = jnp.where(valid, _mm_nt(q_s[rs, :], ksn_ref[0]) + slope * r_new.astype(F32), NEG_INF)
            m_o = m_s[rs, 0:1]
            m_f = jnp.maximum(m_o, jnp.max(s, axis=-1, keepdims=True))
            alpha = jnp.exp(m_o - m_f)
            p = jnp.where(valid, jnp.exp(s - m_f), 0.0)
            l_f = alpha * l_s[rs, 0:1] + jnp.sum(p, axis=-1, keepdims=True)
            o_s = (alpha * acc_s[rs, :] + _mm(p, vsn_ref[0])) / jnp.maximum(l_f, 1e-30)
            _combine_heads(oc_s[rs, :], o_s, ow_s[rs, :], gates, gh_ref, own, hk, nq, out_pairs)
        for k, y in enumerate(out_pairs):
            o_ref[0, :, k * LANES:(k + 1) * LANES] = y


def _nsa_sample2(page_table, q, sm, k_cmp, v_cmp, ks_pool, vs_pool, ks_new, vs_new, win_k, win_v, kw_new, vw_new,
                 bg, gh, nc_valid, ch):
    b, nq, _ = q.shape
    npg = page_table.shape[1]
    nchunks = npg // ch
    offset = npg * PAGE
    ncp = k_cmp.shape[1]
    ns = nc_valid // 4
    nsp = -(-(ns + LANES) // LANES) * LANES
    assert ch * PAGE // SEL_BLOCK <= NSA_DH, "a chunk's blocks must fit the 64 mask lanes"
    a = _sel_matrix(ncp, nc_valid, nsp).astype(BF16)
    rows2 = NSA_KV_HEADS * NSA_GROUP * nq
    kchunk = min(512, ch * PAGE)
    cf = _chunk_features(ch * PAGE)
    per_b = lambda n, w=LANES: pl.BlockSpec((1, n, w), lambda bi, j, t: (bi, 0, 0))
    const = lambda shape: pl.BlockSpec(shape, lambda bi, j, t: (0,) * len(shape))
    grid_spec = pltpu.PrefetchScalarGridSpec(
        num_scalar_prefetch=1,
        grid=(b, nchunks),
        in_specs=[per_b(nq, 4 * LANES), per_b(nq), per_b(ncp), per_b(ncp),
                  pl.BlockSpec(memory_space=pl.ANY), pl.BlockSpec(memory_space=pl.ANY),
                  per_b(PAGE), per_b(PAGE), per_b(LANES, win_k.shape[2]), per_b(LANES, win_v.shape[2]),
                  per_b(PAGE), per_b(PAGE),
                  const(a.shape), const(cf.shape), const((1, LANES)), const((NSA_HEADS, LANES))],
        out_specs=per_b(nq, 4 * LANES),
        scratch_shapes=[pltpu.VMEM((2, LANES, ch * PAGE), F32), pltpu.VMEM((2, LANES, ch * PAGE), F32),
                        pltpu.SemaphoreType.DMA((2, 2)),
                        pltpu.VMEM((rows2, LANES), BF16),
                        pltpu.VMEM((nchunks, rows2, LANES), BF16)]
        + [pltpu.VMEM((rows2, LANES), F32)] * 6,
    )
    return pl.pallas_call(
        functools.partial(_nsa_sample2_kernel, ch=ch, nchunks=nchunks, offset=offset, ns=ns,
                          n_top=min(SEL_TOPK, ns), kchunk=kchunk),
        grid_spec=grid_spec,
        out_shape=jax.ShapeDtypeStruct((b, nq, 4 * LANES), F32),
        compiler_params=_params(("arbitrary", "arbitrary")),
        name="nsa_sample",
    )(page_table, q, sm, k_cmp, v_cmp, ks_pool, vs_pool, ks_new, vs_new, win_k, win_v, kw_new, vw_new, a, cf, bg, gh)


def _mlstm_kernel(xm_ref, vm_ref, om_ref, sm_ref, smt_ref, conv0_ref, cw_ref, cb_ref, wq_ref, wk_ref,
                  bcol_ref, brow_ref, gh_ref, c0_ref, n0_ref, m0_ref,
                  hm_ref, c_ref, n_ref, m_ref, xs_ref, *, L, t_valid):
    c = pl.program_id(1)
    dh = MLSTM_DH

    @pl.when(c == 0)
    def _():
        xs_ref[0:8, :] = jnp.zeros((8, xs_ref.shape[1]), F32)
        xs_ref[5:8, :] = conv0_ref[0]
        c_ref[...] = c0_ref[...]
        n_ref[...] = n0_ref[...]
        m_ref[...] = m0_ref[...]

    xs_ref[8:8 + L, :] = xm_ref[0]
    xc = cb_ref[...]
    for jj in range(4):
        xc = xc + cw_ref[jj:jj + 1, :] * xs_ref[pl.ds(5 + jj, L), :]
    xc = xc * _sigmoid(xc)
    xs_ref[0:8, :] = xs_ref[L:L + 8, :]

    pre_col = sm_ref[0] + brow_ref[...]
    pre_row = smt_ref[0, 0] + bcol_ref[...]
    lf_col = _log_sigmoid(pre_col)
    lf_row = _log_sigmoid(pre_row)
    if t_valid < L:
        rid = lax.broadcasted_iota(jnp.int32, (L, 1), 0) < t_valid
        cid = lax.broadcasted_iota(jnp.int32, (1, L), 1) < t_valid
        lf_col = jnp.where(rid, lf_col, 0.0)
        lf_row = jnp.where(cid, lf_row, 0.0)
        pre_col = jnp.where(rid, pre_col, NEG_INF)
        pre_row = jnp.where(cid, pre_row, NEG_INF)
    ri = lax.broadcasted_iota(jnp.int32, (L, L), 0)
    ci = lax.broadcasted_iota(jnp.int32, (L, L), 1)
    causal = ci <= ri
    tril = causal.astype(BF16)
    triu = (ri <= ci).astype(BF16)
    bcum_col = sum(jnp.dot(tril, part, preferred_element_type=F32) for part in _bf16_parts(lf_col))
    bcum_row = sum(jnp.dot(part, triu, preferred_element_type=F32) for part in _bf16_parts(lf_row))

    for h in range(MLSTM_HEADS):
        hs = slice(h * dh, (h + 1) * dh)
        xh = xc[:, hs]
        q = _mm(xh, wq_ref[h])
        k = _mm(xh, wk_ref[h]) * (dh ** -0.5)
        v = vm_ref[0, :, hs]
        bc = bcum_col[:, 28 + h:29 + h]
        ic = pre_col[:, 24 + h:25 + h]
        br = bcum_row[4 + h:5 + h, :]
        ir = pre_row[h:h + 1, :]
        mh = m_ref[0, :, h:h + 1]
        ch_ = c_ref[0, h]
        nh = n_ref[0, h:h + 1, :]
        dmat = jnp.where(causal, bc - br + ir, NEG_INF)
        inter = bc + mh
        mq = jnp.maximum(inter, jnp.max(dmat, axis=1, keepdims=True))
        a = jnp.exp(dmat - mq) * _mm_nt(q, k)
        wi = jnp.exp(inter - mq)
        num = _mm(a, v) + wi * _mm_nt(q, ch_)
        den = jnp.sum(a, axis=1, keepdims=True) + wi * jnp.sum(q * nh, axis=1, keepdims=True)
        hout = num / jnp.maximum(jnp.abs(den), jnp.exp(-mq))
        btot = bc[L - 1:L, :]
        dec_r = btot - br + ir
        dec_c = btot - bc + ic
        m_new = jnp.maximum(btot + mh, jnp.max(dec_r, axis=1, keepdims=True))
        ws_c = jnp.exp(dec_c - m_new)
        w_c = jnp.exp(btot + mh - m_new)
        c_ref[0, h] = w_c * ch_ + _mm_tn(v * ws_c, k)
        n_ref[0, h:h + 1, :] = w_c * nh + jnp.sum(k * ws_c, axis=0, keepdims=True)
        m_ref[0, :, h:h + 1] = m_new
        y = _rms(hout, gh_ref[h:h + 1, :]) * _sigmoid(om_ref[0, :, hs])
        hm_ref[0, :, hs] = y


def _mlstm(xm, vm, om, sm, conv0, conv_w, conv_b, w_qm, w_km, b_i, b_f, g_head, c0, n0, m0, L, t_valid):
    b, t, w = xm.shape
    nck = t // L
    smt = sm[:, :, 24:32].reshape(b, nck, L, 8).transpose(0, 1, 3, 2)
    brow = jnp.zeros((1, LANES), F32).at[0, 24:28].set(b_i).at[0, 28:32].set(b_f)
    bcol = jnp.concatenate([b_i, b_f]).reshape(8, 1)
    m0p = jnp.zeros((b, 1, LANES), F32).at[:, 0, :MLSTM_HEADS].set(m0)
    tok = lambda: pl.BlockSpec((1, L, w), lambda bi, c: (bi, c, 0))
    const = lambda shape: pl.BlockSpec(shape, lambda bi, c: (0,) * len(shape))
    state = lambda shape: pl.BlockSpec((1,) + shape, lambda bi, c: (bi,) + (0,) * len(shape))
    hm, c_out, n_out, m_out = pl.pallas_call(
        functools.partial(_mlstm_kernel, L=L, t_valid=t_valid),
        grid=(b, nck),
        in_specs=[tok(), tok(), tok(),
                  pl.BlockSpec((1, L, LANES), lambda bi, c: (bi, c, 0)),
                  pl.BlockSpec((1, 1, 8, L), lambda bi, c: (bi, c, 0, 0)),
                  state(conv0.shape[1:]),
                  const(conv_w.shape), const((1, w)), const(w_qm.shape), const(w_km.shape),
                  const((8, 1)), const((1, LANES)), const(g_head.shape),
                  state(c0.shape[1:]), state(n0.shape[1:]), state((1, LANES))],
        out_specs=[tok(), state(c0.shape[1:]), state(n0.shape[1:]), state((1, LANES))],
        out_shape=[jax.ShapeDtypeStruct((b, t, w), F32), jax.ShapeDtypeStruct(c0.shape, F32),
                   jax.ShapeDtypeStruct(n0.shape, F32), jax.ShapeDtypeStruct((b, 1, LANES), F32)],
        scratch_shapes=[pltpu.VMEM((L + 8, w), F32)],
        compiler_params=_params(("arbitrary", "arbitrary")),
        name="mlstm",
    )(xm, vm, om, sm, smt, conv0, conv_w, conv_b.reshape(1, w), w_qm.astype(BF16), w_km.astype(BF16),
      bcol, brow, g_head, c0, n0, m0p)
    return hm, c_out, n_out, m_out[:, 0, :MLSTM_HEADS]


def _mix_kernel(x_ref, on_ref, hm_ref, wo1_ref, wo2_ref, gxa_ref, wxq_ref, x1_ref, qx_ref):
    x1 = x_ref[...] + _mm(on_ref[...], wo1_ref[...]) + _mm(hm_ref[...], wo2_ref[...])
    x1_ref[...] = x1
    qx_ref[...] = _mm(_rms(x1, gxa_ref[...]), wxq_ref[...]) * (XA_DH ** -0.5)


def _mix(x, o_nsa, hm, w_out, g_xa, w_xq, tm):
    m, d = x.shape
    half = o_nsa.shape[1]
    row = lambda w: pl.BlockSpec((tm, w), lambda i: (i, 0))
    const = lambda shape: pl.BlockSpec(shape, lambda i: (0, 0))
    return pl.pallas_call(
        _mix_kernel,
        grid=(m // tm,),
        in_specs=[row(d), row(half), row(half), const((half, d)), const((half, d)), const((1, d)), const((d, d))],
        out_specs=[row(d), row(d)],
        out_shape=[jax.ShapeDtypeStruct((m, d), F32)] * 2,
        compiler_params=_params(("parallel",)),
        name="mix",
    )(x, o_nsa, hm, w_out[:half].astype(BF16), w_out[half:].astype(BF16), g_xa.reshape(1, d), w_xq.astype(BF16))


def _xattn_kernel(qx_ref, mk_ref, mv_ref, o_ref):
    for h in range(XA_HEADS):
        hs = slice(h * XA_DH, (h + 1) * XA_DH)
        s = _mm_nt(qx_ref[0, :, hs], mk_ref[0, :, hs])
        p = jnp.exp(s - jnp.max(s, axis=-1, keepdims=True))
        o_ref[0, :, hs] = _mm(p, mv_ref[0, :, hs]) / jnp.sum(p, axis=-1, keepdims=True)


def _xattn(qx, mem_k, mem_v, tq):
    b, t, d = qx.shape
    nm = mem_k.shape[1]
    mem_spec = pl.BlockSpec((1, nm, d), lambda bi, i: (bi, 0, 0))
    return pl.pallas_call(
        _xattn_kernel,
        grid=(b, t // tq),
        in_specs=[pl.BlockSpec((1, tq, d), lambda bi, i: (bi, i, 0)), mem_spec, mem_spec],
        out_specs=pl.BlockSpec((1, tq, d), lambda bi, i: (bi, i, 0)),
        out_shape=jax.ShapeDtypeStruct((b, t, d), F32),
        compiler_params=_params(("parallel", "parallel")),
        name="xattn",
    )(qx, mem_k, mem_v)


def _ffn_kernel(x1_ref, ox_ref, wxo_ref, gf_ref, wg_ref, wu_ref, wd_ref, gfin_ref, y_ref, x2_s, h_s, acc_s):
    j = pl.program_id(1)

    @pl.when(j == 0)
    def _():
        x2 = x1_ref[...] + _mm(ox_ref[...], wxo_ref[...])
        x2_s[...] = x2
        h_s[...] = _rms(x2, gf_ref[...]).astype(BF16)
        acc_s[...] = jnp.zeros(acc_s.shape, F32)

    h = h_s[...]
    g = jnp.dot(h, wg_ref[...], preferred_element_type=F32)
    u = jnp.dot(h, wu_ref[...], preferred_element_type=F32)
    acc_s[...] += _mm(g * _sigmoid(g) * u, wd_ref[...])

    @pl.when(j == pl.num_programs(1) - 1)
    def _():
        y_ref[...] = _rms(x2_s[...] + acc_s[...], gfin_ref[...])


def _ffn(x1, ox, w_xo, g_ffn, w_gate, w_up, w_down, g_final, tm, tf):
    m, d = x1.shape
    dff = w_gate.shape[1]
    row = pl.BlockSpec((tm, d), lambda i, j: (i, 0))
    vec = pl.BlockSpec((1, d), lambda i, j: (0, 0))
    return pl.pallas_call(
        _ffn_kernel,
        grid=(m // tm, dff // tf),
        in_specs=[row, row, pl.BlockSpec((d, d), lambda i, j: (0, 0)), vec,
                  pl.BlockSpec((d, tf), lambda i, j: (0, j)), pl.BlockSpec((d, tf), lambda i, j: (0, j)),
                  pl.BlockSpec((tf, d), lambda i, j: (j, 0)), vec],
        out_specs=row,
        out_shape=jax.ShapeDtypeStruct((m, d), F32),
        scratch_shapes=[pltpu.VMEM((tm, d), F32), pltpu.VMEM((tm, d), BF16), pltpu.VMEM((tm, d), F32)],
        compiler_params=_params(("parallel", "arbitrary")),
        name="ffn",
    )(x1, ox, w_xo.astype(BF16), g_ffn.reshape(1, d), w_gate.astype(BF16), w_up.astype(BF16),
      w_down.astype(BF16), g_final.reshape(1, d))


def _split_w_in(w_in, nsa_w, kv_w, mlstm_w):
    cuts = np.cumsum([nsa_w] + [kv_w] * 6 + [3 * NSA_HEADS] + [mlstm_w] * 3 + [MLSTM_HEADS] * 2)
    parts = jnp.split(w_in, cuts[:-1].tolist(), axis=1)
    small = jnp.concatenate([parts[7], parts[11], parts[12]], axis=1)
    small = jnp.pad(small, ((0, 0), (0, LANES - small.shape[1])))
    ws = [parts[0]] + list(parts[1:7]) + [small] + list(parts[8:11])
    return [w.astype(BF16) for w in ws]


def _tail(x1, ox, w, tm, b, t):
    d = x1.shape[1]
    dff = w["w_gate"].shape[1]
    tf = dff // 2 if (dff // 2) % LANES == 0 else dff
    y = _ffn(x1, ox.reshape(-1, d), w["w_xo"], w["g_ffn"], w["w_gate"], w["w_up"], w["w_down"], w["g_final"], tm, tf)
    return y.reshape(b, t, d)


def _gate_consts(w):
    bg = jnp.pad(w["b_gate"], (0, LANES - w["b_gate"].shape[0])).reshape(1, LANES)
    gh = jnp.concatenate([w["g_head_nsa"], w["g_head_nsa"]], axis=1)
    return bg, gh


def _prompt_group(x, mem, w):
    b, t, d = x.shape
    m = b * t
    tm = 512
    wp = w["w_in_parts"]
    wt = jnp.concatenate(wp[1:7], axis=1).T
    q, sm, xm, vm, om, kc, vc, ks, vs, kw, vw = _norm_proj(x.reshape(m, d), w["g_mix"], [wp[0]] + wp[7:], tm,
                                                           wt=wt, t=t)
    r3 = lambda a: a.reshape(b, t, a.shape[-1])
    npg = t // PAGE
    table = jnp.zeros((b, npg), jnp.int32)
    zeros_next = jnp.zeros((b, PAGE, LANES), F32)
    k_cmp = _compress(table, kc, zeros_next, *w["cmp_k"], ch=npg, paged=False, feature_major=True)
    v_cmp = _compress(table, vc, zeros_next, *w["cmp_v"], ch=npg, paged=False, feature_major=True)
    bg, gh = _gate_consts(w)
    o_nsa = _nsa_prompt_t(r3(q), r3(sm), k_cmp, v_cmp, ks, vs, kw, vw, bg, gh, LANES)
    L = next(c for c in (256, 128, 64) if t % c == 0)
    hm, c_out, n_out, m_out = _mlstm(
        r3(xm), r3(vm), r3(om), r3(sm), jnp.zeros((b, 3, xm.shape[1]), F32), w["conv_w"], w["conv_b"],
        w["w_qm"], w["w_km"], w["b_i"], w["b_f"], w["g_head_m"],
        jnp.zeros((b, MLSTM_HEADS, MLSTM_DH, MLSTM_DH), F32), jnp.zeros((b, MLSTM_HEADS, MLSTM_DH), F32),
        jnp.zeros((b, MLSTM_HEADS), F32), L, L)
    nm = mem.shape[1]
    mk, mv = _norm_proj(mem.reshape(b * nm, d), w["g_mem"], [w["w_xk"].astype(BF16), w["w_xv"].astype(BF16)],
                        min(512, b * nm))
    x1, qx = _mix(x.reshape(m, d), o_nsa.reshape(m, -1), hm.reshape(m, -1), w["w_out"], w["g_xa"], w["w_xq"], tm)
    ox = _xattn(qx.reshape(b, t, d), mk.reshape(b, nm, d), mv.reshape(b, nm, d), 512)
    y = _tail(x1, ox, w, tm, b, t)
    kv5 = lambda a: a.reshape(b, NSA_KV_HEADS, NSA_DH, a.shape[2]).transpose(0, 3, 1, 2)[None]
    keep = min(WINDOW, t)
    xm3 = r3(xm)
    states = (kv5(kc), kv5(vc), kv5(ks), kv5(vs), kv5(kw[:, :, t - keep:]), kv5(vw[:, :, t - keep:]),
              c_out[None], n_out[None], m_out[None], xm3[None, :, t - 3:],
              mk.reshape(1, b, nm, XA_HEADS, XA_DH), mv.reshape(1, b, nm, XA_HEADS, XA_DH))
    return y, states


def _sample_group(x, pools, page_table, win_k, win_v, conv0, c0, n0, m0, mem_k, mem_v, w):
    b, t, d = x.shape
    m = b * t
    tp = 8
    tm = min(m, 512)
    q, kc, vc, ks, vs, kw, vw, sm, xm, vm, om = _norm_proj(x.reshape(m, d), w["g_mix"], w["w_in_parts"], tm)
    r3 = lambda a: a.reshape(b, t, a.shape[-1])
    pad_t = lambda a: jnp.pad(r3(a), ((0, 0), (0, tp - t), (0, 0)))
    npg = page_table.shape[1]
    past = npg * PAGE
    ch = min(32, npg)
    fm = lambda a: a.transpose(0, 2, 3, 1).reshape(a.shape[0], LANES, a.shape[1])
    pool_kc, pool_vc, pool_ks, pool_vs = [fm(p) for p in pools]
    win_kt, win_vt = fm(win_k), fm(win_v)
    nc_valid = (past + (-(-t // SEL_BLOCK)) * SEL_BLOCK) // CMP_STRIDE
    ncp = -(-nc_valid // LANES) * LANES

    pad_page = lambda a: jnp.pad(r3(a), ((0, 0), (0, PAGE - t), (0, 0)))

    def compressed(pool, new_rows, cw):
        nxt = pad_page(new_rows)
        main = _compress(page_table, pool, nxt, *cw, ch=ch, paged=True, feature_major=True)
        tail = _compress(jnp.arange(b, dtype=jnp.int32).reshape(1, b), nxt, jnp.zeros((1, PAGE, LANES), F32),
                         *cw, ch=b, paged=True, feature_major=False)
        n_tail = nc_valid - npg * CMP_PER_PAGE
        full = jnp.concatenate([main, tail.reshape(b, CMP_PER_PAGE, LANES)[:, :n_tail]], axis=1)
        return jnp.pad(full, ((0, 0), (0, ncp - nc_valid), (0, 0)))

    k_cmp = compressed(pool_kc, kc, w["cmp_k"])
    v_cmp = compressed(pool_vc, vc, w["cmp_v"])
    bg, gh = _gate_consts(w)
    o_nsa = _nsa_sample2(page_table, pad_t(q), pad_t(sm), k_cmp, v_cmp, pool_ks, pool_vs, pad_page(ks), pad_page(vs),
                        win_kt, win_vt, pad_page(kw), pad_page(vw),
                        bg, gh, nc_valid, ch)[:, :t]
    hm, c_out, n_out, m_out = _mlstm(pad_t(xm), pad_t(vm), pad_t(om), pad_t(sm), conv0, w["conv_w"], w["conv_b"],
                                     w["w_qm"], w["w_km"], w["b_i"], w["b_f"], w["g_head_m"], c0, n0, m0, tp, t)
    hm = hm[:, :t]
    x1, qx = _mix(x.reshape(m, d), o_nsa.reshape(m, -1), hm.reshape(m, -1), w["w_out"], w["g_xa"], w["w_xq"], tm)
    nm = mem_k.shape[1]
    ox = _xattn(pad_t(qx), mem_k.reshape(b, nm, d), mem_v.reshape(b, nm, d), tp)[:, :t]
    y = _tail(x1, ox, w, tm, b, t)
    kv5 = lambda a: a.reshape(1, b, t, NSA_KV_HEADS, NSA_DH)
    keep = min(WINDOW, past + t)
    unfm = lambda a: a.reshape(b, NSA_KV_HEADS, NSA_DH, a.shape[2]).transpose(0, 3, 1, 2)[None]
    win5 = lambda old_t, new: unfm(jnp.concatenate([old_t, r3(new).transpose(0, 2, 1)], axis=2)[:, :, -keep:])
    conv_all = jnp.concatenate([conv0, r3(xm)], axis=1)
    states = (kv5(kc), kv5(vc), kv5(ks), kv5(vs), win5(win_kt, kw), win5(win_vt, vw),
              c_out[None], n_out[None], m_out[None], conv_all[None, :, -3:])
    return y, states


def kernel(x_prompt, x_sample, cache_k_cmp, cache_v_cmp, cache_k_slc, cache_v_slc, state_k_win, state_v_win, state_conv, state_C, state_n, state_m, cache_mem_k, cache_mem_v, page_table, mem_prompt, g_mix, w_in, b_gate, cmp_pe_k, cmp_w1_k, cmp_b1_k, cmp_w2_k, cmp_pe_v, cmp_w1_v, cmp_b1_v, cmp_w2_v, g_head_nsa, conv_w, conv_b, w_qm, w_km, b_i, b_f, g_head_m, w_out, g_xa, g_mem, w_xq, w_xk, w_xv, w_xo, g_ffn, w_gate, w_up, w_down, g_final):
    assert w_in.shape[0] == 1, "single-layer decoder"
    l = 0
    w = dict(g_mix=g_mix[l], b_gate=b_gate[l],
             w_in_parts=_split_w_in(w_in[l], NSA_HEADS * NSA_DH, NSA_KV_HEADS * NSA_DH, MLSTM_HEADS * MLSTM_DH),
             cmp_k=(cmp_pe_k[l], cmp_w1_k[l], cmp_b1_k[l], cmp_w2_k[l]),
             cmp_v=(cmp_pe_v[l], cmp_w1_v[l], cmp_b1_v[l], cmp_w2_v[l]),
             g_head_nsa=g_head_nsa[l], conv_w=conv_w[l], conv_b=conv_b[l], w_qm=w_qm[l], w_km=w_km[l],
             b_i=b_i[l], b_f=b_f[l], g_head_m=g_head_m[l], w_out=w_out[l], g_xa=g_xa[l], g_mem=g_mem[l],
             w_xq=w_xq[l], w_xk=w_xk[l], w_xv=w_xv[l], w_xo=w_xo[l], g_ffn=g_ffn[l], w_gate=w_gate[l],
             w_up=w_up[l], w_down=w_down[l], g_final=g_final)
    y_p, st_p = _prompt_group(x_prompt, mem_prompt, w)
    pools = (cache_k_cmp[l], cache_v_cmp[l], cache_k_slc[l], cache_v_slc[l])
    y_s, st_s = _sample_group(x_sample, pools, page_table, state_k_win[l], state_v_win[l], state_conv[l],
                              state_C[l], state_n[l], state_m[l], cache_mem_k[l], cache_mem_v[l], w)
    return (y_p, y_s) + st_p + st_s
```

```python
import functools

import numpy as np
import jax
import jax.numpy as jnp
from jax import lax
from jax.experimental import pallas as pl
from jax.experimental.pallas import tpu as pltpu

F32 = jnp.float32
BF16 = jnp.bfloat16

EPS = 1e-6
NEG_INF = -1e30
FORCE_SCORE = 1e9
PAD_SCORE = -2e38
TAKEN_SCORE = -3e38

LANES = 128
NSA_HEADS = 8
NSA_KV_HEADS = 2
NSA_GROUP = 4
NSA_DH = 64
CMP_STRIDE = 16
SEL_BLOCK = 64
SEL_TOPK = 16
WINDOW = 512
Q_BLOCK = 64
PAGE = 128
CMP_PER_PAGE = PAGE // CMP_STRIDE
MLSTM_HEADS = 4
MLSTM_DH = 128
XA_HEADS = 4
XA_DH = 256
ALIBI = tuple(2.0 ** (-(h + 1)) for h in range(NSA_HEADS))

VMEM_LIMIT = 56 * 1024 * 1024


def _params(sem):
    return pltpu.CompilerParams(dimension_semantics=sem, vmem_limit_bytes=VMEM_LIMIT)


def _mm(a, b):
    return jnp.dot(a.astype(BF16), b.astype(BF16), preferred_element_type=F32)


def _mm_nt(a, b):
    return lax.dot_general(a.astype(BF16), b.astype(BF16), (((1,), (1,)), ((), ())),
                           preferred_element_type=F32)


def _mm_tn(a, b):
    return lax.dot_general(a.astype(BF16), b.astype(BF16), (((0,), (0,)), ((), ())),
                           preferred_element_type=F32)


def _bf16_parts(x):
    hi = x.astype(BF16)
    r1 = x - hi.astype(F32)
    mid = r1.astype(BF16)
    lo = (r1 - mid.astype(F32)).astype(BF16)
    return hi, mid, lo


def _rms(x, g):
    return x * lax.rsqrt(jnp.mean(x * x, axis=-1, keepdims=True) + EPS) * g


def _sigmoid(x):
    return 1.0 / (1.0 + jnp.exp(-x))


def _log_sigmoid(x):
    return jnp.minimum(x, 0.0) - jnp.log(1.0 + jnp.exp(-jnp.abs(x)))


def _gelu_tanh(x):
    return 0.5 * x * (1.0 + jnp.tanh(0.7978845608028654 * (x + 0.044715 * x * x * x)))


def _topk_mask(imp, k):
    col = lax.broadcasted_iota(jnp.int32, imp.shape, 1)
    sel = jnp.zeros(imp.shape, F32)
    work = imp
    for _ in range(k):
        m = jnp.max(work, axis=-1, keepdims=True)
        idx = jnp.min(jnp.where(work == m, col, jnp.int32(2 ** 30)), axis=-1, keepdims=True)
        hit = col == idx
        sel = jnp.where(hit, 1.0, sel)
        work = jnp.where(hit, TAKEN_SCORE, work)
    return sel


def _softmax_parts(s, valid):
    s = jnp.where(valid, s, NEG_INF)
    mx = jnp.max(s, axis=-1, keepdims=True)
    p = jnp.where(valid, jnp.exp(s - mx), 0.0)
    return p, mx, jnp.sum(p, axis=-1, keepdims=True)


def _q_rows_f32(q_ref, hk, nq):
    lane = lax.broadcasted_iota(jnp.int32, (1, LANES), 1)
    own = (lane >= hk * NSA_DH) & (lane < (hk + 1) * NSA_DH)
    parts = []
    for g in range(NSA_GROUP):
        h = hk * NSA_GROUP + g
        blk = q_ref[0, :, (h // 2) * LANES:(h // 2 + 1) * LANES]
        if (h % 2) != hk:
            blk = pltpu.roll(blk, NSA_DH, 1)
        parts.append(blk)
    qh = jnp.concatenate(parts, axis=0)
    return jnp.where(own, qh * (NSA_DH ** -0.5), 0.0), own


def _q_rows(q_ref, hk, nq):
    qh, own = _q_rows_f32(q_ref, hk, nq)
    return qh.astype(BF16), own


def _feature_lanes(hk):
    base = (1 - hk) * NSA_DH
    return base, base + 1, base + 2


def _select_t(imp_t, tq_lane, ns, n_top):
    blk = lax.broadcasted_iota(jnp.int32, (imp_t.shape[0], 1), 0)
    cur = tq_lane >> 6
    forced = (blk == 0) | (blk == cur) | (blk == cur - 1)
    work = jnp.where(forced, FORCE_SCORE, imp_t)
    work = jnp.where(blk * SEL_BLOCK <= tq_lane, work, NEG_INF)
    work = jnp.where(blk < ns, work, PAD_SCORE)
    sel = jnp.zeros(imp_t.shape, F32)
    for _ in range(n_top):
        m = jnp.max(work, axis=0, keepdims=True)
        idx = jnp.min(jnp.where(work == m, blk, jnp.int32(2 ** 30)), axis=0, keepdims=True)
        hit = blk == idx
        sel = jnp.where(hit, 1.0, sel)
        work = jnp.where(hit, TAKEN_SCORE, work)
    return sel


def _row_consts(hk, nq, t0):
    rows = NSA_GROUP * nq
    row = lax.broadcasted_iota(jnp.int32, (rows, 1), 0)
    slope = jnp.full((rows, 1), ALIBI[hk * NSA_GROUP + NSA_GROUP - 1], F32)
    for g in range(NSA_GROUP - 2, -1, -1):
        slope = jnp.where(row < (g + 1) * nq, ALIBI[hk * NSA_GROUP + g], slope)
    tq = t0 + (row & (nq - 1))
    return slope, tq


def _cmp_branch(qh, slope, tq, kc_ref, vc_ref, a_ref, nq):
    ncp = kc_ref.shape[1]
    s = _mm_nt(qh, kc_ref[0])
    cend = lax.broadcasted_iota(jnp.int32, (1, ncp), 1) * CMP_STRIDE + (2 * CMP_STRIDE - 1)
    d = tq - cend
    valid = d >= 0
    p, _, l = _softmax_parts(s - slope * d.astype(F32), valid)
    p = p / jnp.maximum(l, 1e-30)
    o_c = _mm(p, vc_ref[0])
    psum = p[0:nq]
    for g in range(1, NSA_GROUP):
        psum = psum + p[g * nq:(g + 1) * nq]
    return o_c, sum(jnp.dot(part, a_ref[...], preferred_element_type=F32) for part in _bf16_parts(psum))


def _select(imp, tq_q, ns, n_top):
    nsp = imp.shape[1]
    blk = lax.broadcasted_iota(jnp.int32, (1, nsp), 1)
    cur = tq_q >> 6
    forced = (blk == 0) | (blk == cur) | (blk == cur - 1)
    imp = jnp.where(forced, FORCE_SCORE, imp)
    imp = jnp.where(blk * SEL_BLOCK <= tq_q, imp, NEG_INF)
    imp = jnp.where(blk < ns, imp, PAD_SCORE)
    return _topk_mask(imp, n_top)


def _combine_heads(o_c, o_s, o_w, gates, gh_ref, own, hk, nq, out_pairs):
    for g in range(NSA_GROUP):
        h = hk * NSA_GROUP + g
        r = slice(g * nq, (g + 1) * nq)
        og = (gates[:, 3 * h:3 * h + 1] * o_c[r] + gates[:, 3 * h + 1:3 * h + 2] * o_s[r]
              + gates[:, 3 * h + 2:3 * h + 3] * o_w[r])
        og = jnp.where(own, og, 0.0)
        ms = jnp.sum(og * og, axis=-1, keepdims=True) * (1.0 / NSA_DH)
        y = og * lax.rsqrt(ms + EPS) * gh_ref[h:h + 1, :]
        if (h % 2) != hk:
            y = pltpu.roll(y, NSA_DH, 1)
        out_pairs[h // 2] = y if out_pairs[h // 2] is None else out_pairs[h // 2] + y


def _norm_proj_kernel(x_ref, g_ref, *refs, n, n_t):
    h = _rms(x_ref[...], g_ref[...]).astype(BF16)
    has_t = 1 if n_t else 0
    outs = refs[n + has_t:]
    for w_ref, o_ref in zip(refs[:n], outs[:n]):
        o_ref[...] = jnp.dot(h, w_ref[...], preferred_element_type=F32)
    if n_t:
        yt = lax.dot_general(refs[n][...], h, (((1,), (1,)), ((), ())), preferred_element_type=F32)
        for k, o_ref in enumerate(outs[n:]):
            o_ref[0] = yt[k * LANES:(k + 1) * LANES, :]


def _norm_proj(x, g, ws, tm, wt=None, t=None):
    m, d = x.shape
    n = len(ws)
    n_t = 0 if wt is None else wt.shape[0] // LANES
    in_specs = [pl.BlockSpec((tm, d), lambda i: (i, 0)), pl.BlockSpec((1, d), lambda i: (0, 0))]
    in_specs += [pl.BlockSpec(w.shape, lambda i: (0, 0)) for w in ws]
    out_specs = [pl.BlockSpec((tm, w.shape[1]), lambda i: (i, 0)) for w in ws]
    out_shape = [jax.ShapeDtypeStruct((m, w.shape[1]), F32) for w in ws]
    args = [x, g.reshape(1, d), *ws]
    if n_t:
        per_b = t // tm
        in_specs.append(pl.BlockSpec(wt.shape, lambda i: (0, 0)))
        out_specs += [pl.BlockSpec((1, LANES, tm), lambda i: (i // per_b, 0, i % per_b))] * n_t
        out_shape += [jax.ShapeDtypeStruct((m // t, LANES, t), F32)] * n_t
        args.append(wt)
    return pl.pallas_call(
        functools.partial(_norm_proj_kernel, n=n, n_t=n_t),
        grid=(m // tm,),
        in_specs=in_specs,
        out_specs=out_specs,
        out_shape=out_shape,
        compiler_params=_params(("parallel",)),
        name="norm_proj",
    )(*args)


def _compress_kernel(tbl_ref, pool_ref, last_ref, wr_ref, pe_ref, b1_ref, w2_ref, o_ref,
                     buf, xrow, acc, cst, sem, *, ch, nchunks, nbatch, paged, feature_major):
    b = pl.program_id(0)
    j = pl.program_id(1)
    nblk = ch * CMP_PER_PAGE
    half = wr_ref.shape[2] // 2
    npairs = wr_ref.shape[0]

    n = b * nchunks + j
    n_steps = nbatch * nchunks
    cur = n % 2

    def page_copy(bb, k, buf_i, slot):
        if paged:
            src = pool_ref.at[tbl_ref[bb, k]]
        else:
            src = pool_ref.at[bb, :, pl.ds(pl.multiple_of(k * PAGE, PAGE), PAGE)]
        return pltpu.make_async_copy(src, buf.at[buf_i, slot], sem.at[buf_i, slot])

    def step_copies(m, fn):
        bb, jj, buf_i = m // nchunks, m % nchunks, m % 2
        for s in range(ch):
            fn(page_copy(bb, jj * ch + s, buf_i, s))

        @pl.when(jj < nchunks - 1)
        def _():
            fn(page_copy(bb, jnp.minimum((jj + 1) * ch, nchunks * ch - 1), buf_i, ch))

    def to_rows(m, i):
        bb, jj = jnp.minimum(m // nchunks, nbatch - 1), m % nchunks
        for s in range(ch + 1):
            page = buf[i, s]
            rows = page.T if feature_major else page
            if s == ch:
                rows = jnp.where(jj < nchunks - 1, rows, last_ref[bb])
            xrow[i, pl.ds(s * PAGE, PAGE), :] = rows

    @pl.when(n == 0)
    def _():
        if n_steps == 1:
            buf[1] = jnp.zeros(buf.shape[1:], F32)
        buf[0, ch] = jnp.zeros((PAGE, LANES), F32)
        buf[1, ch] = jnp.zeros((PAGE, LANES), F32)
        step_copies(0, lambda c: c.start())
        if n_steps > 1:
            step_copies(1, lambda c: c.start())
        c = jnp.zeros((8, half), F32)
        for pr in range(npairs):
            c = c + _mm(jnp.broadcast_to(pe_ref[0, pr:pr + 1, :], (8, 2 * LANES)), wr_ref[pr, :, 0:half])
            c = c + _mm(jnp.broadcast_to(pe_ref[1, pr:pr + 1, :], (8, 2 * LANES)), wr_ref[pr, :, half:2 * half])
        cst[...] = c + b1_ref[...]
        step_copies(0, lambda c: c.wait())
        to_rows(0, 0)

    @pl.when(n + 2 < n_steps)
    def _():
        step_copies(n + 2, lambda c: c.start())

    @pl.when(n + 1 < n_steps)
    def _():
        step_copies(n + 1, lambda c: c.wait())

    def compute(par):
        to_rows(n + 1, 1 - par)
        total = None
        for pr in range(npairs):
            x0 = xrow[par, pl.ds(2 * pr, nblk + 8, stride=CMP_STRIDE), :]
            x1 = xrow[par, pl.ds(2 * pr + 1, nblk + 8, stride=CMP_STRIDE), :]
            prod = _mm(jnp.concatenate([x0, x1], axis=1), wr_ref[pr])
            total = prod if total is None else total + prod
        acc[...] = total
        hid = acc[0:nblk, 0:half] + acc[pl.ds(1, nblk), half:2 * half] + cst[0:1, :]
        o_ref[0] = _mm(_gelu_tanh(hid), w2_ref[...])

    for par in range(2):
        pl.when(cur == par)(functools.partial(compute, par))


def _compress(table, pool, last_next, pe, w1, b1, w2, ch, paged, feature_major):
    nb, npg = table.shape
    nchunks = npg // ch
    hid = w1.shape[1]
    dh = NSA_DH
    wa = w1[:CMP_STRIDE * dh].reshape(CMP_STRIDE, dh, hid)
    wb = w1[CMP_STRIDE * dh:].reshape(CMP_STRIDE, dh, hid)
    z = jnp.zeros_like(wa)
    wr = jnp.concatenate([jnp.concatenate([wa, z, wb, z], axis=2),
                          jnp.concatenate([z, wa, z, wb], axis=2)], axis=1).astype(BF16)
    wr = wr.reshape(CMP_STRIDE // 2, 2 * LANES, 4 * hid)
    pe1 = jnp.concatenate([pe[:CMP_STRIDE], pe[:CMP_STRIDE]], axis=1).reshape(CMP_STRIDE // 2, 2 * LANES)
    pe2 = jnp.concatenate([pe[CMP_STRIDE:], pe[CMP_STRIDE:]], axis=1).reshape(CMP_STRIDE // 2, 2 * LANES)
    pes = jnp.stack([pe1, pe2])
    b1t = jnp.concatenate([b1, b1]).reshape(1, 2 * hid)
    zz = jnp.zeros_like(w2)
    w2bd = jnp.concatenate([jnp.concatenate([w2, zz], axis=1),
                            jnp.concatenate([zz, w2], axis=1)], axis=0).astype(BF16)
    nblk = ch * CMP_PER_PAGE
    grid_spec = pltpu.PrefetchScalarGridSpec(
        num_scalar_prefetch=1,
        grid=(nb, nchunks),
        in_specs=[pl.BlockSpec(memory_space=pl.ANY),
                  pl.BlockSpec(last_next.shape, lambda b, j, t: (0, 0, 0)),
                  pl.BlockSpec(wr.shape, lambda b, j, t: (0, 0, 0)),
                  pl.BlockSpec(pes.shape, lambda b, j, t: (0, 0, 0)),
                  pl.BlockSpec(b1t.shape, lambda b, j, t: (0, 0)),
                  pl.BlockSpec(w2bd.shape, lambda b, j, t: (0, 0))],
        out_specs=pl.BlockSpec((1, nblk, LANES), lambda b, j, t: (b, j, 0)),
        scratch_shapes=[pltpu.VMEM((2, ch + 1, PAGE, LANES), F32),
                        pltpu.VMEM((2, (ch + 1) * PAGE, LANES), F32),
                        pltpu.VMEM((nblk + 8, 4 * hid), F32),
                        pltpu.VMEM((8, 2 * hid), F32),
                        pltpu.SemaphoreType.DMA((2, ch + 1))],
    )
    return pl.pallas_call(
        functools.partial(_compress_kernel, ch=ch, nchunks=nchunks, nbatch=nb, paged=paged,
                          feature_major=feature_major),
        grid_spec=grid_spec,
        out_shape=jax.ShapeDtypeStruct((nb, npg * CMP_PER_PAGE, LANES), F32),
        compiler_params=_params(("arbitrary", "arbitrary")),
        name="compress",
    )(table, pool, last_next, wr, pes, b1t, w2bd)


def _sel_matrix(nc_rows, nc_valid, ns_cols):
    a = np.zeros((nc_rows, ns_cols), np.float32)
    for j in range(nc_valid // 4):
        for c, wgt in ((4 * j - 1, 0.5), (4 * j, 1.0), (4 * j + 1, 1.0), (4 * j + 2, 1.0), (4 * j + 3, 0.5)):
            if 0 <= c < nc_valid:
                a[c, j] += wgt
    return jnp.asarray(a)


def _head_feature_rows(hk, g, n):
    slope = ALIBI[hk * NSA_GROUP + g]
    row = lax.broadcasted_iota(jnp.int32, (NSA_DH, n), 0)
    return jnp.where(row == 0, slope * SEL_BLOCK,
                     jnp.where(row == 1, slope, jnp.where(row == 2, slope * CMP_STRIDE, 0.0)))


def _swap_halves(x):
    return jnp.concatenate([x[NSA_DH:], x[:NSA_DH]], axis=0)


def _nsa_prompt_t_kernel(qi_ref, ci_ref, q_ref, sm_ref, kc_ref, vc_ref, ks_ref, vs_ref, kw_ref, vw_ref, at_ref,
                         bg_ref, ght_ref, o_ref, ksr, kwr, vsa, vwa, kca, vct, qa_s, oc_s, ow_s, m_s, acc_s, used_ref,
                         *, nq, ns, n_top, kchunk, wpad, cgroup):
    step = pl.program_id(1)
    i = qi_ref[step]
    grp = ci_ref[step]
    c = grp
    t0 = pl.multiple_of(i * nq, nq)
    cols = NSA_GROUP * nq
    t = ks_ref.shape[2]
    nc = kc_ref.shape[1]
    nsr = at_ref.shape[0]
    wkeys = wpad + nq
    last = (t0 + nq + kchunk - 1) // kchunk - 1
    lane = lax.broadcasted_iota(jnp.int32, (1, LANES), 1)
    tq_q = t0 + lax.broadcasted_iota(jnp.int32, (1, nq), 1)
    tq = jnp.concatenate([tq_q] * NSA_GROUP, axis=1)

    @pl.when(step == 0)
    def _():
        srow = lax.broadcasted_iota(jnp.int32, (LANES, 1), 0)
        cidx = lax.broadcasted_iota(jnp.int32, (nc, 1), 0).astype(F32)
        vct_f = jnp.transpose(vc_ref[0]).astype(BF16)
        for hk in range(NSA_KV_HEADS):
            own = (lane >= hk * NSA_DH) & (lane < (hk + 1) * NSA_DH)
            f0, f1, f2 = _feature_lanes(hk)
            kca[hk] = jnp.where(own, kc_ref[0], jnp.where(lane == f2, cidx, 0.0)).astype(BF16)
            vct[hk] = vct_f
            kwr[hk, 0:wpad, :] = jnp.broadcast_to(jnp.where(lane == f0, -1e30, 0.0), (wpad, LANES)).astype(BF16)
            vwa[hk, :, 0:wpad] = jnp.zeros((LANES, wpad), BF16)

        def prep(r, _):
            r0 = pl.multiple_of(r * LANES, LANES)
            cs = pl.ds(r0, LANES)
            pos = r0 + lax.broadcasted_iota(jnp.int32, (LANES, 1), 0)
            blk = pos >> 6
            ks_rows = jnp.transpose(ks_ref[0, :, cs])
            kw_rows = jnp.transpose(kw_ref[0, :, cs])
            for hk in range(NSA_KV_HEADS):
                own = (lane >= hk * NSA_DH) & (lane < (hk + 1) * NSA_DH)
                own_r = (srow >= hk * NSA_DH) & (srow < (hk + 1) * NSA_DH)
                f0, f1, f2 = _feature_lanes(hk)
                feat = jnp.where(lane == f0, blk.astype(F32), jnp.where(lane == f1, (pos & 63).astype(F32), 0.0))
                ksr[hk, cs, 0:LANES] = jnp.where(own, ks_rows, feat).astype(BF16)
                ksr[hk, cs, LANES:2 * LANES] = (lane == blk).astype(BF16)
                kwr[hk, pl.ds(wpad + r0, LANES), :] = jnp.where(own, kw_rows, feat).astype(BF16)
                vsa[hk, :, cs] = jnp.where(own_r, vs_ref[0, :, cs], 1.0).astype(BF16)
                vwa[hk, :, pl.ds(wpad + r0, LANES)] = jnp.where(own_r, vw_ref[0, :, cs], 1.0).astype(BF16)
            return 0

        lax.fori_loop(0, t // LANES, prep, 0)

    @pl.when(c == 0)
    def _():
        qt = jnp.transpose(q_ref[0]) * (NSA_DH ** -0.5)
        cend = lax.broadcasted_iota(jnp.int32, (nc, 1), 0) * CMP_STRIDE + (2 * CMP_STRIDE - 1)
        wrow = lax.broadcasted_iota(jnp.int32, (LANES, 1), 0)
        sel_heads = []
        for hk in range(NSA_KV_HEADS):
            slabs = []
            for g in range(NSA_GROUP):
                h = hk * NSA_GROUP + g
                halves = [qt[h * NSA_DH:(h + 1) * NSA_DH], _head_feature_rows(hk, g, nq)]
                slabs.append(jnp.concatenate(halves if hk == 0 else halves[::-1], axis=0))
            qlo = jnp.concatenate(slabs, axis=1).astype(BF16)
            s = jnp.dot(kca[hk], qlo, preferred_element_type=F32)
            valid = cend <= tq
            s = jnp.where(valid, s, NEG_INF)
            p = jnp.where(valid, jnp.exp(s - jnp.max(s, axis=0, keepdims=True)), 0.0)
            p = p / jnp.maximum(jnp.sum(p, axis=0, keepdims=True), 1e-30)
            oc_s[hk] = jnp.dot(vct[hk], p.astype(BF16), preferred_element_type=F32)
            psum = p[:, 0:nq]
            for g in range(1, NSA_GROUP):
                psum = psum + p[:, g * nq:(g + 1) * nq]
            imp_t = sum(jnp.dot(at_ref[...], part, preferred_element_type=F32)
                        for part in _bf16_parts(psum))
            sel_t = _select_t(imp_t, tq_q, ns, n_top)
            sel_heads.append(sel_t)
            bias = (sel_t - 1.0) * 1e30
            bias = jnp.concatenate([bias, jnp.zeros((LANES - nsr, nq), F32)], axis=0) if nsr < LANES else bias
            qa_s[hk, 0:LANES, :] = qlo
            qa_s[hk, LANES:2 * LANES, :] = jnp.concatenate([bias] * NSA_GROUP, axis=1).astype(BF16)
            s = jnp.dot(kwr[hk, pl.ds(t0, wkeys), :], qlo, preferred_element_type=F32)
            old = jnp.where(t0 - wpad + wrow > tq - WINDOW, s[0:LANES], NEG_INF)
            new = jnp.where(t0 + wrow[0:nq] <= tq, s[wkeys - nq:wkeys], NEG_INF)
            s = jnp.concatenate([old, s[LANES:wkeys - nq], new], axis=0)
            p = jnp.exp(s - jnp.max(s, axis=0, keepdims=True)).astype(BF16)
            ow = jnp.dot(vwa[hk, :, pl.ds(t0, wkeys)], p, preferred_element_type=F32)
            l_w = _swap_halves(ow)
            ow_s[hk] = ow / jnp.maximum(l_w, 1e-30)
            m_s[hk] = jnp.full((8, cols), NEG_INF, F32)
            acc_s[hk] = jnp.zeros((LANES, cols), F32)
        any_sel = sel_heads[0]
        for s_t in sel_heads[1:]:
            any_sel = jnp.maximum(any_sel, s_t)
        per_chunk = kchunk // SEL_BLOCK
        for cc in range(t // kchunk):
            used_ref[cc] = (jnp.max(any_sel[cc * per_chunk:(cc + 1) * per_chunk, :]) > 0.5).astype(jnp.int32)

    def flash(causal, ck):
        k0 = pl.multiple_of(ck * kchunk, kchunk)
        for hk in range(NSA_KV_HEADS):
            s = jnp.dot(ksr[hk, pl.ds(k0, kchunk), :], qa_s[hk], preferred_element_type=F32)
            if causal:
                s = jnp.where(k0 + lax.broadcasted_iota(jnp.int32, (kchunk, 1), 0) <= tq, s, NEG_INF)
            m_old = m_s[hk][0:1, :]
            m_new = jnp.maximum(m_old, jnp.max(s, axis=0, keepdims=True))
            p = jnp.exp(s - m_new).astype(BF16)
            acc_s[hk] = jnp.exp(m_old - m_new) * acc_s[hk] + jnp.dot(vsa[hk, :, pl.ds(k0, kchunk)], p,
                                                                   preferred_element_type=F32)
            m_s[hk] = jnp.broadcast_to(m_new, (8, cols))

    for slot in range(cgroup):
        ck = grp * cgroup + slot
        used = used_ref[jnp.minimum(ck, t // kchunk - 1)] > 0
        pl.when((ck < last) & used)(functools.partial(flash, False, ck))
        pl.when(ck == last)(functools.partial(flash, True, ck))

    @pl.when((grp + 1) * cgroup > last)
    def _():
        gates = jnp.transpose(_sigmoid(sm_ref[0] + bg_ref[...]))
        srow = lax.broadcasted_iota(jnp.int32, (LANES, 1), 0)
        halves = [None] * NSA_HEADS
        for hk in range(NSA_KV_HEADS):
            own_r = (srow >= hk * NSA_DH) & (srow < (hk + 1) * NSA_DH)
            acc = acc_s[hk]
            o_s = acc / jnp.maximum(_swap_halves(acc), 1e-30)
            o_c = oc_s[hk]
            o_w = ow_s[hk]
            for g in range(NSA_GROUP):
                h = hk * NSA_GROUP + g
                cs = slice(g * nq, (g + 1) * nq)
                og = (gates[3 * h:3 * h + 1] * o_c[:, cs] + gates[3 * h + 1:3 * h + 2] * o_s[:, cs]
                      + gates[3 * h + 2:3 * h + 3] * o_w[:, cs])
                og = jnp.where(own_r, og, 0.0)
                ms = jnp.sum(og * og, axis=0, keepdims=True) * (1.0 / NSA_DH)
                y = og * lax.rsqrt(ms + EPS) * ght_ref[:, h:h + 1]
                halves[h] = y[hk * NSA_DH:(hk + 1) * NSA_DH]
        for k in range(NSA_HEADS // 2):
            o_ref[0, :, k * LANES:(k + 1) * LANES] = jnp.transpose(
                jnp.concatenate([halves[2 * k], halves[2 * k + 1]], axis=0))


def _nsa_prompt_t(q, sm, k_cmp, v_cmp, ks, vs, kw, vw, bg, gh, nq):
    b, t, _ = q.shape
    nc = k_cmp.shape[1]
    ns = t // SEL_BLOCK
    nsr = -(-ns // 8) * 8
    assert nq == LANES and nsr <= LANES and nc <= 256 and t % 512 == 0, "prompt NSA kernel shape limits"
    at = _sel_matrix(nc, nc, nsr).T.astype(BF16)
    kchunk = min(512, t)
    wpad = WINDOW
    cgroup = 4
    pairs = [(i, g) for i in range(t // nq) for g in range(-(-(-(-((i + 1) * nq) // kchunk)) // cgroup))]
    qi = jnp.asarray(np.array([p[0] for p in pairs], np.int32))
    ci = jnp.asarray(np.array([p[1] for p in pairs], np.int32))
    cols = NSA_GROUP * nq
    full = lambda n: pl.BlockSpec((1, n, LANES), lambda bi, s, qi, ci: (bi, 0, 0))
    full_t = pl.BlockSpec((1, LANES, t), lambda bi, s, qi, ci: (bi, 0, 0))
    const = lambda shape: pl.BlockSpec(shape, lambda bi, s, qi, ci: (0,) * len(shape))
    tok = lambda w: pl.BlockSpec((1, nq, w), lambda bi, s, qi, ci: (bi, qi[s], 0))
    grid_spec = pltpu.PrefetchScalarGridSpec(
        num_scalar_prefetch=2,
        grid=(b, len(pairs)),
        in_specs=[tok(4 * LANES), tok(LANES), full(nc), full(nc), full_t, full_t, full_t, full_t,
                  const(at.shape), const((1, LANES)), const((LANES, NSA_HEADS))],
        out_specs=tok(4 * LANES),
        scratch_shapes=[pltpu.VMEM((NSA_KV_HEADS, t, 2 * LANES), BF16),
                        pltpu.VMEM((NSA_KV_HEADS, wpad + t, LANES), BF16),
                        pltpu.VMEM((NSA_KV_HEADS, LANES, t), BF16),
                        pltpu.VMEM((NSA_KV_HEADS, LANES, wpad + t), BF16),
                        pltpu.VMEM((NSA_KV_HEADS, nc, LANES), BF16),
                        pltpu.VMEM((NSA_KV_HEADS, LANES, nc), BF16),
                        pltpu.VMEM((NSA_KV_HEADS, 2 * LANES, cols), BF16),
                        pltpu.VMEM((NSA_KV_HEADS, LANES, cols), F32),
                        pltpu.VMEM((NSA_KV_HEADS, LANES, cols), F32),
                        pltpu.VMEM((NSA_KV_HEADS, 8, cols), F32),
                        pltpu.VMEM((NSA_KV_HEADS, LANES, cols), F32),
                        pltpu.SMEM((t // kchunk,), jnp.int32)],
    )
    return pl.pallas_call(
        functools.partial(_nsa_prompt_t_kernel, nq=nq, ns=ns, n_top=min(SEL_TOPK, ns), kchunk=kchunk, wpad=wpad,
                          cgroup=cgroup),
        grid_spec=grid_spec,
        out_shape=jax.ShapeDtypeStruct((b, t, 4 * LANES), F32),
        compiler_params=_params(("arbitrary", "arbitrary")),
        name="nsa_prompt",
    )(qi, ci, q, sm, k_cmp, v_cmp, ks, vs, kw, vw, at, bg, gh.T)


def _chunk_features(nkeys):
    key = np.arange(nkeys)
    c = np.zeros((LANES, nkeys), np.float32)
    c[key // SEL_BLOCK, key] = 1.0
    c[NSA_DH, :] = key // SEL_BLOCK
    c[NSA_DH + 1, :] = key % SEL_BLOCK
    c[NSA_DH + 2, :] = 1.0
    return jnp.asarray(c, BF16)


def _nsa_sample2_kernel(pt_ref, q_ref, sm_ref, kc_ref, vc_ref, ks_pool, vs_pool, ksn_ref, vsn_ref,
                        wk_ref, wv_ref, kwn_ref, vwn_ref, a_ref, cf_ref, bg_ref, gh_ref, o_ref,
                        kbuf, vbuf, sem, q_s, bias_s, selnew_s, oc_s, ow_s, m_s, l_s, acc_s,
                        *, ch, nchunks, offset, ns, n_top, kchunk):
    b = pl.program_id(0)
    j = pl.program_id(1)
    nq = q_ref.shape[1]
    rows = NSA_GROUP * nq
    w_pre = wk_ref.shape[2]
    nkeys = ch * PAGE
    n = b * nchunks + j
    cur = n % 2
    lane = lax.broadcasted_iota(jnp.int32, (1, LANES), 1)

    def step_copies(bb, jj, buf_i, fn):
        for s in range(ch):
            page = pt_ref[bb, jj * ch + s]
            dst = pl.ds(s * PAGE, PAGE)
            fn(pltpu.make_async_copy(ks_pool.at[page], kbuf.at[buf_i, :, dst], sem.at[0, buf_i]))
            fn(pltpu.make_async_copy(vs_pool.at[page], vbuf.at[buf_i, :, dst], sem.at[1, buf_i]))

    @pl.when(n == 0)
    def _():
        step_copies(b, j, cur, lambda c: c.start())

    @pl.when(n + 1 < pl.num_programs(0) * nchunks)
    def _():
        wrap = j + 1 == nchunks
        step_copies(jnp.where(wrap, b + 1, b), jnp.where(wrap, 0, j + 1), 1 - cur, lambda c: c.start())

    @pl.when(j == 0)
    def _():
        qhs, imps = [], []
        for hk in range(NSA_KV_HEADS):
            rs = slice(hk * rows, (hk + 1) * rows)
            qh, _ = _q_rows(q_ref, hk, nq)
            slope, tq = _row_consts(hk, nq, offset)
            o_c, imp = _cmp_branch(qh, slope, tq, kc_ref, vc_ref, a_ref, nq)
            q_s[rs, :] = qh
            oc_s[rs, :] = o_c
            qhs.append(qh)
            imps.append(imp)
        _, tq = _row_consts(0, nq, offset)
        sel_all = _select(jnp.concatenate(imps, axis=0), jnp.concatenate([tq[0:nq]] * NSA_KV_HEADS, axis=0),
                          ns, n_top)
        for hk in range(NSA_KV_HEADS):
            rs = slice(hk * rows, (hk + 1) * rows)
            qh = qhs[hk]
            slope, tq = _row_consts(hk, nq, offset)
            sel_rows = jnp.concatenate([sel_all[hk * nq:(hk + 1) * nq]] * NSA_GROUP, axis=0)
            for jj in range(nchunks):
                nb_c = nkeys // SEL_BLOCK
                blocks = sel_rows[:, jj * nb_c:jj * nb_c + LANES]
                upper = jnp.where(lane < nb_c, (blocks - 1.0) * 1e30,
                                  jnp.where(lane == NSA_DH, slope * SEL_BLOCK,
                                            jnp.where(lane == NSA_DH + 1, slope,
                                                      jnp.where(lane == NSA_DH + 2,
                                                                slope * float(jj * nkeys - offset), 0.0))))
                bias_s[jj, rs, :] = upper.astype(BF16)
            new_blk = offset // SEL_BLOCK
            selnew_s[rs, :] = jnp.broadcast_to(sel_rows[:, new_blk:new_blk + 1], (rows, LANES))
            pos1 = offset - w_pre + lax.broadcasted_iota(jnp.int32, (1, w_pre), 1)
            d1 = tq - pos1
            pos2 = offset + lax.broadcasted_iota(jnp.int32, (1, kwn_ref.shape[1]), 1)
            d2 = tq - pos2
            v1 = (d1 >= 0) & (d1 < WINDOW)
            v2 = (d2 >= 0) & (d2 < WINDOW)
            s1 = jnp.where(v1, _mm(qh, wk_ref[0]) - slope * d1.astype(F32), NEG_INF)
            s2 = jnp.where(v2, _mm_nt(qh, kwn_ref[0]) - slope * d2.astype(F32), NEG_INF)
            mx = jnp.maximum(jnp.max(s1, axis=-1, keepdims=True), jnp.max(s2, axis=-1, keepdims=True))
            p1 = jnp.where(v1, jnp.exp(s1 - mx), 0.0)
            p2 = jnp.where(v2, jnp.exp(s2 - mx), 0.0)
            l_w = jnp.sum(p1, axis=-1, keepdims=True) + jnp.sum(p2, axis=-1, keepdims=True)
            ow_s[rs, :] = (_mm_nt(p1, wv_ref[0]) + _mm(p2, vwn_ref[0])) / jnp.maximum(l_w, 1e-30)
        m_s[...] = jnp.full(m_s.shape, NEG_INF, F32)
        l_s[...] = jnp.zeros(l_s.shape, F32)
        acc_s[...] = jnp.zeros(acc_s.shape, F32)

    step_copies(b, j, cur, lambda c: c.wait())

    qa = jnp.concatenate([q_s[...], bias_s[j]], axis=1)
    tiles = []
    for c in range(nkeys // kchunk):
        cs = pl.ds(c * kchunk, kchunk)
        ka = jnp.concatenate([kbuf[cur, :, cs].astype(BF16), cf_ref[:, cs]], axis=0)
        tiles.append(jnp.dot(qa, ka, preferred_element_type=F32))
    s = jnp.concatenate(tiles, axis=1)
    m = m_s[:, 0:1]
    m_new = jnp.maximum(m, jnp.max(s, axis=-1, keepdims=True))
    alpha = jnp.exp(m - m_new)
    p = jnp.exp(s - m_new)
    m_s[...] = jnp.broadcast_to(m_new, m_s.shape)
    l_s[...] = jnp.broadcast_to(alpha * l_s[:, 0:1] + jnp.sum(p, axis=-1, keepdims=True), l_s.shape)
    acc_s[...] = alpha * acc_s[...] + _mm_nt(p, vbuf[cur])

    @pl.when(j == nchunks - 1)
    def _():
        gates = _sigmoid(sm_ref[0] + bg_ref[...])
        out_pairs = [None] * (NSA_HEADS // 2)
        nnew = ksn_ref.shape[1]
        r_new = lax.broadcasted_iota(jnp.int32, (1, nnew), 1)
        for hk in range(NSA_KV_HEADS):
            rs = slice(hk * rows, (hk + 1) * rows)
            own = (lane >= hk * NSA_DH) & (lane < (hk + 1) * NSA_DH)
            slope, tq = _row_consts(hk, nq, 0)
            valid = (r_new <= tq) & (selnew_s[rs, 0:1] > 0.5)
            s = jnp.where(valid, _mm_nt(q_s[rs, :], ksn_ref[0]) + slope * r_new.astype(F32), NEG_INF)
            m_o = m_s[rs, 0:1]
            m_f = jnp.maximum(m_o, jnp.max(s, axis=-1, keepdims=True))
            alpha = jnp.exp(m_o - m_f)
            p = jnp.where(valid, jnp.exp(s - m_f), 0.0)
            l_f = alpha * l_s[rs, 0:1] + jnp.sum(p, axis=-1, keepdims=True)
            o_s = (alpha * acc_s[rs, :] + _mm(p, vsn_ref[0])) / jnp.maximum(l_f, 1e-30)
            _combine_heads(oc_s[rs, :], o_s, ow_s[rs, :], gates, gh_ref, own, hk, nq, out_pairs)
        for k, y in enumerate(out_pairs):
            o_ref[0, :, k * LANES:(k + 1) * LANES] = y


def _nsa_sample2(page_table, q, sm, k_cmp, v_cmp, ks_pool, vs_pool, ks_new, vs_new, win_k, win_v, kw_new, vw_new,
                 bg, gh, nc_valid, ch):
    b, nq, _ = q.shape
    npg = page_table.shape[1]
    nchunks = npg // ch
    offset = npg * PAGE
    ncp = k_cmp.shape[1]
    ns = nc_valid // 4
    nsp = -(-(ns + LANES) // LANES) * LANES
    assert ch * PAGE // SEL_BLOCK <= NSA_DH, "a chunk's blocks must fit the 64 mask lanes"
    a = _sel_matrix(ncp, nc_valid, nsp).astype(BF16)
    rows2 = NSA_KV_HEADS * NSA_GROUP * nq
    kchunk = min(512, ch * PAGE)
    cf = _chunk_features(ch * PAGE)
    per_b = lambda n, w=LANES: pl.BlockSpec((1, n, w), lambda bi, j, t: (bi, 0, 0))
    const = lambda shape: pl.BlockSpec(shape, lambda bi, j, t: (0,) * len(shape))
    grid_spec = pltpu.PrefetchScalarGridSpec(
        num_scalar_prefetch=1,
        grid=(b, nchunks),
        in_specs=[per_b(nq, 4 * LANES), per_b(nq), per_b(ncp), per_b(ncp),
                  pl.BlockSpec(memory_space=pl.ANY), pl.BlockSpec(memory_space=pl.ANY),
                  per_b(PAGE), per_b(PAGE), per_b(LANES, win_k.shape[2]), per_b(LANES, win_v.shape[2]),
                  per_b(PAGE), per_b(PAGE),
                  const(a.shape), const(cf.shape), const((1, LANES)), const((NSA_HEADS, LANES))],
        out_specs=per_b(nq, 4 * LANES),
        scratch_shapes=[pltpu.VMEM((2, LANES, ch * PAGE), F32), pltpu.VMEM((2, LANES, ch * PAGE), F32),
                        pltpu.SemaphoreType.DMA((2, 2)),
                        pltpu.VMEM((rows2, LANES), BF16),
                        pltpu.VMEM((nchunks, rows2, LANES), BF16)]
        + [pltpu.VMEM((rows2, LANES), F32)] * 6,
    )
    return pl.pallas_call(
        functools.partial(_nsa_sample2_kernel, ch=ch, nchunks=nchunks, offset=offset, ns=ns,
                          n_top=min(SEL_TOPK, ns), kchunk=kchunk),
        grid_spec=grid_spec,
        out_shape=jax.ShapeDtypeStruct((b, nq, 4 * LANES), F32),
        compiler_params=_params(("arbitrary", "arbitrary")),
        name="nsa_sample",
    )(page_table, q, sm, k_cmp, v_cmp, ks_pool, vs_pool, ks_new, vs_new, win_k, win_v, kw_new, vw_new, a, cf, bg, gh)


def _mlstm_kernel(xm_ref, vm_ref, om_ref, sm_ref, smt_ref, conv0_ref, cw_ref, cb_ref, wq_ref, wk_ref,
                  bcol_ref, brow_ref, gh_ref, c0_ref, n0_ref, m0_ref,
                  hm_ref, c_ref, n_ref, m_ref, xs_ref, *, L, t_valid):
    c = pl.program_id(1)
    dh = MLSTM_DH

    @pl.when(c == 0)
    def _():
        xs_ref[0:8, :] = jnp.zeros((8, xs_ref.shape[1]), F32)
        xs_ref[5:8, :] = conv0_ref[0]
        c_ref[...] = c0_ref[...]
        n_ref[...] = n0_ref[...]
        m_ref[...] = m0_ref[...]

    xs_ref[8:8 + L, :] = xm_ref[0]
    xc = cb_ref[...]
    for jj in range(4):
        xc = xc + cw_ref[jj:jj + 1, :] * xs_ref[pl.ds(5 + jj, L), :]
    xc = xc * _sigmoid(xc)
    xs_ref[0:8, :] = xs_ref[L:L + 8, :]

    pre_col = sm_ref[0] + brow_ref[...]
    pre_row = smt_ref[0, 0] + bcol_ref[...]
    lf_col = _log_sigmoid(pre_col)
    lf_row = _log_sigmoid(pre_row)
    if t_valid < L:
        rid = lax.broadcasted_iota(jnp.int32, (L, 1), 0) < t_valid
        cid = lax.broadcasted_iota(jnp.int32, (1, L), 1) < t_valid
        lf_col = jnp.where(rid, lf_col, 0.0)
        lf_row = jnp.where(cid, lf_row, 0.0)
        pre_col = jnp.where(rid, pre_col, NEG_INF)
        pre_row = jnp.where(cid, pre_row, NEG_INF)
    ri = lax.broadcasted_iota(jnp.int32, (L, L), 0)
    ci = lax.broadcasted_iota(jnp.int32, (L, L), 1)
    causal = ci <= ri
    tril = causal.astype(BF16)
    triu = (ri <= ci).astype(BF16)
    bcum_col = sum(jnp.dot(tril, part, preferred_element_type=F32) for part in _bf16_parts(lf_col))
    bcum_row = sum(jnp.dot(part, triu, preferred_element_type=F32) for part in _bf16_parts(lf_row))

    for h in range(MLSTM_HEADS):
        hs = slice(h * dh, (h + 1) * dh)
        xh = xc[:, hs]
        q = _mm(xh, wq_ref[h])
        k = _mm(xh, wk_ref[h]) * (dh ** -0.5)
        v = vm_ref[0, :, hs]
        bc = bcum_col[:, 28 + h:29 + h]
        ic = pre_col[:, 24 + h:25 + h]
        br = bcum_row[4 + h:5 + h, :]
        ir = pre_row[h:h + 1, :]
        mh = m_ref[0, :, h:h + 1]
        ch_ = c_ref[0, h]
        nh = n_ref[0, h:h + 1, :]
        dmat = jnp.where(causal, bc - br + ir, NEG_INF)
        inter = bc + mh
        mq = jnp.maximum(inter, jnp.max(dmat, axis=1, keepdims=True))
        a = jnp.exp(dmat - mq) * _mm_nt(q, k)
        wi = jnp.exp(inter - mq)
        num = _mm(a, v) + wi * _mm_nt(q, ch_)
        den = jnp.sum(a, axis=1, keepdims=True) + wi * jnp.sum(q * nh, axis=1, keepdims=True)
        hout = num / jnp.maximum(jnp.abs(den), jnp.exp(-mq))
        btot = bc[L - 1:L, :]
        dec_r = btot - br + ir
        dec_c = btot - bc + ic
        m_new = jnp.maximum(btot + mh, jnp.max(dec_r, axis=1, keepdims=True))
        ws_c = jnp.exp(dec_c - m_new)
        w_c = jnp.exp(btot + mh - m_new)
        c_ref[0, h] = w_c * ch_ + _mm_tn(v * ws_c, k)
        n_ref[0, h:h + 1, :] = w_c * nh + jnp.sum(k * ws_c, axis=0, keepdims=True)
        m_ref[0, :, h:h + 1] = m_new
        y = _rms(hout, gh_ref[h:h + 1, :]) * _sigmoid(om_ref[0, :, hs])
        hm_ref[0, :, hs] = y


def _mlstm(xm, vm, om, sm, conv0, conv_w, conv_b, w_qm, w_km, b_i, b_f, g_head, c0, n0, m0, L, t_valid):
    b, t, w = xm.shape
    nck = t // L
    smt = sm[:, :, 24:32].reshape(b, nck, L, 8).transpose(0, 1, 3, 2)
    brow = jnp.zeros((1, LANES), F32).at[0, 24:28].set(b_i).at[0, 28:32].set(b_f)
    bcol = jnp.concatenate([b_i, b_f]).reshape(8, 1)
    m0p = jnp.zeros((b, 1, LANES), F32).at[:, 0, :MLSTM_HEADS].set(m0)
    tok = lambda: pl.BlockSpec((1, L, w), lambda bi, c: (bi, c, 0))
    const = lambda shape: pl.BlockSpec(shape, lambda bi, c: (0,) * len(shape))
    state = lambda shape: pl.BlockSpec((1,) + shape, lambda bi, c: (bi,) + (0,) * len(shape))
    hm, c_out, n_out, m_out = pl.pallas_call(
        functools.partial(_mlstm_kernel, L=L, t_valid=t_valid),
        grid=(b, nck),
        in_specs=[tok(), tok(), tok(),
                  pl.BlockSpec((1, L, LANES), lambda bi, c: (bi, c, 0)),
                  pl.BlockSpec((1, 1, 8, L), lambda bi, c: (bi, c, 0, 0)),
                  state(conv0.shape[1:]),
                  const(conv_w.shape), const((1, w)), const(w_qm.shape), const(w_km.shape),
                  const((8, 1)), const((1, LANES)), const(g_head.shape),
                  state(c0.shape[1:]), state(n0.shape[1:]), state((1, LANES))],
        out_specs=[tok(), state(c0.shape[1:]), state(n0.shape[1:]), state((1, LANES))],
        out_shape=[jax.ShapeDtypeStruct((b, t, w), F32), jax.ShapeDtypeStruct(c0.shape, F32),
                   jax.ShapeDtypeStruct(n0.shape, F32), jax.ShapeDtypeStruct((b, 1, LANES), F32)],
        scratch_shapes=[pltpu.VMEM((L + 8, w), F32)],
        compiler_params=_params(("arbitrary", "arbitrary")),
        name="mlstm",
    )(xm, vm, om, sm, smt, conv0, conv_w, conv_b.reshape(1, w), w_qm.astype(BF16), w_km.astype(BF16),
      bcol, brow, g_head, c0, n0, m0p)
    return hm, c_out, n_out, m_out[:, 0, :MLSTM_HEADS]


def _mix_kernel(x_ref, on_ref, hm_ref, wo1_ref, wo2_ref, gxa_ref, wxq_ref, x1_ref, qx_ref):
    x1 = x_ref[...] + _mm(on_ref[...], wo1_ref[...]) + _mm(hm_ref[...], wo2_ref[...])
    x1_ref[...] = x1
    qx_ref[...] = _mm(_rms(x1, gxa_ref[...]), wxq_ref[...]) * (XA_DH ** -0.5)


def _mix(x, o_nsa, hm, w_out, g_xa, w_xq, tm):
    m, d = x.shape
    half = o_nsa.shape[1]
    row = lambda w: pl.BlockSpec((tm, w), lambda i: (i, 0))
    const = lambda shape: pl.BlockSpec(shape, lambda i: (0, 0))
    return pl.pallas_call(
        _mix_kernel,
        grid=(m // tm,),
        in_specs=[row(d), row(half), row(half), const((half, d)), const((half, d)), const((1, d)), const((d, d))],
        out_specs=[row(d), row(d)],
        out_shape=[jax.ShapeDtypeStruct((m, d), F32)] * 2,
        compiler_params=_params(("parallel",)),
        name="mix",
    )(x, o_nsa, hm, w_out[:half].astype(BF16), w_out[half:].astype(BF16), g_xa.reshape(1, d), w_xq.astype(BF16))


def _xattn_kernel(qx_ref, mk_ref, mv_ref, o_ref):
    for h in range(XA_HEADS):
        hs = slice(h * XA_DH, (h + 1) * XA_DH)
        s = _mm_nt(qx_ref[0, :, hs], mk_ref[0, :, hs])
        p = jnp.exp(s - jnp.max(s, axis=-1, keepdims=True))
        o_ref[0, :, hs] = _mm(p, mv_ref[0, :, hs]) / jnp.sum(p, axis=-1, keepdims=True)


def _xattn(qx, mem_k, mem_v, tq):
    b, t, d = qx.shape
    nm = mem_k.shape[1]
    mem_spec = pl.BlockSpec((1, nm, d), lambda bi, i: (bi, 0, 0))
    return pl.pallas_call(
        _xattn_kernel,
        grid=(b, t // tq),
        in_specs=[pl.BlockSpec((1, tq, d), lambda bi, i: (bi, i, 0)), mem_spec, mem_spec],
        out_specs=pl.BlockSpec((1, tq, d), lambda bi, i: (bi, i, 0)),
        out_shape=jax.ShapeDtypeStruct((b, t, d), F32),
        compiler_params=_params(("parallel", "parallel")),
        name="xattn",
    )(qx, mem_k, mem_v)


def _ffn_kernel(x1_ref, ox_ref, wxo_ref, gf_ref, wg_ref, wu_ref, wd_ref, gfin_ref, y_ref, x2_s, h_s, acc_s):
    j = pl.program_id(1)

    @pl.when(j == 0)
    def _():
        x2 = x1_ref[...] + _mm(ox_ref[...], wxo_ref[...])
        x2_s[...] = x2
        h_s[...] = _rms(x2, gf_ref[...]).astype(BF16)
        acc_s[...] = jnp.zeros(acc_s.shape, F32)

    h = h_s[...]
    g = jnp.dot(h, wg_ref[...], preferred_element_type=F32)
    u = jnp.dot(h, wu_ref[...], preferred_element_type=F32)
    acc_s[...] += _mm(g * _sigmoid(g) * u, wd_ref[...])

    @pl.when(j == pl.num_programs(1) - 1)
    def _():
        y_ref[...] = _rms(x2_s[...] + acc_s[...], gfin_ref[...])


def _ffn(x1, ox, w_xo, g_ffn, w_gate, w_up, w_down, g_final, tm, tf):
    m, d = x1.shape
    dff = w_gate.shape[1]
    row = pl.BlockSpec((tm, d), lambda i, j: (i, 0))
    vec = pl.BlockSpec((1, d), lambda i, j: (0, 0))
    return pl.pallas_call(
        _ffn_kernel,
        grid=(m // tm, dff // tf),
        in_specs=[row, row, pl.BlockSpec((d, d), lambda i, j: (0, 0)), vec,
                  pl.BlockSpec((d, tf), lambda i, j: (0, j)), pl.BlockSpec((d, tf), lambda i, j: (0, j)),
                  pl.BlockSpec((tf, d), lambda i, j: (j, 0)), vec],
        out_specs=row,
        out_shape=jax.ShapeDtypeStruct((m, d), F32),
        scratch_shapes=[pltpu.VMEM((tm, d), F32), pltpu.VMEM((tm, d), BF16), pltpu.VMEM((tm, d), F32)],
        compiler_params=_params(("parallel", "arbitrary")),
        name="ffn",
    )(x1, ox, w_xo.astype(BF16), g_ffn.reshape(1, d), w_gate.astype(BF16), w_up.astype(BF16),
      w_down.astype(BF16), g_final.reshape(1, d))


def _split_w_in(w_in, nsa_w, kv_w, mlstm_w):
    cuts = np.cumsum([nsa_w] + [kv_w] * 6 + [3 * NSA_HEADS] + [mlstm_w] * 3 + [MLSTM_HEADS] * 2)
    parts = jnp.split(w_in, cuts[:-1].tolist(), axis=1)
    small = jnp.concatenate([parts[7], parts[11], parts[12]], axis=1)
    small = jnp.pad(small, ((0, 0), (0, LANES - small.shape[1])))
    ws = [parts[0]] + list(parts[1:7]) + [small] + list(parts[8:11])
    return [w.astype(BF16) for w in ws]


def _tail(x1, ox, w, tm, b, t):
    d = x1.shape[1]
    dff = w["w_gate"].shape[1]
    tf = dff // 2 if (dff // 2) % LANES == 0 else dff
    y = _ffn(x1, ox.reshape(-1, d), w["w_xo"], w["g_ffn"], w["w_gate"], w["w_up"], w["w_down"], w["g_final"], tm, tf)
    return y.reshape(b, t, d)


def _gate_consts(w):
    bg = jnp.pad(w["b_gate"], (0, LANES - w["b_gate"].shape[0])).reshape(1, LANES)
    gh = jnp.concatenate([w["g_head_nsa"], w["g_head_nsa"]], axis=1)
    return bg, gh


def _prompt_group(x, mem, w):
    b, t, d = x.shape
    m = b * t
    tm = 512
    wp = w["w_in_parts"]
    wt = jnp.concatenate(wp[1:7], axis=1).T
    q, sm, xm, vm, om, kc, vc, ks, vs, kw, vw = _norm_proj(x.reshape(m, d), w["g_mix"], [wp[0]] + wp[7:], tm,
                                                           wt=wt, t=t)
    r3 = lambda a: a.reshape(b, t, a.shape[-1])
    npg = t // PAGE
    table = jnp.zeros((b, npg), jnp.int32)
    zeros_next = jnp.zeros((b, PAGE, LANES), F32)
    k_cmp = _compress(table, kc, zeros_next, *w["cmp_k"], ch=npg, paged=False, feature_major=True)
    v_cmp = _compress(table, vc, zeros_next, *w["cmp_v"], ch=npg, paged=False, feature_major=True)
    bg, gh = _gate_consts(w)
    o_nsa = _nsa_prompt_t(r3(q), r3(sm), k_cmp, v_cmp, ks, vs, kw, vw, bg, gh, LANES)
    L = next(c for c in (256, 128, 64) if t % c == 0)
    hm, c_out, n_out, m_out = _mlstm(
        r3(xm), r3(vm), r3(om), r3(sm), jnp.zeros((b, 3, xm.shape[1]), F32), w["conv_w"], w["conv_b"],
        w["w_qm"], w["w_km"], w["b_i"], w["b_f"], w["g_head_m"],
        jnp.zeros((b, MLSTM_HEADS, MLSTM_DH, MLSTM_DH), F32), jnp.zeros((b, MLSTM_HEADS, MLSTM_DH), F32),
        jnp.zeros((b, MLSTM_HEADS), F32), L, L)
    nm = mem.shape[1]
    mk, mv = _norm_proj(mem.reshape(b * nm, d), w["g_mem"], [w["w_xk"].astype(BF16), w["w_xv"].astype(BF16)],
                        min(512, b * nm))
    x1, qx = _mix(x.reshape(m, d), o_nsa.reshape(m, -1), hm.reshape(m, -1), w["w_out"], w["g_xa"], w["w_xq"], tm)
    ox = _xattn(qx.reshape(b, t, d), mk.reshape(b, nm, d), mv.reshape(b, nm, d), 512)
    y = _tail(x1, ox, w, tm, b, t)
    kv5 = lambda a: a.reshape(b, NSA_KV_HEADS, NSA_DH, a.shape[2]).transpose(0, 3, 1, 2)[None]
    keep = min(WINDOW, t)
    xm3 = r3(xm)
    states = (kv5(kc), kv5(vc), kv5(ks), kv5(vs), kv5(kw[:, :, t - keep:]), kv5(vw[:, :, t - keep:]),
              c_out[None], n_out[None], m_out[None], xm3[None, :, t - 3:],
              mk.reshape(1, b, nm, XA_HEADS, XA_DH), mv.reshape(1, b, nm, XA_HEADS, XA_DH))
    return y, states


def _sample_group(x, pools, page_table, win_k, win_v, conv0, c0, n0, m0, mem_k, mem_v, w):
    b, t, d = x.shape
    m = b * t
    tp = 8
    tm = min(m, 512)
    q, kc, vc, ks, vs, kw, vw, sm, xm, vm, om = _norm_proj(x.reshape(m, d), w["g_mix"], w["w_in_parts"], tm)
    r3 = lambda a: a.reshape(b, t, a.shape[-1])
    pad_t = lambda a: jnp.pad(r3(a), ((0, 0), (0, tp - t), (0, 0)))
    npg = page_table.shape[1]
    past = npg * PAGE
    ch = min(32, npg)
    fm = lambda a: a.transpose(0, 2, 3, 1).reshape(a.shape[0], LANES, a.shape[1])
    pool_kc, pool_vc, pool_ks, pool_vs = [fm(p) for p in pools]
    win_kt, win_vt = fm(win_k), fm(win_v)
    nc_valid = (past + (-(-t // SEL_BLOCK)) * SEL_BLOCK) // CMP_STRIDE
    ncp = -(-nc_valid // LANES) * LANES

    pad_page = lambda a: jnp.pad(r3(a), ((0, 0), (0, PAGE - t), (0, 0)))

    def compressed(pool, new_rows, cw):
        nxt = pad_page(new_rows)
        main = _compress(page_table, pool, nxt, *cw, ch=ch, paged=True, feature_major=True)
        tail = _compress(jnp.arange(b, dtype=jnp.int32).reshape(1, b), nxt, jnp.zeros((1, PAGE, LANES), F32),
                         *cw, ch=b, paged=True, feature_major=False)
        n_tail = nc_valid - npg * CMP_PER_PAGE
        full = jnp.concatenate([main, tail.reshape(b, CMP_PER_PAGE, LANES)[:, :n_tail]], axis=1)
        return jnp.pad(full, ((0, 0), (0, ncp - nc_valid), (0, 0)))

    k_cmp = compressed(pool_kc, kc, w["cmp_k"])
    v_cmp = compressed(pool_vc, vc, w["cmp_v"])
    bg, gh = _gate_consts(w)
    o_nsa = _nsa_sample2(page_table, pad_t(q), pad_t(sm), k_cmp, v_cmp, pool_ks, pool_vs, pad_page(ks), pad_page(vs),
                        win_kt, win_vt, pad_page(kw), pad_page(vw),
                        bg, gh, nc_valid, ch)[:, :t]
    hm, c_out, n_out, m_out = _mlstm(pad_t(xm), pad_t(vm), pad_t(om), pad_t(sm), conv0, w["conv_w"], w["conv_b"],
                                     w["w_qm"], w["w_km"], w["b_i"], w["b_f"], w["g_head_m"], c0, n0, m0, tp, t)
    hm = hm[:, :t]
    x1, qx = _mix(x.reshape(m, d), o_nsa.reshape(m, -1), hm.reshape(m, -1), w["w_out"], w["g_xa"], w["w_xq"], tm)
    nm = mem_k.shape[1]
    ox = _xattn(pad_t(qx), mem_k.reshape(b, nm, d), mem_v.reshape(b, nm, d), tp)[:, :t]
    y = _tail(x1, ox, w, tm, b, t)
    kv5 = lambda a: a.reshape(1, b, t, NSA_KV_HEADS, NSA_DH)
    keep = min(WINDOW, past + t)
    unfm = lambda a: a.reshape(b, NSA_KV_HEADS, NSA_DH, a.shape[2]).transpose(0, 3, 1, 2)[None]
    win5 = lambda old_t, new: unfm(jnp.concatenate([old_t, r3(new).transpose(0, 2, 1)], axis=2)[:, :, -keep:])
    conv_all = jnp.concatenate([conv0, r3(xm)], axis=1)
    states = (kv5(kc), kv5(vc), kv5(ks), kv5(vs), win5(win_kt, kw), win5(win_vt, vw),
              c_out[None], n_out[None], m_out[None], conv_all[None, :, -3:])
    return y, states


def kernel(x_prompt, x_sample, cache_k_cmp, cache_v_cmp, cache_k_slc, cache_v_slc, state_k_win, state_v_win, state_conv, state_C, state_n, state_m, cache_mem_k, cache_mem_v, page_table, mem_prompt, g_mix, w_in, b_gate, cmp_pe_k, cmp_w1_k, cmp_b1_k, cmp_w2_k, cmp_pe_v, cmp_w1_v, cmp_b1_v, cmp_w2_v, g_head_nsa, conv_w, conv_b, w_qm, w_km, b_i, b_f, g_head_m, w_out, g_xa, g_mem, w_xq, w_xk, w_xv, w_xo, g_ffn, w_gate, w_up, w_down, g_final):
    assert w_in.shape[0] == 1, "single-layer decoder"
    l = 0
    w = dict(g_mix=g_mix[l], b_gate=b_gate[l],
             w_in_parts=_split_w_in(w_in[l], NSA_HEADS * NSA_DH, NSA_KV_HEADS * NSA_DH, MLSTM_HEADS * MLSTM_DH),
             cmp_k=(cmp_pe_k[l], cmp_w1_k[l], cmp_b1_k[l], cmp_w2_k[l]),
             cmp_v=(cmp_pe_v[l], cmp_w1_v[l], cmp_b1_v[l], cmp_w2_v[l]),
             g_head_nsa=g_head_nsa[l], conv_w=conv_w[l], conv_b=conv_b[l], w_qm=w_qm[l], w_km=w_km[l],
             b_i=b_i[l], b_f=b_f[l], g_head_m=g_head_m[l], w_out=w_out[l], g_xa=g_xa[l], g_mem=g_mem[l],
             w_xq=w_xq[l], w_xk=w_xk[l], w_xv=w_xv[l], w_xo=w_xo[l], g_ffn=g_ffn[l], w_gate=w_gate[l],
             w_up=w_up[l], w_down=w_down[l], g_final=g_final)
    y_p, st_p = _prompt_group(x_prompt, mem_prompt, w)
    pools = (cache_k_cmp[l], cache_v_cmp[l], cache_k_slc[l], cache_v_slc[l])
    y_s, st_s = _sample_group(x_sample, pools, page_table, state_k_win[l], state_v_win[l], state_conv[l],
                              state_C[l], state_n[l], state_m[l], cache_mem_k[l], cache_mem_v[l], w)
    return (y_p, y_s) + st_p + st_s
```

```python
import functools

import numpy as np
import jax
import jax.numpy as jnp
from jax import lax
from jax.experimental import pallas as pl
from jax.experimental.pallas import tpu as pltpu

F32 = jnp.float32
BF16 = jnp.bfloat16

EPS = 1e-6
NEG_INF = -1e30
FORCE_SCORE = 1e9
PAD_SCORE = -2e38
TAKEN_SCORE = -3e38

LANES = 128
NSA_HEADS = 8
NSA_KV_HEADS = 2
NSA_GROUP = 4
NSA_DH = 64
CMP_STRIDE = 16
SEL_BLOCK = 64
SEL_TOPK = 16
WINDOW = 512
PAGE = 128
CMP_PER_PAGE = PAGE // CMP_STRIDE
MLSTM_HEADS = 4
MLSTM_DH = 128
XA_HEADS = 4
XA_DH = 256
ALIBI = tuple(2.0 ** (-(h + 1)) for h in range(NSA_HEADS))

VMEM_LIMIT = 56 * 1024 * 1024


def _params(sem):
    return pltpu.CompilerParams(dimension_semantics=sem, vmem_limit_bytes=VMEM_LIMIT)


def _mm(a, b):
    return jnp.dot(a.astype(BF16), b.astype(BF16), preferred_element_type=F32)


def _mm_nt(a, b):
    return lax.dot_general(a.astype(BF16), b.astype(BF16), (((1,), (1,)), ((), ())),
                           preferred_element_type=F32)


def _mm_tn(a, b):
    return lax.dot_general(a.astype(BF16), b.astype(BF16), (((0,), (0,)), ((), ())),
                           preferred_element_type=F32)


def _bf16_parts(x):
    hi = x.astype(BF16)
    r1 = x - hi.astype(F32)
    mid = r1.astype(BF16)
    lo = (r1 - mid.astype(F32)).astype(BF16)
    return hi, mid, lo


def _rms(x, g):
    return x * lax.rsqrt(jnp.mean(x * x, axis=-1, keepdims=True) + EPS) * g


def _sigmoid(x):
    return 1.0 / (1.0 + jnp.exp(-x))


def _log_sigmoid(x):
    return jnp.minimum(x, 0.0) - jnp.log(1.0 + jnp.exp(-jnp.abs(x)))


def _gelu_tanh(x):
    return 0.5 * x * (1.0 + jnp.tanh(0.7978845608028654 * (x + 0.044715 * x * x * x)))


def _topk_mask(imp, k):
    col = lax.broadcasted_iota(jnp.int32, imp.shape, 1)
    sel = jnp.zeros(imp.shape, F32)
    work = imp
    for _ in range(k):
        m = jnp.max(work, axis=-1, keepdims=True)
        idx = jnp.min(jnp.where(work == m, col, jnp.int32(2 ** 30)), axis=-1, keepdims=True)
        hit = col == idx
        sel = jnp.where(hit, 1.0, sel)
        work = jnp.where(hit, TAKEN_SCORE, work)
    return sel


def _softmax_parts(s, valid):
    s = jnp.where(valid, s, NEG_INF)
    mx = jnp.max(s, axis=-1, keepdims=True)
    p = jnp.where(valid, jnp.exp(s - mx), 0.0)
    return p, mx, jnp.sum(p, axis=-1, keepdims=True)


def _q_rows_f32(q_ref, hk, nq):
    lane = lax.broadcasted_iota(jnp.int32, (1, LANES), 1)
    own = (lane >= hk * NSA_DH) & (lane < (hk + 1) * NSA_DH)
    parts = []
    for g in range(NSA_GROUP):
        h = hk * NSA_GROUP + g
        blk = q_ref[0, :, (h // 2) * LANES:(h // 2 + 1) * LANES]
        if (h % 2) != hk:
            blk = pltpu.roll(blk, NSA_DH, 1)
        parts.append(blk)
    qh = jnp.concatenate(parts, axis=0)
    return jnp.where(own, qh * (NSA_DH ** -0.5), 0.0), own


def _q_rows(q_ref, hk, nq):
    qh, own = _q_rows_f32(q_ref, hk, nq)
    return qh.astype(BF16), own


def _feature_lanes(hk):
    base = (1 - hk) * NSA_DH
    return base, base + 1, base + 2


def _select_t(imp_t, tq_lane, ns, n_top):
    blk = lax.broadcasted_iota(jnp.int32, (imp_t.shape[0], 1), 0)
    cur = tq_lane >> 6
    forced = (blk == 0) | (blk == cur) | (blk == cur - 1)
    work = jnp.where(forced, FORCE_SCORE, imp_t)
    work = jnp.where(blk * SEL_BLOCK <= tq_lane, work, NEG_INF)
    work = jnp.where(blk < ns, work, PAD_SCORE)
    sel = jnp.zeros(imp_t.shape, F32)
    for _ in range(n_top):
        m = jnp.max(work, axis=0, keepdims=True)
        idx = jnp.min(jnp.where(work == m, blk, jnp.int32(2 ** 30)), axis=0, keepdims=True)
        hit = blk == idx
        sel = jnp.where(hit, 1.0, sel)
        work = jnp.where(hit, TAKEN_SCORE, work)
    return sel


def _row_consts(hk, nq, t0):
    rows = NSA_GROUP * nq
    row = lax.broadcasted_iota(jnp.int32, (rows, 1), 0)
    slope = jnp.full((rows, 1), ALIBI[hk * NSA_GROUP + NSA_GROUP - 1], F32)
    for g in range(NSA_GROUP - 2, -1, -1):
        slope = jnp.where(row < (g + 1) * nq, ALIBI[hk * NSA_GROUP + g], slope)
    tq = t0 + (row & (nq - 1))
    return slope, tq


def _cmp_branch(qh, slope, tq, kc_ref, vc_ref, a_ref, nq):
    ncp = kc_ref.shape[1]
    s = _mm_nt(qh, kc_ref[0])
    cend = lax.broadcasted_iota(jnp.int32, (1, ncp), 1) * CMP_STRIDE + (2 * CMP_STRIDE - 1)
    d = tq - cend
    valid = d >= 0
    p, _, l = _softmax_parts(s - slope * d.astype(F32), valid)
    p = p / jnp.maximum(l, 1e-30)
    o_c = _mm(p, vc_ref[0])
    psum = p[0:nq]
    for g in range(1, NSA_GROUP):
        psum = psum + p[g * nq:(g + 1) * nq]
    return o_c, sum(jnp.dot(part, a_ref[...], preferred_element_type=F32) for part in _bf16_parts(psum))


def _select(imp, tq_q, ns, n_top):
    nsp = imp.shape[1]
    blk = lax.broadcasted_iota(jnp.int32, (1, nsp), 1)
    cur = tq_q >> 6
    forced = (blk == 0) | (blk == cur) | (blk == cur - 1)
    imp = jnp.where(forced, FORCE_SCORE, imp)
    imp = jnp.where(blk * SEL_BLOCK <= tq_q, imp, NEG_INF)
    imp = jnp.where(blk < ns, imp, PAD_SCORE)
    return _topk_mask(imp, n_top)


def _combine_heads(o_c, o_s, o_w, gates, gh_ref, own, hk, nq, out_pairs):
    for g in range(NSA_GROUP):
        h = hk * NSA_GROUP + g
        r = slice(g * nq, (g + 1) * nq)
        og = (gates[:, 3 * h:3 * h + 1] * o_c[r] + gates[:, 3 * h + 1:3 * h + 2] * o_s[r]
              + gates[:, 3 * h + 2:3 * h + 3] * o_w[r])
        og = jnp.where(own, og, 0.0)
        ms = jnp.sum(og * og, axis=-1, keepdims=True) * (1.0 / NSA_DH)
        y = og * lax.rsqrt(ms + EPS) * gh_ref[h:h + 1, :]
        if (h % 2) != hk:
            y = pltpu.roll(y, NSA_DH, 1)
        out_pairs[h // 2] = y if out_pairs[h // 2] is None else out_pairs[h // 2] + y


def _norm_proj_kernel(x_ref, g_ref, *refs, n, n_t):
    h = _rms(x_ref[...], g_ref[...]).astype(BF16)
    has_t = 1 if n_t else 0
    outs = refs[n + has_t:]
    for w_ref, o_ref in zip(refs[:n], outs[:n]):
        o_ref[...] = jnp.dot(h, w_ref[...], preferred_element_type=F32)
    if n_t:
        yt = lax.dot_general(refs[n][...], h, (((1,), (1,)), ((), ())), preferred_element_type=F32)
        for k, o_ref in enumerate(outs[n:]):
            o_ref[0] = yt[k * LANES:(k + 1) * LANES, :]


def _norm_proj(x, g, ws, tm, wt=None, t=None):
    m, d = x.shape
    n = len(ws)
    n_t = 0 if wt is None else wt.shape[0] // LANES
    in_specs = [pl.BlockSpec((tm, d), lambda i: (i, 0)), pl.BlockSpec((1, d), lambda i: (0, 0))]
    in_specs += [pl.BlockSpec(w.shape, lambda i: (0, 0)) for w in ws]
    out_specs = [pl.BlockSpec((tm, w.shape[1]), lambda i: (i, 0)) for w in ws]
    out_shape = [jax.ShapeDtypeStruct((m, w.shape[1]), F32) for w in ws]
    args = [x, g.reshape(1, d), *ws]
    if n_t:
        per_b = t // tm
        in_specs.append(pl.BlockSpec(wt.shape, lambda i: (0, 0)))
        out_specs += [pl.BlockSpec((1, LANES, tm), lambda i: (i // per_b, 0, i % per_b))] * n_t
        out_shape += [jax.ShapeDtypeStruct((m // t, LANES, t), F32)] * n_t
        args.append(wt)
    return pl.pallas_call(
        functools.partial(_norm_proj_kernel, n=n, n_t=n_t),
        grid=(m // tm,),
        in_specs=in_specs,
        out_specs=out_specs,
        out_shape=out_shape,
        compiler_params=_params(("parallel",)),
        name="norm_proj",
    )(*args)


def _compress_kernel(tbl_ref, pool_ref, last_ref, wr_ref, pe_ref, b1_ref, w2_ref, o_ref,
                     buf, xrow, acc, cst, sem, *, ch, nchunks, nbatch, paged, feature_major):
    b = pl.program_id(0)
    j = pl.program_id(1)
    nblk = ch * CMP_PER_PAGE
    half = wr_ref.shape[2] // 2
    npairs = wr_ref.shape[0]

    n = b * nchunks + j
    n_steps = nbatch * nchunks
    cur = n % 2

    def page_copy(bb, k, buf_i, slot):
        if paged:
            src = pool_ref.at[tbl_ref[bb, k]]
        else:
            src = pool_ref.at[bb, :, pl.ds(pl.multiple_of(k * PAGE, PAGE), PAGE)]
        return pltpu.make_async_copy(src, buf.at[buf_i, slot], sem.at[buf_i, slot])

    def step_copies(m, fn):
        bb, jj, buf_i = m // nchunks, m % nchunks, m % 2
        for s in range(ch):
            fn(page_copy(bb, jj * ch + s, buf_i, s))

        @pl.when(jj < nchunks - 1)
        def _():
            fn(page_copy(bb, jnp.minimum((jj + 1) * ch, nchunks * ch - 1), buf_i, ch))

    def to_rows(m, i):
        bb, jj = jnp.minimum(m // nchunks, nbatch - 1), m % nchunks
        for s in range(ch + 1):
            page = buf[i, s]
            rows = page.T if feature_major else page
            if s == ch:
                rows = jnp.where(jj < nchunks - 1, rows, last_ref[bb])
            xrow[i, pl.ds(s * PAGE, PAGE), :] = rows

    @pl.when(n == 0)
    def _():
        if n_steps == 1:
            buf[1] = jnp.zeros(buf.shape[1:], F32)
        buf[0, ch] = jnp.zeros((PAGE, LANES), F32)
        buf[1, ch] = jnp.zeros((PAGE, LANES), F32)
        step_copies(0, lambda c: c.start())
        if n_steps > 1:
            step_copies(1, lambda c: c.start())
        c = jnp.zeros((8, half), F32)
        for pr in range(npairs):
            c = c + _mm(jnp.broadcast_to(pe_ref[0, pr:pr + 1, :], (8, 2 * LANES)), wr_ref[pr, :, 0:half])
            c = c + _mm(jnp.broadcast_to(pe_ref[1, pr:pr + 1, :], (8, 2 * LANES)), wr_ref[pr, :, half:2 * half])
        cst[...] = c + b1_ref[...]
        step_copies(0, lambda c: c.wait())
        to_rows(0, 0)

    @pl.when(n + 2 < n_steps)
    def _():
        step_copies(n + 2, lambda c: c.start())

    @pl.when(n + 1 < n_steps)
    def _():
        step_copies(n + 1, lambda c: c.wait())

    def compute(par):
        to_rows(n + 1, 1 - par)
        total = None
        for pr in range(npairs):
            x0 = xrow[par, pl.ds(2 * pr, nblk + 8, stride=CMP_STRIDE), :]
            x1 = xrow[par, pl.ds(2 * pr + 1, nblk + 8, stride=CMP_STRIDE), :]
            prod = _mm(jnp.concatenate([x0, x1], axis=1), wr_ref[pr])
            total = prod if total is None else total + prod
        acc[...] = total
        hid = acc[0:nblk, 0:half] + acc[pl.ds(1, nblk), half:2 * half] + cst[0:1, :]
        o_ref[0] = _mm(_gelu_tanh(hid), w2_ref[...])

    for par in range(2):
        pl.when(cur == par)(functools.partial(compute, par))


def _compress(table, pool, last_next, pe, w1, b1, w2, ch, paged, feature_major):
    nb, npg = table.shape
    nchunks = npg // ch
    hid = w1.shape[1]
    dh = NSA_DH
    wa = w1[:CMP_STRIDE * dh].reshape(CMP_STRIDE, dh, hid)
    wb = w1[CMP_STRIDE * dh:].reshape(CMP_STRIDE, dh, hid)
    z = jnp.zeros_like(wa)
    wr = jnp.concatenate([jnp.concatenate([wa, z, wb, z], axis=2),
                          jnp.concatenate([z, wa, z, wb], axis=2)], axis=1).astype(BF16)
    wr = wr.reshape(CMP_STRIDE // 2, 2 * LANES, 4 * hid)
    pe1 = jnp.concatenate([pe[:CMP_STRIDE], pe[:CMP_STRIDE]], axis=1).reshape(CMP_STRIDE // 2, 2 * LANES)
    pe2 = jnp.concatenate([pe[CMP_STRIDE:], pe[CMP_STRIDE:]], axis=1).reshape(CMP_STRIDE // 2, 2 * LANES)
    pes = jnp.stack([pe1, pe2])
    b1t = jnp.concatenate([b1, b1]).reshape(1, 2 * hid)
    zz = jnp.zeros_like(w2)
    w2bd = jnp.concatenate([jnp.concatenate([w2, zz], axis=1),
                            jnp.concatenate([zz, w2], axis=1)], axis=0).astype(BF16)
    nblk = ch * CMP_PER_PAGE
    grid_spec = pltpu.PrefetchScalarGridSpec(
        num_scalar_prefetch=1,
        grid=(nb, nchunks),
        in_specs=[pl.BlockSpec(memory_space=pl.ANY),
                  pl.BlockSpec(last_next.shape, lambda b, j, t: (0, 0, 0)),
                  pl.BlockSpec(wr.shape, lambda b, j, t: (0, 0, 0)),
                  pl.BlockSpec(pes.shape, lambda b, j, t: (0, 0, 0)),
                  pl.BlockSpec(b1t.shape, lambda b, j, t: (0, 0)),
                  pl.BlockSpec(w2bd.shape, lambda b, j, t: (0, 0))],
        out_specs=pl.BlockSpec((1, nblk, LANES), lambda b, j, t: (b, j, 0)),
        scratch_shapes=[pltpu.VMEM((2, ch + 1, PAGE, LANES), F32),
                        pltpu.VMEM((2, (ch + 1) * PAGE, LANES), F32),
                        pltpu.VMEM((nblk + 8, 4 * hid), F32),
                        pltpu.VMEM((8, 2 * hid), F32),
                        pltpu.SemaphoreType.DMA((2, ch + 1))],
    )
    return pl.pallas_call(
        functools.partial(_compress_kernel, ch=ch, nchunks=nchunks, nbatch=nb, paged=paged,
                          feature_major=feature_major),
        grid_spec=grid_spec,
        out_shape=jax.ShapeDtypeStruct((nb, npg * CMP_PER_PAGE, LANES), F32),
        compiler_params=_params(("arbitrary", "arbitrary")),
        name="compress",
    )(table, pool, last_next, wr, pes, b1t, w2bd)


def _sel_matrix(nc_rows, nc_valid, ns_cols):
    a = np.zeros((nc_rows, ns_cols), np.float32)
    for j in range(nc_valid // 4):
        for c, wgt in ((4 * j - 1, 0.5), (4 * j, 1.0), (4 * j + 1, 1.0), (4 * j + 2, 1.0), (4 * j + 3, 0.5)):
            if 0 <= c < nc_valid:
                a[c, j] += wgt
    return jnp.asarray(a)


def _head_feature_rows(hk, g, n):
    slope = ALIBI[hk * NSA_GROUP + g]
    row = lax.broadcasted_iota(jnp.int32, (NSA_DH, n), 0)
    return jnp.where(row == 0, slope * SEL_BLOCK,
                     jnp.where(row == 1, slope, jnp.where(row == 2, slope * CMP_STRIDE, 0.0)))


def _swap_halves(x):
    return jnp.concatenate([x[NSA_DH:], x[:NSA_DH]], axis=0)


def _nsa_prompt_t_kernel(qi_ref, ci_ref, q_ref, sm_ref, kc_ref, vc_ref, ks_ref, vs_ref, kw_ref, vw_ref, at_ref,
                         bg_ref, ght_ref, o_ref, ksr, kwr, vsa, vwa, kca, vct, qa_s, oc_s, ow_s, m_s, acc_s, used_ref,
                         *, nq, ns, n_top, kchunk, wpad, cgroup):
    step = pl.program_id(1)
    i = qi_ref[step]
    grp = ci_ref[step]
    c = grp
    t0 = pl.multiple_of(i * nq, nq)
    cols = NSA_GROUP * nq
    t = ks_ref.shape[2]
    nc = kc_ref.shape[1]
    nsr = at_ref.shape[0]
    wkeys = wpad + nq
    last = (t0 + nq + kchunk - 1) // kchunk - 1
    lane = lax.broadcasted_iota(jnp.int32, (1, LANES), 1)
    tq_q = t0 + lax.broadcasted_iota(jnp.int32, (1, nq), 1)
    tq = jnp.concatenate([tq_q] * NSA_GROUP, axis=1)

    @pl.when(step == 0)
    def _():
        srow = lax.broadcasted_iota(jnp.int32, (LANES, 1), 0)
        cidx = lax.broadcasted_iota(jnp.int32, (nc, 1), 0).astype(F32)
        vct_f = jnp.transpose(vc_ref[0]).astype(BF16)
        for hk in range(NSA_KV_HEADS):
            own = (lane >= hk * NSA_DH) & (lane < (hk + 1) * NSA_DH)
            f0, f1, f2 = _feature_lanes(hk)
            kca[hk] = jnp.where(own, kc_ref[0], jnp.where(lane == f2, cidx, 0.0)).astype(BF16)
            vct[hk] = vct_f
            kwr[hk, 0:wpad, :] = jnp.broadcast_to(jnp.where(lane == f0, -1e30, 0.0), (wpad, LANES)).astype(BF16)
            vwa[hk, :, 0:wpad] = jnp.zeros((LANES, wpad), BF16)

        def prep(r, _):
            r0 = pl.multiple_of(r * LANES, LANES)
            cs = pl.ds(r0, LANES)
            pos = r0 + lax.broadcasted_iota(jnp.int32, (LANES, 1), 0)
            blk = pos >> 6
            ks_rows = jnp.transpose(ks_ref[0, :, cs])
            kw_rows = jnp.transpose(kw_ref[0, :, cs])
            for hk in range(NSA_KV_HEADS):
                own = (lane >= hk * NSA_DH) & (lane < (hk + 1) * NSA_DH)
                own_r = (srow >= hk * NSA_DH) & (srow < (hk + 1) * NSA_DH)
                f0, f1, f2 = _feature_lanes(hk)
                feat = jnp.where(lane == f0, blk.astype(F32), jnp.where(lane == f1, (pos & 63).astype(F32), 0.0))
                ksr[hk, cs, 0:LANES] = jnp.where(own, ks_rows, feat).astype(BF16)
                ksr[hk, cs, LANES:2 * LANES] = (lane == blk).astype(BF16)
                kwr[hk, pl.ds(wpad + r0, LANES), :] = jnp.where(own, kw_rows, feat).astype(BF16)
                vsa[hk, :, cs] = jnp.where(own_r, vs_ref[0, :, cs], 1.0).astype(BF16)
                vwa[hk, :, pl.ds(wpad + r0, LANES)] = jnp.where(own_r, vw_ref[0, :, cs], 1.0).astype(BF16)
            return 0

        lax.fori_loop(0, t // LANES, prep, 0)

    @pl.when(c == 0)
    def _():
        qt = jnp.transpose(q_ref[0]) * (NSA_DH ** -0.5)
        cend = lax.broadcasted_iota(jnp.int32, (nc, 1), 0) * CMP_STRIDE + (2 * CMP_STRIDE - 1)
        wrow = lax.broadcasted_iota(jnp.int32, (LANES, 1), 0)
        sel_heads = []
        for hk in range(NSA_KV_HEADS):
            slabs = []
            for g in range(NSA_GROUP):
                h = hk * NSA_GROUP + g
                halves = [qt[h * NSA_DH:(h + 1) * NSA_DH], _head_feature_rows(hk, g, nq)]
                slabs.append(jnp.concatenate(halves if hk == 0 else halves[::-1], axis=0))
            qlo = jnp.concatenate(slabs, axis=1).astype(BF16)
            s = jnp.dot(kca[hk], qlo, preferred_element_type=F32)
            valid = cend <= tq
            s = jnp.where(valid, s, NEG_INF)
            p = jnp.where(valid, jnp.exp(s - jnp.max(s, axis=0, keepdims=True)), 0.0)
            p = p / jnp.maximum(jnp.sum(p, axis=0, keepdims=True), 1e-30)
            oc_s[hk] = jnp.dot(vct[hk], p.astype(BF16), preferred_element_type=F32)
            psum = p[:, 0:nq]
            for g in range(1, NSA_GROUP):
                psum = psum + p[:, g * nq:(g + 1) * nq]
            imp_t = sum(jnp.dot(at_ref[...], part, preferred_element_type=F32)
                        for part in _bf16_parts(psum))
            sel_t = _select_t(imp_t, tq_q, ns, n_top)
            sel_heads.append(sel_t)
            bias = (sel_t - 1.0) * 1e30
            bias = jnp.concatenate([bias, jnp.zeros((LANES - nsr, nq), F32)], axis=0) if nsr < LANES else bias
            qa_s[hk, 0:LANES, :] = qlo
            qa_s[hk, LANES:2 * LANES, :] = jnp.concatenate([bias] * NSA_GROUP, axis=1).astype(BF16)
            s = jnp.dot(kwr[hk, pl.ds(t0, wkeys), :], qlo, preferred_element_type=F32)
            old = jnp.where(t0 - wpad + wrow > tq - WINDOW, s[0:LANES], NEG_INF)
            new = jnp.where(t0 + wrow[0:nq] <= tq, s[wkeys - nq:wkeys], NEG_INF)
            s = jnp.concatenate([old, s[LANES:wkeys - nq], new], axis=0)
            p = jnp.exp(s - jnp.max(s, axis=0, keepdims=True)).astype(BF16)
            ow = jnp.dot(vwa[hk, :, pl.ds(t0, wkeys)], p, preferred_element_type=F32)
            l_w = _swap_halves(ow)
            ow_s[hk] = ow / jnp.maximum(l_w, 1e-30)
            m_s[hk] = jnp.full((8, cols), NEG_INF, F32)
            acc_s[hk] = jnp.zeros((LANES, cols), F32)
        any_sel = sel_heads[0]
        for s_t in sel_heads[1:]:
            any_sel = jnp.maximum(any_sel, s_t)
        per_chunk = kchunk // SEL_BLOCK
        for cc in range(t // kchunk):
            used_ref[cc] = (jnp.max(any_sel[cc * per_chunk:(cc + 1) * per_chunk, :]) > 0.5).astype(jnp.int32)

    def flash(causal, ck):
        k0 = pl.multiple_of(ck * kchunk, kchunk)
        for hk in range(NSA_KV_HEADS):
            s = jnp.dot(ksr[hk, pl.ds(k0, kchunk), :], qa_s[hk], preferred_element_type=F32)
            if causal:
                s = jnp.where(k0 + lax.broadcasted_iota(jnp.int32, (kchunk, 1), 0) <= tq, s, NEG_INF)
            m_old = m_s[hk][0:1, :]
            m_new = jnp.maximum(m_old, jnp.max(s, axis=0, keepdims=True))
            p = jnp.exp(s - m_new).astype(BF16)
            acc_s[hk] = jnp.exp(m_old - m_new) * acc_s[hk] + jnp.dot(vsa[hk, :, pl.ds(k0, kchunk)], p,
                                                                   preferred_element_type=F32)
            m_s[hk] = jnp.broadcast_to(m_new, (8, cols))

    for slot in range(cgroup):
        ck = grp * cgroup + slot
        used = used_ref[jnp.minimum(ck, t // kchunk - 1)] > 0
        pl.when((ck < last) & used)(functools.partial(flash, False, ck))
        pl.when(ck == last)(functools.partial(flash, True, ck))

    @pl.when((grp + 1) * cgroup > last)
    def _():
        gates = jnp.transpose(_sigmoid(sm_ref[0] + bg_ref[...]))
        srow = lax.broadcasted_iota(jnp.int32, (LANES, 1), 0)
        halves = [None] * NSA_HEADS
        for hk in range(NSA_KV_HEADS):
            own_r = (srow >= hk * NSA_DH) & (srow < (hk + 1) * NSA_DH)
            acc = acc_s[hk]
            o_s = acc / jnp.maximum(_swap_halves(acc), 1e-30)
            o_c = oc_s[hk]
            o_w = ow_s[hk]
            for g in range(NSA_GROUP):
                h = hk * NSA_GROUP + g
                cs = slice(g * nq, (g + 1) * nq)
                og = (gates[3 * h:3 * h + 1] * o_c[:, cs] + gates[3 * h + 1:3 * h + 2] * o_s[:, cs]
                      + gates[3 * h + 2:3 * h + 3] * o_w[:, cs])
                og = jnp.where(own_r, og, 0.0)
                ms = jnp.sum(og * og, axis=0, keepdims=True) * (1.0 / NSA_DH)
                y = og * lax.rsqrt(ms + EPS) * ght_ref[:, h:h + 1]
                halves[h] = y[hk * NSA_DH:(hk + 1) * NSA_DH]
        for k in range(NSA_HEADS // 2):
            o_ref[0, :, k * LANES:(k + 1) * LANES] = jnp.transpose(
                jnp.concatenate([halves[2 * k], halves[2 * k + 1]], axis=0))


def _nsa_prompt_t(q, sm, k_cmp, v_cmp, ks, vs, kw, vw, bg, gh, nq):
    b, t, _ = q.shape
    nc = k_cmp.shape[1]
    ns = t // SEL_BLOCK
    nsr = -(-ns // 8) * 8
    assert nq == LANES and nsr <= LANES and nc <= 256 and t % 512 == 0, "prompt NSA kernel shape limits"
    at = _sel_matrix(nc, nc, nsr).T.astype(BF16)
    kchunk = min(512, t)
    wpad = WINDOW
    cgroup = t // kchunk
    pairs = [(i, g) for i in range(t // nq) for g in range(-(-(-(-((i + 1) * nq) // kchunk)) // cgroup))]
    qi = jnp.asarray(np.array([p[0] for p in pairs], np.int32))
    ci = jnp.asarray(np.array([p[1] for p in pairs], np.int32))
    cols = NSA_GROUP * nq
    full = lambda n: pl.BlockSpec((1, n, LANES), lambda bi, s, qi, ci: (bi, 0, 0))
    full_t = pl.BlockSpec((1, LANES, t), lambda bi, s, qi, ci: (bi, 0, 0))
    const = lambda shape: pl.BlockSpec(shape, lambda bi, s, qi, ci: (0,) * len(shape))
    tok = lambda w: pl.BlockSpec((1, nq, w), lambda bi, s, qi, ci: (bi, qi[s], 0))
    grid_spec = pltpu.PrefetchScalarGridSpec(
        num_scalar_prefetch=2,
        grid=(b, len(pairs)),
        in_specs=[tok(4 * LANES), tok(LANES), full(nc), full(nc), full_t, full_t, full_t, full_t,
                  const(at.shape), const((1, LANES)), const((LANES, NSA_HEADS))],
        out_specs=tok(4 * LANES),
        scratch_shapes=[pltpu.VMEM((NSA_KV_HEADS, t, 2 * LANES), BF16),
                        pltpu.VMEM((NSA_KV_HEADS, wpad + t, LANES), BF16),
                        pltpu.VMEM((NSA_KV_HEADS, LANES, t), BF16),
                        pltpu.VMEM((NSA_KV_HEADS, LANES, wpad + t), BF16),
                        pltpu.VMEM((NSA_KV_HEADS, nc, LANES), BF16),
                        pltpu.VMEM((NSA_KV_HEADS, LANES, nc), BF16),
                        pltpu.VMEM((NSA_KV_HEADS, 2 * LANES, cols), BF16),
                        pltpu.VMEM((NSA_KV_HEADS, LANES, cols), F32),
                        pltpu.VMEM((NSA_KV_HEADS, LANES, cols), F32),
                        pltpu.VMEM((NSA_KV_HEADS, 8, cols), F32),
                        pltpu.VMEM((NSA_KV_HEADS, LANES, cols), F32),
                        pltpu.SMEM((t // kchunk,), jnp.int32)],
    )
    return pl.pallas_call(
        functools.partial(_nsa_prompt_t_kernel, nq=nq, ns=ns, n_top=min(SEL_TOPK, ns), kchunk=kchunk, wpad=wpad,
                          cgroup=cgroup),
        grid_spec=grid_spec,
        out_shape=jax.ShapeDtypeStruct((b, t, 4 * LANES), F32),
        compiler_params=_params(("arbitrary", "arbitrary")),
        name="nsa_prompt",
    )(qi, ci, q, sm, k_cmp, v_cmp, ks, vs, kw, vw, at, bg, gh.T)


def _chunk_features(nkeys):
    key = np.arange(nkeys)
    c = np.zeros((LANES, nkeys), np.float32)
    c[key // SEL_BLOCK, key] = 1.0
    c[NSA_DH, :] = key // SEL_BLOCK
    c[NSA_DH + 1, :] = key % SEL_BLOCK
    c[NSA_DH + 2, :] = 1.0
    return jnp.asarray(c, BF16)


def _nsa_sample2_kernel(pt_ref, q_ref, sm_ref, kc_ref, vc_ref, ks_pool, vs_pool, ksn_ref, vsn_ref,
                        wk_ref, wv_ref, kwn_ref, vwn_ref, a_ref, cf_ref, bg_ref, gh_ref, o_ref,
                        kbuf, vbuf, sem, q_s, bias_s, selnew_s, oc_s, ow_s, m_s, l_s, acc_s,
                        *, ch, nchunks, offset, ns, n_top, kchunk):
    b = pl.program_id(0)
    j = pl.program_id(1)
    nq = q_ref.shape[1]
    rows = NSA_GROUP * nq
    w_pre = wk_ref.shape[2]
    nkeys = ch * PAGE
    n = b * nchunks + j
    cur = n % 2
    lane = lax.broadcasted_iota(jnp.int32, (1, LANES), 1)

    def step_copies(bb, jj, buf_i, fn):
        for s in range(ch):
            page = pt_ref[bb, jj * ch + s]
            dst = pl.ds(s * PAGE, PAGE)
            fn(pltpu.make_async_copy(ks_pool.at[page], kbuf.at[buf_i, :, dst], sem.at[0, buf_i]))
            fn(pltpu.make_async_copy(vs_pool.at[page], vbuf.at[buf_i, :, dst], sem.at[1, buf_i]))

    @pl.when(n == 0)
    def _():
        step_copies(b, j, cur, lambda c: c.start())

    @pl.when(n + 1 < pl.num_programs(0) * nchunks)
    def _():
        wrap = j + 1 == nchunks
        step_copies(jnp.where(wrap, b + 1, b), jnp.where(wrap, 0, j + 1), 1 - cur, lambda c: c.start())

    @pl.when(j == 0)
    def _():
        qhs, imps = [], []
        for hk in range(NSA_KV_HEADS):
            rs = slice(hk * rows, (hk + 1) * rows)
            qh, _ = _q_rows(q_ref, hk, nq)
            slope, tq = _row_consts(hk, nq, offset)
            o_c, imp = _cmp_branch(qh, slope, tq, kc_ref, vc_ref, a_ref, nq)
            q_s[rs, :] = qh
            oc_s[rs, :] = o_c
            qhs.append(qh)
            imps.append(imp)
        _, tq = _row_consts(0, nq, offset)
        sel_all = _select(jnp.concatenate(imps, axis=0), jnp.concatenate([tq[0:nq]] * NSA_KV_HEADS, axis=0),
                          ns, n_top)
        for hk in range(NSA_KV_HEADS):
            rs = slice(hk * rows, (hk + 1) * rows)
            qh = qhs[hk]
            slope, tq = _row_consts(hk, nq, offset)
            sel_rows = jnp.concatenate([sel_all[hk * nq:(hk + 1) * nq]] * NSA_GROUP, axis=0)
            for jj in range(nchunks):
                nb_c = nkeys // SEL_BLOCK
                blocks = sel_rows[:, jj * nb_c:jj * nb_c + LANES]
                upper = jnp.where(lane < nb_c, (blocks - 1.0) * 1e30,
                                  jnp.where(lane == NSA_DH, slope * SEL_BLOCK,
                                            jnp.where(lane == NSA_DH + 1, slope,
                                                      jnp.where(lane == NSA_DH + 2,
                                                                slope * float(jj * nkeys - offset), 0.0))))
                bias_s[jj, rs, :] = upper.astype(BF16)
            new_blk = offset // SEL_BLOCK
            selnew_s[rs, :] = jnp.broadcast_to(sel_rows[:, new_blk:new_blk + 1], (rows, LANES))
            pos1 = offset - w_pre + lax.broadcasted_iota(jnp.int32, (1, w_pre), 1)
            d1 = tq - pos1
            pos2 = offset + lax.broadcasted_iota(jnp.int32, (1, kwn_ref.shape[1]), 1)
            d2 = tq - pos2
            v1 = (d1 >= 0) & (d1 < WINDOW)
            v2 = (d2 >= 0) & (d2 < WINDOW)
            s1 = jnp.where(v1, _mm(qh, wk_ref[0]) - slope * d1.astype(F32), NEG_INF)
            s2 = jnp.where(v2, _mm_nt(qh, kwn_ref[0]) - slope * d2.astype(F32), NEG_INF)
            mx = jnp.maximum(jnp.max(s1, axis=-1, keepdims=True), jnp.max(s2, axis=-1, keepdims=True))
            p1 = jnp.where(v1, jnp.exp(s1 - mx), 0.0)
            p2 = jnp.where(v2, jnp.exp(s2 - mx), 0.0)
            l_w = jnp.sum(p1, axis=-1, keepdims=True) + jnp.sum(p2, axis=-1, keepdims=True)
            ow_s[rs, :] = (_mm_nt(p1, wv_ref[0]) + _mm(p2, vwn_ref[0])) / jnp.maximum(l_w, 1e-30)
        m_s[...] = jnp.full(m_s.shape, NEG_INF, F32)
        l_s[...] = jnp.zeros(l_s.shape, F32)
        acc_s[...] = jnp.zeros(acc_s.shape, F32)

    step_copies(b, j, cur, lambda c: c.wait())

    qa = jnp.concatenate([q_s[...], bias_s[j]], axis=1)
    tiles = []
    for c in range(nkeys // kchunk):
        cs = pl.ds(c * kchunk, kchunk)
        ka = jnp.concatenate([kbuf[cur, :, cs].astype(BF16), cf_ref[:, cs]], axis=0)
        tiles.append(jnp.dot(qa, ka, preferred_element_type=F32))
    s = jnp.concatenate(tiles, axis=1)
    m = m_s[:, 0:1]
    m_new = jnp.maximum(m, jnp.max(s, axis=-1, keepdims=True))
    alpha = jnp.exp(m - m_new)
    p = jnp.exp(s - m_new)
    m_s[...] = jnp.broadcast_to(m_new, m_s.shape)
    l_s[...] = jnp.broadcast_to(alpha * l_s[:, 0:1] + jnp.sum(p, axis=-1, keepdims=True), l_s.shape)
    acc_s[...] = alpha * acc_s[...] + _mm_nt(p, vbuf[cur])

    @pl.when(j == nchunks - 1)
    def _():
        gates = _sigmoid(sm_ref[0] + bg_ref[...])
        out_pairs = [None] * (NSA_HEADS // 2)
        nnew = ksn_ref.shape[1]
        r_new = lax.broadcasted_iota(jnp.int32, (1, nnew), 1)
        for hk in range(NSA_KV_HEADS):
            rs = slice(hk * rows, (hk + 1) * rows)
            own = (lane >= hk * NSA_DH) & (lane < (hk + 1) * NSA_DH)
            slope, tq = _row_consts(hk, nq, 0)
            valid = (r_new <= tq) & (selnew_s[rs, 0:1] > 0.5)
            s = jnp.where(valid, _mm_nt(q_s[rs, :], ksn_ref[0]) + slope * r_new.astype(F32), NEG_INF)
            m_o = m_s[rs, 0:1]
            m_f = jnp.maximum(m_o, jnp.max(s, axis=-1, keepdims=True))
            alpha = jnp.exp(m_o - m_f)
            p = jnp.where(valid, jnp.exp(s - m_f), 0.0)
            l_f = alpha * l_s[rs, 0:1] + jnp.sum(p, axis=-1, keepdims=True)
            o_s = (alpha * acc_s[rs, :] + _mm(p, vsn_ref[0])) / jnp.maximum(l_f, 1e-30)
            _combine_heads(oc_s[rs, :], o_s, ow_s[rs, :], gates, gh_ref, own, hk, nq, out_pairs)
        for k, y in enumerate(out_pairs):
            o_ref[0, :, k * LANES:(k + 1) * LANES] = y


def _nsa_sample2(page_table, q, sm, k_cmp, v_cmp, ks_pool, vs_pool, ks_new, vs_new, win_k, win_v, kw_new, vw_new,
                 bg, gh, nc_valid, ch):
    b, nq, _ = q.shape
    npg = page_table.shape[1]
    nchunks = npg // ch
    offset = npg * PAGE
    ncp = k_cmp.shape[1]
    ns = nc_valid // 4
    nsp = -(-(ns + LANES) // LANES) * LANES
    assert ch * PAGE // SEL_BLOCK <= NSA_DH, "a chunk's blocks must fit the 64 mask lanes"
    a = _sel_matrix(ncp, nc_valid, nsp).astype(BF16)
    rows2 = NSA_KV_HEADS * NSA_GROUP * nq
    kchunk = min(512, ch * PAGE)
    cf = _chunk_features(ch * PAGE)
    per_b = lambda n, w=LANES: pl.BlockSpec((1, n, w), lambda bi, j, t: (bi, 0, 0))
    const = lambda shape: pl.BlockSpec(shape, lambda bi, j, t: (0,) * len(shape))
    grid_spec = pltpu.PrefetchScalarGridSpec(
        num_scalar_prefetch=1,
        grid=(b, nchunks),
        in_specs=[per_b(nq, 4 * LANES), per_b(nq), per_b(ncp), per_b(ncp),
                  pl.BlockSpec(memory_space=pl.ANY), pl.BlockSpec(memory_space=pl.ANY),
                  per_b(PAGE), per_b(PAGE), per_b(LANES, win_k.shape[2]), per_b(LANES, win_v.shape[2]),
                  per_b(PAGE), per_b(PAGE),
                  const(a.shape), const(cf.shape), const((1, LANES)), const((NSA_HEADS, LANES))],
        out_specs=per_b(nq, 4 * LANES),
        scratch_shapes=[pltpu.VMEM((2, LANES, ch * PAGE), F32), pltpu.VMEM((2, LANES, ch * PAGE), F32),
                        pltpu.SemaphoreType.DMA((2, 2)),
                        pltpu.VMEM((rows2, LANES), BF16),
                        pltpu.VMEM((nchunks, rows2, LANES), BF16)]
        + [pltpu.VMEM((rows2, LANES), F32)] * 6,
    )
    return pl.pallas_call(
        functools.partial(_nsa_sample2_kernel, ch=ch, nchunks=nchunks, offset=offset, ns=ns,
                          n_top=min(SEL_TOPK, ns), kchunk=kchunk),
        grid_spec=grid_spec,
        out_shape=jax.ShapeDtypeStruct((b, nq, 4 * LANES), F32),
        compiler_params=_params(("arbitrary", "arbitrary")),
        name="nsa_sample",
    )(page_table, q, sm, k_cmp, v_cmp, ks_pool, vs_pool, ks_new, vs_new, win_k, win_v, kw_new, vw_new, a, cf, bg, gh)


def _mlstm_kernel(xm_ref, vm_ref, om_ref, sm_ref, smt_ref, conv0_ref, cw_ref, cb_ref, wq_ref, wk_ref,
                  bcol_ref, brow_ref, gh_ref, c0_ref, n0_ref, m0_ref,
                  hm_ref, c_ref, n_ref, m_ref, xs_ref, *, L, t_valid):
    c = pl.program_id(1)
    dh = MLSTM_DH

    @pl.when(c == 0)
    def _():
        xs_ref[0:8, :] = jnp.zeros((8, xs_ref.shape[1]), F32)
        xs_ref[5:8, :] = conv0_ref[0]
        c_ref[...] = c0_ref[...]
        n_ref[...] = n0_ref[...]
        m_ref[...] = m0_ref[...]

    xs_ref[8:8 + L, :] = xm_ref[0]
    xc = cb_ref[...]
    for jj in range(4):
        xc = xc + cw_ref[jj:jj + 1, :] * xs_ref[pl.ds(5 + jj, L), :]
    xc = xc * _sigmoid(xc)
    xs_ref[0:8, :] = xs_ref[L:L + 8, :]

    pre_col = sm_ref[0] + brow_ref[...]
    pre_row = smt_ref[0, 0] + bcol_ref[...]
    lf_col = _log_sigmoid(pre_col)
    lf_row = _log_sigmoid(pre_row)
    if t_valid < L:
        rid = lax.broadcasted_iota(jnp.int32, (L, 1), 0) < t_valid
        cid = lax.broadcasted_iota(jnp.int32, (1, L), 1) < t_valid
        lf_col = jnp.where(rid, lf_col, 0.0)
        lf_row = jnp.where(cid, lf_row, 0.0)
        pre_col = jnp.where(rid, pre_col, NEG_INF)
        pre_row = jnp.where(cid, pre_row, NEG_INF)
    ri = lax.broadcasted_iota(jnp.int32, (L, L), 0)
    ci = lax.broadcasted_iota(jnp.int32, (L, L), 1)
    causal = ci <= ri
    tril = causal.astype(BF16)
    triu = (ri <= ci).astype(BF16)
    bcum_col = sum(jnp.dot(tril, part, preferred_element_type=F32) for part in _bf16_parts(lf_col))
    bcum_row = sum(jnp.dot(part, triu, preferred_element_type=F32) for part in _bf16_parts(lf_row))

    for h in range(MLSTM_HEADS):
        hs = slice(h * dh, (h + 1) * dh)
        xh = xc[:, hs]
        q = _mm(xh, wq_ref[h])
        k = _mm(xh, wk_ref[h]) * (dh ** -0.5)
        v = vm_ref[0, :, hs]
        bc = bcum_col[:, 28 + h:29 + h]
        ic = pre_col[:, 24 + h:25 + h]
        br = bcum_row[4 + h:5 + h, :]
        ir = pre_row[h:h + 1, :]
        mh = m_ref[0, :, h:h + 1]
        ch_ = c_ref[0, h]
        nh = n_ref[0, h:h + 1, :]
        dmat = jnp.where(causal, bc - br + ir, NEG_INF)
        inter = bc + mh
        mq = jnp.maximum(inter, jnp.max(dmat, axis=1, keepdims=True))
        a = jnp.exp(dmat - mq) * _mm_nt(q, k)
        wi = jnp.exp(inter - mq)
        num = _mm(a, v) + wi * _mm_nt(q, ch_)
        den = jnp.sum(a, axis=1, keepdims=True) + wi * jnp.sum(q * nh, axis=1, keepdims=True)
        hout = num / jnp.maximum(jnp.abs(den), jnp.exp(-mq))
        btot = bc[L - 1:L, :]
        dec_r = btot - br + ir
        dec_c = btot - bc + ic
        m_new = jnp.maximum(btot + mh, jnp.max(dec_r, axis=1, keepdims=True))
        ws_c = jnp.exp(dec_c - m_new)
        w_c = jnp.exp(btot + mh - m_new)
        c_ref[0, h] = w_c * ch_ + _mm_tn(v * ws_c, k)
        n_ref[0, h:h + 1, :] = w_c * nh + jnp.sum(k * ws_c, axis=0, keepdims=True)
        m_ref[0, :, h:h + 1] = m_new
        y = _rms(hout, gh_ref[h:h + 1, :]) * _sigmoid(om_ref[0, :, hs])
        hm_ref[0, :, hs] = y


def _mlstm(xm, vm, om, sm, conv0, conv_w, conv_b, w_qm, w_km, b_i, b_f, g_head, c0, n0, m0, L, t_valid):
    b, t, w = xm.shape
    nck = t // L
    smt = sm[:, :, 24:32].reshape(b, nck, L, 8).transpose(0, 1, 3, 2)
    brow = jnp.zeros((1, LANES), F32).at[0, 24:28].set(b_i).at[0, 28:32].set(b_f)
    bcol = jnp.concatenate([b_i, b_f]).reshape(8, 1)
    m0p = jnp.zeros((b, 1, LANES), F32).at[:, 0, :MLSTM_HEADS].set(m0)
    tok = lambda: pl.BlockSpec((1, L, w), lambda bi, c: (bi, c, 0))
    const = lambda shape: pl.BlockSpec(shape, lambda bi, c: (0,) * len(shape))
    state = lambda shape: pl.BlockSpec((1,) + shape, lambda bi, c: (bi,) + (0,) * len(shape))
    hm, c_out, n_out, m_out = pl.pallas_call(
        functools.partial(_mlstm_kernel, L=L, t_valid=t_valid),
        grid=(b, nck),
        in_specs=[tok(), tok(), tok(),
                  pl.BlockSpec((1, L, LANES), lambda bi, c: (bi, c, 0)),
                  pl.BlockSpec((1, 1, 8, L), lambda bi, c: (bi, c, 0, 0)),
                  state(conv0.shape[1:]),
                  const(conv_w.shape), const((1, w)), const(w_qm.shape), const(w_km.shape),
                  const((8, 1)), const((1, LANES)), const(g_head.shape),
                  state(c0.shape[1:]), state(n0.shape[1:]), state((1, LANES))],
        out_specs=[tok(), state(c0.shape[1:]), state(n0.shape[1:]), state((1, LANES))],
        out_shape=[jax.ShapeDtypeStruct((b, t, w), F32), jax.ShapeDtypeStruct(c0.shape, F32),
                   jax.ShapeDtypeStruct(n0.shape, F32), jax.ShapeDtypeStruct((b, 1, LANES), F32)],
        scratch_shapes=[pltpu.VMEM((L + 8, w), F32)],
        compiler_params=_params(("arbitrary", "arbitrary")),
        name="mlstm",
    )(xm, vm, om, sm, smt, conv0, conv_w, conv_b.reshape(1, w), w_qm.astype(BF16), w_km.astype(BF16),
      bcol, brow, g_head, c0, n0, m0p)
    return hm, c_out, n_out, m_out[:, 0, :MLSTM_HEADS]


def _mix_kernel(x_ref, on_ref, hm_ref, wo1_ref, wo2_ref, gxa_ref, wxq_ref, x1_ref, qx_ref):
    x1 = x_ref[...] + _mm(on_ref[...], wo1_ref[...]) + _mm(hm_ref[...], wo2_ref[...])
    x1_ref[...] = x1
    qx_ref[...] = _mm(_rms(x1, gxa_ref[...]), wxq_ref[...]) * (XA_DH ** -0.5)


def _mix(x, o_nsa, hm, w_out, g_xa, w_xq, tm):
    m, d = x.shape
    half = o_nsa.shape[1]
    row = lambda w: pl.BlockSpec((tm, w), lambda i: (i, 0))
    const = lambda shape: pl.BlockSpec(shape, lambda i: (0, 0))
    return pl.pallas_call(
        _mix_kernel,
        grid=(m // tm,),
        in_specs=[row(d), row(half), row(half), const((half, d)), const((half, d)), const((1, d)), const((d, d))],
        out_specs=[row(d), row(d)],
        out_shape=[jax.ShapeDtypeStruct((m, d), F32)] * 2,
        compiler_params=_params(("parallel",)),
        name="mix",
    )(x, o_nsa, hm, w_out[:half].astype(BF16), w_out[half:].astype(BF16), g_xa.reshape(1, d), w_xq.astype(BF16))


def _xattn_kernel(qx_ref, mk_ref, mv_ref, o_ref):
    for h in range(XA_HEADS):
        hs = slice(h * XA_DH, (h + 1) * XA_DH)
        s = _mm_nt(qx_ref[0, :, hs], mk_ref[0, :, hs])
        p = jnp.exp(s - jnp.max(s, axis=-1, keepdims=True))
        o_ref[0, :, hs] = _mm(p, mv_ref[0, :, hs]) / jnp.sum(p, axis=-1, keepdims=True)


def _xattn(qx, mem_k, mem_v, tq):
    b, t, d = qx.shape
    nm = mem_k.shape[1]
    mem_spec = pl.BlockSpec((1, nm, d), lambda bi, i: (bi, 0, 0))
    return pl.pallas_call(
        _xattn_kernel,
        grid=(b, t // tq),
        in_specs=[pl.BlockSpec((1, tq, d), lambda bi, i: (bi, i, 0)), mem_spec, mem_spec],
        out_specs=pl.BlockSpec((1, tq, d), lambda bi, i: (bi, i, 0)),
        out_shape=jax.ShapeDtypeStruct((b, t, d), F32),
        compiler_params=_params(("parallel", "parallel")),
        name="xattn",
    )(qx, mem_k, mem_v)


def _ffn_kernel(x1_ref, ox_ref, wxo_ref, gf_ref, wg_ref, wu_ref, wd_ref, gfin_ref, y_ref, x2_s, h_s, acc_s):
    j = pl.program_id(1)

    @pl.when(j == 0)
    def _():
        x2 = x1_ref[...] + _mm(ox_ref[...], wxo_ref[...])
        x2_s[...] = x2
        h_s[...] = _rms(x2, gf_ref[...]).astype(BF16)
        acc_s[...] = jnp.zeros(acc_s.shape, F32)

    h = h_s[...]
    g = jnp.dot(h, wg_ref[...], preferred_element_type=F32)
    u = jnp.dot(h, wu_ref[...], preferred_element_type=F32)
    acc_s[...] += _mm(g * _sigmoid(g) * u, wd_ref[...])

    @pl.when(j == pl.num_programs(1) - 1)
    def _():
        y_ref[...] = _rms(x2_s[...] + acc_s[...], gfin_ref[...])


def _ffn(x1, ox, w_xo, g_ffn, w_gate, w_up, w_down, g_final, tm, tf):
    m, d = x1.shape
    dff = w_gate.shape[1]
    row = pl.BlockSpec((tm, d), lambda i, j: (i, 0))
    vec = pl.BlockSpec((1, d), lambda i, j: (0, 0))
    return pl.pallas_call(
        _ffn_kernel,
        grid=(m // tm, dff // tf),
        in_specs=[row, row, pl.BlockSpec((d, d), lambda i, j: (0, 0)), vec,
                  pl.BlockSpec((d, tf), lambda i, j: (0, j)), pl.BlockSpec((d, tf), lambda i, j: (0, j)),
                  pl.BlockSpec((tf, d), lambda i, j: (j, 0)), vec],
        out_specs=row,
        out_shape=jax.ShapeDtypeStruct((m, d), F32),
        scratch_shapes=[pltpu.VMEM((tm, d), F32), pltpu.VMEM((tm, d), BF16), pltpu.VMEM((tm, d), F32)],
        compiler_params=_params(("parallel", "arbitrary")),
        name="ffn",
    )(x1, ox, w_xo.astype(BF16), g_ffn.reshape(1, d), w_gate.astype(BF16), w_up.astype(BF16),
      w_down.astype(BF16), g_final.reshape(1, d))


def _split_w_in(w_in, nsa_w, kv_w, mlstm_w):
    cuts = np.cumsum([nsa_w] + [kv_w] * 6 + [3 * NSA_HEADS] + [mlstm_w] * 3 + [MLSTM_HEADS] * 2)
    parts = jnp.split(w_in, cuts[:-1].tolist(), axis=1)
    small = jnp.concatenate([parts[7], parts[11], parts[12]], axis=1)
    small = jnp.pad(small, ((0, 0), (0, LANES - small.shape[1])))
    ws = [parts[0]] + list(parts[1:7]) + [small] + list(parts[8:11])
    return [w.astype(BF16) for w in ws]


def _tail(x1, ox, w, tm, b, t):
    d = x1.shape[1]
    dff = w["w_gate"].shape[1]
    tf = dff // 2 if (dff // 2) % LANES == 0 else dff
    y = _ffn(x1, ox.reshape(-1, d), w["w_xo"], w["g_ffn"], w["w_gate"], w["w_up"], w["w_down"], w["g_final"], tm, tf)
    return y.reshape(b, t, d)


def _gate_consts(w):
    bg = jnp.pad(w["b_gate"], (0, LANES - w["b_gate"].shape[0])).reshape(1, LANES)
    gh = jnp.concatenate([w["g_head_nsa"], w["g_head_nsa"]], axis=1)
    return bg, gh


def _prompt_group(x, mem, w):
    b, t, d = x.shape
    m = b * t
    tm = 512
    wp = w["w_in_parts"]
    wt = jnp.concatenate(wp[1:7], axis=1).T
    q, sm, xm, vm, om, kc, vc, ks, vs, kw, vw = _norm_proj(x.reshape(m, d), w["g_mix"], [wp[0]] + wp[7:], tm,
                                                           wt=wt, t=t)
    r3 = lambda a: a.reshape(b, t, a.shape[-1])
    npg = t // PAGE
    table = jnp.zeros((b, npg), jnp.int32)
    zeros_next = jnp.zeros((b, PAGE, LANES), F32)
    k_cmp = _compress(table, kc, zeros_next, *w["cmp_k"], ch=npg, paged=False, feature_major=True)
    v_cmp = _compress(table, vc, zeros_next, *w["cmp_v"], ch=npg, paged=False, feature_major=True)
    bg, gh = _gate_consts(w)
    o_nsa = _nsa_prompt_t(r3(q), r3(sm), k_cmp, v_cmp, ks, vs, kw, vw, bg, gh, LANES)
    L = next(c for c in (256, 128, 64) if t % c == 0)
    hm, c_out, n_out, m_out = _mlstm(
        r3(xm), r3(vm), r3(om), r3(sm), jnp.zeros((b, 3, xm.shape[1]), F32), w["conv_w"], w["conv_b"],
        w["w_qm"], w["w_km"], w["b_i"], w["b_f"], w["g_head_m"],
        jnp.zeros((b, MLSTM_HEADS, MLSTM_DH, MLSTM_DH), F32), jnp.zeros((b, MLSTM_HEADS, MLSTM_DH), F32),
        jnp.zeros((b, MLSTM_HEADS), F32), L, L)
    nm = mem.shape[1]
    mk, mv = _norm_proj(mem.reshape(b * nm, d), w["g_mem"], [w["w_xk"].astype(BF16), w["w_xv"].astype(BF16)],
                        min(512, b * nm))
    x1, qx = _mix(x.reshape(m, d), o_nsa.reshape(m, -1), hm.reshape(m, -1), w["w_out"], w["g_xa"], w["w_xq"], tm)
    ox = _xattn(qx.reshape(b, t, d), mk.reshape(b, nm, d), mv.reshape(b, nm, d), 512)
    y = _tail(x1, ox, w, tm, b, t)
    kv5 = lambda a: a.reshape(b, NSA_KV_HEADS, NSA_DH, a.shape[2]).transpose(0, 3, 1, 2)[None]
    keep = min(WINDOW, t)
    xm3 = r3(xm)
    states = (kv5(kc), kv5(vc), kv5(ks), kv5(vs), kv5(kw[:, :, t - keep:]), kv5(vw[:, :, t - keep:]),
              c_out[None], n_out[None], m_out[None], xm3[None, :, t - 3:],
              mk.reshape(1, b, nm, XA_HEADS, XA_DH), mv.reshape(1, b, nm, XA_HEADS, XA_DH))
    return y, states


def _sample_group(x, pools, page_table, win_k, win_v, conv0, c0, n0, m0, mem_k, mem_v, w):
    b, t, d = x.shape
    m = b * t
    tp = 8
    tm = min(m, 512)
    q, kc, vc, ks, vs, kw, vw, sm, xm, vm, om = _norm_proj(x.reshape(m, d), w["g_mix"], w["w_in_parts"], tm)
    r3 = lambda a: a.reshape(b, t, a.shape[-1])
    pad_t = lambda a: jnp.pad(r3(a), ((0, 0), (0, tp - t), (0, 0)))
    npg = page_table.shape[1]
    past = npg * PAGE
    ch = min(32, npg)
    fm = lambda a: a.transpose(0, 2, 3, 1).reshape(a.shape[0], LANES, a.shape[1])
    pool_kc, pool_vc, pool_ks, pool_vs = [fm(p) for p in pools]
    win_kt, win_vt = fm(win_k), fm(win_v)
    nc_valid = (past + (-(-t // SEL_BLOCK)) * SEL_BLOCK) // CMP_STRIDE
    ncp = -(-nc_valid // LANES) * LANES

    pad_page = lambda a: jnp.pad(r3(a), ((0, 0), (0, PAGE - t), (0, 0)))

    def compressed(pool, new_rows, cw):
        nxt = pad_page(new_rows)
        main = _compress(page_table, pool, nxt, *cw, ch=ch, paged=True, feature_major=True)
        tail = _compress(jnp.arange(b, dtype=jnp.int32).reshape(1, b), nxt, jnp.zeros((1, PAGE, LANES), F32),
                         *cw, ch=b, paged=True, feature_major=False)
        n_tail = nc_valid - npg * CMP_PER_PAGE
        full = jnp.concatenate([main, tail.reshape(b, CMP_PER_PAGE, LANES)[:, :n_tail]], axis=1)
        return jnp.pad(full, ((0, 0), (0, ncp - nc_valid), (0, 0)))

    k_cmp = compressed(pool_kc, kc, w["cmp_k"])
    v_cmp = compressed(pool_vc, vc, w["cmp_v"])
    bg, gh = _gate_consts(w)
    o_nsa = _nsa_sample2(page_table, pad_t(q), pad_t(sm), k_cmp, v_cmp, pool_ks, pool_vs, pad_page(ks), pad_page(vs),
                        win_kt, win_vt, pad_page(kw), pad_page(vw),
                        bg, gh, nc_valid, ch)[:, :t]
    hm, c_out, n_out, m_out = _mlstm(pad_t(xm), pad_t(vm), pad_t(om), pad_t(sm), conv0, w["conv_w"], w["conv_b"],
                                     w["w_qm"], w["w_km"], w["b_i"], w["b_f"], w["g_head_m"], c0, n0, m0, tp, t)
    hm = hm[:, :t]
    x1, qx = _mix(x.reshape(m, d), o_nsa.reshape(m, -1), hm.reshape(m, -1), w["w_out"], w["g_xa"], w["w_xq"], tm)
    nm = mem_k.shape[1]
    ox = _xattn(pad_t(qx), mem_k.reshape(b, nm, d), mem_v.reshape(b, nm, d), tp)[:, :t]
    y = _tail(x1, ox, w, tm, b, t)
    kv5 = lambda a: a.reshape(1, b, t, NSA_KV_HEADS, NSA_DH)
    keep = min(WINDOW, past + t)
    unfm = lambda a: a.reshape(b, NSA_KV_HEADS, NSA_DH, a.shape[2]).transpose(0, 3, 1, 2)[None]
    win5 = lambda old_t, new: unfm(jnp.concatenate([old_t, r3(new).transpose(0, 2, 1)], axis=2)[:, :, -keep:])
    conv_all = jnp.concatenate([conv0, r3(xm)], axis=1)
    states = (kv5(kc), kv5(vc), kv5(ks), kv5(vs), win5(win_kt, kw), win5(win_vt, vw),
              c_out[None], n_out[None], m_out[None], conv_all[None, :, -3:])
    return y, states


def kernel(x_prompt, x_sample, cache_k_cmp, cache_v_cmp, cache_k_slc, cache_v_slc, state_k_win, state_v_win, state_conv, state_C, state_n, state_m, cache_mem_k, cache_mem_v, page_table, mem_prompt, g_mix, w_in, b_gate, cmp_pe_k, cmp_w1_k, cmp_b1_k, cmp_w2_k, cmp_pe_v, cmp_w1_v, cmp_b1_v, cmp_w2_v, g_head_nsa, conv_w, conv_b, w_qm, w_km, b_i, b_f, g_head_m, w_out, g_xa, g_mem, w_xq, w_xk, w_xv, w_xo, g_ffn, w_gate, w_up, w_down, g_final):
    assert w_in.shape[0] == 1, "single-layer decoder"
    l = 0
    w = dict(g_mix=g_mix[l], b_gate=b_gate[l],
             w_in_parts=_split_w_in(w_in[l], NSA_HEADS * NSA_DH, NSA_KV_HEADS * NSA_DH, MLSTM_HEADS * MLSTM_DH),
             cmp_k=(cmp_pe_k[l], cmp_w1_k[l], cmp_b1_k[l], cmp_w2_k[l]),
             cmp_v=(cmp_pe_v[l], cmp_w1_v[l], cmp_b1_v[l], cmp_w2_v[l]),
             g_head_nsa=g_head_nsa[l], conv_w=conv_w[l], conv_b=conv_b[l], w_qm=w_qm[l], w_km=w_km[l],
             b_i=b_i[l], b_f=b_f[l], g_head_m=g_head_m[l], w_out=w_out[l], g_xa=g_xa[l], g_mem=g_mem[l],
             w_xq=w_xq[l], w_xk=w_xk[l], w_xv=w_xv[l], w_xo=w_xo[l], g_ffn=g_ffn[l], w_gate=w_gate[l],
             w_up=w_up[l], w_down=w_down[l], g_final=g_final)
    y_p, st_p = _prompt_group(x_prompt, mem_prompt, w)
    pools = (cache_k_cmp[l], cache_v_cmp[l], cache_k_slc[l], cache_v_slc[l])
    y_s, st_s = _sample_group(x_sample, pools, page_table, state_k_win[l], state_v_win[l], state_conv[l],
                              state_C[l], state_n[l], state_m[l], cache_mem_k[l], cache_mem_v[l], w)
    return (y_p, y_s) + st_p + st_s
```
